```python
import jax, jax.numpy as jnp
from jax import lax
import numpy as np

D_MODEL = 2048
BATCH = 16
SEQ = 256
DEPTH = 2
DEC_BATCH = 4
DEC_SEQ = 2048
PAST_LEN = 256

GRID_W = 64
CHUNK = 128
Q_BLOCK = 128
HEAD_DIM = 128
N_Q_HEADS = 8
N_KV_HEADS = 2
Q_PER_KV = N_Q_HEADS // N_KV_HEADS
ATTN_WIDTH = N_Q_HEADS * HEAD_DIM
KV_WIDTH = N_KV_HEADS * HEAD_DIM
N_SGU_HEADS = 8
SGU_HEAD_DIM = 128
SGU_WIDTH = N_SGU_HEADS * SGU_HEAD_DIM
MIX_WIDTH = ATTN_WIDTH + SGU_WIDTH
IN_WIDTH = ATTN_WIDTH + 2 * KV_WIDTH + 2 * SGU_WIDTH
ROPE_THETA = 10000.0
ROPE_AXIS_DIM = HEAD_DIM // 2
D_FF = 5632
N_EXPERTS = 8
TOP_K = 2
D_FF_EXPERT = 2816
N_DENSE = (DEPTH + 1) // 2
N_MOE = DEPTH // 2
N_MOD = 6
EPS = 1e-6

kernel_name = "hybrid_sgu_gqa_diffusion_step"


def rmsnorm(x, g):
    xf = x.astype(jnp.float32)
    y = xf * lax.rsqrt(jnp.mean(xf * xf, axis=-1, keepdims=True) + EPS)
    return (y * g.astype(jnp.float32)).astype(x.dtype)


def axial_angles(n_tokens):
    n_rows = n_tokens // GRID_W
    rows = jnp.broadcast_to(jnp.arange(n_rows)[:, None], (n_rows, GRID_W)).reshape(-1)
    cols = jnp.broadcast_to(jnp.arange(GRID_W)[None, :], (n_rows, GRID_W)).reshape(-1)
    inv = ROPE_THETA ** (-jnp.arange(0, ROPE_AXIS_DIM, 2, dtype=jnp.float32) / ROPE_AXIS_DIM)
    return rows.astype(jnp.float32)[:, None] * inv, cols.astype(jnp.float32)[:, None] * inv


def rotate_half_axis(x, ang):
    half = ROPE_AXIS_DIM // 2
    x1, x2 = x[..., :half], x[..., half:]
    cos = jnp.cos(ang)[:, None, :]
    sin = jnp.sin(ang)[:, None, :]
    return jnp.concatenate([x1 * cos - x2 * sin, x2 * cos + x1 * sin], axis=-1)


def apply_rope_2d(x):
    ang_r, ang_c = axial_angles(x.shape[1])
    xf = x.astype(jnp.float32)
    out = jnp.concatenate([rotate_half_axis(xf[..., :ROPE_AXIS_DIM], ang_r),
                           rotate_half_axis(xf[..., ROPE_AXIS_DIM:], ang_c)], axis=-1)
    return out.astype(x.dtype)


def block_attention(q, k, v):
    B, Lq = q.shape[0], q.shape[1]
    nb = Lq // Q_BLOCK
    qb = q.reshape(B, nb, Q_BLOCK, N_KV_HEADS, Q_PER_KV, HEAD_DIM).transpose(1, 0, 2, 3, 4, 5)
    scale = HEAD_DIM ** -0.5

    def one_block(q_blk):
        s = jnp.einsum('bqkgd,bskd->bkgqs', q_blk, k, preferred_element_type=jnp.float32) * scale
        p = jax.nn.softmax(s, axis=-1)
        return jnp.einsum('bkgqs,bskd->bqkgd', p.astype(v.dtype), v)

    out = lax.map(one_block, qb)
    return out.transpose(1, 0, 2, 3, 4, 5).reshape(B, Lq, ATTN_WIDTH)


def chunk_spatial_gating(u, g, w_s, b_s, g_norm):
    B, L = u.shape[0], u.shape[1]
    n = L // CHUNK
    gh = rmsnorm(g.reshape(B, n, CHUNK, N_SGU_HEADS, SGU_HEAD_DIM), g_norm)
    mixed = jnp.einsum('hpq,bnqhd->bnphd', w_s, gh) + b_s.T[None, None, :, :, None]
    return u * mixed.reshape(B, L, SGU_WIDTH)


def modulation(cond, w_ada, b_ada):
    m = jax.nn.silu(cond[..., None, :]) @ w_ada + b_ada
    return jnp.split(m, N_MOD, axis=-1)


def mix_projections(h, w_in, q_norm, k_norm):
    B, L = h.shape[0], h.shape[1]
    proj = h @ w_in
    q, k, v, u, g = jnp.split(proj, [ATTN_WIDTH, ATTN_WIDTH + KV_WIDTH, ATTN_WIDTH + 2 * KV_WIDTH,
                                     ATTN_WIDTH + 2 * KV_WIDTH + SGU_WIDTH], axis=-1)
    q = rmsnorm(q.reshape(B, L, N_Q_HEADS, HEAD_DIM), q_norm)
    k = rmsnorm(k.reshape(B, L, N_KV_HEADS, HEAD_DIM), k_norm)
    v = v.reshape(B, L, N_KV_HEADS, HEAD_DIM)
    return q, k, v, u, g


def merge_groups(attn_out, sgu_out, out_norm, w_out):
    o = jnp.concatenate([rmsnorm(attn_out, out_norm[:ATTN_WIDTH]),
                         rmsnorm(sgu_out, out_norm[ATTN_WIDTH:])], axis=-1)
    return o @ w_out


def swiglu(h, w_gate, w_up, w_down):
    return (jax.nn.silu(h @ w_gate) * (h @ w_up)) @ w_down


def moe_swiglu(h, w_router, b_router, w_gate, w_up, w_down):
    B, L, D = h.shape
    t = h.reshape(-1, D)
    logits = (t @ w_router).astype(jnp.float32) + b_router.astype(jnp.float32)
    top_val, top_idx = lax.top_k(logits, TOP_K)
    top_w = jax.nn.softmax(top_val, axis=-1)
    combine = jnp.sum(jax.nn.one_hot(top_idx, N_EXPERTS, dtype=jnp.float32) * top_w[..., None], axis=1)
    out = jnp.zeros_like(t)
    for e in range(N_EXPERTS):
        out = out + combine[:, e:e + 1].astype(t.dtype) * swiglu(t, w_gate[e], w_up[e], w_down[e])
    return out.reshape(B, L, D)


def channel_mixer(i, h, ffn_w_gate, ffn_w_up, ffn_w_down, w_router, b_router,
                  moe_w_gate, moe_w_up, moe_w_down):
    j = i // 2
    if i % 2 == 0:
        return swiglu(h, ffn_w_gate[j], ffn_w_up[j], ffn_w_down[j])
    return moe_swiglu(h, w_router[j], b_router[j], moe_w_gate[j], moe_w_up[j], moe_w_down[j])


def setup_inputs(seed: int = 0) -> dict:
    key = jax.random.key(seed)
    ks = jax.random.split(key, 32)
    f32 = jnp.float32
    nrm = lambda k, shape, s: jax.random.normal(k, shape, f32) * s
    D = D_MODEL
    return {
        "x_prompt": nrm(ks[0], (BATCH, SEQ, D), 1.0),
        "x_sample": nrm(ks[1], (DEC_BATCH, DEC_SEQ, D), 1.0),
        "cache_k": nrm(ks[2], (DEC_BATCH, DEPTH, PAST_LEN, N_KV_HEADS, HEAD_DIM), 1.0),
        "cache_v": nrm(ks[3], (DEC_BATCH, DEPTH, PAST_LEN, N_KV_HEADS, HEAD_DIM), 1.0),
        "c": nrm(ks[4], (DEC_BATCH, D), 1.0),
        "c_ctx": nrm(ks[5], (D,), 1.0),
        "w_ada": nrm(ks[6], (DEPTH, D, N_MOD * D), D ** -0.5),
        "b_ada": nrm(ks[7], (DEPTH, N_MOD * D), 0.01),
        "norm1_g": 1.0 + nrm(ks[8], (DEPTH, D), 0.02),
        "norm2_g": 1.0 + nrm(ks[9], (DEPTH, D), 0.02),
        "w_in": nrm(ks[10], (DEPTH, D, IN_WIDTH), D ** -0.5),
        "q_norm_g": 1.0 + nrm(ks[11], (DEPTH, HEAD_DIM), 0.02),
        "k_norm_g": 1.0 + nrm(ks[12], (DEPTH, HEAD_DIM), 0.02),
        "sgu_norm_g": 1.0 + nrm(ks[13], (DEPTH, N_SGU_HEADS, SGU_HEAD_DIM), 0.02),
        "w_spatial": nrm(ks[14], (DEPTH, N_SGU_HEADS, CHUNK, CHUNK), CHUNK ** -0.5),
        "b_spatial": 1.0 + nrm(ks[15], (DEPTH, N_SGU_HEADS, CHUNK), 0.01),
        "out_norm_g": 1.0 + nrm(ks[16], (DEPTH, MIX_WIDTH), 0.02),
        "w_out": nrm(ks[17], (DEPTH, MIX_WIDTH, D), MIX_WIDTH ** -0.5),
        "ffn_w_gate": nrm(ks[18], (N_DENSE, D, D_FF), D ** -0.5),
        "ffn_w_up": nrm(ks[19], (N_DENSE, D, D_FF), D ** -0.5),
        "ffn_w_down": nrm(ks[20], (N_DENSE, D_FF, D), D_FF ** -0.5),
        "w_router": nrm(ks[21], (N_MOE, D, N_EXPERTS), D ** -0.5),
        "b_router": nrm(ks[22], (N_MOE, N_EXPERTS), 0.01),
        "moe_w_gate": nrm(ks[23], (N_MOE, N_EXPERTS, D, D_FF_EXPERT), D ** -0.5),
        "moe_w_up": nrm(ks[24], (N_MOE, N_EXPERTS, D, D_FF_EXPERT), D ** -0.5),
        "moe_w_down": nrm(ks[25], (N_MOE, N_EXPERTS, D_FF_EXPERT, D), D_FF_EXPERT ** -0.5),
    }


def reference(x_prompt, x_sample, cache_k, cache_v, c, c_ctx, w_ada, b_ada, norm1_g, norm2_g,
              w_in, q_norm_g, k_norm_g, sgu_norm_g, w_spatial, b_spatial, out_norm_g, w_out,
              ffn_w_gate, ffn_w_up, ffn_w_down, w_router, b_router, moe_w_gate, moe_w_up, moe_w_down):
    x = x_prompt
    ks_out, vs_out = [], []
    for i in range(DEPTH):
        sh1, sc1, g1, sh2, sc2, g2 = modulation(c_ctx, w_ada[i], b_ada[i])
        h = rmsnorm(x, norm1_g[i]) * (1.0 + sc1) + sh1
        q, k, v, u, g = mix_projections(h, w_in[i], q_norm_g[i], k_norm_g[i])
        attn = block_attention(q, k, v)
        sgu = chunk_spatial_gating(u, g, w_spatial[i], b_spatial[i], sgu_norm_g[i])
        x = x + g1 * merge_groups(attn, sgu, out_norm_g[i], w_out[i])
        h2 = rmsnorm(x, norm2_g[i]) * (1.0 + sc2) + sh2
        x = x + g2 * channel_mixer(i, h2, ffn_w_gate, ffn_w_up, ffn_w_down, w_router, b_router,
                                   moe_w_gate, moe_w_up, moe_w_down)
        ks_out.append(k)
        vs_out.append(v)
    y_prompt = x
    new_cache_k = jnp.stack(ks_out, axis=1)
    new_cache_v = jnp.stack(vs_out, axis=1)

    x = x_sample
    for i in range(DEPTH):
        sh1, sc1, g1, sh2, sc2, g2 = modulation(c, w_ada[i], b_ada[i])
        h = rmsnorm(x, norm1_g[i]) * (1.0 + sc1) + sh1
        q, k, v, u, g = mix_projections(h, w_in[i], q_norm_g[i], k_norm_g[i])
        q = apply_rope_2d(q)
        k = apply_rope_2d(k)
        k_all = jnp.concatenate([cache_k[:, i].astype(k.dtype), k], axis=1)
        v_all = jnp.concatenate([cache_v[:, i].astype(v.dtype), v], axis=1)
        attn = block_attention(q, k_all, v_all)
        sgu = chunk_spatial_gating(u, g, w_spatial[i], b_spatial[i], sgu_norm_g[i])
        x = x + g1 * merge_groups(attn, sgu, out_norm_g[i], w_out[i])
        h2 = rmsnorm(x, norm2_g[i]) * (1.0 + sc2) + sh2
        x = x + g2 * channel_mixer(i, h2, ffn_w_gate, ffn_w_up, ffn_w_down, w_router, b_router,
                                   moe_w_gate, moe_w_up, moe_w_down)
    y_sample = x
    return (y_prompt, y_sample, new_cache_k, new_cache_v)
```

```python
import functools

import jax
import jax.numpy as jnp
from jax import lax
from jax.experimental import pallas as pl
from jax.experimental.pallas import tpu as pltpu

F32 = jnp.float32
BF16 = jnp.bfloat16
U32 = jnp.uint32
I32 = jnp.int32

D = 2048
BATCH, SEQ = 16, 256
DEC_BATCH, DEC_SEQ = 4, 2048
PAST_LEN = 256
DEPTH = 2
GRID_W = 64
CHUNK = 128
HEAD_DIM = 128
N_Q_HEADS, N_KV_HEADS = 8, 2
Q_PER_KV = N_Q_HEADS // N_KV_HEADS
ATTN_WIDTH = N_Q_HEADS * HEAD_DIM
KV_WIDTH = N_KV_HEADS * HEAD_DIM
N_SGU_HEADS = 8
SGU_WIDTH = N_SGU_HEADS * HEAD_DIM
IN_WIDTH = ATTN_WIDTH + 2 * KV_WIDTH + 2 * SGU_WIDTH
ROPE_THETA = 10000.0
ROPE_AXIS_DIM = HEAD_DIM // 2
D_FF = 5632
N_EXPERTS = 8
TOP_K = 2
D_FF_EXPERT = 2816
N_MOD = 6
EPS = 1e-6
ATTN_SCALE = HEAD_DIM ** -0.5

MP = BATCH * SEQ
MS = DEC_BATCH * DEC_SEQ
M = MP + MS
N_COND = 8
LANES = 128

VMEM_LIMIT = 56 * 1024 * 1024

TM = 1024
T_NORM = 512
T_Q = 256
T_MOE = 512
P_MOE = M * TOP_K + N_EXPERTS * T_MOE
NT_MOE = P_MOE // T_MOE
TN_GU = D_FF_EXPERT // 2
TN_DN = D // 2
T_DISPATCH = 256
T_COMBINE = 256


def _params(n_axes):
    return pltpu.CompilerParams(dimension_semantics=("arbitrary",) * n_axes,
                                vmem_limit_bytes=VMEM_LIMIT)


def _cond_row(i, t):
    return jnp.where(i < MP // t, 0, 1 + (i - MP // t) // (DEC_SEQ // t))


def _mod_spec(layer, chunk, t, tn=D, axis=0, naxes=1):
    per = D // tn

    def index_map(*g):
        col = chunk * per + (g[1 - axis] if (naxes == 2 and per > 1) else 0)
        return (layer, _cond_row(g[axis], t), 0, col)

    return pl.BlockSpec((None, None, 1, tn), index_map)


def _ada_kernel(c_ref, w_ref, b_ref, o_ref):
    c = c_ref[...]
    s = (c * jax.nn.sigmoid(c)).astype(BF16)
    o_ref[...] = jnp.dot(s, w_ref[...].astype(BF16), preferred_element_type=F32) + b_ref[...]


def _modulation(cond, w_ada, b_ada):
    tn = 1024
    width = N_MOD * D
    return pl.pallas_call(
        _ada_kernel,
        grid=(DEPTH, width // tn),
        in_specs=[pl.BlockSpec((N_COND, D), lambda l, n: (0, 0)),
                  pl.BlockSpec((None, D, tn), lambda l, n: (l, 0, n)),
                  pl.BlockSpec((None, 1, tn), lambda l, n: (l, 0, n))],
        out_specs=pl.BlockSpec((None, N_COND, tn), lambda l, n: (l, 0, n)),
        out_shape=jax.ShapeDtypeStruct((DEPTH, N_COND, width), F32),
        compiler_params=_params(2),
        name="modulation",
    )(cond, w_ada, b_ada.reshape(DEPTH, 1, width))


def _modulated_norm(x, g, sc, sh):
    y = x * lax.rsqrt(jnp.mean(x * x, axis=-1, keepdims=True) + EPS)
    return (y * g) * (1.0 + sc) + sh


def _prenorm_kernel(x_ref, g_ref, sc_ref, sh_ref, o_ref):
    o_ref[...] = _modulated_norm(x_ref[...], g_ref[...], sc_ref[...], sh_ref[...]).astype(BF16)


def _prenorm(x, gain, mod, layer, sc_chunk, sh_chunk):
    t = T_NORM
    return pl.pallas_call(
        _prenorm_kernel,
        grid=(M // t,),
        in_specs=[pl.BlockSpec((t, D), lambda i: (i, 0)),
                  pl.BlockSpec((None, 1, D), lambda i: (layer, 0, 0)),
                  _mod_spec(layer, sc_chunk, t),
                  _mod_spec(layer, sh_chunk, t)],
        out_specs=pl.BlockSpec((t, D), lambda i: (i, 0)),
        out_shape=jax.ShapeDtypeStruct((M, D), BF16),
        compiler_params=_params(1),
        name="prenorm",
    )(x, gain.reshape(DEPTH, 1, D), mod, mod)


def _pack_halves(h):
    hb = h.astype(BF16).astype(F32)
    lo = lax.bitcast_convert_type(hb[:, :D // 2], U32) >> 16
    hi = lax.bitcast_convert_type(hb[:, D // 2:], U32) & jnp.uint32(0xFFFF0000)
    return lo | hi


def _unpack_halves(w):
    lo = lax.bitcast_convert_type(w << 16, F32).astype(BF16)
    hi = lax.bitcast_convert_type(w & jnp.uint32(0xFFFF0000), F32).astype(BF16)
    return jnp.concatenate([lo, hi], axis=1)


def _split_bf16(a):
    hi = a.astype(BF16)
    return hi, (a - hi.astype(F32)).astype(BF16)


def _prenorm_router_kernel(x_ref, g_ref, sc_ref, sh_ref, wr_ref, br_ref, hp_ref, idx_ref, wt_ref):
    h = _modulated_norm(x_ref[...], g_ref[...], sc_ref[...], sh_ref[...])
    hp_ref[...] = _pack_halves(h)
    h_hi, h_lo = _split_bf16(h)
    w_hi, w_lo = _split_bf16(wr_ref[...])
    logits = (jnp.dot(h_hi, w_hi, preferred_element_type=F32)
              + jnp.dot(h_lo, w_hi, preferred_element_type=F32)
              + jnp.dot(h_hi, w_lo, preferred_element_type=F32)) + br_ref[...]
    lane = lax.broadcasted_iota(I32, logits.shape, 1)
    neg = jnp.float32(-jnp.inf)
    lg = jnp.where(lane < N_EXPERTS, logits, neg)
    m1 = jnp.max(lg, axis=-1, keepdims=True)
    i1 = jnp.min(jnp.where(lg == m1, lane, LANES), axis=-1, keepdims=True)
    lg2 = jnp.where(lane == i1, neg, lg)
    m2 = jnp.max(lg2, axis=-1, keepdims=True)
    i2 = jnp.min(jnp.where(lg2 == m2, lane, LANES), axis=-1, keepdims=True)
    e = jnp.exp(m2 - m1)
    w1 = 1.0 / (1.0 + e)
    w2 = e / (1.0 + e)
    idx_ref[...] = jnp.where(lane == 0, i1, jnp.where(lane == 1, i2, 0))
    wt_ref[...] = jnp.where(lane == 0, w1, jnp.where(lane == 1, w2, 0.0))


def _prenorm_router(x, gain, mod, layer, sc_chunk, sh_chunk, w_router, b_router):
    t = T_NORM
    wr = jnp.zeros((D, LANES), F32).at[:, :N_EXPERTS].set(w_router)
    br = jnp.zeros((1, LANES), F32).at[0, :N_EXPERTS].set(b_router)
    return pl.pallas_call(
        _prenorm_router_kernel,
        grid=(M // t,),
        in_specs=[pl.BlockSpec((t, D), lambda i: (i, 0)),
                  pl.BlockSpec((None, 1, D), lambda i: (layer, 0, 0)),
                  _mod_spec(layer, sc_chunk, t),
                  _mod_spec(layer, sh_chunk, t),
                  pl.BlockSpec((D, LANES), lambda i: (0, 0)),
                  pl.BlockSpec((1, LANES), lambda i: (0, 0))],
        out_specs=[pl.BlockSpec((t, D // 2), lambda i: (i, 0)),
                   pl.BlockSpec((t, LANES), lambda i: (i, 0)),
                   pl.BlockSpec((t, LANES), lambda i: (i, 0))],
        out_shape=[jax.ShapeDtypeStruct((M, D // 2), U32),
                   jax.ShapeDtypeStruct((M, LANES), I32),
                   jax.ShapeDtypeStruct((M, LANES), F32)],
        compiler_params=_params(1),
        name="prenorm_router",
    )(x, gain.reshape(DEPTH, 1, D), mod, mod, wr, br)


CAST_ROWS = 256


def _cast_rows(src_ref, dst_ref):
    def body(r, carry):
        rs = pl.ds(pl.multiple_of(r * CAST_ROWS, CAST_ROWS), CAST_ROWS)
        dst_ref[rs, :] = src_ref[rs, :].astype(BF16)
        return carry

    lax.fori_loop(0, src_ref.shape[0] // CAST_ROWS, body, 0)


def _cast_weight_once(w_ref, wbf_ref):
    @pl.when(pl.program_id(1) == 0)
    def _():
        _cast_rows(w_ref, wbf_ref)


def _head_rmsnorm(a, g):
    return a * lax.rsqrt(jnp.mean(a * a, axis=-1, keepdims=True) + EPS) * g


def _rope(a, cos, sin):
    lane = lax.broadcasted_iota(I32, a.shape, 1)
    quarter = ROPE_AXIS_DIM // 2
    partner = jnp.where((lane % ROPE_AXIS_DIM) < quarter,
                        pltpu.roll(a, HEAD_DIM - quarter, 1), pltpu.roll(a, quarter, 1))
    return a * cos + partner * sin


def _proj_q_kernel(x_ref, w_ref, g_ref, cos_ref, sin_ref, o_ref, wbf_ref):
    _cast_weight_once(w_ref, wbf_ref)
    acc = jnp.dot(x_ref[...], wbf_ref[...], preferred_element_type=F32)
    g, cos, sin = g_ref[...], cos_ref[...], sin_ref[...]
    for h in range(acc.shape[1] // HEAD_DIM):
        sl = slice(h * HEAD_DIM, (h + 1) * HEAD_DIM)
        q = _rope(_head_rmsnorm(acc[:, sl], g), cos, sin)
        o_ref[:, sl] = (q * ATTN_SCALE).astype(BF16)


def _proj_kv_kernel(x_ref, w_ref, g_ref, cos_ref, sin_ref, kf_ref, kb_ref, vf_ref, vb_ref, wbf_ref):
    _cast_weight_once(w_ref, wbf_ref)
    acc = jnp.dot(x_ref[...], wbf_ref[...], preferred_element_type=F32)
    g, cos, sin = g_ref[...], cos_ref[...], sin_ref[...]
    for h in range(N_KV_HEADS):
        sl = slice(h * HEAD_DIM, (h + 1) * HEAD_DIM)
        k = _head_rmsnorm(acc[:, sl], g)
        kf_ref[:, sl] = k
        kb_ref[:, sl] = _rope(k, cos, sin).astype(BF16)
    v = acc[:, KV_WIDTH:]
    vf_ref[...] = v
    vb_ref[...] = v.astype(BF16)


def _proj_plain_kernel(x_ref, w_ref, o_ref, wbf_ref):
    _cast_weight_once(w_ref, wbf_ref)
    o_ref[...] = jnp.dot(x_ref[...], wbf_ref[...], preferred_element_type=F32).astype(BF16)


def _proj_headnorm_kernel(x_ref, w_ref, g_ref, o_ref, wbf_ref):
    _cast_weight_once(w_ref, wbf_ref)
    acc = jnp.dot(x_ref[...], wbf_ref[...], preferred_element_type=F32)
    for h in range(acc.shape[1] // HEAD_DIM):
        sl = slice(h * HEAD_DIM, (h + 1) * HEAD_DIM)
        o_ref[:, sl] = _head_rmsnorm(acc[:, sl], g_ref[:, sl]).astype(BF16)


def _rope_tables():
    n_rows = DEC_SEQ // GRID_W
    rows = jnp.broadcast_to(jnp.arange(n_rows)[:, None], (n_rows, GRID_W)).reshape(-1)
    cols = jnp.broadcast_to(jnp.arange(GRID_W)[None, :], (n_rows, GRID_W)).reshape(-1)
    inv = ROPE_THETA ** (-jnp.arange(0, ROPE_AXIS_DIM, 2, dtype=F32) / ROPE_AXIS_DIM)
    ang_r = rows.astype(F32)[:, None] * inv
    ang_c = cols.astype(F32)[:, None] * inv
    cos = jnp.concatenate([jnp.cos(ang_r), jnp.cos(ang_r), jnp.cos(ang_c), jnp.cos(ang_c)], axis=1)
    sin = jnp.concatenate([-jnp.sin(ang_r), jnp.sin(ang_r), -jnp.sin(ang_c), jnp.sin(ang_c)], axis=1)
    cos = jnp.concatenate([jnp.ones((TM, HEAD_DIM), F32), cos], axis=0)
    sin = jnp.concatenate([jnp.zeros((TM, HEAD_DIM), F32), sin], axis=0)
    return cos, sin


def _rope_block(m):
    return jnp.where(m < MP // TM, 0, 1 + (m - MP // TM) % (DEC_SEQ // TM))


def _in_proj_call(kernel, h, w_in, layer, col0, width, tn, extra_in, extra_specs, outs, name):
    assert col0 % tn == 0 and width % tn == 0
    x_spec = pl.BlockSpec((TM, D), lambda n, m: (m, 0))
    w_spec = pl.BlockSpec((None, D, tn), lambda n, m: (layer, 0, col0 // tn + n))
    out_specs = [pl.BlockSpec((TM, bw), lambda n, m: (m, n)) for (_, bw, _) in outs]
    out_shape = [jax.ShapeDtypeStruct((M, w), dt) for (w, _, dt) in outs]
    return pl.pallas_call(
        kernel,
        grid=(width // tn, M // TM),
        in_specs=[x_spec, w_spec] + extra_specs,
        out_specs=out_specs,
        out_shape=out_shape,
        scratch_shapes=[pltpu.VMEM((D, tn), BF16)],
        compiler_params=_params(2),
        name=name,
    )(h, w_in, *extra_in)


def _in_projections(h, w_in, q_norm_g, k_norm_g, sgu_norm_g, cos, sin, layer):
    head_gain = pl.BlockSpec((None, 1, HEAD_DIM), lambda n, m: (layer, 0, 0))
    rope_spec = pl.BlockSpec((TM, HEAD_DIM), lambda n, m: (_rope_block(m), 0))
    tn = 2 * KV_WIDTH
    (q,) = _in_proj_call(_proj_q_kernel, h, w_in, layer, 0, ATTN_WIDTH, tn,
                         [q_norm_g.reshape(DEPTH, 1, HEAD_DIM), cos, sin],
                         [head_gain, rope_spec, rope_spec],
                         [(ATTN_WIDTH, tn, BF16)], "proj_q")
    kf, kb, vf, vb = _in_proj_call(_proj_kv_kernel, h, w_in, layer, ATTN_WIDTH, 2 * KV_WIDTH, tn,
                                   [k_norm_g.reshape(DEPTH, 1, HEAD_DIM), cos, sin],
                                   [head_gain, rope_spec, rope_spec],
                                   [(KV_WIDTH, KV_WIDTH, F32), (KV_WIDTH, KV_WIDTH, BF16),
                                    (KV_WIDTH, KV_WIDTH, F32), (KV_WIDTH, KV_WIDTH, BF16)],
                                   "proj_kv")
    (u,) = _in_proj_call(_proj_plain_kernel, h, w_in, layer, ATTN_WIDTH + 2 * KV_WIDTH, SGU_WIDTH, tn,
                         [], [], [(SGU_WIDTH, tn, BF16)], "proj_u")
    (gh,) = _in_proj_call(_proj_headnorm_kernel, h, w_in, layer,
                          ATTN_WIDTH + 2 * KV_WIDTH + SGU_WIDTH, SGU_WIDTH, tn,
                          [sgu_norm_g.reshape(DEPTH, 1, SGU_WIDTH)],
                          [pl.BlockSpec((None, 1, tn), lambda n, m: (layer, 0, n))],
                          [(SGU_WIDTH, tn, BF16)], "proj_g")
    return q, kf, kb, vf, vb, u, gh


def _qk(q, k):
    return lax.dot_general(q, k, (((1,), (1,)), ((), ())), preferred_element_type=F32)


def _attn_kernel(*refs, has_cache):
    if has_cache:
        q_ref, k_ref, v_ref, kc_ref, vc_ref, o_ref = refs
        kc = kc_ref[...].astype(BF16)
        vc = vc_ref[...].astype(BF16)
    else:
        q_ref, k_ref, v_ref, o_ref = refs
    k = k_ref[...]
    v = v_ref[...]
    for g in range(Q_PER_KV):
        sl = slice(g * HEAD_DIM, (g + 1) * HEAD_DIM)
        q = q_ref[:, sl]
        s = _qk(q, k)
        m = jnp.max(s, axis=-1, keepdims=True)
        if has_cache:
            sc = _qk(q, kc)
            m = jnp.maximum(m, jnp.max(sc, axis=-1, keepdims=True))
        p = jnp.exp(s - m)
        l = jnp.sum(p, axis=-1, keepdims=True)
        o = jnp.dot(p.astype(BF16), v, preferred_element_type=F32)
        if has_cache:
            pc = jnp.exp(sc - m)
            l = l + jnp.sum(pc, axis=-1, keepdims=True)
            o = o + jnp.dot(pc.astype(BF16), vc, preferred_element_type=F32)
        o_ref[:, sl] = (o / l).astype(BF16)


def _attention(q, kb, vb, cache_k, cache_v, layer, *, batch, seq, row0):
    has_cache = cache_k is not None
    nq = seq // T_Q
    qw = Q_PER_KV * HEAD_DIM
    q_spec = pl.BlockSpec((T_Q, qw), lambda b, kv, i: (row0 // T_Q + b * nq + i, kv))
    kv_spec = pl.BlockSpec((seq, HEAD_DIM), lambda b, kv, i: (row0 // seq + b, kv))
    in_specs = [q_spec, kv_spec, kv_spec]
    args = [q, kb, vb]
    if has_cache:
        c_spec = pl.BlockSpec((None, None, PAST_LEN, HEAD_DIM), lambda b, kv, i: (b, layer, 0, kv))
        in_specs += [c_spec, c_spec]
        args += [cache_k.reshape(DEC_BATCH, DEPTH, PAST_LEN, KV_WIDTH),
                 cache_v.reshape(DEC_BATCH, DEPTH, PAST_LEN, KV_WIDTH)]
    return pl.pallas_call(
        functools.partial(_attn_kernel, has_cache=has_cache),
        grid=(batch, N_KV_HEADS, nq),
        in_specs=in_specs,
        out_specs=pl.BlockSpec((T_Q, qw), lambda b, kv, i: (b * nq + i, kv)),
        out_shape=jax.ShapeDtypeStruct((batch * seq, ATTN_WIDTH), BF16),
        compiler_params=_params(3),
        name="attention_cached" if has_cache else "attention",
    )(*args)


def _sgu_merge_kernel(u_ref, gh_ref, ap_ref, as_ref, ws_ref, bs_ref, gn_ref, o_ref, sgu_ref):
    t = u_ref.shape[0]
    is_ctx = pl.program_id(0) < MP // t
    a = jnp.where(is_ctx, ap_ref[...], as_ref[...]).astype(F32)
    a = a * lax.rsqrt(jnp.mean(a * a, axis=-1, keepdims=True) + EPS) * gn_ref[:, :ATTN_WIDTH]
    o_ref[:, :ATTN_WIDTH] = a.astype(BF16)
    for h in range(N_SGU_HEADS):
        cs = slice(h * HEAD_DIM, (h + 1) * HEAD_DIM)
        w = ws_ref[h].astype(BF16)
        b = bs_ref[h]
        for c in range(t // CHUNK):
            rs = slice(c * CHUNK, (c + 1) * CHUNK)
            mixed = jnp.dot(w, gh_ref[rs, cs], preferred_element_type=F32) + b
            sgu_ref[rs, cs] = u_ref[rs, cs].astype(F32) * mixed
    s = sgu_ref[...]
    s = s * lax.rsqrt(jnp.mean(s * s, axis=-1, keepdims=True) + EPS) * gn_ref[:, ATTN_WIDTH:]
    o_ref[:, ATTN_WIDTH:] = s.astype(BF16)


def _sgu_merge(u, gh, attn_ctx, attn_lat, w_spatial, b_spatial, out_norm_g, layer):
    t = T_NORM
    bias = jnp.broadcast_to(b_spatial[:, :, :, None], (DEPTH, N_SGU_HEADS, CHUNK, HEAD_DIM))
    row = lambda w: pl.BlockSpec((t, w), lambda i: (i, 0))
    n_ctx = MP // t
    ctx_spec = pl.BlockSpec((t, ATTN_WIDTH), lambda i: (jnp.minimum(i, n_ctx - 1), 0))
    lat_spec = pl.BlockSpec((t, ATTN_WIDTH), lambda i: (jnp.maximum(i - n_ctx, 0), 0))
    return pl.pallas_call(
        _sgu_merge_kernel,
        grid=(M // t,),
        in_specs=[row(SGU_WIDTH), row(SGU_WIDTH), ctx_spec, lat_spec,
                  pl.BlockSpec((None, N_SGU_HEADS, CHUNK, CHUNK), lambda i: (layer, 0, 0, 0)),
                  pl.BlockSpec((None, N_SGU_HEADS, CHUNK, HEAD_DIM), lambda i: (layer, 0, 0, 0)),
                  pl.BlockSpec((None, 1, D), lambda i: (layer, 0, 0))],
        out_specs=row(D),
        out_shape=jax.ShapeDtypeStruct((M, D), BF16),
        scratch_shapes=[pltpu.VMEM((t, SGU_WIDTH), F32)],
        compiler_params=_params(1),
        name="sgu_merge",
    )(u, gh, attn_ctx, attn_lat, w_spatial, bias, out_norm_g.reshape(DEPTH, 1, D))


def _mm_resid_kernel(a_ref, w_ref, x_ref, g_ref, o_ref, wbf_ref):
    _cast_weight_once(w_ref, wbf_ref)
    acc = jnp.dot(a_ref[...], wbf_ref[...], preferred_element_type=F32)
    o_ref[...] = x_ref[...] + g_ref[...] * acc


def _mm_resid(a, w, x, mod, layer, w_index, gate_chunk, tm, tn):
    k = a.shape[1]
    return pl.pallas_call(
        _mm_resid_kernel,
        grid=(D // tn, M // tm),
        in_specs=[pl.BlockSpec((tm, k), lambda n, m: (m, 0)),
                  pl.BlockSpec((None, k, tn), lambda n, m: (w_index, 0, n)),
                  pl.BlockSpec((tm, tn), lambda n, m: (m, n)),
                  _mod_spec(layer, gate_chunk, tm, tn=tn, axis=1, naxes=2)],
        out_specs=pl.BlockSpec((tm, tn), lambda n, m: (m, n)),
        out_shape=jax.ShapeDtypeStruct((M, D), F32),
        scratch_shapes=[pltpu.VMEM((k, tn), BF16)],
        compiler_params=_params(2),
        name="mm_resid",
    )(a, w, x, mod)


def _swiglu(a, b):
    return a * jax.nn.sigmoid(a) * b


def _ffn_gu_kernel(x_ref, wg_ref, wu_ref, o_ref, wgb_ref, wub_ref):
    @pl.when(pl.program_id(1) == 0)
    def _():
        _cast_rows(wg_ref, wgb_ref)
        _cast_rows(wu_ref, wub_ref)
    x = x_ref[...]
    a = jnp.dot(x, wgb_ref[...], preferred_element_type=F32)
    b = jnp.dot(x, wub_ref[...], preferred_element_type=F32)
    o_ref[...] = _swiglu(a, b).astype(BF16)


def _ffn_gate_up(h, w_gate, w_up, j):
    tn = 512
    w_spec = pl.BlockSpec((None, D, tn), lambda n, m: (j, 0, n))
    return pl.pallas_call(
        _ffn_gu_kernel,
        grid=(D_FF // tn, M // TM),
        in_specs=[pl.BlockSpec((TM, D), lambda n, m: (m, 0)), w_spec, w_spec],
        out_specs=pl.BlockSpec((TM, tn), lambda n, m: (m, n)),
        out_shape=jax.ShapeDtypeStruct((M, D_FF), BF16),
        scratch_shapes=[pltpu.VMEM((D, tn), BF16), pltpu.VMEM((D, tn), BF16)],
        compiler_params=_params(2),
        name="ffn_gate_up",
    )(h, w_gate, w_up)


def _route_meta(idx):
    t = T_MOE
    experts = jnp.arange(N_EXPERTS, dtype=I32)
    onehot = (idx[:, :, None] == experts[None, None, :]).astype(I32).sum(axis=1)
    csum = jnp.cumsum(onehot, axis=0)
    rank = csum - onehot
    count = csum[-1]
    ntile = (count + t - 1) // t
    tile_end = jnp.cumsum(ntile)
    tile_start = tile_end - ntile
    nused = tile_end[-1]
    pos = (tile_start * t)[idx] + jnp.take_along_axis(rank, idx, axis=1)
    j = jnp.arange(NT_MOE, dtype=I32)
    te_raw = jnp.minimum(jnp.sum(j[:, None] >= tile_end[None, :], axis=1), N_EXPERTS - 1).astype(I32)
    te = jnp.where(j < nused, te_raw, te_raw[nused - 1])
    first = ((j == tile_start[te]) & (j < nused)).astype(I32)
    later = (ntile[None, :] > 0) & (experts[None, :] > te[:, None])
    nxt = jnp.min(jnp.where(later, experts[None, :], N_EXPERTS), axis=1)
    nxt = jnp.where(nxt == N_EXPERTS, -1, nxt).astype(I32)
    return pos.astype(I32), te, first, nxt, nused.reshape(1).astype(I32)


def _dispatch_kernel(p0_ref, p1_ref, hp_ref, xs_in_ref, xs_ref, sem):
    del xs_in_ref
    t = hp_ref.shape[0]
    base = pl.program_id(0) * t

    def issue(r, carry):
        src = hp_ref.at[pl.ds(r, 1), :]
        pltpu.make_async_copy(src, xs_ref.at[pl.ds(p0_ref[base + r], 1), :], sem.at[0]).start()
        pltpu.make_async_copy(src, xs_ref.at[pl.ds(p1_ref[base + r], 1), :], sem.at[1]).start()
        return carry

    lax.fori_loop(0, t, issue, 0, unroll=8)
    pltpu.make_async_copy(hp_ref, xs_ref.at[pl.ds(0, t), :], sem.at[0]).wait()
    pltpu.make_async_copy(hp_ref, xs_ref.at[pl.ds(0, t), :], sem.at[1]).wait()


def _dispatch(hp, pos0, pos1):
    t = T_DISPATCH
    xs0 = jnp.zeros((P_MOE, D // 2), U32)
    return pl.pallas_call(
        _dispatch_kernel,
        grid_spec=pltpu.PrefetchScalarGridSpec(
            num_scalar_prefetch=2,
            grid=(M // t,),
            in_specs=[pl.BlockSpec((t, D // 2), lambda i, p0, p1: (i, 0)),
                      pl.BlockSpec(memory_space=pl.ANY)],
            out_specs=pl.BlockSpec(memory_space=pl.ANY),
            scratch_shapes=[pltpu.SemaphoreType.DMA((2,))]),
        out_shape=jax.ShapeDtypeStruct((P_MOE, D // 2), U32),
        input_output_aliases={3: 0},
        compiler_params=_params(1),
        name="moe_dispatch",
    )(pos0, pos1, hp, xs0)


def _expert_weight_stream(w_refs, stage_refs, bf_refs, sem, te_ref, first_ref, nxt_ref, tn):
    c = pl.program_id(0)
    j = pl.program_id(1)
    nc = pl.num_programs(0)

    def copies(e, cc):
        col = pl.multiple_of(cc * tn, LANES)
        return [pltpu.make_async_copy(w.at[e, :, pl.ds(col, tn)], st, sem.at[k])
                for k, (w, st) in enumerate(zip(w_refs, stage_refs))]

    def start(e, cc):
        for cp in copies(e, cc):
            cp.start()

    @pl.when((c == 0) & (j == 0))
    def _():
        start(te_ref[0], 0)

    @pl.when(first_ref[j] == 1)
    def _():
        for cp in copies(0, 0):
            cp.wait()
        for st, bf in zip(stage_refs, bf_refs):
            _cast_rows(st, bf)
        ne = nxt_ref[j]

        @pl.when(ne >= 0)
        def _():
            start(ne, c)

        @pl.when((ne < 0) & (c + 1 < nc))
        def _():
            start(te_ref[0], c + 1)


def _gmm_gate_up_kernel(te_ref, first_ref, nxt_ref, nused_ref, xs_ref, wg_ref, wu_ref, o_ref,
                        sg_ref, su_ref, wgb_ref, wub_ref, sem):
    _expert_weight_stream((wg_ref, wu_ref), (sg_ref, su_ref), (wgb_ref, wub_ref), sem,
                          te_ref, first_ref, nxt_ref, TN_GU)

    @pl.when(pl.program_id(1) < nused_ref[0])
    def _():
        half = T_MOE // 2
        for r in range(2):
            rs = slice(r * half, (r + 1) * half)
            x = _unpack_halves(xs_ref[rs, :])
            a = jnp.dot(x, wgb_ref[...], preferred_element_type=F32)
            b = jnp.dot(x, wub_ref[...], preferred_element_type=F32)
            o_ref[rs, :] = _swiglu(a, b).astype(BF16)

    @pl.when(pl.program_id(1) >= nused_ref[0])
    def _():
        o_ref[...] = jnp.zeros(o_ref.shape, o_ref.dtype)


def _gmm_down_kernel(te_ref, first_ref, nxt_ref, nused_ref, a_ref, wd_ref, o_ref,
                     sd_ref, wdb_ref, sem):
    _expert_weight_stream((wd_ref,), (sd_ref,), (wdb_ref,), sem, te_ref, first_ref, nxt_ref, TN_DN)

    @pl.when(pl.program_id(1) < nused_ref[0])
    def _():
        o_ref[...] = jnp.dot(a_ref[...], wdb_ref[...], preferred_element_type=F32)

    @pl.when(pl.program_id(1) >= nused_ref[0])
    def _():
        o_ref[...] = jnp.zeros(o_ref.shape, o_ref.dtype)


def _used_tile(j, nused):
    return jnp.minimum(j, nused[0] - 1)


def _gmm_gate_up(xs, w_gate, w_up, meta):
    te, first, nxt, nused = meta
    tn = TN_GU
    return pl.pallas_call(
        _gmm_gate_up_kernel,
        grid_spec=pltpu.PrefetchScalarGridSpec(
            num_scalar_prefetch=4,
            grid=(D_FF_EXPERT // tn, NT_MOE),
            in_specs=[pl.BlockSpec((T_MOE, D // 2), lambda c, j, te, fi, nx, nu: (_used_tile(j, nu), 0)),
                      pl.BlockSpec(memory_space=pl.ANY),
                      pl.BlockSpec(memory_space=pl.ANY)],
            out_specs=pl.BlockSpec((T_MOE, tn), lambda c, j, te, fi, nx, nu: (j, c)),
            scratch_shapes=[pltpu.VMEM((D, tn), F32), pltpu.VMEM((D, tn), F32),
                            pltpu.VMEM((D, tn), BF16), pltpu.VMEM((D, tn), BF16),
                            pltpu.SemaphoreType.DMA((2,))]),
        out_shape=jax.ShapeDtypeStruct((P_MOE, D_FF_EXPERT), BF16),
        compiler_params=_params(2),
        name="moe_gate_up",
    )(te, first, nxt, nused, xs, w_gate, w_up)


def _gmm_down(act, w_down, meta):
    te, first, nxt, nused = meta
    tn = TN_DN
    return pl.pallas_call(
        _gmm_down_kernel,
        grid_spec=pltpu.PrefetchScalarGridSpec(
            num_scalar_prefetch=4,
            grid=(D // tn, NT_MOE),
            in_specs=[pl.BlockSpec((T_MOE, D_FF_EXPERT), lambda c, j, te, fi, nx, nu: (_used_tile(j, nu), 0)),
                      pl.BlockSpec(memory_space=pl.ANY)],
            out_specs=pl.BlockSpec((T_MOE, tn), lambda c, j, te, fi, nx, nu: (j, c)),
            scratch_shapes=[pltpu.VMEM((D_FF_EXPERT, tn), F32), pltpu.VMEM((D_FF_EXPERT, tn), BF16),
                            pltpu.SemaphoreType.DMA((1,))]),
        out_shape=jax.ShapeDtypeStruct((P_MOE, D), F32),
        compiler_params=_params(2),
        name="moe_down",
    )(te, first, nxt, nused, act, w_down)


def _combine_kernel(p0_ref, p1_ref, ys_ref, x_ref, g_ref, w_ref, o_ref, a_ref, b_ref, sem):
    t = x_ref.shape[0]
    i = pl.program_id(0)
    n = pl.num_programs(0)

    def issue(step, slot):
        base = step * t

        def body(r, carry):
            pltpu.make_async_copy(ys_ref.at[pl.ds(p0_ref[base + r], 1), :],
                                  a_ref.at[slot, pl.ds(r, 1), :], sem.at[0, slot]).start()
            pltpu.make_async_copy(ys_ref.at[pl.ds(p1_ref[base + r], 1), :],
                                  b_ref.at[slot, pl.ds(r, 1), :], sem.at[1, slot]).start()
            return carry

        lax.fori_loop(0, t, body, 0, unroll=8)

    @pl.when(i == 0)
    def _():
        issue(0, 0)

    @pl.when(i + 1 < n)
    def _():
        issue(i + 1, (i + 1) % 2)

    slot = i % 2
    pltpu.make_async_copy(ys_ref.at[pl.ds(0, t), :], a_ref.at[slot], sem.at[0, slot]).wait()
    pltpu.make_async_copy(ys_ref.at[pl.ds(0, t), :], b_ref.at[slot], sem.at[1, slot]).wait()
    w = w_ref[...]
    moe = w[:, 0:1] * a_ref[slot] + w[:, 1:2] * b_ref[slot]
    o_ref[...] = x_ref[...] + g_ref[...] * moe


def _combine(ys, x, mod, layer, gate_chunk, wts, pos0, pos1):
    t = T_COMBINE
    mod_spec = _mod_spec(layer, gate_chunk, t)
    mod_map = mod_spec.index_map
    return pl.pallas_call(
        _combine_kernel,
        grid_spec=pltpu.PrefetchScalarGridSpec(
            num_scalar_prefetch=2,
            grid=(M // t,),
            in_specs=[pl.BlockSpec(memory_space=pl.ANY),
                      pl.BlockSpec((t, D), lambda i, p0, p1: (i, 0)),
                      pl.BlockSpec(mod_spec.block_shape, lambda i, p0, p1: mod_map(i)),
                      pl.BlockSpec((t, LANES), lambda i, p0, p1: (i, 0))],
            out_specs=pl.BlockSpec((t, D), lambda i, p0, p1: (i, 0)),
            scratch_shapes=[pltpu.VMEM((2, t, D), F32), pltpu.VMEM((2, t, D), F32),
                            pltpu.SemaphoreType.DMA((2, 2))]),
        out_shape=jax.ShapeDtypeStruct((M, D), F32),
        compiler_params=_params(1),
        name="moe_combine",
    )(pos0, pos1, ys, x, mod, wts)


def _moe(x, mod, layer, norm2_g, w_router, b_router, w_gate, w_up, w_down):
    hp, idx, wts = _prenorm_router(x, norm2_g, mod, layer, 4, 3, w_router, b_router)
    pos, te, first, nxt, nused = _route_meta(idx[:, :TOP_K])
    pos0, pos1 = pos[:, 0], pos[:, 1]
    meta = (te, first, nxt, nused)
    xs = _dispatch(hp, pos0, pos1)
    act = _gmm_gate_up(xs, w_gate, w_up, meta)
    ys = _gmm_down(act, w_down, meta)
    return _combine(ys, x, mod, layer, 5, wts, pos0, pos1)


def kernel(x_prompt, x_sample, cache_k, cache_v, c, c_ctx, w_ada, b_ada, norm1_g, norm2_g, w_in, q_norm_g, k_norm_g, sgu_norm_g, w_spatial, b_spatial, out_norm_g, w_out, ffn_w_gate, ffn_w_up, ffn_w_down, w_router, b_router, moe_w_gate, moe_w_up, moe_w_down):
    x = jnp.concatenate([x_prompt.reshape(MP, D), x_sample.reshape(MS, D)], axis=0)
    cond = jnp.concatenate([c_ctx[None, :], c, jnp.zeros((N_COND - 1 - DEC_BATCH, D), F32)], axis=0)
    mod = _modulation(cond, w_ada, b_ada).reshape(DEPTH, N_COND, 1, N_MOD * D)
    cos, sin = _rope_tables()

    new_k, new_v = [], []
    for i in range(DEPTH):
        h = _prenorm(x, norm1_g, mod, i, 1, 0)
        q, kf, kb, vf, vb, u, gh = _in_projections(h, w_in, q_norm_g, k_norm_g, sgu_norm_g, cos, sin, i)
        attn_ctx = _attention(q, kb, vb, None, None, i, batch=BATCH, seq=SEQ, row0=0)
        attn_lat = _attention(q, kb, vb, cache_k, cache_v, i, batch=DEC_BATCH, seq=DEC_SEQ, row0=MP)
        o = _sgu_merge(u, gh, attn_ctx, attn_lat, w_spatial, b_spatial, out_norm_g, i)
        x = _mm_resid(o, w_out, x, mod, i, i, 2, TM, 1024)
        j = i // 2
        if i % 2 == 0:
            h2 = _prenorm(x, norm2_g, mod, i, 4, 3)
            act = _ffn_gate_up(h2, ffn_w_gate, ffn_w_up, j)
            x = _mm_resid(act, ffn_w_down, x, mod, i, j, 5, 512, 512)
        else:
            x = _moe(x, mod, i, norm2_g, w_router[j], b_router[j],
                     moe_w_gate[j], moe_w_up[j], moe_w_down[j])
        new_k.append(kf[:MP].reshape(BATCH, SEQ, N_KV_HEADS, HEAD_DIM))
        new_v.append(vf[:MP].reshape(BATCH, SEQ, N_KV_HEADS, HEAD_DIM))

    y_prompt = x[:MP].reshape(BATCH, SEQ, D)
    y_sample = x[MP:].reshape(DEC_BATCH, DEC_SEQ, D)
    return (y_prompt, y_sample, jnp.stack(new_k, axis=1), jnp.stack(new_v, axis=1))
```

```python
import functools

import jax
import jax.numpy as jnp
from jax import lax
from jax.experimental import pallas as pl
from jax.experimental.pallas import tpu as pltpu

F32 = jnp.float32
BF16 = jnp.bfloat16
I32 = jnp.int32

D = 2048
BATCH, SEQ = 16, 256
DEC_BATCH, DEC_SEQ = 4, 2048
PAST_LEN = 256
DEPTH = 2
GRID_W = 64
CHUNK = 128
HEAD_DIM = 128
N_Q_HEADS, N_KV_HEADS = 8, 2
Q_PER_KV = N_Q_HEADS // N_KV_HEADS
ATTN_WIDTH = N_Q_HEADS * HEAD_DIM
KV_WIDTH = N_KV_HEADS * HEAD_DIM
N_SGU_HEADS = 8
SGU_WIDTH = N_SGU_HEADS * HEAD_DIM
IN_WIDTH = ATTN_WIDTH + 2 * KV_WIDTH + 2 * SGU_WIDTH
ROPE_THETA = 10000.0
ROPE_AXIS_DIM = HEAD_DIM // 2
D_FF = 5632
N_EXPERTS = 8
TOP_K = 2
D_FF_EXPERT = 2816
N_MOD = 6
EPS = 1e-6
ATTN_SCALE = HEAD_DIM ** -0.5

MP = BATCH * SEQ
MS = DEC_BATCH * DEC_SEQ
M = MP + MS
N_COND = 8
LANES = 128
SUBLANES = 8
SUBLANE_BITS = 3

VMEM_LIMIT = 56 * 1024 * 1024

TM = 1024
TN_IN = 2 * KV_WIDTH
T_NORM = 512
T_Q = 256
T_MOE = 512
P_MOE = M * TOP_K + N_EXPERTS * T_MOE
NT_MOE = P_MOE // T_MOE
TN_GU = D_FF_EXPERT // 2
TN_DN = D // 2
T_DISPATCH = 256
T_COMBINE = 256
ZERO_ROWS = T_MOE // 2
PAD_BITS = ZERO_ROWS.bit_length()


def _params(n_axes):
    return pltpu.CompilerParams(dimension_semantics=("arbitrary",) * n_axes,
                                vmem_limit_bytes=VMEM_LIMIT)


def _cond_row(i, t):
    return jnp.where(i < MP // t, 0, 1 + (i - MP // t) // (DEC_SEQ // t))


def _is_ctx_tile(i, t):
    return i < MP // t


def _stream_in(x, t, width, row_of, col_of):
    n_ctx = MP // t
    pair = isinstance(x, tuple)
    base = 0 if pair else n_ctx
    ctx = pl.BlockSpec((t, width), lambda *g: (jnp.minimum(row_of(*g), n_ctx - 1), col_of(*g)))
    lat = pl.BlockSpec((t, width), lambda *g: (base + jnp.maximum(row_of(*g) - n_ctx, 0), col_of(*g)))
    return (list(x) if pair else [x, x]), [ctx, lat]


def _mod_spec(layer, chunk, t, row_of, col_of=None, tn=D):
    per = D // tn

    def index_map(*g):
        col = chunk * per + (col_of(*g) if col_of is not None else 0)
        return (layer, _cond_row(row_of(*g), t), 0, col)

    return pl.BlockSpec((None, None, 1, tn), index_map)


def _ada_kernel(c_ref, w_ref, b_ref, o_ref):
    c = c_ref[...]
    s = (c * jax.nn.sigmoid(c)).astype(BF16)
    o_ref[...] = jnp.dot(s, w_ref[...].astype(BF16), preferred_element_type=F32) + b_ref[...]


def _modulation(cond, w_ada, b_ada):
    tn = 1024
    width = N_MOD * D
    return pl.pallas_call(
        _ada_kernel,
        grid=(DEPTH, width // tn),
        in_specs=[pl.BlockSpec((N_COND, D), lambda l, n: (0, 0)),
                  pl.BlockSpec((None, D, tn), lambda l, n: (l, 0, n)),
                  pl.BlockSpec((None, 1, tn), lambda l, n: (l, 0, n))],
        out_specs=pl.BlockSpec((None, N_COND, tn), lambda l, n: (l, 0, n)),
        out_shape=jax.ShapeDtypeStruct((DEPTH, N_COND, width), F32),
        compiler_params=_params(2),
        name="modulation",
    )(cond, w_ada, b_ada.reshape(DEPTH, 1, width))


def _modulated_norm(x, g, sc, sh):
    y = x * lax.rsqrt(jnp.mean(x * x, axis=-1, keepdims=True) + EPS)
    return (y * g) * (1.0 + sc) + sh


def _pick(xa_ref, xb_ref, axis):
    t = xa_ref.shape[0]
    return jnp.where(_is_ctx_tile(pl.program_id(axis), t), xa_ref[...], xb_ref[...])


def _prenorm_kernel(xa_ref, xb_ref, g_ref, sc_ref, sh_ref, o_ref):
    x = _pick(xa_ref, xb_ref, 0)
    o_ref[...] = _modulated_norm(x, g_ref[...], sc_ref[...], sh_ref[...]).astype(BF16)


def _prenorm(x, gain, mod, layer, sc_chunk, sh_chunk):
    t = T_NORM
    row = lambda i: i
    x_args, x_specs = _stream_in(x, t, D, row, lambda i: 0)
    return pl.pallas_call(
        _prenorm_kernel,
        grid=(M // t,),
        in_specs=x_specs + [pl.BlockSpec((None, 1, D), lambda i: (layer, 0, 0)),
                            _mod_spec(layer, sc_chunk, t, row),
                            _mod_spec(layer, sh_chunk, t, row)],
        out_specs=pl.BlockSpec((t, D), lambda i: (i, 0)),
        out_shape=jax.ShapeDtypeStruct((M, D), BF16),
        compiler_params=_params(1),
        name="prenorm",
    )(*x_args, gain.reshape(DEPTH, 1, D), mod, mod)


def _split_bf16(a):
    hi = a.astype(BF16)
    return hi, (a - hi.astype(F32)).astype(BF16)


def _prenorm_router_kernel(xa_ref, xb_ref, g_ref, sc_ref, sh_ref, wr_ref, br_ref, h_ref, idx_ref, wt_ref):
    h = _modulated_norm(_pick(xa_ref, xb_ref, 0), g_ref[...], sc_ref[...], sh_ref[...])
    h_ref[...] = h
    h_hi, h_lo = _split_bf16(h)
    w_hi, w_lo = _split_bf16(wr_ref[...])
    logits = (jnp.dot(h_hi, w_hi, preferred_element_type=F32)
              + jnp.dot(h_lo, w_hi, preferred_element_type=F32)
              + jnp.dot(h_hi, w_lo, preferred_element_type=F32)) + br_ref[...]
    lane = lax.broadcasted_iota(I32, logits.shape, 1)
    neg = jnp.float32(-jnp.inf)
    lg = jnp.where(lane < N_EXPERTS, logits, neg)
    m1 = jnp.max(lg, axis=-1, keepdims=True)
    i1 = jnp.min(jnp.where(lg == m1, lane, LANES), axis=-1, keepdims=True)
    lg2 = jnp.where(lane == i1, neg, lg)
    m2 = jnp.max(lg2, axis=-1, keepdims=True)
    i2 = jnp.min(jnp.where(lg2 == m2, lane, LANES), axis=-1, keepdims=True)
    e = jnp.exp(m2 - m1)
    w1 = 1.0 / (1.0 + e)
    w2 = e / (1.0 + e)
    idx_ref[...] = jnp.where(lane == 0, i1, jnp.where(lane == 1, i2, 0))
    wt_ref[...] = jnp.where(lane == 0, w1, jnp.where(lane == 1, w2, 0.0))


def _prenorm_router(x, gain, mod, layer, sc_chunk, sh_chunk, w_router, b_router):
    t = T_NORM
    row = lambda i: i
    x_args, x_specs = _stream_in(x, t, D, row, lambda i: 0)
    wr = jnp.zeros((D, LANES), F32).at[:, :N_EXPERTS].set(w_router)
    br = jnp.zeros((1, LANES), F32).at[0, :N_EXPERTS].set(b_router)
    return pl.pallas_call(
        _prenorm_router_kernel,
        grid=(M // t,),
        in_specs=x_specs + [pl.BlockSpec((None, 1, D), lambda i: (layer, 0, 0)),
                            _mod_spec(layer, sc_chunk, t, row),
                            _mod_spec(layer, sh_chunk, t, row),
                            pl.BlockSpec((D, LANES), lambda i: (0, 0)),
                            pl.BlockSpec((1, LANES), lambda i: (0, 0))],
        out_specs=[pl.BlockSpec((t, D), lambda i: (i, 0)),
                   pl.BlockSpec((t, LANES), lambda i: (i, 0)),
                   pl.BlockSpec((t, LANES), lambda i: (i, 0))],
        out_shape=[jax.ShapeDtypeStruct((M, D), F32),
                   jax.ShapeDtypeStruct((M, LANES), I32),
                   jax.ShapeDtypeStruct((M, LANES), F32)],
        compiler_params=_params(1),
        name="prenorm_router",
    )(*x_args, gain.reshape(DEPTH, 1, D), mod, mod, wr, br)


CAST_ROWS = 256


def _cast_rows(src_ref, dst_ref):
    def body(r, carry):
        rs = pl.ds(pl.multiple_of(r * CAST_ROWS, CAST_ROWS), CAST_ROWS)
        dst_ref[rs, :] = src_ref[rs, :].astype(BF16)
        return carry

    lax.fori_loop(0, src_ref.shape[0] // CAST_ROWS, body, 0)


def _cast_weight_once(w_ref, wbf_ref):
    @pl.when(pl.program_id(1) == 0)
    def _():
        _cast_rows(w_ref, wbf_ref)


def _head_rms(a):
    return lax.rsqrt(jnp.mean(a * a, axis=-1, keepdims=True) + EPS)


def _rope_partner(ag, perm):
    hi, lo = _split_bf16(ag)
    return (jnp.dot(hi, perm, preferred_element_type=F32)
            + jnp.dot(lo, perm, preferred_element_type=F32))


def _proj_q_kernel(x_ref, w_ref, g_ref, perm_ref, cos_ref, sin_ref, o_ref, wbf_ref):
    _cast_weight_once(w_ref, wbf_ref)
    acc = jnp.dot(x_ref[...], wbf_ref[...], preferred_element_type=F32)
    ag = acc * g_ref[...]
    partner = _rope_partner(ag, perm_ref[...])
    cos, sin = cos_ref[...], sin_ref[...]
    for h in range(acc.shape[1] // HEAD_DIM):
        sl = slice(h * HEAD_DIM, (h + 1) * HEAD_DIM)
        r = _head_rms(acc[:, sl]) * ATTN_SCALE
        o_ref[:, sl] = ((ag[:, sl] * cos + partner[:, sl] * sin) * r).astype(BF16)


def _proj_kv_kernel(x_ref, w_ref, g_ref, perm_ref, cos_ref, sin_ref,
                    kf_ref, kb_ref, vf_ref, vb_ref, wbf_ref):
    _cast_weight_once(w_ref, wbf_ref)
    acc = jnp.dot(x_ref[...], wbf_ref[...], preferred_element_type=F32)
    k = acc[:, :KV_WIDTH]
    ag = k * g_ref[...]
    partner = _rope_partner(ag, perm_ref[...])
    cos, sin = cos_ref[...], sin_ref[...]
    for h in range(N_KV_HEADS):
        sl = slice(h * HEAD_DIM, (h + 1) * HEAD_DIM)
        r = _head_rms(k[:, sl])
        kf_ref[:, sl] = ag[:, sl] * r
        kb_ref[:, sl] = ((ag[:, sl] * cos + partner[:, sl] * sin) * r).astype(BF16)
    v = acc[:, KV_WIDTH:]
    vf_ref[...] = v
    vb_ref[...] = v.astype(BF16)


def _proj_plain_kernel(x_ref, w_ref, o_ref, wbf_ref):
    _cast_weight_once(w_ref, wbf_ref)
    o_ref[...] = jnp.dot(x_ref[...], wbf_ref[...], preferred_element_type=F32).astype(BF16)


def _proj_headnorm_kernel(x_ref, w_ref, g_ref, o_ref, wbf_ref):
    _cast_weight_once(w_ref, wbf_ref)
    acc = jnp.dot(x_ref[...], wbf_ref[...], preferred_element_type=F32)
    for h in range(acc.shape[1] // HEAD_DIM):
        sl = slice(h * HEAD_DIM, (h + 1) * HEAD_DIM)
        a = acc[:, sl]
        o_ref[:, sl] = (a * _head_rms(a) * g_ref[:, sl]).astype(BF16)


def _rope_tables():
    n_rows = DEC_SEQ // GRID_W
    rows = jnp.broadcast_to(jnp.arange(n_rows)[:, None], (n_rows, GRID_W)).reshape(-1)
    cols = jnp.broadcast_to(jnp.arange(GRID_W)[None, :], (n_rows, GRID_W)).reshape(-1)
    inv = ROPE_THETA ** (-jnp.arange(0, ROPE_AXIS_DIM, 2, dtype=F32) / ROPE_AXIS_DIM)
    ang_r = rows.astype(F32)[:, None] * inv
    ang_c = cols.astype(F32)[:, None] * inv
    cos = jnp.concatenate([jnp.cos(ang_r), jnp.cos(ang_r), jnp.cos(ang_c), jnp.cos(ang_c)], axis=1)
    sin = jnp.concatenate([-jnp.sin(ang_r), jnp.sin(ang_r), -jnp.sin(ang_c), jnp.sin(ang_c)], axis=1)
    cos = jnp.concatenate([jnp.ones((TM, HEAD_DIM), F32), cos], axis=0)
    sin = jnp.concatenate([jnp.zeros((TM, HEAD_DIM), F32), sin], axis=0)
    return cos, sin


def _partner_matrix(n_heads):
    w = n_heads * HEAD_DIM
    quarter = ROPE_AXIS_DIM // 2
    j = jnp.arange(w)
    partner = jnp.where((j % ROPE_AXIS_DIM) < quarter, j + quarter, j - quarter)
    return (jnp.arange(w)[:, None] == partner[None, :]).astype(BF16)


def _rope_block(m):
    return jnp.where(m < MP // TM, 0, 1 + (m - MP // TM) % (DEC_SEQ // TM))


def _in_proj_call(kernel, h, w_in, layer, col0, width, extra_in, extra_specs, outs, name):
    tn = TN_IN
    assert col0 % tn == 0 and width % tn == 0
    x_spec = pl.BlockSpec((TM, D), lambda n, m: (m, 0))
    w_spec = pl.BlockSpec((None, D, tn), lambda n, m: (layer, 0, col0 // tn + n))
    out_specs = [pl.BlockSpec((TM, bw), lambda n, m: (m, n)) for (_, bw, _) in outs]
    out_shape = [jax.ShapeDtypeStruct((M, w), dt) for (w, _, dt) in outs]
    return pl.pallas_call(
        kernel,
        grid=(width // tn, M // TM),
        in_specs=[x_spec, w_spec] + extra_specs,
        out_specs=out_specs,
        out_shape=out_shape,
        scratch_shapes=[pltpu.VMEM((D, tn), BF16)],
        compiler_params=_params(2),
        name=name,
    )(h, w_in, *extra_in)


def _in_projections(h, w_in, q_norm_g, k_norm_g, sgu_norm_g, cos, sin, layer):
    tn = TN_IN
    rope_spec = pl.BlockSpec((TM, HEAD_DIM), lambda n, m: (_rope_block(m), 0))
    const = lambda shape: pl.BlockSpec(shape, lambda n, m: (0,) * len(shape))
    q_heads = tn // HEAD_DIM
    q_gain = jnp.tile(q_norm_g[layer], q_heads)[None, :]
    k_gain = jnp.tile(k_norm_g[layer], N_KV_HEADS)[None, :]
    (q,) = _in_proj_call(_proj_q_kernel, h, w_in, layer, 0, ATTN_WIDTH,
                         [q_gain, _partner_matrix(q_heads), cos, sin],
                         [const((1, tn)), const((tn, tn)), rope_spec, rope_spec],
                         [(ATTN_WIDTH, tn, BF16)], "proj_q")
    kf, kb, vf, vb = _in_proj_call(_proj_kv_kernel, h, w_in, layer, ATTN_WIDTH, 2 * KV_WIDTH,
                                   [k_gain, _partner_matrix(N_KV_HEADS), cos, sin],
                                   [const((1, KV_WIDTH)), const((KV_WIDTH, KV_WIDTH)), rope_spec, rope_spec],
                                   [(KV_WIDTH, KV_WIDTH, F32), (KV_WIDTH, KV_WIDTH, BF16),
                                    (KV_WIDTH, KV_WIDTH, F32), (KV_WIDTH, KV_WIDTH, BF16)],
                                   "proj_kv")
    (u,) = _in_proj_call(_proj_plain_kernel, h, w_in, layer, ATTN_WIDTH + 2 * KV_WIDTH, SGU_WIDTH,
                         [], [], [(SGU_WIDTH, tn, BF16)], "proj_u")
    (gh,) = _in_proj_call(_proj_headnorm_kernel, h, w_in, layer,
                          ATTN_WIDTH + 2 * KV_WIDTH + SGU_WIDTH, SGU_WIDTH,
                          [sgu_norm_g.reshape(DEPTH, 1, SGU_WIDTH)],
                          [pl.BlockSpec((None, 1, tn), lambda n, m: (layer, 0, n))],
                          [(SGU_WIDTH, tn, BF16)], "proj_g")
    return q, kf, kb, vf, vb, u, gh


def _qk(q, k):
    return lax.dot_general(q, k, (((1,), (1,)), ((), ())), preferred_element_type=F32)


def _attn_kernel(*refs, has_cache):
    if has_cache:
        q_ref, k_ref, v_ref, kc_ref, vc_ref, o_ref = refs
        kc = kc_ref[...].astype(BF16)
        vc = vc_ref[...].astype(BF16)
    else:
        q_ref, k_ref, v_ref, o_ref = refs
    k = k_ref[...]
    v = v_ref[...]
    for g in range(Q_PER_KV):
        sl = slice(g * HEAD_DIM, (g + 1) * HEAD_DIM)
        q = q_ref[:, sl]
        s = _qk(q, k)
        m = jnp.max(s, axis=-1, keepdims=True)
        if has_cache:
            sc = _qk(q, kc)
            m = jnp.maximum(m, jnp.max(sc, axis=-1, keepdims=True))
        p = jnp.exp(s - m)
        l = jnp.sum(p, axis=-1, keepdims=True)
        o = jnp.dot(p.astype(BF16), v, preferred_element_type=F32)
        if has_cache:
            pc = jnp.exp(sc - m)
            l = l + jnp.sum(pc, axis=-1, keepdims=True)
            o = o + jnp.dot(pc.astype(BF16), vc, preferred_element_type=F32)
        o_ref[:, sl] = (o / l).astype(BF16)


def _attention(q, kb, vb, cache_k, cache_v, layer, *, batch, seq, row0):
    has_cache = cache_k is not None
    nq = seq // T_Q
    qw = Q_PER_KV * HEAD_DIM
    q_spec = pl.BlockSpec((T_Q, qw), lambda b, kv, i: (row0 // T_Q + b * nq + i, kv))
    kv_spec = pl.BlockSpec((seq, HEAD_DIM), lambda b, kv, i: (row0 // seq + b, kv))
    in_specs = [q_spec, kv_spec, kv_spec]
    args = [q, kb, vb]
    if has_cache:
        c_spec = pl.BlockSpec((None, None, PAST_LEN, HEAD_DIM), lambda b, kv, i: (b, layer, 0, kv))
        in_specs += [c_spec, c_spec]
        args += [cache_k.reshape(DEC_BATCH, DEPTH, PAST_LEN, KV_WIDTH),
                 cache_v.reshape(DEC_BATCH, DEPTH, PAST_LEN, KV_WIDTH)]
    return pl.pallas_call(
        functools.partial(_attn_kernel, has_cache=has_cache),
        grid=(batch, N_KV_HEADS, nq),
        in_specs=in_specs,
        out_specs=pl.BlockSpec((T_Q, qw), lambda b, kv, i: (b * nq + i, kv)),
        out_shape=jax.ShapeDtypeStruct((batch * seq, ATTN_WIDTH), BF16),
        compiler_params=_params(3),
        name="attention_cached" if has_cache else "attention",
    )(*args)


def _sgu_merge_kernel(u_ref, gh_ref, ap_ref, as_ref, ws_ref, bs_ref, gn_ref, o_ref, sgu_ref):
    t = u_ref.shape[0]
    a = _pick(ap_ref, as_ref, 0).astype(F32)
    a = a * lax.rsqrt(jnp.mean(a * a, axis=-1, keepdims=True) + EPS) * gn_ref[:, :ATTN_WIDTH]
    o_ref[:, :ATTN_WIDTH] = a.astype(BF16)
    for h in range(N_SGU_HEADS):
        cs = slice(h * HEAD_DIM, (h + 1) * HEAD_DIM)
        w = ws_ref[h].astype(BF16)
        b = bs_ref[h]
        for c in range(t // CHUNK):
            rs = slice(c * CHUNK, (c + 1) * CHUNK)
            mixed = jnp.dot(w, gh_ref[rs, cs], preferred_element_type=F32) + b
            sgu_ref[rs, cs] = u_ref[rs, cs].astype(F32) * mixed
    s = sgu_ref[...]
    s = s * lax.rsqrt(jnp.mean(s * s, axis=-1, keepdims=True) + EPS) * gn_ref[:, ATTN_WIDTH:]
    o_ref[:, ATTN_WIDTH:] = s.astype(BF16)


def _sgu_merge(u, gh, attn_ctx, attn_lat, w_spatial, b_spatial, out_norm_g, layer):
    t = T_NORM
    bias = jnp.broadcast_to(b_spatial[:, :, :, None], (DEPTH, N_SGU_HEADS, CHUNK, HEAD_DIM))
    row = lambda w: pl.BlockSpec((t, w), lambda i: (i, 0))
    a_args, a_specs = _stream_in((attn_ctx, attn_lat), t, ATTN_WIDTH, lambda i: i, lambda i: 0)
    return pl.pallas_call(
        _sgu_merge_kernel,
        grid=(M // t,),
        in_specs=[row(SGU_WIDTH), row(SGU_WIDTH)] + a_specs + [
            pl.BlockSpec((None, N_SGU_HEADS, CHUNK, CHUNK), lambda i: (layer, 0, 0, 0)),
            pl.BlockSpec((None, N_SGU_HEADS, CHUNK, HEAD_DIM), lambda i: (layer, 0, 0, 0)),
            pl.BlockSpec((None, 1, D), lambda i: (layer, 0, 0))],
        out_specs=row(D),
        out_shape=jax.ShapeDtypeStruct((M, D), BF16),
        scratch_shapes=[pltpu.VMEM((t, SGU_WIDTH), F32)],
        compiler_params=_params(1),
        name="sgu_merge",
    )(u, gh, *a_args, w_spatial, bias, out_norm_g.reshape(DEPTH, 1, D))


def _mm_resid_kernel(a_ref, w_ref, xa_ref, xb_ref, g_ref, o_ref, wbf_ref):
    _cast_weight_once(w_ref, wbf_ref)
    acc = jnp.dot(a_ref[...], wbf_ref[...], preferred_element_type=F32)
    o_ref[...] = _pick(xa_ref, xb_ref, 1) + g_ref[...] * acc


def _mm_resid(a, w, x, mod, layer, w_index, gate_chunk, tm, tn):
    k = a.shape[1]
    row = lambda n, m: m
    col = lambda n, m: n
    x_args, x_specs = _stream_in(x, tm, tn, row, col)
    return pl.pallas_call(
        _mm_resid_kernel,
        grid=(D // tn, M // tm),
        in_specs=[pl.BlockSpec((tm, k), lambda n, m: (m, 0)),
                  pl.BlockSpec((None, k, tn), lambda n, m: (w_index, 0, n))] + x_specs + [
                  _mod_spec(layer, gate_chunk, tm, row, col, tn=tn)],
        out_specs=pl.BlockSpec((tm, tn), lambda n, m: (m, n)),
        out_shape=jax.ShapeDtypeStruct((M, D), F32),
        scratch_shapes=[pltpu.VMEM((k, tn), BF16)],
        compiler_params=_params(2),
        name="mm_resid",
    )(a, w, *x_args, mod)


def _swiglu(a, b):
    return a * jax.nn.sigmoid(a) * b


def _ffn_gu_kernel(x_ref, wg_ref, wu_ref, o_ref, wgb_ref, wub_ref):
    @pl.when(pl.program_id(1) == 0)
    def _():
        _cast_rows(wg_ref, wgb_ref)
        _cast_rows(wu_ref, wub_ref)
    x = x_ref[...]
    a = jnp.dot(x, wgb_ref[...], preferred_element_type=F32)
    b = jnp.dot(x, wub_ref[...], preferred_element_type=F32)
    o_ref[...] = _swiglu(a, b).astype(BF16)


def _ffn_gate_up(h, w_gate, w_up, j):
    tn = 512
    w_spec = pl.BlockSpec((None, D, tn), lambda n, m: (j, 0, n))
    return pl.pallas_call(
        _ffn_gu_kernel,
        grid=(D_FF // tn, M // TM),
        in_specs=[pl.BlockSpec((TM, D), lambda n, m: (m, 0)), w_spec, w_spec],
        out_specs=pl.BlockSpec((TM, tn), lambda n, m: (m, n)),
        out_shape=jax.ShapeDtypeStruct((M, D_FF), BF16),
        scratch_shapes=[pltpu.VMEM((D, tn), BF16), pltpu.VMEM((D, tn), BF16)],
        compiler_params=_params(2),
        name="ffn_gate_up",
    )(h, w_gate, w_up)


def _route_meta(idx):
    t = T_MOE
    experts = jnp.arange(N_EXPERTS, dtype=I32)
    onehot = (idx[:, :, None] == experts[None, None, :]).astype(I32).sum(axis=1)
    csum = jnp.cumsum(onehot, axis=0)
    rank = csum - onehot
    count = csum[-1]
    ntile = (count + t - 1) // t
    tile_end = jnp.cumsum(ntile)
    tile_start = tile_end - ntile
    nused = tile_end[-1]
    pos = (tile_start * t)[idx] + jnp.take_along_axis(rank, idx, axis=1)
    j = jnp.arange(NT_MOE, dtype=I32)
    te_raw = jnp.minimum(jnp.sum(j[:, None] >= tile_end[None, :], axis=1), N_EXPERTS - 1).astype(I32)
    te = jnp.where(j < nused, te_raw, te_raw[nused - 1])
    first = ((j == tile_start[te]) & (j < nused)).astype(I32)
    later = (ntile[None, :] > 0) & (experts[None, :] > te[:, None])
    nxt = jnp.min(jnp.where(later, experts[None, :], N_EXPERTS), axis=1)
    nxt = jnp.where(nxt == N_EXPERTS, -1, nxt).astype(I32)
    pad_start = tile_start * t + count
    pad_len = ntile * t - count
    tail = jnp.stack([nused * t, (NT_MOE - nused) * (t // ZERO_ROWS)])
    zinfo = jnp.concatenate([pad_start, pad_len, tail]).astype(I32)
    return pos.astype(I32), zinfo, (te, first, nxt, nused.reshape(1).astype(I32))


def _dispatch_kernel(p0_ref, p1_ref, z_ref, h_ref, xs_ref, zero_ref, sem, zsem):
    t = h_ref.shape[0]
    i = pl.program_id(0)
    base = i * t

    def clear_padding(start):
        def go(n, off):
            cp = pltpu.make_async_copy(zero_ref.at[pl.ds(0, n), :], xs_ref.at[pl.ds(off, n), :], zsem.at[0])
            cp.start() if start else cp.wait()

        for e in range(N_EXPERTS):
            off, ln = z_ref[e], z_ref[N_EXPERTS + e]
            end = off + ln
            for b in range(SUBLANE_BITS, PAD_BITS):
                @pl.when(((ln >> b) & 1) == 1)
                def _():
                    go(1 << b, pl.multiple_of(end - ((ln >> b) << b), SUBLANES))
            for k in range(SUBLANES - 1):
                @pl.when(k < (ln & (SUBLANES - 1)))
                def _():
                    go(1, off + k)
        tail0, n_tail = z_ref[2 * N_EXPERTS], z_ref[2 * N_EXPERTS + 1]

        def tail_body(k, carry):
            go(ZERO_ROWS, pl.multiple_of(tail0 + k * ZERO_ROWS, SUBLANES))
            return carry

        lax.fori_loop(0, n_tail, tail_body, 0)

    @pl.when(i == 0)
    def _():
        zero_ref[...] = jnp.zeros(zero_ref.shape, zero_ref.dtype)
        clear_padding(True)

    def issue(r, carry):
        src = h_ref.at[pl.ds(r, 1), :]
        pltpu.make_async_copy(src, xs_ref.at[pl.ds(p0_ref[base + r], 1), :], sem.at[0]).start()
        pltpu.make_async_copy(src, xs_ref.at[pl.ds(p1_ref[base + r], 1), :], sem.at[1]).start()
        return carry

    lax.fori_loop(0, t, issue, 0, unroll=8)
    pltpu.make_async_copy(h_ref, xs_ref.at[pl.ds(0, t), :], sem.at[0]).wait()
    pltpu.make_async_copy(h_ref, xs_ref.at[pl.ds(0, t), :], sem.at[1]).wait()

    @pl.when(i == 0)
    def _():
        clear_padding(False)


def _dispatch(h, pos0, pos1, zinfo):
    t = T_DISPATCH
    return pl.pallas_call(
        _dispatch_kernel,
        grid_spec=pltpu.PrefetchScalarGridSpec(
            num_scalar_prefetch=3,
            grid=(M // t,),
            in_specs=[pl.BlockSpec((t, D), lambda i, p0, p1, z: (i, 0))],
            out_specs=pl.BlockSpec(memory_space=pl.ANY),
            scratch_shapes=[pltpu.VMEM((ZERO_ROWS, D), F32),
                            pltpu.SemaphoreType.DMA((2,)), pltpu.SemaphoreType.DMA((1,))]),
        out_shape=jax.ShapeDtypeStruct((P_MOE, D), F32),
        compiler_params=_params(1),
        name="moe_dispatch",
    )(pos0, pos1, zinfo, h)


def _expert_weight_stream(w_refs, stage_refs, bf_refs, sem, te_ref, first_ref, nxt_ref, tn):
    c = pl.program_id(0)
    j = pl.program_id(1)
    nc = pl.num_programs(0)

    def copies(e, cc):
        col = pl.multiple_of(cc * tn, LANES)
        return [pltpu.make_async_copy(w.at[e, :, pl.ds(col, tn)], st, sem.at[k])
                for k, (w, st) in enumerate(zip(w_refs, stage_refs))]

    def start(e, cc):
        for cp in copies(e, cc):
            cp.start()

    @pl.when((c == 0) & (j == 0))
    def _():
        start(te_ref[0], 0)

    @pl.when(first_ref[j] == 1)
    def _():
        for cp in copies(0, 0):
            cp.wait()
        for st, bf in zip(stage_refs, bf_refs):
            _cast_rows(st, bf)
        ne = nxt_ref[j]

        @pl.when(ne >= 0)
        def _():
            start(ne, c)

        @pl.when((ne < 0) & (c + 1 < nc))
        def _():
            start(te_ref[0], c + 1)


def _gmm_gate_up_kernel(te_ref, first_ref, nxt_ref, nused_ref, xs_ref, wg_ref, wu_ref, o_ref,
                        sg_ref, su_ref, wgb_ref, wub_ref, sem):
    _expert_weight_stream((wg_ref, wu_ref), (sg_ref, su_ref), (wgb_ref, wub_ref), sem,
                          te_ref, first_ref, nxt_ref, TN_GU)

    @pl.when(pl.program_id(1) < nused_ref[0])
    def _():
        half = T_MOE // 2
        for r in range(2):
            rs = slice(r * half, (r + 1) * half)
            x = xs_ref[rs, :].astype(BF16)
            a = jnp.dot(x, wgb_ref[...], preferred_element_type=F32)
            b = jnp.dot(x, wub_ref[...], preferred_element_type=F32)
            o_ref[rs, :] = _swiglu(a, b).astype(BF16)

    @pl.when(pl.program_id(1) >= nused_ref[0])
    def _():
        o_ref[...] = jnp.zeros(o_ref.shape, o_ref.dtype)


def _gmm_down_kernel(te_ref, first_ref, nxt_ref, nused_ref, a_ref, wd_ref, o_ref,
                     sd_ref, wdb_ref, sem):
    _expert_weight_stream((wd_ref,), (sd_ref,), (wdb_ref,), sem, te_ref, first_ref, nxt_ref, TN_DN)

    @pl.when(pl.program_id(1) < nused_ref[0])
    def _():
        o_ref[...] = jnp.dot(a_ref[...], wdb_ref[...], preferred_element_type=F32)

    @pl.when(pl.program_id(1) >= nused_ref[0])
    def _():
        o_ref[...] = jnp.zeros(o_ref.shape, o_ref.dtype)


def _used_tile(j, nused):
    return jnp.minimum(j, nused[0] - 1)


def _gmm_gate_up(xs, w_gate, w_up, meta):
    te, first, nxt, nused = meta
    tn = TN_GU
    return pl.pallas_call(
        _gmm_gate_up_kernel,
        grid_spec=pltpu.PrefetchScalarGridSpec(
            num_scalar_prefetch=4,
            grid=(D_FF_EXPERT // tn, NT_MOE),
            in_specs=[pl.BlockSpec((T_MOE, D), lambda c, j, te, fi, nx, nu: (_used_tile(j, nu), 0)),
                      pl.BlockSpec(memory_space=pl.ANY),
                      pl.BlockSpec(memory_space=pl.ANY)],
            out_specs=pl.BlockSpec((T_MOE, tn), lambda c, j, te, fi, nx, nu: (j, c)),
            scratch_shapes=[pltpu.VMEM((D, tn), F32), pltpu.VMEM((D, tn), F32),
                            pltpu.VMEM((D, tn), BF16), pltpu.VMEM((D, tn), BF16),
                            pltpu.SemaphoreType.DMA((2,))]),
        out_shape=jax.ShapeDtypeStruct((P_MOE, D_FF_EXPERT), BF16),
        compiler_params=_params(2),
        name="moe_gate_up",
    )(te, first, nxt, nused, xs, w_gate, w_up)


def _gmm_down(act, w_down, meta):
    te, first, nxt, nused = meta
    tn = TN_DN
    return pl.pallas_call(
        _gmm_down_kernel,
        grid_spec=pltpu.PrefetchScalarGridSpec(
            num_scalar_prefetch=4,
            grid=(D // tn, NT_MOE),
            in_specs=[pl.BlockSpec((T_MOE, D_FF_EXPERT), lambda c, j, te, fi, nx, nu: (_used_tile(j, nu), 0)),
                      pl.BlockSpec(memory_space=pl.ANY)],
            out_specs=pl.BlockSpec((T_MOE, tn), lambda c, j, te, fi, nx, nu: (j, c)),
            scratch_shapes=[pltpu.VMEM((D_FF_EXPERT, tn), F32), pltpu.VMEM((D_FF_EXPERT, tn), BF16),
                            pltpu.SemaphoreType.DMA((1,))]),
        out_shape=jax.ShapeDtypeStruct((P_MOE, D), F32),
        compiler_params=_params(2),
        name="moe_down",
    )(te, first, nxt, nused, act, w_down)


def _combine_kernel(p0_ref, p1_ref, ys_ref, xa_ref, xb_ref, g_ref, w_ref, oc_ref, ol_ref,
                    a_ref, b_ref, sem):
    t = xa_ref.shape[0]
    i = pl.program_id(0)
    n = pl.num_programs(0)

    def issue(step, slot):
        base = step * t

        def body(r, carry):
            pltpu.make_async_copy(ys_ref.at[pl.ds(p0_ref[base + r], 1), :],
                                  a_ref.at[slot, pl.ds(r, 1), :], sem.at[0, slot]).start()
            pltpu.make_async_copy(ys_ref.at[pl.ds(p1_ref[base + r], 1), :],
                                  b_ref.at[slot, pl.ds(r, 1), :], sem.at[1, slot]).start()
            return carry

        lax.fori_loop(0, t, body, 0, unroll=8)

    @pl.when(i == 0)
    def _():
        issue(0, 0)

    @pl.when(i + 1 < n)
    def _():
        issue(i + 1, (i + 1) % 2)

    slot = i % 2
    pltpu.make_async_copy(ys_ref.at[pl.ds(0, t), :], a_ref.at[slot], sem.at[0, slot]).wait()
    pltpu.make_async_copy(ys_ref.at[pl.ds(0, t), :], b_ref.at[slot], sem.at[1, slot]).wait()
    w = w_ref[...]
    moe = w[:, 0:1] * a_ref[slot] + w[:, 1:2] * b_ref[slot]
    y = _pick(xa_ref, xb_ref, 0) + g_ref[...] * moe
    is_ctx = _is_ctx_tile(i, t)

    @pl.when(is_ctx)
    def _():
        oc_ref[...] = y

    @pl.when(jnp.logical_not(is_ctx))
    def _():
        ol_ref[...] = y


def _combine(ys, x, mod, layer, gate_chunk, wts, pos0, pos1):
    t = T_COMBINE
    n_ctx = MP // t
    row = lambda i, *_: i
    x_args, x_specs = _stream_in(x, t, D, row, lambda i, *_: 0)
    return pl.pallas_call(
        _combine_kernel,
        grid_spec=pltpu.PrefetchScalarGridSpec(
            num_scalar_prefetch=2,
            grid=(M // t,),
            in_specs=[pl.BlockSpec(memory_space=pl.ANY)] + x_specs + [
                _mod_spec(layer, gate_chunk, t, row),
                pl.BlockSpec((t, LANES), lambda i, p0, p1: (i, 0))],
            out_specs=[pl.BlockSpec((t, D), lambda i, p0, p1: (jnp.minimum(i, n_ctx - 1), 0)),
                       pl.BlockSpec((t, D), lambda i, p0, p1: (jnp.maximum(i - n_ctx, 0), 0))],
            scratch_shapes=[pltpu.VMEM((2, t, D), F32), pltpu.VMEM((2, t, D), F32),
                            pltpu.SemaphoreType.DMA((2, 2))]),
        out_shape=[jax.ShapeDtypeStruct((MP, D), F32), jax.ShapeDtypeStruct((MS, D), F32)],
        compiler_params=_params(1),
        name="moe_combine",
    )(pos0, pos1, ys, *x_args, mod, wts)


def _moe(x, mod, layer, norm2_g, w_router, b_router, w_gate, w_up, w_down):
    h, idx, wts = _prenorm_router(x, norm2_g, mod, layer, 4, 3, w_router, b_router)
    pos, zinfo, meta = _route_meta(idx[:, :TOP_K])
    pos0, pos1 = pos[:, 0], pos[:, 1]
    xs = _dispatch(h, pos0, pos1, zinfo)
    act = _gmm_gate_up(xs, w_gate, w_up, meta)
    ys = _gmm_down(act, w_down, meta)
    return _combine(ys, x, mod, layer, 5, wts, pos0, pos1)


def kernel(x_prompt, x_sample, cache_k, cache_v, c, c_ctx, w_ada, b_ada, norm1_g, norm2_g, w_in, q_norm_g, k_norm_g, sgu_norm_g, w_spatial, b_spatial, out_norm_g, w_out, ffn_w_gate, ffn_w_up, ffn_w_down, w_router, b_router, moe_w_gate, moe_w_up, moe_w_down):
    assert DEPTH == 2
    x = (x_prompt.reshape(MP, D), x_sample.reshape(MS, D))
    cond = jnp.concatenate([c_ctx[None, :], c, jnp.zeros((N_COND - 1 - DEC_BATCH, D), F32)], axis=0)
    mod = _modulation(cond, w_ada, b_ada).reshape(DEPTH, N_COND, 1, N_MOD * D)
    cos, sin = _rope_tables()

    new_k, new_v = [], []
    for i in range(DEPTH):
        h = _prenorm(x, norm1_g, mod, i, 1, 0)
        q, kf, kb, vf, vb, u, gh = _in_projections(h, w_in, q_norm_g, k_norm_g, sgu_norm_g, cos, sin, i)
        attn_ctx = _attention(q, kb, vb, None, None, i, batch=BATCH, seq=SEQ, row0=0)
        attn_lat = _attention(q, kb, vb, cache_k, cache_v, i, batch=DEC_BATCH, seq=DEC_SEQ, row0=MP)
        o = _sgu_merge(u, gh, attn_ctx, attn_lat, w_spatial, b_spatial, out_norm_g, i)
        x = _mm_resid(o, w_out, x, mod, i, i, 2, TM, 1024)
        j = i // 2
        if i % 2 == 0:
            h2 = _prenorm(x, norm2_g, mod, i, 4, 3)
            act = _ffn_gate_up(h2, ffn_w_gate, ffn_w_up, j)
            x = _mm_resid(act, ffn_w_down, x, mod, i, j, 5, 512, 512)
        else:
            x = _moe(x, mod, i, norm2_g, w_router[j], b_router[j],
                     moe_w_gate[j], moe_w_up[j], moe_w_down[j])
        new_k.append(kf[:MP].reshape(BATCH, SEQ, N_KV_HEADS, HEAD_DIM))
        new_v.append(vf[:MP].reshape(BATCH, SEQ, N_KV_HEADS, HEAD_DIM))

    y_prompt = x[0].reshape(BATCH, SEQ, D)
    y_sample = x[1].reshape(DEC_BATCH, DEC_SEQ, D)
    return (y_prompt, y_sample, jnp.stack(new_k, axis=1), jnp.stack(new_v, axis=1))
```

```python
import functools

import jax
import jax.numpy as jnp
from jax import lax
from jax.experimental import pallas as pl
from jax.experimental.pallas import tpu as pltpu

F32 = jnp.float32
BF16 = jnp.bfloat16
I32 = jnp.int32

D = 2048
BATCH, SEQ = 16, 256
DEC_BATCH, DEC_SEQ = 4, 2048
PAST_LEN = 256
DEPTH = 2
GRID_W = 64
CHUNK = 128
HEAD_DIM = 128
N_Q_HEADS, N_KV_HEADS = 8, 2
Q_PER_KV = N_Q_HEADS // N_KV_HEADS
ATTN_WIDTH = N_Q_HEADS * HEAD_DIM
KV_WIDTH = N_KV_HEADS * HEAD_DIM
N_SGU_HEADS = 8
SGU_WIDTH = N_SGU_HEADS * HEAD_DIM
IN_WIDTH = ATTN_WIDTH + 2 * KV_WIDTH + 2 * SGU_WIDTH
ROPE_THETA = 10000.0
ROPE_AXIS_DIM = HEAD_DIM // 2
D_FF = 5632
N_EXPERTS = 8
TOP_K = 2
D_FF_EXPERT = 2816
N_MOD = 6
EPS = 1e-6
ATTN_SCALE = HEAD_DIM ** -0.5
LOG2_E = 1.4426950408889634

MP = BATCH * SEQ
MS = DEC_BATCH * DEC_SEQ
M = MP + MS
N_COND = 8
LANES = 128
SUBLANES = 8
SUBLANE_BITS = 3

VMEM_LIMIT = 56 * 1024 * 1024

TM = 1024
TN_IN = 2 * KV_WIDTH
T_NORM = 512
T_Q = 512
T_MOE = 512
P_MOE = M * TOP_K + N_EXPERTS * T_MOE
NT_MOE = P_MOE // T_MOE
TN_GU = D_FF_EXPERT // 2
TN_DN = D // 2
T_DISPATCH = 1024
T_COMBINE = 256
ZERO_ROWS = T_MOE // 2
PAD_BITS = ZERO_ROWS.bit_length()


def _params(n_axes):
    return pltpu.CompilerParams(dimension_semantics=("arbitrary",) * n_axes,
                                vmem_limit_bytes=VMEM_LIMIT)


def _cond_row(i, t):
    return jnp.where(i < MP // t, 0, 1 + (i - MP // t) // (DEC_SEQ // t))


def _is_ctx_tile(i, t):
    return i < MP // t


def _stream_in(x, t, width, row_of, col_of):
    n_ctx = MP // t
    pair = isinstance(x, tuple)
    base = 0 if pair else n_ctx
    ctx = pl.BlockSpec((t, width), lambda *g: (jnp.minimum(row_of(*g), n_ctx - 1), col_of(*g)))
    lat = pl.BlockSpec((t, width), lambda *g: (base + jnp.maximum(row_of(*g) - n_ctx, 0), col_of(*g)))
    return (list(x) if pair else [x, x]), [ctx, lat]


def _mod_spec(layer, chunk, t, row_of, col_of=None, tn=D):
    per = D // tn

    def index_map(*g):
        col = chunk * per + (col_of(*g) if col_of is not None else 0)
        return (layer, _cond_row(row_of(*g), t), 0, col)

    return pl.BlockSpec((None, None, 1, tn), index_map)


def _ada_kernel(c_ref, w_ref, b_ref, o_ref):
    c = c_ref[...]
    s = (c * jax.nn.sigmoid(c)).astype(BF16)
    o_ref[...] = jnp.dot(s, w_ref[...].astype(BF16), preferred_element_type=F32) + b_ref[...]


def _modulation(cond, w_ada, b_ada):
    tn = 1024
    width = N_MOD * D
    return pl.pallas_call(
        _ada_kernel,
        grid=(DEPTH, width // tn),
        in_specs=[pl.BlockSpec((N_COND, D), lambda l, n: (0, 0)),
                  pl.BlockSpec((None, D, tn), lambda l, n: (l, 0, n)),
                  pl.BlockSpec((None, 1, tn), lambda l, n: (l, 0, n))],
        out_specs=pl.BlockSpec((None, N_COND, tn), lambda l, n: (l, 0, n)),
        out_shape=jax.ShapeDtypeStruct((DEPTH, N_COND, width), F32),
        compiler_params=_params(2),
        name="modulation",
    )(cond, w_ada, b_ada.reshape(DEPTH, 1, width))


def _modulated_norm(x, g, sc, sh):
    y = x * lax.rsqrt(jnp.mean(x * x, axis=-1, keepdims=True) + EPS)
    return (y * g) * (1.0 + sc) + sh


def _pick(xa_ref, xb_ref, axis):
    t = xa_ref.shape[0]
    return jnp.where(_is_ctx_tile(pl.program_id(axis), t), xa_ref[...], xb_ref[...])


def _prenorm_kernel(xa_ref, xb_ref, g_ref, sc_ref, sh_ref, o_ref):
    x = _pick(xa_ref, xb_ref, 0)
    o_ref[...] = _modulated_norm(x, g_ref[...], sc_ref[...], sh_ref[...]).astype(BF16)


def _prenorm(x, gain, mod, layer, sc_chunk, sh_chunk):
    t = T_NORM
    row = lambda i: i
    x_args, x_specs = _stream_in(x, t, D, row, lambda i: 0)
    return pl.pallas_call(
        _prenorm_kernel,
        grid=(M // t,),
        in_specs=x_specs + [pl.BlockSpec((None, 1, D), lambda i: (layer, 0, 0)),
                            _mod_spec(layer, sc_chunk, t, row),
                            _mod_spec(layer, sh_chunk, t, row)],
        out_specs=pl.BlockSpec((t, D), lambda i: (i, 0)),
        out_shape=jax.ShapeDtypeStruct((M, D), BF16),
        compiler_params=_params(1),
        name="prenorm",
    )(*x_args, gain.reshape(DEPTH, 1, D), mod, mod)


def _split_bf16(a):
    hi = a.astype(BF16)
    return hi, (a - hi.astype(F32)).astype(BF16)


def _prenorm_router_kernel(xa_ref, xb_ref, g_ref, sc_ref, sh_ref, wr_ref, br_ref, h_ref, idx_ref, wt_ref):
    h = _modulated_norm(_pick(xa_ref, xb_ref, 0), g_ref[...], sc_ref[...], sh_ref[...])
    h_ref[...] = h
    h_hi, h_lo = _split_bf16(h)
    w_hi, w_lo = _split_bf16(wr_ref[...])
    logits = (jnp.dot(h_hi, w_hi, preferred_element_type=F32)
              + jnp.dot(h_lo, w_hi, preferred_element_type=F32)
              + jnp.dot(h_hi, w_lo, preferred_element_type=F32)) + br_ref[...]
    lane = lax.broadcasted_iota(I32, logits.shape, 1)
    neg = jnp.float32(-jnp.inf)
    lg = jnp.where(lane < N_EXPERTS, logits, neg)
    m1 = jnp.max(lg, axis=-1, keepdims=True)
    i1 = jnp.min(jnp.where(lg == m1, lane, LANES), axis=-1, keepdims=True)
    lg2 = jnp.where(lane == i1, neg, lg)
    m2 = jnp.max(lg2, axis=-1, keepdims=True)
    i2 = jnp.min(jnp.where(lg2 == m2, lane, LANES), axis=-1, keepdims=True)
    e = jnp.exp(m2 - m1)
    w1 = 1.0 / (1.0 + e)
    w2 = e / (1.0 + e)
    idx_ref[...] = jnp.where(lane == 0, i1, jnp.where(lane == 1, i2, 0))
    wt_ref[...] = jnp.where(lane == 0, w1, jnp.where(lane == 1, w2, 0.0))


def _prenorm_router(x, gain, mod, layer, sc_chunk, sh_chunk, w_router, b_router):
    t = T_NORM
    row = lambda i: i
    x_args, x_specs = _stream_in(x, t, D, row, lambda i: 0)
    wr = jnp.zeros((D, LANES), F32).at[:, :N_EXPERTS].set(w_router)
    br = jnp.zeros((1, LANES), F32).at[0, :N_EXPERTS].set(b_router)
    return pl.pallas_call(
        _prenorm_router_kernel,
        grid=(M // t,),
        in_specs=x_specs + [pl.BlockSpec((None, 1, D), lambda i: (layer, 0, 0)),
                            _mod_spec(layer, sc_chunk, t, row),
                            _mod_spec(layer, sh_chunk, t, row),
                            pl.BlockSpec((D, LANES), lambda i: (0, 0)),
                            pl.BlockSpec((1, LANES), lambda i: (0, 0))],
        out_specs=[pl.BlockSpec((t, D), lambda i: (i, 0)),
                   pl.BlockSpec((t, LANES), lambda i: (i, 0)),
                   pl.BlockSpec((t, LANES), lambda i: (i, 0))],
        out_shape=[jax.ShapeDtypeStruct((M, D), F32),
                   jax.ShapeDtypeStruct((M, LANES), I32),
                   jax.ShapeDtypeStruct((M, LANES), F32)],
        compiler_params=_params(1),
        name="prenorm_router",
    )(*x_args, gain.reshape(DEPTH, 1, D), mod, mod, wr, br)


CAST_ROWS = 256


def _cast_rows(src_ref, dst_ref):
    def body(r, carry):
        rs = pl.ds(pl.multiple_of(r * CAST_ROWS, CAST_ROWS), CAST_ROWS)
        dst_ref[rs, :] = src_ref[rs, :].astype(BF16)
        return carry

    lax.fori_loop(0, src_ref.shape[0] // CAST_ROWS, body, 0)


def _cast_weight_once(w_ref, wbf_ref):
    @pl.when(pl.program_id(1) == 0)
    def _():
        _cast_rows(w_ref, wbf_ref)


def _head_rms(a):
    return lax.rsqrt(jnp.mean(a * a, axis=-1, keepdims=True) + EPS)


def _rope_partner(ag, perm):
    hi, lo = _split_bf16(ag)
    return (jnp.dot(hi, perm, preferred_element_type=F32)
            + jnp.dot(lo, perm, preferred_element_type=F32))


def _proj_q_kernel(x_ref, w_ref, g_ref, perm_ref, cos_ref, sin_ref, o_ref, wbf_ref):
    _cast_weight_once(w_ref, wbf_ref)
    acc = jnp.dot(x_ref[...], wbf_ref[...], preferred_element_type=F32)
    ag = acc * g_ref[...]
    partner = _rope_partner(ag, perm_ref[...])
    cos, sin = cos_ref[...], sin_ref[...]
    for h in range(acc.shape[1] // HEAD_DIM):
        sl = slice(h * HEAD_DIM, (h + 1) * HEAD_DIM)
        r = _head_rms(acc[:, sl]) * (ATTN_SCALE * LOG2_E)
        o_ref[:, sl] = ((ag[:, sl] * cos + partner[:, sl] * sin) * r).astype(BF16)


def _proj_kv_kernel(x_ref, w_ref, g_ref, perm_ref, cos_ref, sin_ref,
                    kf_ref, kb_ref, vf_ref, vb_ref, wbf_ref):
    _cast_weight_once(w_ref, wbf_ref)
    acc = jnp.dot(x_ref[...], wbf_ref[...], preferred_element_type=F32)
    k = acc[:, :KV_WIDTH]
    ag = k * g_ref[...]
    partner = _rope_partner(ag, perm_ref[...])
    cos, sin = cos_ref[...], sin_ref[...]
    for h in range(N_KV_HEADS):
        sl = slice(h * HEAD_DIM, (h + 1) * HEAD_DIM)
        r = _head_rms(k[:, sl])
        kf_ref[:, sl] = ag[:, sl] * r
        kb_ref[:, sl] = ((ag[:, sl] * cos + partner[:, sl] * sin) * r).astype(BF16)
    v = acc[:, KV_WIDTH:]
    vf_ref[...] = v
    vb_ref[...] = v.astype(BF16)


def _proj_plain_kernel(x_ref, w_ref, o_ref, wbf_ref):
    _cast_weight_once(w_ref, wbf_ref)
    o_ref[...] = jnp.dot(x_ref[...], wbf_ref[...], preferred_element_type=F32).astype(BF16)


def _proj_headnorm_kernel(x_ref, w_ref, g_ref, o_ref, wbf_ref):
    _cast_weight_once(w_ref, wbf_ref)
    acc = jnp.dot(x_ref[...], wbf_ref[...], preferred_element_type=F32)
    for h in range(acc.shape[1] // HEAD_DIM):
        sl = slice(h * HEAD_DIM, (h + 1) * HEAD_DIM)
        a = acc[:, sl]
        o_ref[:, sl] = (a * _head_rms(a) * g_ref[:, sl]).astype(BF16)


def _rope_tables():
    n_rows = DEC_SEQ // GRID_W
    rows = jnp.broadcast_to(jnp.arange(n_rows)[:, None], (n_rows, GRID_W)).reshape(-1)
    cols = jnp.broadcast_to(jnp.arange(GRID_W)[None, :], (n_rows, GRID_W)).reshape(-1)
    inv = ROPE_THETA ** (-jnp.arange(0, ROPE_AXIS_DIM, 2, dtype=F32) / ROPE_AXIS_DIM)
    ang_r = rows.astype(F32)[:, None] * inv
    ang_c = cols.astype(F32)[:, None] * inv
    cos = jnp.concatenate([jnp.cos(ang_r), jnp.cos(ang_r), jnp.cos(ang_c), jnp.cos(ang_c)], axis=1)
    sin = jnp.concatenate([-jnp.sin(ang_r), jnp.sin(ang_r), -jnp.sin(ang_c), jnp.sin(ang_c)], axis=1)
    cos = jnp.concatenate([jnp.ones((TM, HEAD_DIM), F32), cos], axis=0)
    sin = jnp.concatenate([jnp.zeros((TM, HEAD_DIM), F32), sin], axis=0)
    return cos, sin


def _partner_matrix(n_heads):
    w = n_heads * HEAD_DIM
    quarter = ROPE_AXIS_DIM // 2
    j = jnp.arange(w)
    partner = jnp.where((j % ROPE_AXIS_DIM) < quarter, j + quarter, j - quarter)
    return (jnp.arange(w)[:, None] == partner[None, :]).astype(BF16)


def _rope_block(m):
    return jnp.where(m < MP // TM, 0, 1 + (m - MP // TM) % (DEC_SEQ // TM))


def _in_proj_call(kernel, h, w_in, layer, col0, width, extra_in, extra_specs, outs, name):
    tn = TN_IN
    assert col0 % tn == 0 and width % tn == 0
    x_spec = pl.BlockSpec((TM, D), lambda n, m: (m, 0))
    w_spec = pl.BlockSpec((None, D, tn), lambda n, m: (layer, 0, col0 // tn + n))
    out_specs = [pl.BlockSpec((TM, bw), lambda n, m: (m, n)) for (_, bw, _) in outs]
    out_shape = [jax.ShapeDtypeStruct((M, w), dt) for (w, _, dt) in outs]
    return pl.pallas_call(
        kernel,
        grid=(width // tn, M // TM),
        in_specs=[x_spec, w_spec] + extra_specs,
        out_specs=out_specs,
        out_shape=out_shape,
        scratch_shapes=[pltpu.VMEM((D, tn), BF16)],
        compiler_params=_params(2),
        name=name,
    )(h, w_in, *extra_in)


def _in_projections(h, w_in, q_norm_g, k_norm_g, sgu_norm_g, cos, sin, layer):
    tn = TN_IN
    rope_spec = pl.BlockSpec((TM, HEAD_DIM), lambda n, m: (_rope_block(m), 0))
    const = lambda shape: pl.BlockSpec(shape, lambda n, m: (0,) * len(shape))
    q_heads = tn // HEAD_DIM
    q_gain = jnp.tile(q_norm_g[layer], q_heads)[None, :]
    k_gain = jnp.tile(k_norm_g[layer], N_KV_HEADS)[None, :]
    (q,) = _in_proj_call(_proj_q_kernel, h, w_in, layer, 0, ATTN_WIDTH,
                         [q_gain, _partner_matrix(q_heads), cos, sin],
                         [const((1, tn)), const((tn, tn)), rope_spec, rope_spec],
                         [(ATTN_WIDTH, tn, BF16)], "proj_q")
    kf, kb, vf, vb = _in_proj_call(_proj_kv_kernel, h, w_in, layer, ATTN_WIDTH, 2 * KV_WIDTH,
                                   [k_gain, _partner_matrix(N_KV_HEADS), cos, sin],
                                   [const((1, KV_WIDTH)), const((KV_WIDTH, KV_WIDTH)), rope_spec, rope_spec],
                                   [(KV_WIDTH, KV_WIDTH, F32), (KV_WIDTH, KV_WIDTH, BF16),
                                    (KV_WIDTH, KV_WIDTH, F32), (KV_WIDTH, KV_WIDTH, BF16)],
                                   "proj_kv")
    (u,) = _in_proj_call(_proj_plain_kernel, h, w_in, layer, ATTN_WIDTH + 2 * KV_WIDTH, SGU_WIDTH,
                         [], [], [(SGU_WIDTH, tn, BF16)], "proj_u")
    (gh,) = _in_proj_call(_proj_headnorm_kernel, h, w_in, layer,
                          ATTN_WIDTH + 2 * KV_WIDTH + SGU_WIDTH, SGU_WIDTH,
                          [sgu_norm_g.reshape(DEPTH, 1, SGU_WIDTH)],
                          [pl.BlockSpec((None, 1, tn), lambda n, m: (layer, 0, n))],
                          [(SGU_WIDTH, tn, BF16)], "proj_g")
    return q, kf, kb, vf, vb, u, gh


def _qk(q, k):
    return lax.dot_general(q, k, (((1,), (1,)), ((), ())), preferred_element_type=F32)


def _attn_kernel(*refs, has_cache):
    def with_ones(v):
        return jnp.concatenate([v, jnp.ones_like(v)], axis=1)

    if has_cache:
        q_ref, k_ref, v_ref, kc_ref, vc_ref, o_ref = refs
        kc = kc_ref[...].astype(BF16)
        vc = with_ones(vc_ref[...].astype(BF16))
    else:
        q_ref, k_ref, v_ref, o_ref = refs
    k = k_ref[...]
    v = with_ones(v_ref[...])
    for g in range(Q_PER_KV):
        sl = slice(g * HEAD_DIM, (g + 1) * HEAD_DIM)
        q = q_ref[:, sl]
        s = _qk(q, k)
        m = jnp.max(s, axis=-1, keepdims=True)
        if has_cache:
            sc = _qk(q, kc)
            m = jnp.maximum(m, jnp.max(sc, axis=-1, keepdims=True))
        o = jnp.dot(jnp.exp2(s - m).astype(BF16), v, preferred_element_type=F32)
        if has_cache:
            o = o + jnp.dot(jnp.exp2(sc - m).astype(BF16), vc, preferred_element_type=F32)
        o_ref[:, sl] = (o[:, :HEAD_DIM] / o[:, HEAD_DIM:]).astype(BF16)


def _attention(q, kb, vb, cache_k, cache_v, layer, *, batch, seq, row0):
    has_cache = cache_k is not None
    tq = min(T_Q, seq)
    nq = seq // tq
    qw = Q_PER_KV * HEAD_DIM
    q_spec = pl.BlockSpec((tq, qw), lambda b, kv, i: (row0 // tq + b * nq + i, kv))
    kv_spec = pl.BlockSpec((seq, HEAD_DIM), lambda b, kv, i: (row0 // seq + b, kv))
    in_specs = [q_spec, kv_spec, kv_spec]
    args = [q, kb, vb]
    if has_cache:
        c_spec = pl.BlockSpec((None, None, PAST_LEN, HEAD_DIM), lambda b, kv, i: (b, layer, 0, kv))
        in_specs += [c_spec, c_spec]
        args += [cache_k.reshape(DEC_BATCH, DEPTH, PAST_LEN, KV_WIDTH),
                 cache_v.reshape(DEC_BATCH, DEPTH, PAST_LEN, KV_WIDTH)]
    return pl.pallas_call(
        functools.partial(_attn_kernel, has_cache=has_cache),
        grid=(batch, N_KV_HEADS, nq),
        in_specs=in_specs,
        out_specs=pl.BlockSpec((tq, qw), lambda b, kv, i: (b * nq + i, kv)),
        out_shape=jax.ShapeDtypeStruct((batch * seq, ATTN_WIDTH), BF16),
        compiler_params=_params(3),
        name="attention_cached" if has_cache else "attention",
    )(*args)


def _sgu_merge_kernel(u_ref, gh_ref, ap_ref, as_ref, ws_ref, bs_ref, gn_ref, o_ref, sgu_ref):
    t = u_ref.shape[0]
    a = _pick(ap_ref, as_ref, 0).astype(F32)
    a = a * lax.rsqrt(jnp.mean(a * a, axis=-1, keepdims=True) + EPS) * gn_ref[:, :ATTN_WIDTH]
    o_ref[:, :ATTN_WIDTH] = a.astype(BF16)
    for h in range(N_SGU_HEADS):
        cs = slice(h * HEAD_DIM, (h + 1) * HEAD_DIM)
        w = ws_ref[h].astype(BF16)
        b = bs_ref[h]
        for c in range(t // CHUNK):
            rs = slice(c * CHUNK, (c + 1) * CHUNK)
            mixed = jnp.dot(w, gh_ref[rs, cs], preferred_element_type=F32) + b
            sgu_ref[rs, cs] = u_ref[rs, cs].astype(F32) * mixed
    s = sgu_ref[...]
    s = s * lax.rsqrt(jnp.mean(s * s, axis=-1, keepdims=True) + EPS) * gn_ref[:, ATTN_WIDTH:]
    o_ref[:, ATTN_WIDTH:] = s.astype(BF16)


def _sgu_merge(u, gh, attn_ctx, attn_lat, w_spatial, b_spatial, out_norm_g, layer):
    t = T_NORM
    bias = jnp.broadcast_to(b_spatial[:, :, :, None], (DEPTH, N_SGU_HEADS, CHUNK, HEAD_DIM))
    row = lambda w: pl.BlockSpec((t, w), lambda i: (i, 0))
    a_args, a_specs = _stream_in((attn_ctx, attn_lat), t, ATTN_WIDTH, lambda i: i, lambda i: 0)
    return pl.pallas_call(
        _sgu_merge_kernel,
        grid=(M // t,),
        in_specs=[row(SGU_WIDTH), row(SGU_WIDTH)] + a_specs + [
            pl.BlockSpec((None, N_SGU_HEADS, CHUNK, CHUNK), lambda i: (layer, 0, 0, 0)),
            pl.BlockSpec((None, N_SGU_HEADS, CHUNK, HEAD_DIM), lambda i: (layer, 0, 0, 0)),
            pl.BlockSpec((None, 1, D), lambda i: (layer, 0, 0))],
        out_specs=row(D),
        out_shape=jax.ShapeDtypeStruct((M, D), BF16),
        scratch_shapes=[pltpu.VMEM((t, SGU_WIDTH), F32)],
        compiler_params=_params(1),
        name="sgu_merge",
    )(u, gh, *a_args, w_spatial, bias, out_norm_g.reshape(DEPTH, 1, D))


def _mm_resid_kernel(a_ref, w_ref, xa_ref, xb_ref, g_ref, o_ref, wbf_ref):
    _cast_weight_once(w_ref, wbf_ref)
    acc = jnp.dot(a_ref[...], wbf_ref[...], preferred_element_type=F32)
    o_ref[...] = _pick(xa_ref, xb_ref, 1) + g_ref[...] * acc


def _mm_resid(a, w, x, mod, layer, w_index, gate_chunk, tm, tn):
    k = a.shape[1]
    row = lambda n, m: m
    col = lambda n, m: n
    x_args, x_specs = _stream_in(x, tm, tn, row, col)
    return pl.pallas_call(
        _mm_resid_kernel,
        grid=(D // tn, M // tm),
        in_specs=[pl.BlockSpec((tm, k), lambda n, m: (m, 0)),
                  pl.BlockSpec((None, k, tn), lambda n, m: (w_index, 0, n))] + x_specs + [
                  _mod_spec(layer, gate_chunk, tm, row, col, tn=tn)],
        out_specs=pl.BlockSpec((tm, tn), lambda n, m: (m, n)),
        out_shape=jax.ShapeDtypeStruct((M, D), F32),
        scratch_shapes=[pltpu.VMEM((k, tn), BF16)],
        compiler_params=_params(2),
        name="mm_resid",
    )(a, w, *x_args, mod)


def _swiglu(a, b):
    return a * jax.nn.sigmoid(a) * b


def _ffn_gu_kernel(x_ref, wg_ref, wu_ref, o_ref, wgb_ref, wub_ref):
    @pl.when(pl.program_id(1) == 0)
    def _():
        _cast_rows(wg_ref, wgb_ref)
        _cast_rows(wu_ref, wub_ref)
    x = x_ref[...]
    a = jnp.dot(x, wgb_ref[...], preferred_element_type=F32)
    b = jnp.dot(x, wub_ref[...], preferred_element_type=F32)
    o_ref[...] = _swiglu(a, b).astype(BF16)


def _ffn_gate_up(h, w_gate, w_up, j):
    tm, tn = 2 * TM, 512
    w_spec = pl.BlockSpec((None, D, tn), lambda n, m: (j, 0, n))
    return pl.pallas_call(
        _ffn_gu_kernel,
        grid=(D_FF // tn, M // tm),
        in_specs=[pl.BlockSpec((tm, D), lambda n, m: (m, 0)), w_spec, w_spec],
        out_specs=pl.BlockSpec((tm, tn), lambda n, m: (m, n)),
        out_shape=jax.ShapeDtypeStruct((M, D_FF), BF16),
        scratch_shapes=[pltpu.VMEM((D, tn), BF16), pltpu.VMEM((D, tn), BF16)],
        compiler_params=_params(2),
        name="ffn_gate_up",
    )(h, w_gate, w_up)


def _route_meta(idx):
    t = T_MOE
    experts = jnp.arange(N_EXPERTS, dtype=I32)
    onehot = (idx[:, :, None] == experts[None, None, :]).astype(I32).sum(axis=1)
    csum = jnp.cumsum(onehot, axis=0)
    rank = csum - onehot
    count = csum[-1]
    ntile = (count + t - 1) // t
    tile_end = jnp.cumsum(ntile)
    tile_start = tile_end - ntile
    nused = tile_end[-1]
    pos = (tile_start * t)[idx] + jnp.take_along_axis(rank, idx, axis=1)
    j = jnp.arange(NT_MOE, dtype=I32)
    te_raw = jnp.minimum(jnp.sum(j[:, None] >= tile_end[None, :], axis=1), N_EXPERTS - 1).astype(I32)
    te = jnp.where(j < nused, te_raw, te_raw[nused - 1])
    first = ((j == tile_start[te]) & (j < nused)).astype(I32)
    later = (ntile[None, :] > 0) & (experts[None, :] > te[:, None])
    nxt = jnp.min(jnp.where(later, experts[None, :], N_EXPERTS), axis=1)
    nxt = jnp.where(nxt == N_EXPERTS, -1, nxt).astype(I32)
    pad_start = tile_start * t + count
    pad_len = ntile * t - count
    tail = jnp.stack([nused * t, (NT_MOE - nused) * (t // ZERO_ROWS)])
    zinfo = jnp.concatenate([pad_start, pad_len, tail]).astype(I32)
    return pos.astype(I32), zinfo, (te, first, nxt, nused.reshape(1).astype(I32))


def _dispatch_kernel(p0_ref, p1_ref, z_ref, h_ref, xs_ref, zero_ref, sem, zsem):
    t = h_ref.shape[0]
    i = pl.program_id(0)
    base = i * t

    def clear_padding(start):
        def go(n, off):
            cp = pltpu.make_async_copy(zero_ref.at[pl.ds(0, n), :], xs_ref.at[pl.ds(off, n), :], zsem.at[0])
            cp.start() if start else cp.wait()

        for e in range(N_EXPERTS):
            off, ln = z_ref[e], z_ref[N_EXPERTS + e]
            end = off + ln
            for b in range(SUBLANE_BITS, PAD_BITS):
                @pl.when(((ln >> b) & 1) == 1)
                def _():
                    go(1 << b, pl.multiple_of(end - ((ln >> b) << b), SUBLANES))
            for k in range(SUBLANES - 1):
                @pl.when(k < (ln & (SUBLANES - 1)))
                def _():
                    go(1, off + k)
        tail0, n_tail = z_ref[2 * N_EXPERTS], z_ref[2 * N_EXPERTS + 1]

        def tail_body(k, carry):
            go(ZERO_ROWS, pl.multiple_of(tail0 + k * ZERO_ROWS, SUBLANES))
            return carry

        lax.fori_loop(0, n_tail, tail_body, 0)

    @pl.when(i == 0)
    def _():
        zero_ref[...] = jnp.zeros(zero_ref.shape, zero_ref.dtype)
        clear_padding(True)

    def issue(r, carry):
        src = h_ref.at[pl.ds(r, 1), :]
        pltpu.make_async_copy(src, xs_ref.at[pl.ds(p0_ref[base + r], 1), :], sem.at[0]).start()
        pltpu.make_async_copy(src, xs_ref.at[pl.ds(p1_ref[base + r], 1), :], sem.at[1]).start()
        return carry

    lax.fori_loop(0, t, issue, 0, unroll=8)
    pltpu.make_async_copy(h_ref, xs_ref.at[pl.ds(0, t), :], sem.at[0]).wait()
    pltpu.make_async_copy(h_ref, xs_ref.at[pl.ds(0, t), :], sem.at[1]).wait()

    @pl.when(i == 0)
    def _():
        clear_padding(False)


def _dispatch(h, pos0, pos1, zinfo):
    t = T_DISPATCH
    return pl.pallas_call(
        _dispatch_kernel,
        grid_spec=pltpu.PrefetchScalarGridSpec(
            num_scalar_prefetch=3,
            grid=(M // t,),
            in_specs=[pl.BlockSpec((t, D), lambda i, p0, p1, z: (i, 0))],
            out_specs=pl.BlockSpec(memory_space=pl.ANY),
            scratch_shapes=[pltpu.VMEM((ZERO_ROWS, D), F32),
                            pltpu.SemaphoreType.DMA((2,)), pltpu.SemaphoreType.DMA((1,))]),
        out_shape=jax.ShapeDtypeStruct((P_MOE, D), F32),
        compiler_params=_params(1),
        name="moe_dispatch",
    )(pos0, pos1, zinfo, h)


def _expert_weight_stream(w_refs, stage_refs, bf_refs, sem, te_ref, first_ref, nxt_ref, tn):
    c = pl.program_id(0)
    j = pl.program_id(1)
    nc = pl.num_programs(0)

    def copies(e, cc):
        col = pl.multiple_of(cc * tn, LANES)
        return [pltpu.make_async_copy(w.at[e, :, pl.ds(col, tn)], st, sem.at[k])
                for k, (w, st) in enumerate(zip(w_refs, stage_refs))]

    def start(e, cc):
        for cp in copies(e, cc):
            cp.start()

    @pl.when((c == 0) & (j == 0))
    def _():
        start(te_ref[0], 0)

    @pl.when(first_ref[j] == 1)
    def _():
        for cp in copies(0, 0):
            cp.wait()
        for st, bf in zip(stage_refs, bf_refs):
            _cast_rows(st, bf)
        ne = nxt_ref[j]

        @pl.when(ne >= 0)
        def _():
            start(ne, c)

        @pl.when((ne < 0) & (c + 1 < nc))
        def _():
            start(te_ref[0], c + 1)


def _gmm_gate_up_kernel(te_ref, first_ref, nxt_ref, nused_ref, xs_ref, wg_ref, wu_ref, o_ref,
                        sg_ref, su_ref, wgb_ref, wub_ref, sem):
    _expert_weight_stream((wg_ref, wu_ref), (sg_ref, su_ref), (wgb_ref, wub_ref), sem,
                          te_ref, first_ref, nxt_ref, TN_GU)

    @pl.when(pl.program_id(1) < nused_ref[0])
    def _():
        half = T_MOE // 2
        for r in range(2):
            rs = slice(r * half, (r + 1) * half)
            x = xs_ref[rs, :].astype(BF16)
            a = jnp.dot(x, wgb_ref[...], preferred_element_type=F32)
            b = jnp.dot(x, wub_ref[...], preferred_element_type=F32)
            o_ref[rs, :] = _swiglu(a, b).astype(BF16)

    @pl.when(pl.program_id(1) >= nused_ref[0])
    def _():
        o_ref[...] = jnp.zeros(o_ref.shape, o_ref.dtype)


def _gmm_down_kernel(te_ref, first_ref, nxt_ref, nused_ref, a_ref, wd_ref, o_ref,
                     sd_ref, wdb_ref, sem):
    _expert_weight_stream((wd_ref,), (sd_ref,), (wdb_ref,), sem, te_ref, first_ref, nxt_ref, TN_DN)

    @pl.when(pl.program_id(1) < nused_ref[0])
    def _():
        o_ref[...] = jnp.dot(a_ref[...], wdb_ref[...], preferred_element_type=F32)

    @pl.when(pl.program_id(1) >= nused_ref[0])
    def _():
        o_ref[...] = jnp.zeros(o_ref.shape, o_ref.dtype)


def _used_tile(j, nused):
    return jnp.minimum(j, nused[0] - 1)


def _gmm_gate_up(xs, w_gate, w_up, meta):
    te, first, nxt, nused = meta
    tn = TN_GU
    return pl.pallas_call(
        _gmm_gate_up_kernel,
        grid_spec=pltpu.PrefetchScalarGridSpec(
            num_scalar_prefetch=4,
            grid=(D_FF_EXPERT // tn, NT_MOE),
            in_specs=[pl.BlockSpec((T_MOE, D), lambda c, j, te, fi, nx, nu: (_used_tile(j, nu), 0)),
                      pl.BlockSpec(memory_space=pl.ANY),
                      pl.BlockSpec(memory_space=pl.ANY)],
            out_specs=pl.BlockSpec((T_MOE, tn), lambda c, j, te, fi, nx, nu: (j, c)),
            scratch_shapes=[pltpu.VMEM((D, tn), F32), pltpu.VMEM((D, tn), F32),
                            pltpu.VMEM((D, tn), BF16), pltpu.VMEM((D, tn), BF16),
                            pltpu.SemaphoreType.DMA((2,))]),
        out_shape=jax.ShapeDtypeStruct((P_MOE, D_FF_EXPERT), BF16),
        compiler_params=_params(2),
        name="moe_gate_up",
    )(te, first, nxt, nused, xs, w_gate, w_up)


def _gmm_down(act, w_down, meta):
    te, first, nxt, nused = meta
    tn = TN_DN
    return pl.pallas_call(
        _gmm_down_kernel,
        grid_spec=pltpu.PrefetchScalarGridSpec(
            num_scalar_prefetch=4,
            grid=(D // tn, NT_MOE),
            in_specs=[pl.BlockSpec((T_MOE, D_FF_EXPERT), lambda c, j, te, fi, nx, nu: (_used_tile(j, nu), 0)),
                      pl.BlockSpec(memory_space=pl.ANY)],
            out_specs=pl.BlockSpec((T_MOE, tn), lambda c, j, te, fi, nx, nu: (j, c)),
            scratch_shapes=[pltpu.VMEM((D_FF_EXPERT, tn), F32), pltpu.VMEM((D_FF_EXPERT, tn), BF16),
                            pltpu.SemaphoreType.DMA((1,))]),
        out_shape=jax.ShapeDtypeStruct((P_MOE, D), F32),
        compiler_params=_params(2),
        name="moe_down",
    )(te, first, nxt, nused, act, w_down)


def _combine_kernel(p0_ref, p1_ref, ys_ref, xa_ref, xb_ref, g_ref, w_ref, oc_ref, ol_ref,
                    a_ref, b_ref, sem):
    t = xa_ref.shape[0]
    i = pl.program_id(0)
    n = pl.num_programs(0)

    def issue(step, slot):
        base = step * t

        def body(r, carry):
            pltpu.make_async_copy(ys_ref.at[pl.ds(p0_ref[base + r], 1), :],
                                  a_ref.at[slot, pl.ds(r, 1), :], sem.at[0, slot]).start()
            pltpu.make_async_copy(ys_ref.at[pl.ds(p1_ref[base + r], 1), :],
                                  b_ref.at[slot, pl.ds(r, 1), :], sem.at[1, slot]).start()
            return carry

        lax.fori_loop(0, t, body, 0, unroll=8)

    @pl.when(i == 0)
    def _():
        issue(0, 0)

    @pl.when(i + 1 < n)
    def _():
        issue(i + 1, (i + 1) % 2)

    slot = i % 2
    pltpu.make_async_copy(ys_ref.at[pl.ds(0, t), :], a_ref.at[slot], sem.at[0, slot]).wait()
    pltpu.make_async_copy(ys_ref.at[pl.ds(0, t), :], b_ref.at[slot], sem.at[1, slot]).wait()
    w = w_ref[...]
    moe = w[:, 0:1] * a_ref[slot] + w[:, 1:2] * b_ref[slot]
    y = _pick(xa_ref, xb_ref, 0) + g_ref[...] * moe
    is_ctx = _is_ctx_tile(i, t)

    @pl.when(is_ctx)
    def _():
        oc_ref[...] = y

    @pl.when(jnp.logical_not(is_ctx))
    def _():
        ol_ref[...] = y


def _combine(ys, x, mod, layer, gate_chunk, wts, pos0, pos1):
    t = T_COMBINE
    n_ctx = MP // t
    row = lambda i, *_: i
    x_args, x_specs = _stream_in(x, t, D, row, lambda i, *_: 0)
    return pl.pallas_call(
        _combine_kernel,
        grid_spec=pltpu.PrefetchScalarGridSpec(
            num_scalar_prefetch=2,
            grid=(M // t,),
            in_specs=[pl.BlockSpec(memory_space=pl.ANY)] + x_specs + [
                _mod_spec(layer, gate_chunk, t, row),
                pl.BlockSpec((t, LANES), lambda i, p0, p1: (i, 0))],
            out_specs=[pl.BlockSpec((t, D), lambda i, p0, p1: (jnp.minimum(i, n_ctx - 1), 0)),
                       pl.BlockSpec((t, D), lambda i, p0, p1: (jnp.maximum(i - n_ctx, 0), 0))],
            scratch_shapes=[pltpu.VMEM((2, t, D), F32), pltpu.VMEM((2, t, D), F32),
                            pltpu.SemaphoreType.DMA((2, 2))]),
        out_shape=[jax.ShapeDtypeStruct((MP, D), F32), jax.ShapeDtypeStruct((MS, D), F32)],
        compiler_params=_params(1),
        name="moe_combine",
    )(pos0, pos1, ys, *x_args, mod, wts)


def _moe(x, mod, layer, norm2_g, w_router, b_router, w_gate, w_up, w_down):
    h, idx, wts = _prenorm_router(x, norm2_g, mod, layer, 4, 3, w_router, b_router)
    pos, zinfo, meta = _route_meta(idx[:, :TOP_K])
    pos0, pos1 = pos[:, 0], pos[:, 1]
    xs = _dispatch(h, pos0, pos1, zinfo)
    act = _gmm_gate_up(xs, w_gate, w_up, meta)
    ys = _gmm_down(act, w_down, meta)
    return _combine(ys, x, mod, layer, 5, wts, pos0, pos1)


def kernel(x_prompt, x_sample, cache_k, cache_v, c, c_ctx, w_ada, b_ada, norm1_g, norm2_g, w_in, q_norm_g, k_norm_g, sgu_norm_g, w_spatial, b_spatial, out_norm_g, w_out, ffn_w_gate, ffn_w_up, ffn_w_down, w_router, b_router, moe_w_gate, moe_w_up, moe_w_down):
    assert DEPTH == 2
    x = (x_prompt.reshape(MP, D), x_sample.reshape(MS, D))
    cond = jnp.concatenate([c_ctx[None, :], c, jnp.zeros((N_COND - 1 - DEC_BATCH, D), F32)], axis=0)
    mod = _modulation(cond, w_ada, b_ada).reshape(DEPTH, N_COND, 1, N_MOD * D)
    cos, sin = _rope_tables()

    new_k, new_v = [], []
    for i in range(DEPTH):
        h = _prenorm(x, norm1_g, mod, i, 1, 0)
        q, kf, kb, vf, vb, u, gh = _in_projections(h, w_in, q_norm_g, k_norm_g, sgu_norm_g, cos, sin, i)
        attn_ctx = _attention(q, kb, vb, None, None, i, batch=BATCH, seq=SEQ, row0=0)
        attn_lat = _attention(q, kb, vb, cache_k, cache_v, i, batch=DEC_BATCH, seq=DEC_SEQ, row0=MP)
        o = _sgu_merge(u, gh, attn_ctx, attn_lat, w_spatial, b_spatial, out_norm_g, i)
        x = _mm_resid(o, w_out, x, mod, i, i, 2, TM, 1024)
        j = i // 2
        if i % 2 == 0:
            h2 = _prenorm(x, norm2_g, mod, i, 4, 3)
            act = _ffn_gate_up(h2, ffn_w_gate, ffn_w_up, j)
            x = _mm_resid(act, ffn_w_down, x, mod, i, j, 5, 512, 512)
        else:
            x = _moe(x, mod, i, norm2_g, w_router[j], b_router[j],
                     moe_w_gate[j], moe_w_up[j], moe_w_down[j])
        new_k.append(kf[:MP].reshape(BATCH, SEQ, N_KV_HEADS, HEAD_DIM))
        new_v.append(vf[:MP].reshape(BATCH, SEQ, N_KV_HEADS, HEAD_DIM))

    y_prompt = x[0].reshape(BATCH, SEQ, D)
    y_sample = x[1].reshape(DEC_BATCH, DEC_SEQ, D)
    return (y_prompt, y_sample, jnp.stack(new_k, axis=1), jnp.stack(new_v, axis=1))
```

```python
import functools

import jax
import jax.numpy as jnp
from jax import lax
from jax.experimental import pallas as pl
from jax.experimental.pallas import tpu as pltpu

F32 = jnp.float32
BF16 = jnp.bfloat16
I32 = jnp.int32

D = 2048
BATCH, SEQ = 16, 256
DEC_BATCH, DEC_SEQ = 4, 2048
PAST_LEN = 256
DEPTH = 2
GRID_W = 64
CHUNK = 128
HEAD_DIM = 128
N_Q_HEADS, N_KV_HEADS = 8, 2
Q_PER_KV = N_Q_HEADS // N_KV_HEADS
ATTN_WIDTH = N_Q_HEADS * HEAD_DIM
KV_WIDTH = N_KV_HEADS * HEAD_DIM
N_SGU_HEADS = 8
SGU_WIDTH = N_SGU_HEADS * HEAD_DIM
IN_WIDTH = ATTN_WIDTH + 2 * KV_WIDTH + 2 * SGU_WIDTH
ROPE_THETA = 10000.0
ROPE_AXIS_DIM = HEAD_DIM // 2
D_FF = 5632
N_EXPERTS = 8
TOP_K = 2
D_FF_EXPERT = 2816
N_MOD = 6
EPS = 1e-6
ATTN_SCALE = HEAD_DIM ** -0.5
LOG2_E = 1.4426950408889634

MP = BATCH * SEQ
MS = DEC_BATCH * DEC_SEQ
M = MP + MS
N_COND = 8
LANES = 128
SUBLANES = 8
SUBLANE_BITS = 3

VMEM_LIMIT = 56 * 1024 * 1024

TM = 1024
TN_IN = 2 * KV_WIDTH
T_NORM = 512
T_Q = 512
T_MOE = 512
P_MOE = M * TOP_K + N_EXPERTS * T_MOE
NT_MOE = P_MOE // T_MOE
TN_GU = D_FF_EXPERT // 2
TN_DN = D // 2
T_DISPATCH = 1024
T_COMBINE = 256
ZERO_ROWS = T_MOE // 2
PAD_BITS = ZERO_ROWS.bit_length()


def _params(n_axes):
    return pltpu.CompilerParams(dimension_semantics=("arbitrary",) * n_axes,
                                vmem_limit_bytes=VMEM_LIMIT)


def _cond_row(i, t):
    return jnp.where(i < MP // t, 0, 1 + (i - MP // t) // (DEC_SEQ // t))


def _is_ctx_tile(i, t):
    return i < MP // t


def _stream_in(x, t, width, row_of, col_of):
    n_ctx = MP // t
    pair = isinstance(x, tuple)
    base = 0 if pair else n_ctx
    ctx = pl.BlockSpec((t, width), lambda *g: (jnp.minimum(row_of(*g), n_ctx - 1), col_of(*g)))
    lat = pl.BlockSpec((t, width), lambda *g: (base + jnp.maximum(row_of(*g) - n_ctx, 0), col_of(*g)))
    return (list(x) if pair else [x, x]), [ctx, lat]


def _mod_spec(layer, chunk, t, row_of, col_of=None, tn=D):
    per = D // tn

    def index_map(*g):
        col = chunk * per + (col_of(*g) if col_of is not None else 0)
        return (layer, _cond_row(row_of(*g), t), 0, col)

    return pl.BlockSpec((None, None, 1, tn), index_map)


def _ada_kernel(c_ref, w_ref, b_ref, o_ref):
    c = c_ref[...]
    s = (c * jax.nn.sigmoid(c)).astype(BF16)
    o_ref[...] = jnp.dot(s, w_ref[...].astype(BF16), preferred_element_type=F32) + b_ref[...]


def _modulation(cond, w_ada, b_ada):
    tn = 1024
    width = N_MOD * D
    return pl.pallas_call(
        _ada_kernel,
        grid=(DEPTH, width // tn),
        in_specs=[pl.BlockSpec((N_COND, D), lambda l, n: (0, 0)),
                  pl.BlockSpec((None, D, tn), lambda l, n: (l, 0, n)),
                  pl.BlockSpec((None, 1, tn), lambda l, n: (l, 0, n))],
        out_specs=pl.BlockSpec((None, N_COND, tn), lambda l, n: (l, 0, n)),
        out_shape=jax.ShapeDtypeStruct((DEPTH, N_COND, width), F32),
        compiler_params=_params(2),
        name="modulation",
    )(cond, w_ada, b_ada.reshape(DEPTH, 1, width))


def _modulated_norm(x, g, sc, sh):
    y = x * lax.rsqrt(jnp.mean(x * x, axis=-1, keepdims=True) + EPS)
    return (y * g) * (1.0 + sc) + sh


def _pick(xa_ref, xb_ref, axis):
    t = xa_ref.shape[0]
    return jnp.where(_is_ctx_tile(pl.program_id(axis), t), xa_ref[...], xb_ref[...])


def _stream_tile_copy(xa_ref, xb_ref, lat_row0, buf_ref, sem, tile, slot, start):
    t = buf_ref.shape[1]

    def copy(src_ref, row):
        return pltpu.make_async_copy(src_ref.at[pl.ds(pl.multiple_of(row, t), t), :],
                                     buf_ref.at[slot], sem.at[slot])

    if not start:
        copy(xa_ref, 0).wait()
        return
    is_ctx = _is_ctx_tile(tile, t)

    @pl.when(is_ctx)
    def _():
        copy(xa_ref, tile * t).start()

    @pl.when(jnp.logical_not(is_ctx))
    def _():
        copy(xb_ref, lat_row0 + (tile - MP // t) * t).start()


def _next_stream_tile(xa_ref, xb_ref, lat_row0, buf_ref, sem, tile, n_tiles):
    @pl.when(tile == 0)
    def _():
        _stream_tile_copy(xa_ref, xb_ref, lat_row0, buf_ref, sem, 0, 0, True)

    @pl.when(tile + 1 < n_tiles)
    def _():
        _stream_tile_copy(xa_ref, xb_ref, lat_row0, buf_ref, sem, tile + 1, (tile + 1) % 2, True)

    slot = tile % 2
    _stream_tile_copy(xa_ref, xb_ref, lat_row0, buf_ref, sem, tile, slot, False)
    return slot


def _stream_hbm(x):
    return (x[0], x[1], 0) if isinstance(x, tuple) else (x, x, MP)


def _split_bf16(a):
    hi = a.astype(BF16)
    return hi, (a - hi.astype(F32)).astype(BF16)


def _route_top2(h, wr, br, idx_ref, wt_ref):
    h_hi, h_lo = _split_bf16(h)
    w_hi, w_lo = _split_bf16(wr)
    logits = (jnp.dot(h_hi, w_hi, preferred_element_type=F32)
              + jnp.dot(h_lo, w_hi, preferred_element_type=F32)
              + jnp.dot(h_hi, w_lo, preferred_element_type=F32)) + br
    lane = lax.broadcasted_iota(I32, logits.shape, 1)
    neg = jnp.float32(-jnp.inf)
    lg = jnp.where(lane < N_EXPERTS, logits, neg)
    m1 = jnp.max(lg, axis=-1, keepdims=True)
    i1 = jnp.min(jnp.where(lg == m1, lane, LANES), axis=-1, keepdims=True)
    lg2 = jnp.where(lane == i1, neg, lg)
    m2 = jnp.max(lg2, axis=-1, keepdims=True)
    i2 = jnp.min(jnp.where(lg2 == m2, lane, LANES), axis=-1, keepdims=True)
    e = jnp.exp(m2 - m1)
    w1 = 1.0 / (1.0 + e)
    w2 = e / (1.0 + e)
    idx_ref[...] = jnp.where(lane == 0, i1, jnp.where(lane == 1, i2, 0))
    wt_ref[...] = jnp.where(lane == 0, w1, jnp.where(lane == 1, w2, 0.0))


W_PIECE = 512


def _out_proj_kernel(*refs, layer, lat_row0, route):
    if route:
        (o_ref, w_ref, xa_ref, xb_ref, gate_ref, g_ref, sc_ref, sh_ref, wr_ref, br_ref,
         xn_ref, h_ref, idx_ref, wt_ref, stage_ref, wbf_ref, xbuf_ref, wsem, xsem) = refs
    else:
        (o_ref, w_ref, xa_ref, xb_ref, gate_ref, g_ref, sc_ref, sh_ref,
         xn_ref, h_ref, stage_ref, wbf_ref, xbuf_ref, wsem, xsem) = refs
    i = pl.program_id(0)

    @pl.when(i == 0)
    def _():
        for p in range(D // W_PIECE):
            cols = pl.ds(p * W_PIECE, W_PIECE)
            cp = pltpu.make_async_copy(w_ref.at[layer, :, cols], stage_ref, wsem.at[0])
            cp.start()
            cp.wait()
            _cast_rows(stage_ref, wbf_ref.at[:, cols])

    slot = _next_stream_tile(xa_ref, xb_ref, lat_row0, xbuf_ref, xsem, i, pl.num_programs(0))
    acc = jnp.dot(o_ref[...], wbf_ref[...], preferred_element_type=F32)
    x_new = xbuf_ref[slot] + gate_ref[...] * acc
    xn_ref[...] = x_new
    h = _modulated_norm(x_new, g_ref[...], sc_ref[...], sh_ref[...])
    if route:
        h_ref[...] = h
        _route_top2(h, wr_ref[...], br_ref[...], idx_ref, wt_ref)
    else:
        h_ref[...] = h.astype(BF16)


def _out_proj(o, w_out, x, mod, norm2_g, layer, router=None):
    t = T_NORM
    row = lambda i: i
    xa, xb, lat_row0 = _stream_hbm(x)
    route = router is not None
    anyspace = pl.BlockSpec(memory_space=pl.ANY)
    rows = lambda w: pl.BlockSpec((t, w), lambda i: (i, 0))
    in_specs = [rows(D), anyspace, anyspace, anyspace,
                _mod_spec(layer, 2, t, row),
                pl.BlockSpec((None, 1, D), lambda i: (layer, 0, 0)),
                _mod_spec(layer, 4, t, row), _mod_spec(layer, 3, t, row)]
    args = [o, w_out, xa, xb, mod, norm2_g.reshape(DEPTH, 1, D), mod, mod]
    out_specs = [rows(D), rows(D)]
    out_shape = [jax.ShapeDtypeStruct((M, D), F32), jax.ShapeDtypeStruct((M, D), F32 if route else BF16)]
    if route:
        w_router, b_router = router
        args += [jnp.zeros((D, LANES), F32).at[:, :N_EXPERTS].set(w_router),
                 jnp.zeros((1, LANES), F32).at[0, :N_EXPERTS].set(b_router)]
        in_specs += [pl.BlockSpec((D, LANES), lambda i: (0, 0)), pl.BlockSpec((1, LANES), lambda i: (0, 0))]
        out_specs += [rows(LANES), rows(LANES)]
        out_shape += [jax.ShapeDtypeStruct((M, LANES), I32), jax.ShapeDtypeStruct((M, LANES), F32)]
    return pl.pallas_call(
        functools.partial(_out_proj_kernel, layer=layer, lat_row0=lat_row0, route=route),
        grid=(M // t,),
        in_specs=in_specs,
        out_specs=out_specs,
        out_shape=out_shape,
        scratch_shapes=[pltpu.VMEM((D, W_PIECE), F32), pltpu.VMEM((D, D), BF16),
                        pltpu.VMEM((2, t, D), F32),
                        pltpu.SemaphoreType.DMA((1,)), pltpu.SemaphoreType.DMA((2,))],
        compiler_params=_params(1),
        name="out_proj_router" if route else "out_proj",
    )(*args)


CAST_ROWS = 256


def _cast_rows(src_ref, dst_ref):
    def body(r, carry):
        rs = pl.ds(pl.multiple_of(r * CAST_ROWS, CAST_ROWS), CAST_ROWS)
        dst_ref[rs, :] = src_ref[rs, :].astype(BF16)
        return carry

    lax.fori_loop(0, src_ref.shape[0] // CAST_ROWS, body, 0)


def _cast_weight_once(w_ref, wbf_ref):
    @pl.when(pl.program_id(1) == 0)
    def _():
        _cast_rows(w_ref, wbf_ref)


def _head_rms(a):
    return lax.rsqrt(jnp.mean(a * a, axis=-1, keepdims=True) + EPS)


def _rope_partner(ag, perm):
    hi, lo = _split_bf16(ag)
    return (jnp.dot(hi, perm, preferred_element_type=F32)
            + jnp.dot(lo, perm, preferred_element_type=F32))


def _cast_kernel(w_ref, o_ref):
    o_ref[...] = w_ref[...].astype(BF16)


def _cast_in_weights(w_in):
    spec = pl.BlockSpec((None, D, TN_IN), lambda l, n: (l, 0, n))
    return pl.pallas_call(
        _cast_kernel,
        grid=(DEPTH, IN_WIDTH // TN_IN),
        in_specs=[spec],
        out_specs=spec,
        out_shape=jax.ShapeDtypeStruct(w_in.shape, BF16),
        compiler_params=_params(2),
        name="cast_w_in",
    )(w_in)


N_Q_TILES = ATTN_WIDTH // TN_IN
KV_TILE = N_Q_TILES
U_TILE0 = KV_TILE + 1
G_TILE0 = U_TILE0 + SGU_WIDTH // TN_IN
N_IN_TILES = IN_WIDTH // TN_IN


def _in_proj_kernel(xa_ref, xb_ref, n1_ref, sc_ref, sh_ref, w_ref, qg_ref, kg_ref, sg_ref,
                    pq_ref, pk_ref, cos_ref, sin_ref,
                    q_ref, kf_ref, kb_ref, vf_ref, vb_ref, u_ref, gh_ref,
                    xbuf_ref, h_ref, xsem, *, lat_row0):
    m = pl.program_id(0)
    n = pl.program_id(1)

    @pl.when(n == 0)
    def _():
        slot = _next_stream_tile(xa_ref, xb_ref, lat_row0, xbuf_ref, xsem, m, pl.num_programs(0))
        h = _modulated_norm(xbuf_ref[slot], n1_ref[...], sc_ref[...], sh_ref[...])
        h_ref[...] = h.astype(BF16)

    acc = jnp.dot(h_ref[...], w_ref[...], preferred_element_type=F32)

    @pl.when(n < N_Q_TILES)
    def _():
        ag = acc * qg_ref[...]
        partner = _rope_partner(ag, pq_ref[...])
        cos, sin = cos_ref[...], sin_ref[...]
        for h in range(TN_IN // HEAD_DIM):
            sl = slice(h * HEAD_DIM, (h + 1) * HEAD_DIM)
            r = _head_rms(acc[:, sl]) * (ATTN_SCALE * LOG2_E)
            q_ref[:, sl] = ((ag[:, sl] * cos + partner[:, sl] * sin) * r).astype(BF16)

    @pl.when(n == KV_TILE)
    def _():
        k = acc[:, :KV_WIDTH]
        ag = k * kg_ref[...]
        partner = _rope_partner(ag, pk_ref[...])
        cos, sin = cos_ref[...], sin_ref[...]
        for h in range(N_KV_HEADS):
            sl = slice(h * HEAD_DIM, (h + 1) * HEAD_DIM)
            r = _head_rms(k[:, sl])
            kf_ref[:, sl] = ag[:, sl] * r
            kb_ref[:, sl] = ((ag[:, sl] * cos + partner[:, sl] * sin) * r).astype(BF16)
        v = acc[:, KV_WIDTH:]
        vf_ref[...] = v
        vb_ref[...] = v.astype(BF16)

    @pl.when((n >= U_TILE0) & (n < G_TILE0))
    def _():
        u_ref[...] = acc.astype(BF16)

    @pl.when(n >= G_TILE0)
    def _():
        for h in range(TN_IN // HEAD_DIM):
            sl = slice(h * HEAD_DIM, (h + 1) * HEAD_DIM)
            a = acc[:, sl]
            gh_ref[:, sl] = (a * _head_rms(a) * sg_ref[:, sl]).astype(BF16)


def _rope_tables():
    n_rows = DEC_SEQ // GRID_W
    rows = jnp.broadcast_to(jnp.arange(n_rows)[:, None], (n_rows, GRID_W)).reshape(-1)
    cols = jnp.broadcast_to(jnp.arange(GRID_W)[None, :], (n_rows, GRID_W)).reshape(-1)
    inv = ROPE_THETA ** (-jnp.arange(0, ROPE_AXIS_DIM, 2, dtype=F32) / ROPE_AXIS_DIM)
    ang_r = rows.astype(F32)[:, None] * inv
    ang_c = cols.astype(F32)[:, None] * inv
    cos = jnp.concatenate([jnp.cos(ang_r), jnp.cos(ang_r), jnp.cos(ang_c), jnp.cos(ang_c)], axis=1)
    sin = jnp.concatenate([-jnp.sin(ang_r), jnp.sin(ang_r), -jnp.sin(ang_c), jnp.sin(ang_c)], axis=1)
    cos = jnp.concatenate([jnp.ones((TM, HEAD_DIM), F32), cos], axis=0)
    sin = jnp.concatenate([jnp.zeros((TM, HEAD_DIM), F32), sin], axis=0)
    return cos, sin


def _partner_matrix(n_heads):
    w = n_heads * HEAD_DIM
    quarter = ROPE_AXIS_DIM // 2
    j = jnp.arange(w)
    partner = jnp.where((j % ROPE_AXIS_DIM) < quarter, j + quarter, j - quarter)
    return (jnp.arange(w)[:, None] == partner[None, :]).astype(BF16)


def _rope_block(m):
    return jnp.where(m < MP // TM, 0, 1 + (m - MP // TM) % (DEC_SEQ // TM))


def _in_projections(x, w_in_bf, mod, norm1_g, q_norm_g, k_norm_g, sgu_norm_g, cos, sin, layer):
    tn = TN_IN
    xa, xb, lat_row0 = _stream_hbm(x)
    row = lambda m, n: m
    anyspace = pl.BlockSpec(memory_space=pl.ANY)
    const = lambda shape: pl.BlockSpec(shape, lambda m, n: (0,) * len(shape))
    rope_spec = pl.BlockSpec((TM, HEAD_DIM), lambda m, n: (_rope_block(m), 0))
    q_heads = tn // HEAD_DIM
    q_gain = jnp.tile(q_norm_g[layer], q_heads)[None, :]
    k_gain = jnp.tile(k_norm_g[layer], N_KV_HEADS)[None, :]
    g_tile = lambda n: jnp.clip(n - G_TILE0, 0, SGU_WIDTH // tn - 1)
    kv_out = pl.BlockSpec((TM, KV_WIDTH), lambda m, n: (m, 0))
    kv_shape = lambda dt: jax.ShapeDtypeStruct((M, KV_WIDTH), dt)
    return pl.pallas_call(
        functools.partial(_in_proj_kernel, lat_row0=lat_row0),
        grid=(M // TM, N_IN_TILES),
        in_specs=[anyspace, anyspace,
                  pl.BlockSpec((None, 1, D), lambda m, n: (layer, 0, 0)),
                  _mod_spec(layer, 1, TM, row), _mod_spec(layer, 0, TM, row),
                  pl.BlockSpec((None, D, tn), lambda m, n: (layer, 0, n)),
                  const((1, tn)), const((1, KV_WIDTH)),
                  pl.BlockSpec((None, 1, tn), lambda m, n: (layer, 0, g_tile(n))),
                  const((tn, tn)), const((KV_WIDTH, KV_WIDTH)), rope_spec, rope_spec],
        out_specs=[pl.BlockSpec((TM, tn), lambda m, n: (m, jnp.minimum(n, N_Q_TILES - 1))),
                   kv_out, kv_out, kv_out, kv_out,
                   pl.BlockSpec((TM, tn), lambda m, n: (m, jnp.clip(n - U_TILE0, 0, SGU_WIDTH // tn - 1))),
                   pl.BlockSpec((TM, tn), lambda m, n: (m, g_tile(n)))],
        out_shape=[jax.ShapeDtypeStruct((M, ATTN_WIDTH), BF16),
                   kv_shape(F32), kv_shape(BF16), kv_shape(F32), kv_shape(BF16),
                   jax.ShapeDtypeStruct((M, SGU_WIDTH), BF16),
                   jax.ShapeDtypeStruct((M, SGU_WIDTH), BF16)],
        scratch_shapes=[pltpu.VMEM((2, TM, D), F32), pltpu.VMEM((TM, D), BF16),
                        pltpu.SemaphoreType.DMA((2,))],
        compiler_params=_params(2),
        name="in_proj",
    )(xa, xb, norm1_g.reshape(DEPTH, 1, D), mod, mod, w_in_bf, q_gain, k_gain,
      sgu_norm_g.reshape(DEPTH, 1, SGU_WIDTH), _partner_matrix(q_heads), _partner_matrix(N_KV_HEADS),
      cos, sin)


def _qk(q, k):
    return lax.dot_general(q, k, (((1,), (1,)), ((), ())), preferred_element_type=F32)


def _attn_kernel(*refs, has_cache):
    def with_ones(v):
        return jnp.concatenate([v, jnp.ones_like(v)], axis=1)

    if has_cache:
        q_ref, k_ref, v_ref, kc_ref, vc_ref, o_ref = refs
        kc = kc_ref[...].astype(BF16)
        vc = with_ones(vc_ref[...].astype(BF16))
    else:
        q_ref, k_ref, v_ref, o_ref = refs
    k = k_ref[...]
    v = with_ones(v_ref[...])
    for g in range(Q_PER_KV):
        sl = slice(g * HEAD_DIM, (g + 1) * HEAD_DIM)
        q = q_ref[:, sl]
        s = _qk(q, k)
        m = jnp.max(s, axis=-1, keepdims=True)
        if has_cache:
            sc = _qk(q, kc)
            m = jnp.maximum(m, jnp.max(sc, axis=-1, keepdims=True))
        o = jnp.dot(jnp.exp2(s - m).astype(BF16), v, preferred_element_type=F32)
        if has_cache:
            o = o + jnp.dot(jnp.exp2(sc - m).astype(BF16), vc, preferred_element_type=F32)
        o_ref[:, sl] = (o[:, :HEAD_DIM] / o[:, HEAD_DIM:]).astype(BF16)


def _attention(q, kb, vb, cache_k, cache_v, layer, *, batch, seq, row0):
    has_cache = cache_k is not None
    tq = min(T_Q, seq)
    nq = seq // tq
    qw = Q_PER_KV * HEAD_DIM
    q_spec = pl.BlockSpec((tq, qw), lambda b, kv, i: (row0 // tq + b * nq + i, kv))
    kv_spec = pl.BlockSpec((seq, HEAD_DIM), lambda b, kv, i: (row0 // seq + b, kv))
    in_specs = [q_spec, kv_spec, kv_spec]
    args = [q, kb, vb]
    if has_cache:
        c_spec = pl.BlockSpec((None, None, PAST_LEN, HEAD_DIM), lambda b, kv, i: (b, layer, 0, kv))
        in_specs += [c_spec, c_spec]
        args += [cache_k.reshape(DEC_BATCH, DEPTH, PAST_LEN, KV_WIDTH),
                 cache_v.reshape(DEC_BATCH, DEPTH, PAST_LEN, KV_WIDTH)]
    return pl.pallas_call(
        functools.partial(_attn_kernel, has_cache=has_cache),
        grid=(batch, N_KV_HEADS, nq),
        in_specs=in_specs,
        out_specs=pl.BlockSpec((tq, qw), lambda b, kv, i: (b * nq + i, kv)),
        out_shape=jax.ShapeDtypeStruct((batch * seq, ATTN_WIDTH), BF16),
        compiler_params=_params(3),
        name="attention_cached" if has_cache else "attention",
    )(*args)


def _sgu_merge_kernel(u_ref, gh_ref, ap_ref, as_ref, ws_ref, bs_ref, gn_ref, o_ref, sgu_ref):
    t = u_ref.shape[0]
    a = _pick(ap_ref, as_ref, 0).astype(F32)
    a = a * lax.rsqrt(jnp.mean(a * a, axis=-1, keepdims=True) + EPS) * gn_ref[:, :ATTN_WIDTH]
    o_ref[:, :ATTN_WIDTH] = a.astype(BF16)
    for h in range(N_SGU_HEADS):
        cs = slice(h * HEAD_DIM, (h + 1) * HEAD_DIM)
        w = ws_ref[h].astype(BF16)
        b = bs_ref[h]
        for c in range(t // CHUNK):
            rs = slice(c * CHUNK, (c + 1) * CHUNK)
            mixed = jnp.dot(w, gh_ref[rs, cs], preferred_element_type=F32) + b
            sgu_ref[rs, cs] = u_ref[rs, cs].astype(F32) * mixed
    s = sgu_ref[...]
    s = s * lax.rsqrt(jnp.mean(s * s, axis=-1, keepdims=True) + EPS) * gn_ref[:, ATTN_WIDTH:]
    o_ref[:, ATTN_WIDTH:] = s.astype(BF16)


def _sgu_merge(u, gh, attn_ctx, attn_lat, w_spatial, b_spatial, out_norm_g, layer):
    t = T_NORM
    bias = jnp.broadcast_to(b_spatial[:, :, :, None], (DEPTH, N_SGU_HEADS, CHUNK, HEAD_DIM))
    row = lambda w: pl.BlockSpec((t, w), lambda i: (i, 0))
    a_args, a_specs = _stream_in((attn_ctx, attn_lat), t, ATTN_WIDTH, lambda i: i, lambda i: 0)
    return pl.pallas_call(
        _sgu_merge_kernel,
        grid=(M // t,),
        in_specs=[row(SGU_WIDTH), row(SGU_WIDTH)] + a_specs + [
            pl.BlockSpec((None, N_SGU_HEADS, CHUNK, CHUNK), lambda i: (layer, 0, 0, 0)),
            pl.BlockSpec((None, N_SGU_HEADS, CHUNK, HEAD_DIM), lambda i: (layer, 0, 0, 0)),
            pl.BlockSpec((None, 1, D), lambda i: (layer, 0, 0))],
        out_specs=row(D),
        out_shape=jax.ShapeDtypeStruct((M, D), BF16),
        scratch_shapes=[pltpu.VMEM((t, SGU_WIDTH), F32)],
        compiler_params=_params(1),
        name="sgu_merge",
    )(u, gh, *a_args, w_spatial, bias, out_norm_g.reshape(DEPTH, 1, D))


def _mm_resid_kernel(a_ref, w_ref, xa_ref, xb_ref, g_ref, o_ref, wbf_ref):
    _cast_weight_once(w_ref, wbf_ref)
    acc = jnp.dot(a_ref[...], wbf_ref[...], preferred_element_type=F32)
    o_ref[...] = _pick(xa_ref, xb_ref, 1) + g_ref[...] * acc


def _mm_resid(a, w, x, mod, layer, w_index, gate_chunk, tm, tn):
    k = a.shape[1]
    row = lambda n, m: m
    col = lambda n, m: n
    x_args, x_specs = _stream_in(x, tm, tn, row, col)
    return pl.pallas_call(
        _mm_resid_kernel,
        grid=(D // tn, M // tm),
        in_specs=[pl.BlockSpec((tm, k), lambda n, m: (m, 0)),
                  pl.BlockSpec((None, k, tn), lambda n, m: (w_index, 0, n))] + x_specs + [
                  _mod_spec(layer, gate_chunk, tm, row, col, tn=tn)],
        out_specs=pl.BlockSpec((tm, tn), lambda n, m: (m, n)),
        out_shape=jax.ShapeDtypeStruct((M, D), F32),
        scratch_shapes=[pltpu.VMEM((k, tn), BF16)],
        compiler_params=_params(2),
        name="mm_resid",
    )(a, w, *x_args, mod)


def _swiglu(a, b):
    return a * jax.nn.sigmoid(a) * b


def _ffn_gu_kernel(x_ref, wg_ref, wu_ref, o_ref, wgb_ref, wub_ref):
    @pl.when(pl.program_id(1) == 0)
    def _():
        _cast_rows(wg_ref, wgb_ref)
        _cast_rows(wu_ref, wub_ref)
    x = x_ref[...]
    a = jnp.dot(x, wgb_ref[...], preferred_element_type=F32)
    b = jnp.dot(x, wub_ref[...], preferred_element_type=F32)
    o_ref[...] = _swiglu(a, b).astype(BF16)


def _ffn_gate_up(h, w_gate, w_up, j):
    tm, tn = TM, 512
    w_spec = pl.BlockSpec((None, D, tn), lambda n, m: (j, 0, n))
    return pl.pallas_call(
        _ffn_gu_kernel,
        grid=(D_FF // tn, M // tm),
        in_specs=[pl.BlockSpec((tm, D), lambda n, m: (m, 0)), w_spec, w_spec],
        out_specs=pl.BlockSpec((tm, tn), lambda n, m: (m, n)),
        out_shape=jax.ShapeDtypeStruct((M, D_FF), BF16),
        scratch_shapes=[pltpu.VMEM((D, tn), BF16), pltpu.VMEM((D, tn), BF16)],
        compiler_params=_params(2),
        name="ffn_gate_up",
    )(h, w_gate, w_up)


def _route_meta(idx):
    t = T_MOE
    experts = jnp.arange(N_EXPERTS, dtype=I32)
    onehot = (idx[:, :, None] == experts[None, None, :]).astype(I32).sum(axis=1)
    csum = jnp.cumsum(onehot, axis=0)
    rank = csum - onehot
    count = csum[-1]
    ntile = (count + t - 1) // t
    tile_end = jnp.cumsum(ntile)
    tile_start = tile_end - ntile
    nused = tile_end[-1]
    pos = (tile_start * t)[idx] + jnp.take_along_axis(rank, idx, axis=1)
    j = jnp.arange(NT_MOE, dtype=I32)
    te_raw = jnp.minimum(jnp.sum(j[:, None] >= tile_end[None, :], axis=1), N_EXPERTS - 1).astype(I32)
    te = jnp.where(j < nused, te_raw, te_raw[nused - 1])
    first = ((j == tile_start[te]) & (j < nused)).astype(I32)
    later = (ntile[None, :] > 0) & (experts[None, :] > te[:, None])
    nxt = jnp.min(jnp.where(later, experts[None, :], N_EXPERTS), axis=1)
    nxt = jnp.where(nxt == N_EXPERTS, -1, nxt).astype(I32)
    pad_start = tile_start * t + count
    pad_len = ntile * t - count
    tail = jnp.stack([nused * t, (NT_MOE - nused) * (t // ZERO_ROWS)])
    zinfo = jnp.concatenate([pad_start, pad_len, tail]).astype(I32)
    return pos.astype(I32), zinfo, (te, first, nxt, nused.reshape(1).astype(I32))


def _dispatch_kernel(p0_ref, p1_ref, z_ref, h_ref, xs_ref, zero_ref, sem, zsem):
    t = h_ref.shape[0]
    i = pl.program_id(0)
    base = i * t

    def clear_padding(start):
        def go(n, off):
            cp = pltpu.make_async_copy(zero_ref.at[pl.ds(0, n), :], xs_ref.at[pl.ds(off, n), :], zsem.at[0])
            cp.start() if start else cp.wait()

        for e in range(N_EXPERTS):
            off, ln = z_ref[e], z_ref[N_EXPERTS + e]
            end = off + ln
            for b in range(SUBLANE_BITS, PAD_BITS):
                @pl.when(((ln >> b) & 1) == 1)
                def _():
                    go(1 << b, pl.multiple_of(end - ((ln >> b) << b), SUBLANES))
            for k in range(SUBLANES - 1):
                @pl.when(k < (ln & (SUBLANES - 1)))
                def _():
                    go(1, off + k)
        tail0, n_tail = z_ref[2 * N_EXPERTS], z_ref[2 * N_EXPERTS + 1]

        def tail_body(k, carry):
            go(ZERO_ROWS, pl.multiple_of(tail0 + k * ZERO_ROWS, SUBLANES))
            return carry

        lax.fori_loop(0, n_tail, tail_body, 0)

    @pl.when(i == 0)
    def _():
        zero_ref[...] = jnp.zeros(zero_ref.shape, zero_ref.dtype)
        clear_padding(True)

    def issue(r, carry):
        src = h_ref.at[pl.ds(r, 1), :]
        pltpu.make_async_copy(src, xs_ref.at[pl.ds(p0_ref[base + r], 1), :], sem.at[0]).start()
        pltpu.make_async_copy(src, xs_ref.at[pl.ds(p1_ref[base + r], 1), :], sem.at[1]).start()
        return carry

    lax.fori_loop(0, t, issue, 0, unroll=8)
    pltpu.make_async_copy(h_ref, xs_ref.at[pl.ds(0, t), :], sem.at[0]).wait()
    pltpu.make_async_copy(h_ref, xs_ref.at[pl.ds(0, t), :], sem.at[1]).wait()

    @pl.when(i == 0)
    def _():
        clear_padding(False)


def _dispatch(h, pos0, pos1, zinfo):
    t = T_DISPATCH
    return pl.pallas_call(
        _dispatch_kernel,
        grid_spec=pltpu.PrefetchScalarGridSpec(
            num_scalar_prefetch=3,
            grid=(M // t,),
            in_specs=[pl.BlockSpec((t, D), lambda i, p0, p1, z: (i, 0))],
            out_specs=pl.BlockSpec(memory_space=pl.ANY),
            scratch_shapes=[pltpu.VMEM((ZERO_ROWS, D), F32),
                            pltpu.SemaphoreType.DMA((2,)), pltpu.SemaphoreType.DMA((1,))]),
        out_shape=jax.ShapeDtypeStruct((P_MOE, D), F32),
        compiler_params=_params(1),
        name="moe_dispatch",
    )(pos0, pos1, zinfo, h)


def _expert_weight_stream(w_refs, stage_refs, bf_refs, sem, te_ref, first_ref, nxt_ref, tn):
    c = pl.program_id(0)
    j = pl.program_id(1)
    nc = pl.num_programs(0)

    def copies(e, cc):
        col = pl.multiple_of(cc * tn, LANES)
        return [pltpu.make_async_copy(w.at[e, :, pl.ds(col, tn)], st, sem.at[k])
                for k, (w, st) in enumerate(zip(w_refs, stage_refs))]

    def start(e, cc):
        for cp in copies(e, cc):
            cp.start()

    @pl.when((c == 0) & (j == 0))
    def _():
        start(te_ref[0], 0)

    @pl.when(first_ref[j] == 1)
    def _():
        for cp in copies(0, 0):
            cp.wait()
        for st, bf in zip(stage_refs, bf_refs):
            _cast_rows(st, bf)
        ne = nxt_ref[j]

        @pl.when(ne >= 0)
        def _():
            start(ne, c)

        @pl.when((ne < 0) & (c + 1 < nc))
        def _():
            start(te_ref[0], c + 1)


def _gmm_gate_up_kernel(te_ref, first_ref, nxt_ref, nused_ref, xs_ref, wg_ref, wu_ref, o_ref,
                        sg_ref, su_ref, wgb_ref, wub_ref, sem):
    _expert_weight_stream((wg_ref, wu_ref), (sg_ref, su_ref), (wgb_ref, wub_ref), sem,
                          te_ref, first_ref, nxt_ref, TN_GU)

    @pl.when(pl.program_id(1) < nused_ref[0])
    def _():
        half = T_MOE // 2
        for r in range(2):
            rs = slice(r * half, (r + 1) * half)
            x = xs_ref[rs, :].astype(BF16)
            a = jnp.dot(x, wgb_ref[...], preferred_element_type=F32)
            b = jnp.dot(x, wub_ref[...], preferred_element_type=F32)
            o_ref[rs, :] = _swiglu(a, b).astype(BF16)

    @pl.when(pl.program_id(1) >= nused_ref[0])
    def _():
        o_ref[...] = jnp.zeros(o_ref.shape, o_ref.dtype)


def _gmm_down_kernel(te_ref, first_ref, nxt_ref, nused_ref, a_ref, wd_ref, o_ref,
                     sd_ref, wdb_ref, sem):
    _expert_weight_stream((wd_ref,), (sd_ref,), (wdb_ref,), sem, te_ref, first_ref, nxt_ref, TN_DN)

    @pl.when(pl.program_id(1) < nused_ref[0])
    def _():
        o_ref[...] = jnp.dot(a_ref[...], wdb_ref[...], preferred_element_type=F32)

    @pl.when(pl.program_id(1) >= nused_ref[0])
    def _():
        o_ref[...] = jnp.zeros(o_ref.shape, o_ref.dtype)


def _used_tile(j, nused):
    return jnp.minimum(j, nused[0] - 1)


def _gmm_gate_up(xs, w_gate, w_up, meta):
    te, first, nxt, nused = meta
    tn = TN_GU
    return pl.pallas_call(
        _gmm_gate_up_kernel,
        grid_spec=pltpu.PrefetchScalarGridSpec(
            num_scalar_prefetch=4,
            grid=(D_FF_EXPERT // tn, NT_MOE),
            in_specs=[pl.BlockSpec((T_MOE, D), lambda c, j, te, fi, nx, nu: (_used_tile(j, nu), 0)),
                      pl.BlockSpec(memory_space=pl.ANY),
                      pl.BlockSpec(memory_space=pl.ANY)],
            out_specs=pl.BlockSpec((T_MOE, tn), lambda c, j, te, fi, nx, nu: (j, c)),
            scratch_shapes=[pltpu.VMEM((D, tn), F32), pltpu.VMEM((D, tn), F32),
                            pltpu.VMEM((D, tn), BF16), pltpu.VMEM((D, tn), BF16),
                            pltpu.SemaphoreType.DMA((2,))]),
        out_shape=jax.ShapeDtypeStruct((P_MOE, D_FF_EXPERT), BF16),
        compiler_params=_params(2),
        name="moe_gate_up",
    )(te, first, nxt, nused, xs, w_gate, w_up)


def _gmm_down(act, w_down, meta):
    te, first, nxt, nused = meta
    tn = TN_DN
    return pl.pallas_call(
        _gmm_down_kernel,
        grid_spec=pltpu.PrefetchScalarGridSpec(
            num_scalar_prefetch=4,
            grid=(D // tn, NT_MOE),
            in_specs=[pl.BlockSpec((T_MOE, D_FF_EXPERT), lambda c, j, te, fi, nx, nu: (_used_tile(j, nu), 0)),
                      pl.BlockSpec(memory_space=pl.ANY)],
            out_specs=pl.BlockSpec((T_MOE, tn), lambda c, j, te, fi, nx, nu: (j, c)),
            scratch_shapes=[pltpu.VMEM((D_FF_EXPERT, tn), F32), pltpu.VMEM((D_FF_EXPERT, tn), BF16),
                            pltpu.SemaphoreType.DMA((1,))]),
        out_shape=jax.ShapeDtypeStruct((P_MOE, D), F32),
        compiler_params=_params(2),
        name="moe_down",
    )(te, first, nxt, nused, act, w_down)


def _combine_kernel(p0_ref, p1_ref, ys_ref, xa_ref, xb_ref, g_ref, w_ref, oc_ref, ol_ref,
                    a_ref, b_ref, sem):
    t = xa_ref.shape[0]
    i = pl.program_id(0)
    n = pl.num_programs(0)

    def issue(step, slot):
        base = step * t

        def body(r, carry):
            pltpu.make_async_copy(ys_ref.at[pl.ds(p0_ref[base + r], 1), :],
                                  a_ref.at[slot, pl.ds(r, 1), :], sem.at[0, slot]).start()
            pltpu.make_async_copy(ys_ref.at[pl.ds(p1_ref[base + r], 1), :],
                                  b_ref.at[slot, pl.ds(r, 1), :], sem.at[1, slot]).start()
            return carry

        lax.fori_loop(0, t, body, 0, unroll=8)

    @pl.when(i == 0)
    def _():
        issue(0, 0)

    @pl.when(i + 1 < n)
    def _():
        issue(i + 1, (i + 1) % 2)

    slot = i % 2
    pltpu.make_async_copy(ys_ref.at[pl.ds(0, t), :], a_ref.at[slot], sem.at[0, slot]).wait()
    pltpu.make_async_copy(ys_ref.at[pl.ds(0, t), :], b_ref.at[slot], sem.at[1, slot]).wait()
    w = w_ref[...]
    moe = w[:, 0:1] * a_ref[slot] + w[:, 1:2] * b_ref[slot]
    y = _pick(xa_ref, xb_ref, 0) + g_ref[...] * moe
    is_ctx = _is_ctx_tile(i, t)

    @pl.when(is_ctx)
    def _():
        oc_ref[...] = y

    @pl.when(jnp.logical_not(is_ctx))
    def _():
        ol_ref[...] = y


def _combine(ys, x, mod, layer, gate_chunk, wts, pos0, pos1):
    t = T_COMBINE
    n_ctx = MP // t
    row = lambda i, *_: i
    x_args, x_specs = _stream_in(x, t, D, row, lambda i, *_: 0)
    return pl.pallas_call(
        _combine_kernel,
        grid_spec=pltpu.PrefetchScalarGridSpec(
            num_scalar_prefetch=2,
            grid=(M // t,),
            in_specs=[pl.BlockSpec(memory_space=pl.ANY)] + x_specs + [
                _mod_spec(layer, gate_chunk, t, row),
                pl.BlockSpec((t, LANES), lambda i, p0, p1: (i, 0))],
            out_specs=[pl.BlockSpec((t, D), lambda i, p0, p1: (jnp.minimum(i, n_ctx - 1), 0)),
                       pl.BlockSpec((t, D), lambda i, p0, p1: (jnp.maximum(i - n_ctx, 0), 0))],
            scratch_shapes=[pltpu.VMEM((2, t, D), F32), pltpu.VMEM((2, t, D), F32),
                            pltpu.SemaphoreType.DMA((2, 2))]),
        out_shape=[jax.ShapeDtypeStruct((MP, D), F32), jax.ShapeDtypeStruct((MS, D), F32)],
        compiler_params=_params(1),
        name="moe_combine",
    )(pos0, pos1, ys, *x_args, mod, wts)


def _moe(x, h, idx, wts, mod, layer, w_gate, w_up, w_down):
    pos, zinfo, meta = _route_meta(idx[:, :TOP_K])
    pos0, pos1 = pos[:, 0], pos[:, 1]
    xs = _dispatch(h, pos0, pos1, zinfo)
    act = _gmm_gate_up(xs, w_gate, w_up, meta)
    ys = _gmm_down(act, w_down, meta)
    return _combine(ys, x, mod, layer, 5, wts, pos0, pos1)


def kernel(x_prompt, x_sample, cache_k, cache_v, c, c_ctx, w_ada, b_ada, norm1_g, norm2_g, w_in, q_norm_g, k_norm_g, sgu_norm_g, w_spatial, b_spatial, out_norm_g, w_out, ffn_w_gate, ffn_w_up, ffn_w_down, w_router, b_router, moe_w_gate, moe_w_up, moe_w_down):
    assert DEPTH == 2
    x = (x_prompt.reshape(MP, D), x_sample.reshape(MS, D))
    cond = jnp.concatenate([c_ctx[None, :], c, jnp.zeros((N_COND - 1 - DEC_BATCH, D), F32)], axis=0)
    mod = _modulation(cond, w_ada, b_ada).reshape(DEPTH, N_COND, 1, N_MOD * D)
    cos, sin = _rope_tables()
    w_in_bf = _cast_in_weights(w_in)

    new_k, new_v = [], []
    for i in range(DEPTH):
        q, kf, kb, vf, vb, u, gh = _in_projections(x, w_in_bf, mod, norm1_g, q_norm_g, k_norm_g,
                                                   sgu_norm_g, cos, sin, i)
        attn_ctx = _attention(q, kb, vb, None, None, i, batch=BATCH, seq=SEQ, row0=0)
        attn_lat = _attention(q, kb, vb, cache_k, cache_v, i, batch=DEC_BATCH, seq=DEC_SEQ, row0=MP)
        o = _sgu_merge(u, gh, attn_ctx, attn_lat, w_spatial, b_spatial, out_norm_g, i)
        j = i // 2
        if i % 2 == 0:
            x, h2 = _out_proj(o, w_out, x, mod, norm2_g, i)
            act = _ffn_gate_up(h2, ffn_w_gate, ffn_w_up, j)
            x = _mm_resid(act, ffn_w_down, x, mod, i, j, 5, 512, 512)
        else:
            x, h2, idx, wts = _out_proj(o, w_out, x, mod, norm2_g, i, router=(w_router[j], b_router[j]))
            x = _moe(x, h2, idx, wts, mod, i, moe_w_gate[j], moe_w_up[j], moe_w_down[j])
        new_k.append(kf[:MP].reshape(BATCH, SEQ, N_KV_HEADS, HEAD_DIM))
        new_v.append(vf[:MP].reshape(BATCH, SEQ, N_KV_HEADS, HEAD_DIM))

    y_prompt = x[0].reshape(BATCH, SEQ, D)
    y_sample = x[1].reshape(DEC_BATCH, DEC_SEQ, D)
    return (y_prompt, y_sample, jnp.stack(new_k, axis=1), jnp.stack(new_v, axis=1))
```

```python
import functools

import jax
import jax.numpy as jnp
from jax import lax
from jax.experimental import pallas as pl
from jax.experimental.pallas import tpu as pltpu

F32 = jnp.float32
BF16 = jnp.bfloat16
I32 = jnp.int32

D = 2048
BATCH, SEQ = 16, 256
DEC_BATCH, DEC_SEQ = 4, 2048
PAST_LEN = 256
DEPTH = 2
GRID_W = 64
CHUNK = 128
HEAD_DIM = 128
N_Q_HEADS, N_KV_HEADS = 8, 2
Q_PER_KV = N_Q_HEADS // N_KV_HEADS
ATTN_WIDTH = N_Q_HEADS * HEAD_DIM
KV_WIDTH = N_KV_HEADS * HEAD_DIM
N_SGU_HEADS = 8
SGU_WIDTH = N_SGU_HEADS * HEAD_DIM
IN_WIDTH = ATTN_WIDTH + 2 * KV_WIDTH + 2 * SGU_WIDTH
ROPE_THETA = 10000.0
ROPE_AXIS_DIM = HEAD_DIM // 2
D_FF = 5632
N_EXPERTS = 8
TOP_K = 2
D_FF_EXPERT = 2816
N_MOD = 6
EPS = 1e-6
ATTN_SCALE = HEAD_DIM ** -0.5
LOG2_E = 1.4426950408889634

MP = BATCH * SEQ
MS = DEC_BATCH * DEC_SEQ
M = MP + MS
N_COND = 8
LANES = 128
SUBLANES = 8
SUBLANE_BITS = 3

VMEM_LIMIT = 56 * 1024 * 1024

TM = 1024
TN_IN = 2 * KV_WIDTH
T_NORM = 512
T_Q = 512
T_MOE = 512
P_MOE = M * TOP_K + N_EXPERTS * T_MOE
NT_MOE = P_MOE // T_MOE
TN_GU = D_FF_EXPERT // 2
TN_DN = D // 2
T_DISPATCH = 1024
T_COMBINE = 256
ZERO_ROWS = T_MOE // 2
PAD_BITS = ZERO_ROWS.bit_length()


def _params(n_axes):
    return pltpu.CompilerParams(dimension_semantics=("arbitrary",) * n_axes,
                                vmem_limit_bytes=VMEM_LIMIT)


def _cond_row(i, t):
    return jnp.where(i < MP // t, 0, 1 + (i - MP // t) // (DEC_SEQ // t))


def _is_ctx_tile(i, t):
    return i < MP // t


def _stream_in(x, t, width, row_of, col_of):
    n_ctx = MP // t
    pair = isinstance(x, tuple)
    base = 0 if pair else n_ctx
    ctx = pl.BlockSpec((t, width), lambda *g: (jnp.minimum(row_of(*g), n_ctx - 1), col_of(*g)))
    lat = pl.BlockSpec((t, width), lambda *g: (base + jnp.maximum(row_of(*g) - n_ctx, 0), col_of(*g)))
    return (list(x) if pair else [x, x]), [ctx, lat]


def _mod_spec(layer, chunk, t, row_of, col_of=None, tn=D):
    per = D // tn

    def index_map(*g):
        col = chunk * per + (col_of(*g) if col_of is not None else 0)
        return (layer, _cond_row(row_of(*g), t), 0, col)

    return pl.BlockSpec((None, None, 1, tn), index_map)


def _ada_kernel(c_ref, w_ref, b_ref, o_ref):
    c = c_ref[...]
    s = (c * jax.nn.sigmoid(c)).astype(BF16)
    o_ref[...] = jnp.dot(s, w_ref[...].astype(BF16), preferred_element_type=F32) + b_ref[...]


def _modulation(cond, w_ada, b_ada):
    tn = 1024
    width = N_MOD * D
    return pl.pallas_call(
        _ada_kernel,
        grid=(DEPTH, width // tn),
        in_specs=[pl.BlockSpec((N_COND, D), lambda l, n: (0, 0)),
                  pl.BlockSpec((None, D, tn), lambda l, n: (l, 0, n)),
                  pl.BlockSpec((None, 1, tn), lambda l, n: (l, 0, n))],
        out_specs=pl.BlockSpec((None, N_COND, tn), lambda l, n: (l, 0, n)),
        out_shape=jax.ShapeDtypeStruct((DEPTH, N_COND, width), F32),
        compiler_params=_params(2),
        name="modulation",
    )(cond, w_ada, b_ada.reshape(DEPTH, 1, width))


def _modulated_norm(x, g, sc, sh):
    y = x * lax.rsqrt(jnp.mean(x * x, axis=-1, keepdims=True) + EPS)
    return (y * g) * (1.0 + sc) + sh


def _pick(xa_ref, xb_ref, axis):
    t = xa_ref.shape[0]
    return jnp.where(_is_ctx_tile(pl.program_id(axis), t), xa_ref[...], xb_ref[...])


def _stream_tile_copy(xa_ref, xb_ref, lat_row0, buf_ref, sem, tile, slot, start):
    t = buf_ref.shape[1]

    def copy(src_ref, row):
        return pltpu.make_async_copy(src_ref.at[pl.ds(pl.multiple_of(row, t), t), :],
                                     buf_ref.at[slot], sem.at[slot])

    if not start:
        copy(xa_ref, 0).wait()
        return
    is_ctx = _is_ctx_tile(tile, t)

    @pl.when(is_ctx)
    def _():
        copy(xa_ref, tile * t).start()

    @pl.when(jnp.logical_not(is_ctx))
    def _():
        copy(xb_ref, lat_row0 + (tile - MP // t) * t).start()


def _next_stream_tile(xa_ref, xb_ref, lat_row0, buf_ref, sem, tile, n_tiles):
    @pl.when(tile == 0)
    def _():
        _stream_tile_copy(xa_ref, xb_ref, lat_row0, buf_ref, sem, 0, 0, True)

    @pl.when(tile + 1 < n_tiles)
    def _():
        _stream_tile_copy(xa_ref, xb_ref, lat_row0, buf_ref, sem, tile + 1, (tile + 1) % 2, True)

    slot = tile % 2
    _stream_tile_copy(xa_ref, xb_ref, lat_row0, buf_ref, sem, tile, slot, False)
    return slot


def _stream_hbm(x):
    return (x[0], x[1], 0) if isinstance(x, tuple) else (x, x, MP)


def _split_bf16(a):
    hi = a.astype(BF16)
    return hi, (a - hi.astype(F32)).astype(BF16)


def _route_top2(h, wr, br, idx_ref, wt_ref):
    h_hi, h_lo = _split_bf16(h)
    w_hi, w_lo = _split_bf16(wr)
    logits = (jnp.dot(h_hi, w_hi, preferred_element_type=F32)
              + jnp.dot(h_lo, w_hi, preferred_element_type=F32)
              + jnp.dot(h_hi, w_lo, preferred_element_type=F32)) + br
    lane = lax.broadcasted_iota(I32, logits.shape, 1)
    neg = jnp.float32(-jnp.inf)
    lg = jnp.where(lane < N_EXPERTS, logits, neg)
    m1 = jnp.max(lg, axis=-1, keepdims=True)
    i1 = jnp.min(jnp.where(lg == m1, lane, LANES), axis=-1, keepdims=True)
    lg2 = jnp.where(lane == i1, neg, lg)
    m2 = jnp.max(lg2, axis=-1, keepdims=True)
    i2 = jnp.min(jnp.where(lg2 == m2, lane, LANES), axis=-1, keepdims=True)
    e = jnp.exp(m2 - m1)
    w1 = 1.0 / (1.0 + e)
    w2 = e / (1.0 + e)
    idx_ref[...] = jnp.where(lane == 0, i1, jnp.where(lane == 1, i2, 0))
    wt_ref[...] = jnp.where(lane == 0, w1, jnp.where(lane == 1, w2, 0.0))


W_PIECE = 512


def _out_proj_kernel(*refs, layer, lat_row0, route):
    if route:
        (o_ref, w_ref, xa_ref, xb_ref, gate_ref, g_ref, sc_ref, sh_ref, wr_ref, br_ref,
         xn_ref, h_ref, idx_ref, wt_ref, stage_ref, wbf_ref, xbuf_ref, wsem, xsem) = refs
    else:
        (o_ref, w_ref, xa_ref, xb_ref, gate_ref, g_ref, sc_ref, sh_ref,
         xn_ref, h_ref, stage_ref, wbf_ref, xbuf_ref, wsem, xsem) = refs
    i = pl.program_id(0)

    @pl.when(i == 0)
    def _():
        for p in range(D // W_PIECE):
            cols = pl.ds(p * W_PIECE, W_PIECE)
            cp = pltpu.make_async_copy(w_ref.at[layer, :, cols], stage_ref, wsem.at[0])
            cp.start()
            cp.wait()
            _cast_rows(stage_ref, wbf_ref.at[:, cols])

    slot = _next_stream_tile(xa_ref, xb_ref, lat_row0, xbuf_ref, xsem, i, pl.num_programs(0))
    acc = jnp.dot(o_ref[...], wbf_ref[...], preferred_element_type=F32)
    x_new = xbuf_ref[slot] + gate_ref[...] * acc
    xn_ref[...] = x_new
    h = _modulated_norm(x_new, g_ref[...], sc_ref[...], sh_ref[...])
    if route:
        h_ref[...] = h
        _route_top2(h, wr_ref[...], br_ref[...], idx_ref, wt_ref)
    else:
        h_ref[...] = h.astype(BF16)


def _out_proj(o, w_out, x, mod, norm2_g, layer, router=None):
    t = T_NORM
    row = lambda i: i
    xa, xb, lat_row0 = _stream_hbm(x)
    route = router is not None
    anyspace = pl.BlockSpec(memory_space=pl.ANY)
    rows = lambda w: pl.BlockSpec((t, w), lambda i: (i, 0))
    in_specs = [rows(D), anyspace, anyspace, anyspace,
                _mod_spec(layer, 2, t, row),
                pl.BlockSpec((None, 1, D), lambda i: (layer, 0, 0)),
                _mod_spec(layer, 4, t, row), _mod_spec(layer, 3, t, row)]
    args = [o, w_out, xa, xb, mod, norm2_g.reshape(DEPTH, 1, D), mod, mod]
    out_specs = [rows(D), rows(D)]
    out_shape = [jax.ShapeDtypeStruct((M, D), F32), jax.ShapeDtypeStruct((M, D), F32 if route else BF16)]
    if route:
        w_router, b_router = router
        args += [jnp.zeros((D, LANES), F32).at[:, :N_EXPERTS].set(w_router),
                 jnp.zeros((1, LANES), F32).at[0, :N_EXPERTS].set(b_router)]
        in_specs += [pl.BlockSpec((D, LANES), lambda i: (0, 0)), pl.BlockSpec((1, LANES), lambda i: (0, 0))]
        out_specs += [rows(LANES), rows(LANES)]
        out_shape += [jax.ShapeDtypeStruct((M, LANES), I32), jax.ShapeDtypeStruct((M, LANES), F32)]
    return pl.pallas_call(
        functools.partial(_out_proj_kernel, layer=layer, lat_row0=lat_row0, route=route),
        grid=(M // t,),
        in_specs=in_specs,
        out_specs=out_specs,
        out_shape=out_shape,
        scratch_shapes=[pltpu.VMEM((D, W_PIECE), F32), pltpu.VMEM((D, D), BF16),
                        pltpu.VMEM((2, t, D), F32),
                        pltpu.SemaphoreType.DMA((1,)), pltpu.SemaphoreType.DMA((2,))],
        compiler_params=_params(1),
        name="out_proj_router" if route else "out_proj",
    )(*args)


CAST_ROWS = 256


def _cast_rows(src_ref, dst_ref):
    def body(r, carry):
        rs = pl.ds(pl.multiple_of(r * CAST_ROWS, CAST_ROWS), CAST_ROWS)
        dst_ref[rs, :] = src_ref[rs, :].astype(BF16)
        return carry

    lax.fori_loop(0, src_ref.shape[0] // CAST_ROWS, body, 0)


def _cast_weight_once(w_ref, wbf_ref):
    @pl.when(pl.program_id(1) == 0)
    def _():
        _cast_rows(w_ref, wbf_ref)


def _head_rms(a):
    return lax.rsqrt(jnp.mean(a * a, axis=-1, keepdims=True) + EPS)


def _rope_partner(ag, perm):
    hi, lo = _split_bf16(ag)
    return (jnp.dot(hi, perm, preferred_element_type=F32)
            + jnp.dot(lo, perm, preferred_element_type=F32))


def _cast_kernel(w_ref, o_ref):
    o_ref[...] = w_ref[...].astype(BF16)


def _cast_in_weights(w_in):
    spec = pl.BlockSpec((None, D, TN_IN), lambda l, n: (l, 0, n))
    return pl.pallas_call(
        _cast_kernel,
        grid=(DEPTH, IN_WIDTH // TN_IN),
        in_specs=[spec],
        out_specs=spec,
        out_shape=jax.ShapeDtypeStruct(w_in.shape, BF16),
        compiler_params=_params(2),
        name="cast_w_in",
    )(w_in)


N_Q_TILES = ATTN_WIDTH // TN_IN
KV_TILE = N_Q_TILES
U_TILE0 = KV_TILE + 1
G_TILE0 = U_TILE0 + SGU_WIDTH // TN_IN
N_IN_TILES = IN_WIDTH // TN_IN


def _in_proj_kernel(xa_ref, xb_ref, n1_ref, sc0_ref, sh0_ref, sc1_ref, sh1_ref, w_ref,
                    qg_ref, kg_ref, sg_ref, pq_ref, pk_ref, cos_ref, sin_ref,
                    q_ref, kf_ref, kb_ref, vf_ref, vb_ref, u_ref, gh_ref,
                    xbuf_ref, h_ref, xsem, *, lat_row0):
    m = pl.program_id(0)
    n = pl.program_id(1)
    nm = pl.num_programs(0)
    cur = m % 2
    nxt = (m + 1) % 2
    has_next = m + 1 < nm
    fetch = functools.partial(_stream_tile_copy, xa_ref, xb_ref, lat_row0, xbuf_ref, xsem)

    def norm_rows(slot, rows, sc_ref, sh_ref):
        x = xbuf_ref[slot, rows, :]
        h_ref[slot, rows, :] = _modulated_norm(x, n1_ref[...], sc_ref[...], sh_ref[...]).astype(BF16)

    @pl.when((m == 0) & (n == 0))
    def _():
        fetch(0, 0, True)
        fetch(0, 0, False)
        norm_rows(0, slice(None), sc0_ref, sh0_ref)

    @pl.when((n == 0) & has_next)
    def _():
        fetch(m + 1, nxt, True)

    def matmul():
        return jnp.dot(h_ref[cur], w_ref[...], preferred_element_type=F32)

    @pl.when(n < N_Q_TILES)
    def _():
        acc = matmul()
        ag = acc * qg_ref[...]
        partner = _rope_partner(ag, pq_ref[...])
        cos, sin = cos_ref[...], sin_ref[...]
        for h in range(TN_IN // HEAD_DIM):
            sl = slice(h * HEAD_DIM, (h + 1) * HEAD_DIM)
            r = _head_rms(acc[:, sl]) * (ATTN_SCALE * LOG2_E)
            q_ref[:, sl] = ((ag[:, sl] * cos + partner[:, sl] * sin) * r).astype(BF16)

    @pl.when(n == KV_TILE)
    def _():
        acc = matmul()
        k = acc[:, :KV_WIDTH]
        ag = k * kg_ref[...]
        partner = _rope_partner(ag, pk_ref[...])
        cos, sin = cos_ref[...], sin_ref[...]
        for h in range(N_KV_HEADS):
            sl = slice(h * HEAD_DIM, (h + 1) * HEAD_DIM)
            r = _head_rms(k[:, sl])
            kf_ref[:, sl] = ag[:, sl] * r
            kb_ref[:, sl] = ((ag[:, sl] * cos + partner[:, sl] * sin) * r).astype(BF16)
        v = acc[:, KV_WIDTH:]
        vf_ref[...] = v
        vb_ref[...] = v.astype(BF16)

    half = TM // (G_TILE0 - U_TILE0)
    for k in range(G_TILE0 - U_TILE0):
        for parity in range(2):
            @pl.when((n == U_TILE0 + k) & has_next & (cur == parity))
            def _():
                if k == 0:
                    fetch(m + 1, 1 - parity, False)
                u_ref[...] = jnp.dot(h_ref[parity], w_ref[...], preferred_element_type=F32).astype(BF16)
                norm_rows(1 - parity, slice(k * half, (k + 1) * half), sc1_ref, sh1_ref)

        @pl.when((n == U_TILE0 + k) & jnp.logical_not(has_next))
        def _():
            u_ref[...] = matmul().astype(BF16)

    @pl.when(n >= G_TILE0)
    def _():
        acc = matmul()
        for h in range(TN_IN // HEAD_DIM):
            sl = slice(h * HEAD_DIM, (h + 1) * HEAD_DIM)
            a = acc[:, sl]
            gh_ref[:, sl] = (a * _head_rms(a) * sg_ref[:, sl]).astype(BF16)


def _rope_tables():
    n_rows = DEC_SEQ // GRID_W
    rows = jnp.broadcast_to(jnp.arange(n_rows)[:, None], (n_rows, GRID_W)).reshape(-1)
    cols = jnp.broadcast_to(jnp.arange(GRID_W)[None, :], (n_rows, GRID_W)).reshape(-1)
    inv = ROPE_THETA ** (-jnp.arange(0, ROPE_AXIS_DIM, 2, dtype=F32) / ROPE_AXIS_DIM)
    ang_r = rows.astype(F32)[:, None] * inv
    ang_c = cols.astype(F32)[:, None] * inv
    cos = jnp.concatenate([jnp.cos(ang_r), jnp.cos(ang_r), jnp.cos(ang_c), jnp.cos(ang_c)], axis=1)
    sin = jnp.concatenate([-jnp.sin(ang_r), jnp.sin(ang_r), -jnp.sin(ang_c), jnp.sin(ang_c)], axis=1)
    cos = jnp.concatenate([jnp.ones((TM, HEAD_DIM), F32), cos], axis=0)
    sin = jnp.concatenate([jnp.zeros((TM, HEAD_DIM), F32), sin], axis=0)
    return cos, sin


def _partner_matrix(n_heads):
    w = n_heads * HEAD_DIM
    quarter = ROPE_AXIS_DIM // 2
    j = jnp.arange(w)
    partner = jnp.where((j % ROPE_AXIS_DIM) < quarter, j + quarter, j - quarter)
    return (jnp.arange(w)[:, None] == partner[None, :]).astype(BF16)


def _rope_block(m):
    return jnp.where(m < MP // TM, 0, 1 + (m - MP // TM) % (DEC_SEQ // TM))


def _in_projections(x, w_in_bf, mod, norm1_g, q_norm_g, k_norm_g, sgu_norm_g, cos, sin, layer):
    tn = TN_IN
    xa, xb, lat_row0 = _stream_hbm(x)
    row = lambda m, n: m
    next_row = lambda m, n: jnp.minimum(m + 1, M // TM - 1)
    anyspace = pl.BlockSpec(memory_space=pl.ANY)
    const = lambda shape: pl.BlockSpec(shape, lambda m, n: (0,) * len(shape))
    rope_spec = pl.BlockSpec((TM, HEAD_DIM), lambda m, n: (_rope_block(m), 0))
    q_heads = tn // HEAD_DIM
    q_gain = jnp.tile(q_norm_g[layer], q_heads)[None, :]
    k_gain = jnp.tile(k_norm_g[layer], N_KV_HEADS)[None, :]
    g_tile = lambda n: jnp.clip(n - G_TILE0, 0, SGU_WIDTH // tn - 1)
    kv_out = pl.BlockSpec((TM, KV_WIDTH), lambda m, n: (m, 0))
    kv_shape = lambda dt: jax.ShapeDtypeStruct((M, KV_WIDTH), dt)
    return pl.pallas_call(
        functools.partial(_in_proj_kernel, lat_row0=lat_row0),
        grid=(M // TM, N_IN_TILES),
        in_specs=[anyspace, anyspace,
                  pl.BlockSpec((None, 1, D), lambda m, n: (layer, 0, 0)),
                  _mod_spec(layer, 1, TM, row), _mod_spec(layer, 0, TM, row),
                  _mod_spec(layer, 1, TM, next_row), _mod_spec(layer, 0, TM, next_row),
                  pl.BlockSpec((None, D, tn), lambda m, n: (layer, 0, n)),
                  const((1, tn)), const((1, KV_WIDTH)),
                  pl.BlockSpec((None, 1, tn), lambda m, n: (layer, 0, g_tile(n))),
                  const((tn, tn)), const((KV_WIDTH, KV_WIDTH)), rope_spec, rope_spec],
        out_specs=[pl.BlockSpec((TM, tn), lambda m, n: (m, jnp.minimum(n, N_Q_TILES - 1))),
                   kv_out, kv_out, kv_out, kv_out,
                   pl.BlockSpec((TM, tn), lambda m, n: (m, jnp.clip(n - U_TILE0, 0, SGU_WIDTH // tn - 1))),
                   pl.BlockSpec((TM, tn), lambda m, n: (m, g_tile(n)))],
        out_shape=[jax.ShapeDtypeStruct((M, ATTN_WIDTH), BF16),
                   kv_shape(F32), kv_shape(BF16), kv_shape(F32), kv_shape(BF16),
                   jax.ShapeDtypeStruct((M, SGU_WIDTH), BF16),
                   jax.ShapeDtypeStruct((M, SGU_WIDTH), BF16)],
        scratch_shapes=[pltpu.VMEM((2, TM, D), F32), pltpu.VMEM((2, TM, D), BF16),
                        pltpu.SemaphoreType.DMA((2,))],
        compiler_params=_params(2),
        name="in_proj",
    )(xa, xb, norm1_g.reshape(DEPTH, 1, D), mod, mod, mod, mod, w_in_bf, q_gain, k_gain,
      sgu_norm_g.reshape(DEPTH, 1, SGU_WIDTH), _partner_matrix(q_heads), _partner_matrix(N_KV_HEADS),
      cos, sin)


def _qk(q, k):
    return lax.dot_general(q, k, (((1,), (1,)), ((), ())), preferred_element_type=F32)


def _attn_kernel(*refs, has_cache):
    def with_ones(v):
        return jnp.concatenate([v, jnp.ones_like(v)], axis=1)

    if has_cache:
        q_ref, k_ref, v_ref, kc_ref, vc_ref, o_ref = refs
        kc = kc_ref[...].astype(BF16)
        vc = with_ones(vc_ref[...].astype(BF16))
    else:
        q_ref, k_ref, v_ref, o_ref = refs
    k = k_ref[...]
    v = with_ones(v_ref[...])
    for g in range(Q_PER_KV):
        sl = slice(g * HEAD_DIM, (g + 1) * HEAD_DIM)
        q = q_ref[:, sl]
        s = _qk(q, k)
        m = jnp.max(s, axis=-1, keepdims=True)
        if has_cache:
            sc = _qk(q, kc)
            m = jnp.maximum(m, jnp.max(sc, axis=-1, keepdims=True))
        o = jnp.dot(jnp.exp2(s - m).astype(BF16), v, preferred_element_type=F32)
        if has_cache:
            o = o + jnp.dot(jnp.exp2(sc - m).astype(BF16), vc, preferred_element_type=F32)
        o_ref[:, sl] = (o[:, :HEAD_DIM] / o[:, HEAD_DIM:]).astype(BF16)


def _attention(q, kb, vb, cache_k, cache_v, layer, *, batch, seq, row0):
    has_cache = cache_k is not None
    tq = min(T_Q, seq)
    nq = seq // tq
    qw = Q_PER_KV * HEAD_DIM
    q_spec = pl.BlockSpec((tq, qw), lambda b, kv, i: (row0 // tq + b * nq + i, kv))
    kv_spec = pl.BlockSpec((seq, HEAD_DIM), lambda b, kv, i: (row0 // seq + b, kv))
    in_specs = [q_spec, kv_spec, kv_spec]
    args = [q, kb, vb]
    if has_cache:
        c_spec = pl.BlockSpec((None, None, PAST_LEN, HEAD_DIM), lambda b, kv, i: (b, layer, 0, kv))
        in_specs += [c_spec, c_spec]
        args += [cache_k.reshape(DEC_BATCH, DEPTH, PAST_LEN, KV_WIDTH),
                 cache_v.reshape(DEC_BATCH, DEPTH, PAST_LEN, KV_WIDTH)]
    return pl.pallas_call(
        functools.partial(_attn_kernel, has_cache=has_cache),
        grid=(batch, N_KV_HEADS, nq),
        in_specs=in_specs,
        out_specs=pl.BlockSpec((tq, qw), lambda b, kv, i: (b * nq + i, kv)),
        out_shape=jax.ShapeDtypeStruct((batch * seq, ATTN_WIDTH), BF16),
        compiler_params=_params(3),
        name="attention_cached" if has_cache else "attention",
    )(*args)


def _sgu_merge_kernel(u_ref, gh_ref, ap_ref, as_ref, ws_ref, bs_ref, gn_ref, o_ref, sgu_ref):
    t = u_ref.shape[0]
    a = _pick(ap_ref, as_ref, 0).astype(F32)
    a = a * lax.rsqrt(jnp.mean(a * a, axis=-1, keepdims=True) + EPS) * gn_ref[:, :ATTN_WIDTH]
    o_ref[:, :ATTN_WIDTH] = a.astype(BF16)
    for h in range(N_SGU_HEADS):
        cs = slice(h * HEAD_DIM, (h + 1) * HEAD_DIM)
        w = ws_ref[h].astype(BF16)
        b = bs_ref[h]
        for c in range(t // CHUNK):
            rs = slice(c * CHUNK, (c + 1) * CHUNK)
            mixed = jnp.dot(w, gh_ref[rs, cs], preferred_element_type=F32) + b
            sgu_ref[rs, cs] = u_ref[rs, cs].astype(F32) * mixed
    s = sgu_ref[...]
    s = s * lax.rsqrt(jnp.mean(s * s, axis=-1, keepdims=True) + EPS) * gn_ref[:, ATTN_WIDTH:]
    o_ref[:, ATTN_WIDTH:] = s.astype(BF16)


def _sgu_merge(u, gh, attn_ctx, attn_lat, w_spatial, b_spatial, out_norm_g, layer):
    t = T_NORM
    bias = jnp.broadcast_to(b_spatial[:, :, :, None], (DEPTH, N_SGU_HEADS, CHUNK, HEAD_DIM))
    row = lambda w: pl.BlockSpec((t, w), lambda i: (i, 0))
    a_args, a_specs = _stream_in((attn_ctx, attn_lat), t, ATTN_WIDTH, lambda i: i, lambda i: 0)
    return pl.pallas_call(
        _sgu_merge_kernel,
        grid=(M // t,),
        in_specs=[row(SGU_WIDTH), row(SGU_WIDTH)] + a_specs + [
            pl.BlockSpec((None, N_SGU_HEADS, CHUNK, CHUNK), lambda i: (layer, 0, 0, 0)),
            pl.BlockSpec((None, N_SGU_HEADS, CHUNK, HEAD_DIM), lambda i: (layer, 0, 0, 0)),
            pl.BlockSpec((None, 1, D), lambda i: (layer, 0, 0))],
        out_specs=row(D),
        out_shape=jax.ShapeDtypeStruct((M, D), BF16),
        scratch_shapes=[pltpu.VMEM((t, SGU_WIDTH), F32)],
        compiler_params=_params(1),
        name="sgu_merge",
    )(u, gh, *a_args, w_spatial, bias, out_norm_g.reshape(DEPTH, 1, D))


def _mm_resid_kernel(a_ref, w_ref, xa_ref, xb_ref, g_ref, o_ref, wbf_ref):
    _cast_weight_once(w_ref, wbf_ref)
    acc = jnp.dot(a_ref[...], wbf_ref[...], preferred_element_type=F32)
    o_ref[...] = _pick(xa_ref, xb_ref, 1) + g_ref[...] * acc


def _mm_resid(a, w, x, mod, layer, w_index, gate_chunk, tm, tn):
    k = a.shape[1]
    row = lambda n, m: m
    col = lambda n, m: n
    x_args, x_specs = _stream_in(x, tm, tn, row, col)
    return pl.pallas_call(
        _mm_resid_kernel,
        grid=(D // tn, M // tm),
        in_specs=[pl.BlockSpec((tm, k), lambda n, m: (m, 0)),
                  pl.BlockSpec((None, k, tn), lambda n, m: (w_index, 0, n))] + x_specs + [
                  _mod_spec(layer, gate_chunk, tm, row, col, tn=tn)],
        out_specs=pl.BlockSpec((tm, tn), lambda n, m: (m, n)),
        out_shape=jax.ShapeDtypeStruct((M, D), F32),
        scratch_shapes=[pltpu.VMEM((k, tn), BF16)],
        compiler_params=_params(2),
        name="mm_resid",
    )(a, w, *x_args, mod)


def _swiglu(a, b):
    return a * jax.nn.sigmoid(a) * b


def _ffn_gu_kernel(x_ref, wg_ref, wu_ref, o_ref, wgb_ref, wub_ref):
    @pl.when(pl.program_id(1) == 0)
    def _():
        _cast_rows(wg_ref, wgb_ref)
        _cast_rows(wu_ref, wub_ref)
    x = x_ref[...]
    a = jnp.dot(x, wgb_ref[...], preferred_element_type=F32)
    b = jnp.dot(x, wub_ref[...], preferred_element_type=F32)
    o_ref[...] = _swiglu(a, b).astype(BF16)


def _ffn_gate_up(h, w_gate, w_up, j):
    tm, tn = TM, 512
    w_spec = pl.BlockSpec((None, D, tn), lambda n, m: (j, 0, n))
    return pl.pallas_call(
        _ffn_gu_kernel,
        grid=(D_FF // tn, M // tm),
        in_specs=[pl.BlockSpec((tm, D), lambda n, m: (m, 0)), w_spec, w_spec],
        out_specs=pl.BlockSpec((tm, tn), lambda n, m: (m, n)),
        out_shape=jax.ShapeDtypeStruct((M, D_FF), BF16),
        scratch_shapes=[pltpu.VMEM((D, tn), BF16), pltpu.VMEM((D, tn), BF16)],
        compiler_params=_params(2),
        name="ffn_gate_up",
    )(h, w_gate, w_up)


def _route_meta(idx):
    t = T_MOE
    experts = jnp.arange(N_EXPERTS, dtype=I32)
    onehot = (idx[:, :, None] == experts[None, None, :]).astype(I32).sum(axis=1)
    csum = jnp.cumsum(onehot, axis=0)
    rank = csum - onehot
    count = csum[-1]
    ntile = (count + t - 1) // t
    tile_end = jnp.cumsum(ntile)
    tile_start = tile_end - ntile
    nused = tile_end[-1]
    pos = (tile_start * t)[idx] + jnp.take_along_axis(rank, idx, axis=1)
    j = jnp.arange(NT_MOE, dtype=I32)
    te_raw = jnp.minimum(jnp.sum(j[:, None] >= tile_end[None, :], axis=1), N_EXPERTS - 1).astype(I32)
    te = jnp.where(j < nused, te_raw, te_raw[nused - 1])
    first = ((j == tile_start[te]) & (j < nused)).astype(I32)
    later = (ntile[None, :] > 0) & (experts[None, :] > te[:, None])
    nxt = jnp.min(jnp.where(later, experts[None, :], N_EXPERTS), axis=1)
    nxt = jnp.where(nxt == N_EXPERTS, -1, nxt).astype(I32)
    pad_start = tile_start * t + count
    pad_len = ntile * t - count
    tail = jnp.stack([nused * t, (NT_MOE - nused) * (t // ZERO_ROWS)])
    zinfo = jnp.concatenate([pad_start, pad_len, tail]).astype(I32)
    return pos.astype(I32), zinfo, (te, first, nxt, nused.reshape(1).astype(I32))


def _dispatch_kernel(p0_ref, p1_ref, z_ref, h_ref, xs_ref, zero_ref, sem, zsem):
    t = h_ref.shape[0]
    i = pl.program_id(0)
    base = i * t

    def clear_padding(start):
        def go(n, off):
            cp = pltpu.make_async_copy(zero_ref.at[pl.ds(0, n), :], xs_ref.at[pl.ds(off, n), :], zsem.at[0])
            cp.start() if start else cp.wait()

        for e in range(N_EXPERTS):
            off, ln = z_ref[e], z_ref[N_EXPERTS + e]
            end = off + ln
            for b in range(SUBLANE_BITS, PAD_BITS):
                @pl.when(((ln >> b) & 1) == 1)
                def _():
                    go(1 << b, pl.multiple_of(end - ((ln >> b) << b), SUBLANES))
            for k in range(SUBLANES - 1):
                @pl.when(k < (ln & (SUBLANES - 1)))
                def _():
                    go(1, off + k)
        tail0, n_tail = z_ref[2 * N_EXPERTS], z_ref[2 * N_EXPERTS + 1]

        def tail_body(k, carry):
            go(ZERO_ROWS, pl.multiple_of(tail0 + k * ZERO_ROWS, SUBLANES))
            return carry

        lax.fori_loop(0, n_tail, tail_body, 0)

    @pl.when(i == 0)
    def _():
        zero_ref[...] = jnp.zeros(zero_ref.shape, zero_ref.dtype)
        clear_padding(True)

    def issue(r, carry):
        src = h_ref.at[pl.ds(r, 1), :]
        pltpu.make_async_copy(src, xs_ref.at[pl.ds(p0_ref[base + r], 1), :], sem.at[0]).start()
        pltpu.make_async_copy(src, xs_ref.at[pl.ds(p1_ref[base + r], 1), :], sem.at[1]).start()
        return carry

    lax.fori_loop(0, t, issue, 0, unroll=8)
    pltpu.make_async_copy(h_ref, xs_ref.at[pl.ds(0, t), :], sem.at[0]).wait()
    pltpu.make_async_copy(h_ref, xs_ref.at[pl.ds(0, t), :], sem.at[1]).wait()

    @pl.when(i == 0)
    def _():
        clear_padding(False)


def _dispatch(h, pos0, pos1, zinfo):
    t = T_DISPATCH
    return pl.pallas_call(
        _dispatch_kernel,
        grid_spec=pltpu.PrefetchScalarGridSpec(
            num_scalar_prefetch=3,
            grid=(M // t,),
            in_specs=[pl.BlockSpec((t, D), lambda i, p0, p1, z: (i, 0))],
            out_specs=pl.BlockSpec(memory_space=pl.ANY),
            scratch_shapes=[pltpu.VMEM((ZERO_ROWS, D), F32),
                            pltpu.SemaphoreType.DMA((2,)), pltpu.SemaphoreType.DMA((1,))]),
        out_shape=jax.ShapeDtypeStruct((P_MOE, D), F32),
        compiler_params=_params(1),
        name="moe_dispatch",
    )(pos0, pos1, zinfo, h)


def _expert_weight_stream(w_refs, stage_refs, bf_refs, sem, te_ref, first_ref, nxt_ref, tn):
    c = pl.program_id(0)
    j = pl.program_id(1)
    nc = pl.num_programs(0)

    def copies(e, cc):
        col = pl.multiple_of(cc * tn, LANES)
        return [pltpu.make_async_copy(w.at[e, :, pl.ds(col, tn)], st, sem.at[k])
                for k, (w, st) in enumerate(zip(w_refs, stage_refs))]

    def start(e, cc):
        for cp in copies(e, cc):
            cp.start()

    @pl.when((c == 0) & (j == 0))
    def _():
        start(te_ref[0], 0)

    @pl.when(first_ref[j] == 1)
    def _():
        for cp in copies(0, 0):
            cp.wait()
        for st, bf in zip(stage_refs, bf_refs):
            _cast_rows(st, bf)
        ne = nxt_ref[j]

        @pl.when(ne >= 0)
        def _():
            start(ne, c)

        @pl.when((ne < 0) & (c + 1 < nc))
        def _():
            start(te_ref[0], c + 1)


def _gmm_gate_up_kernel(te_ref, first_ref, nxt_ref, nused_ref, xs_ref, wg_ref, wu_ref, o_ref,
                        sg_ref, su_ref, wgb_ref, wub_ref, sem):
    _expert_weight_stream((wg_ref, wu_ref), (sg_ref, su_ref), (wgb_ref, wub_ref), sem,
                          te_ref, first_ref, nxt_ref, TN_GU)

    @pl.when(pl.program_id(1) < nused_ref[0])
    def _():
        half = T_MOE // 2
        for r in range(2):
            rs = slice(r * half, (r + 1) * half)
            x = xs_ref[rs, :].astype(BF16)
            a = jnp.dot(x, wgb_ref[...], preferred_element_type=F32)
            b = jnp.dot(x, wub_ref[...], preferred_element_type=F32)
            o_ref[rs, :] = _swiglu(a, b).astype(BF16)

    @pl.when(pl.program_id(1) >= nused_ref[0])
    def _():
        o_ref[...] = jnp.zeros(o_ref.shape, o_ref.dtype)


def _gmm_down_kernel(te_ref, first_ref, nxt_ref, nused_ref, a_ref, wd_ref, o_ref,
                     sd_ref, wdb_ref, sem):
    _expert_weight_stream((wd_ref,), (sd_ref,), (wdb_ref,), sem, te_ref, first_ref, nxt_ref, TN_DN)

    @pl.when(pl.program_id(1) < nused_ref[0])
    def _():
        o_ref[...] = jnp.dot(a_ref[...], wdb_ref[...], preferred_element_type=F32)

    @pl.when(pl.program_id(1) >= nused_ref[0])
    def _():
        o_ref[...] = jnp.zeros(o_ref.shape, o_ref.dtype)


def _used_tile(j, nused):
    return jnp.minimum(j, nused[0] - 1)


def _gmm_gate_up(xs, w_gate, w_up, meta):
    te, first, nxt, nused = meta
    tn = TN_GU
    return pl.pallas_call(
        _gmm_gate_up_kernel,
        grid_spec=pltpu.PrefetchScalarGridSpec(
            num_scalar_prefetch=4,
            grid=(D_FF_EXPERT // tn, NT_MOE),
            in_specs=[pl.BlockSpec((T_MOE, D), lambda c, j, te, fi, nx, nu: (_used_tile(j, nu), 0)),
                      pl.BlockSpec(memory_space=pl.ANY),
                      pl.BlockSpec(memory_space=pl.ANY)],
            out_specs=pl.BlockSpec((T_MOE, tn), lambda c, j, te, fi, nx, nu: (j, c)),
            scratch_shapes=[pltpu.VMEM((D, tn), F32), pltpu.VMEM((D, tn), F32),
                            pltpu.VMEM((D, tn), BF16), pltpu.VMEM((D, tn), BF16),
                            pltpu.SemaphoreType.DMA((2,))]),
        out_shape=jax.ShapeDtypeStruct((P_MOE, D_FF_EXPERT), BF16),
        compiler_params=_params(2),
        name="moe_gate_up",
    )(te, first, nxt, nused, xs, w_gate, w_up)


def _gmm_down(act, w_down, meta):
    te, first, nxt, nused = meta
    tn = TN_DN
    return pl.pallas_call(
        _gmm_down_kernel,
        grid_spec=pltpu.PrefetchScalarGridSpec(
            num_scalar_prefetch=4,
            grid=(D // tn, NT_MOE),
            in_specs=[pl.BlockSpec((T_MOE, D_FF_EXPERT), lambda c, j, te, fi, nx, nu: (_used_tile(j, nu), 0)),
                      pl.BlockSpec(memory_space=pl.ANY)],
            out_specs=pl.BlockSpec((T_MOE, tn), lambda c, j, te, fi, nx, nu: (j, c)),
            scratch_shapes=[pltpu.VMEM((D_FF_EXPERT, tn), F32), pltpu.VMEM((D_FF_EXPERT, tn), BF16),
                            pltpu.SemaphoreType.DMA((1,))]),
        out_shape=jax.ShapeDtypeStruct((P_MOE, D), F32),
        compiler_params=_params(2),
        name="moe_down",
    )(te, first, nxt, nused, act, w_down)


def _combine_kernel(p0_ref, p1_ref, ys_ref, xa_ref, xb_ref, g_ref, w_ref, oc_ref, ol_ref,
                    a_ref, b_ref, sem):
    t = xa_ref.shape[0]
    i = pl.program_id(0)
    n = pl.num_programs(0)

    def issue(step, slot):
        base = step * t

        def body(r, carry):
            pltpu.make_async_copy(ys_ref.at[pl.ds(p0_ref[base + r], 1), :],
                                  a_ref.at[slot, pl.ds(r, 1), :], sem.at[0, slot]).start()
            pltpu.make_async_copy(ys_ref.at[pl.ds(p1_ref[base + r], 1), :],
                                  b_ref.at[slot, pl.ds(r, 1), :], sem.at[1, slot]).start()
            return carry

        lax.fori_loop(0, t, body, 0, unroll=8)

    @pl.when(i == 0)
    def _():
        issue(0, 0)

    @pl.when(i + 1 < n)
    def _():
        issue(i + 1, (i + 1) % 2)

    slot = i % 2
    pltpu.make_async_copy(ys_ref.at[pl.ds(0, t), :], a_ref.at[slot], sem.at[0, slot]).wait()
    pltpu.make_async_copy(ys_ref.at[pl.ds(0, t), :], b_ref.at[slot], sem.at[1, slot]).wait()
    w = w_ref[...]
    moe = w[:, 0:1] * a_ref[slot] + w[:, 1:2] * b_ref[slot]
    y = _pick(xa_ref, xb_ref, 0) + g_ref[...] * moe
    is_ctx = _is_ctx_tile(i, t)

    @pl.when(is_ctx)
    def _():
        oc_ref[...] = y

    @pl.when(jnp.logical_not(is_ctx))
    def _():
        ol_ref[...] = y


def _combine(ys, x, mod, layer, gate_chunk, wts, pos0, pos1):
    t = T_COMBINE
    n_ctx = MP // t
    row = lambda i, *_: i
    x_args, x_specs = _stream_in(x, t, D, row, lambda i, *_: 0)
    return pl.pallas_call(
        _combine_kernel,
        grid_spec=pltpu.PrefetchScalarGridSpec(
            num_scalar_prefetch=2,
            grid=(M // t,),
            in_specs=[pl.BlockSpec(memory_space=pl.ANY)] + x_specs + [
                _mod_spec(layer, gate_chunk, t, row),
                pl.BlockSpec((t, LANES), lambda i, p0, p1: (i, 0))],
            out_specs=[pl.BlockSpec((t, D), lambda i, p0, p1: (jnp.minimum(i, n_ctx - 1), 0)),
                       pl.BlockSpec((t, D), lambda i, p0, p1: (jnp.maximum(i - n_ctx, 0), 0))],
            scratch_shapes=[pltpu.VMEM((2, t, D), F32), pltpu.VMEM((2, t, D), F32),
                            pltpu.SemaphoreType.DMA((2, 2))]),
        out_shape=[jax.ShapeDtypeStruct((MP, D), F32), jax.ShapeDtypeStruct((MS, D), F32)],
        compiler_params=_params(1),
        name="moe_combine",
    )(pos0, pos1, ys, *x_args, mod, wts)


def _moe(x, h, idx, wts, mod, layer, w_gate, w_up, w_down):
    pos, zinfo, meta = _route_meta(idx[:, :TOP_K])
    pos0, pos1 = pos[:, 0], pos[:, 1]
    xs = _dispatch(h, pos0, pos1, zinfo)
    act = _gmm_gate_up(xs, w_gate, w_up, meta)
    ys = _gmm_down(act, w_down, meta)
    return _combine(ys, x, mod, layer, 5, wts, pos0, pos1)


def kernel(x_prompt, x_sample, cache_k, cache_v, c, c_ctx, w_ada, b_ada, norm1_g, norm2_g, w_in, q_norm_g, k_norm_g, sgu_norm_g, w_spatial, b_spatial, out_norm_g, w_out, ffn_w_gate, ffn_w_up, ffn_w_down, w_router, b_router, moe_w_gate, moe_w_up, moe_w_down):
    assert DEPTH == 2
    x = (x_prompt.reshape(MP, D), x_sample.reshape(MS, D))
    cond = jnp.concatenate([c_ctx[None, :], c, jnp.zeros((N_COND - 1 - DEC_BATCH, D), F32)], axis=0)
    mod = _modulation(cond, w_ada, b_ada).reshape(DEPTH, N_COND, 1, N_MOD * D)
    cos, sin = _rope_tables()
    w_in_bf = _cast_in_weights(w_in)

    new_k, new_v = [], []
    for i in range(DEPTH):
        q, kf, kb, vf, vb, u, gh = _in_projections(x, w_in_bf, mod, norm1_g, q_norm_g, k_norm_g,
                                                   sgu_norm_g, cos, sin, i)
        attn_ctx = _attention(q, kb, vb, None, None, i, batch=BATCH, seq=SEQ, row0=0)
        attn_lat = _attention(q, kb, vb, cache_k, cache_v, i, batch=DEC_BATCH, seq=DEC_SEQ, row0=MP)
        o = _sgu_merge(u, gh, attn_ctx, attn_lat, w_spatial, b_spatial, out_norm_g, i)
        j = i // 2
        if i % 2 == 0:
            x, h2 = _out_proj(o, w_out, x, mod, norm2_g, i)
            act = _ffn_gate_up(h2, ffn_w_gate, ffn_w_up, j)
            x = _mm_resid(act, ffn_w_down, x, mod, i, j, 5, 512, 512)
        else:
            x, h2, idx, wts = _out_proj(o, w_out, x, mod, norm2_g, i, router=(w_router[j], b_router[j]))
            x = _moe(x, h2, idx, wts, mod, i, moe_w_gate[j], moe_w_up[j], moe_w_down[j])
        new_k.append(kf[:MP].reshape(BATCH, SEQ, N_KV_HEADS, HEAD_DIM))
        new_v.append(vf[:MP].reshape(BATCH, SEQ, N_KV_HEADS, HEAD_DIM))

    y_prompt = x[0].reshape(BATCH, SEQ, D)
    y_sample = x[1].reshape(DEC_BATCH, DEC_SEQ, D)
    return (y_prompt, y_sample, jnp.stack(new_k, axis=1), jnp.stack(new_v, axis=1))
```

```python
import functools

import jax
import jax.numpy as jnp
from jax import lax
from jax.experimental import pallas as pl
from jax.experimental.pallas import tpu as pltpu

F32 = jnp.float32
BF16 = jnp.bfloat16
I32 = jnp.int32

D = 2048
BATCH, SEQ = 16, 256
DEC_BATCH, DEC_SEQ = 4, 2048
PAST_LEN = 256
DEPTH = 2
GRID_W = 64
CHUNK = 128
HEAD_DIM = 128
N_Q_HEADS, N_KV_HEADS = 8, 2
Q_PER_KV = N_Q_HEADS // N_KV_HEADS
ATTN_WIDTH = N_Q_HEADS * HEAD_DIM
KV_WIDTH = N_KV_HEADS * HEAD_DIM
N_SGU_HEADS = 8
SGU_WIDTH = N_SGU_HEADS * HEAD_DIM
IN_WIDTH = ATTN_WIDTH + 2 * KV_WIDTH + 2 * SGU_WIDTH
ROPE_THETA = 10000.0
ROPE_AXIS_DIM = HEAD_DIM // 2
D_FF = 5632
N_EXPERTS = 8
TOP_K = 2
D_FF_EXPERT = 2816
N_MOD = 6
EPS = 1e-6
ATTN_SCALE = HEAD_DIM ** -0.5
LOG2_E = 1.4426950408889634

MP = BATCH * SEQ
MS = DEC_BATCH * DEC_SEQ
M = MP + MS
N_COND = 8
LANES = 128
SUBLANES = 8
SUBLANE_BITS = 3

VMEM_LIMIT = 56 * 1024 * 1024

TM = 1024
TN_IN = 2 * KV_WIDTH
T_NORM = 512
T_Q = 512
ATTN_ROWS = 1024
T_MOE = 512
P_MOE = M * TOP_K + N_EXPERTS * T_MOE
NT_MOE = P_MOE // T_MOE
TN_GU = D_FF_EXPERT // 2
TN_DN = D // 2
T_DISPATCH = 1024
T_COMBINE = 256
ZERO_ROWS = T_MOE // 2
PAD_BITS = ZERO_ROWS.bit_length()


def _params(n_axes):
    return pltpu.CompilerParams(dimension_semantics=("arbitrary",) * n_axes,
                                vmem_limit_bytes=VMEM_LIMIT)


def _cond_row(i, t):
    return jnp.where(i < MP // t, 0, 1 + (i - MP // t) // (DEC_SEQ // t))


def _is_ctx_tile(i, t):
    return i < MP // t


def _stream_in(x, t, width, row_of, col_of):
    n_ctx = MP // t
    pair = isinstance(x, tuple)
    base = 0 if pair else n_ctx
    ctx = pl.BlockSpec((t, width), lambda *g: (jnp.minimum(row_of(*g), n_ctx - 1), col_of(*g)))
    lat = pl.BlockSpec((t, width), lambda *g: (base + jnp.maximum(row_of(*g) - n_ctx, 0), col_of(*g)))
    return (list(x) if pair else [x, x]), [ctx, lat]


def _mod_spec(layer, chunk, t, row_of, col_of=None, tn=D):
    per = D // tn

    def index_map(*g):
        col = chunk * per + (col_of(*g) if col_of is not None else 0)
        return (layer, _cond_row(row_of(*g), t), 0, col)

    return pl.BlockSpec((None, None, 1, tn), index_map)


def _ada_kernel(c_ref, w_ref, b_ref, o_ref):
    c = c_ref[...]
    s = (c * jax.nn.sigmoid(c)).astype(BF16)
    o_ref[...] = jnp.dot(s, w_ref[...].astype(BF16), preferred_element_type=F32) + b_ref[...]


def _modulation(cond, w_ada, b_ada):
    tn = 1024
    width = N_MOD * D
    return pl.pallas_call(
        _ada_kernel,
        grid=(DEPTH, width // tn),
        in_specs=[pl.BlockSpec((N_COND, D), lambda l, n: (0, 0)),
                  pl.BlockSpec((None, D, tn), lambda l, n: (l, 0, n)),
                  pl.BlockSpec((None, 1, tn), lambda l, n: (l, 0, n))],
        out_specs=pl.BlockSpec((None, N_COND, tn), lambda l, n: (l, 0, n)),
        out_shape=jax.ShapeDtypeStruct((DEPTH, N_COND, width), F32),
        compiler_params=_params(2),
        name="modulation",
    )(cond, w_ada, b_ada.reshape(DEPTH, 1, width))


def _modulated_norm(x, g, sc, sh):
    y = x * lax.rsqrt(jnp.mean(x * x, axis=-1, keepdims=True) + EPS)
    return (y * g) * (1.0 + sc) + sh


def _pick(xa_ref, xb_ref, axis):
    t = xa_ref.shape[0]
    return jnp.where(_is_ctx_tile(pl.program_id(axis), t), xa_ref[...], xb_ref[...])


def _stream_tile_copy(xa_ref, xb_ref, lat_row0, buf_ref, sem, tile, slot, start):
    t = buf_ref.shape[1]

    def copy(src_ref, row):
        return pltpu.make_async_copy(src_ref.at[pl.ds(pl.multiple_of(row, t), t), :],
                                     buf_ref.at[slot], sem.at[slot])

    if not start:
        copy(xa_ref, 0).wait()
        return
    is_ctx = _is_ctx_tile(tile, t)

    @pl.when(is_ctx)
    def _():
        copy(xa_ref, tile * t).start()

    @pl.when(jnp.logical_not(is_ctx))
    def _():
        copy(xb_ref, lat_row0 + (tile - MP // t) * t).start()


def _next_stream_tile(xa_ref, xb_ref, lat_row0, buf_ref, sem, tile, n_tiles):
    @pl.when(tile == 0)
    def _():
        _stream_tile_copy(xa_ref, xb_ref, lat_row0, buf_ref, sem, 0, 0, True)

    @pl.when(tile + 1 < n_tiles)
    def _():
        _stream_tile_copy(xa_ref, xb_ref, lat_row0, buf_ref, sem, tile + 1, (tile + 1) % 2, True)

    slot = tile % 2
    _stream_tile_copy(xa_ref, xb_ref, lat_row0, buf_ref, sem, tile, slot, False)
    return slot


def _stream_hbm(x):
    return (x[0], x[1], 0) if isinstance(x, tuple) else (x, x, MP)


def _split_bf16(a):
    hi = a.astype(BF16)
    return hi, (a - hi.astype(F32)).astype(BF16)


def _route_top2(h, wr, br, idx_ref, wt_ref):
    h_hi, h_lo = _split_bf16(h)
    w_hi, w_lo = _split_bf16(wr)
    logits = (jnp.dot(h_hi, w_hi, preferred_element_type=F32)
              + jnp.dot(h_lo, w_hi, preferred_element_type=F32)
              + jnp.dot(h_hi, w_lo, preferred_element_type=F32)) + br
    lane = lax.broadcasted_iota(I32, logits.shape, 1)
    neg = jnp.float32(-jnp.inf)
    lg = jnp.where(lane < N_EXPERTS, logits, neg)
    m1 = jnp.max(lg, axis=-1, keepdims=True)
    i1 = jnp.min(jnp.where(lg == m1, lane, LANES), axis=-1, keepdims=True)
    lg2 = jnp.where(lane == i1, neg, lg)
    m2 = jnp.max(lg2, axis=-1, keepdims=True)
    i2 = jnp.min(jnp.where(lg2 == m2, lane, LANES), axis=-1, keepdims=True)
    e = jnp.exp(m2 - m1)
    w1 = 1.0 / (1.0 + e)
    w2 = e / (1.0 + e)
    idx_ref[...] = jnp.where(lane == 0, i1, jnp.where(lane == 1, i2, 0))
    wt_ref[...] = jnp.where(lane == 0, w1, jnp.where(lane == 1, w2, 0.0))


W_PIECE = 512


def _out_proj_kernel(*refs, layer, lat_row0, route):
    if route:
        (o_ref, w_ref, xa_ref, xb_ref, gate_ref, g_ref, sc_ref, sh_ref, wr_ref, br_ref,
         xn_ref, h_ref, idx_ref, wt_ref, stage_ref, wbf_ref, xbuf_ref, wsem, xsem) = refs
    else:
        (o_ref, w_ref, xa_ref, xb_ref, gate_ref, g_ref, sc_ref, sh_ref,
         xn_ref, h_ref, stage_ref, wbf_ref, xbuf_ref, wsem, xsem) = refs
    i = pl.program_id(0)

    @pl.when(i == 0)
    def _():
        for p in range(D // W_PIECE):
            cols = pl.ds(p * W_PIECE, W_PIECE)
            cp = pltpu.make_async_copy(w_ref.at[layer, :, cols], stage_ref, wsem.at[0])
            cp.start()
            cp.wait()
            _cast_rows(stage_ref, wbf_ref.at[:, cols])

    slot = _next_stream_tile(xa_ref, xb_ref, lat_row0, xbuf_ref, xsem, i, pl.num_programs(0))
    acc = jnp.dot(o_ref[...], wbf_ref[...], preferred_element_type=F32)
    x_new = xbuf_ref[slot] + gate_ref[...] * acc
    xn_ref[...] = x_new
    h = _modulated_norm(x_new, g_ref[...], sc_ref[...], sh_ref[...])
    if route:
        h_ref[...] = h
        _route_top2(h, wr_ref[...], br_ref[...], idx_ref, wt_ref)
    else:
        h_ref[...] = h.astype(BF16)


def _out_proj(o, w_out, x, mod, norm2_g, layer, router=None):
    t = T_NORM
    row = lambda i: i
    xa, xb, lat_row0 = _stream_hbm(x)
    route = router is not None
    anyspace = pl.BlockSpec(memory_space=pl.ANY)
    rows = lambda w: pl.BlockSpec((t, w), lambda i: (i, 0))
    in_specs = [rows(D), anyspace, anyspace, anyspace,
                _mod_spec(layer, 2, t, row),
                pl.BlockSpec((None, 1, D), lambda i: (layer, 0, 0)),
                _mod_spec(layer, 4, t, row), _mod_spec(layer, 3, t, row)]
    args = [o, w_out, xa, xb, mod, norm2_g.reshape(DEPTH, 1, D), mod, mod]
    out_specs = [rows(D), rows(D)]
    out_shape = [jax.ShapeDtypeStruct((M, D), F32), jax.ShapeDtypeStruct((M, D), F32 if route else BF16)]
    if route:
        w_router, b_router = router
        args += [jnp.zeros((D, LANES), F32).at[:, :N_EXPERTS].set(w_router),
                 jnp.zeros((1, LANES), F32).at[0, :N_EXPERTS].set(b_router)]
        in_specs += [pl.BlockSpec((D, LANES), lambda i: (0, 0)), pl.BlockSpec((1, LANES), lambda i: (0, 0))]
        out_specs += [rows(LANES), rows(LANES)]
        out_shape += [jax.ShapeDtypeStruct((M, LANES), I32), jax.ShapeDtypeStruct((M, LANES), F32)]
    return pl.pallas_call(
        functools.partial(_out_proj_kernel, layer=layer, lat_row0=lat_row0, route=route),
        grid=(M // t,),
        in_specs=in_specs,
        out_specs=out_specs,
        out_shape=out_shape,
        scratch_shapes=[pltpu.VMEM((D, W_PIECE), F32), pltpu.VMEM((D, D), BF16),
                        pltpu.VMEM((2, t, D), F32),
                        pltpu.SemaphoreType.DMA((1,)), pltpu.SemaphoreType.DMA((2,))],
        compiler_params=_params(1),
        name="out_proj_router" if route else "out_proj",
    )(*args)


CAST_ROWS = 256


def _cast_rows(src_ref, dst_ref):
    def body(r, carry):
        rs = pl.ds(pl.multiple_of(r * CAST_ROWS, CAST_ROWS), CAST_ROWS)
        dst_ref[rs, :] = src_ref[rs, :].astype(BF16)
        return carry

    lax.fori_loop(0, src_ref.shape[0] // CAST_ROWS, body, 0)


def _cast_weight_once(w_ref, wbf_ref):
    @pl.when(pl.program_id(1) == 0)
    def _():
        _cast_rows(w_ref, wbf_ref)


def _head_rms(a):
    return lax.rsqrt(jnp.mean(a * a, axis=-1, keepdims=True) + EPS)


def _rope_partner(ag, perm):
    hi, lo = _split_bf16(ag)
    return (jnp.dot(hi, perm, preferred_element_type=F32)
            + jnp.dot(lo, perm, preferred_element_type=F32))


def _cast_kernel(w_ref, o_ref):
    o_ref[...] = w_ref[...].astype(BF16)


def _cast_in_weights(w_in):
    spec = pl.BlockSpec((None, D, TN_IN), lambda l, n: (l, 0, n))
    return pl.pallas_call(
        _cast_kernel,
        grid=(DEPTH, IN_WIDTH // TN_IN),
        in_specs=[spec],
        out_specs=spec,
        out_shape=jax.ShapeDtypeStruct(w_in.shape, BF16),
        compiler_params=_params(2),
        name="cast_w_in",
    )(w_in)


N_Q_TILES = ATTN_WIDTH // TN_IN
KV_TILE = N_Q_TILES
U_TILE0 = KV_TILE + 1
G_TILE0 = U_TILE0 + SGU_WIDTH // TN_IN
N_IN_TILES = IN_WIDTH // TN_IN


def _in_proj_kernel(xa_ref, xb_ref, n1_ref, sc0_ref, sh0_ref, sc1_ref, sh1_ref, w_ref,
                    qg_ref, kg_ref, sg_ref, pq_ref, pk_ref, cos_ref, sin_ref,
                    q_ref, kf_ref, kb_ref, vf_ref, vb_ref, u_ref, gh_ref,
                    xbuf_ref, h_ref, xsem, *, lat_row0):
    m = pl.program_id(0)
    n = pl.program_id(1)
    nm = pl.num_programs(0)
    cur = m % 2
    nxt = (m + 1) % 2
    has_next = m + 1 < nm
    fetch = functools.partial(_stream_tile_copy, xa_ref, xb_ref, lat_row0, xbuf_ref, xsem)

    def norm_rows(slot, rows, sc_ref, sh_ref):
        x = xbuf_ref[slot, rows, :]
        h_ref[slot, rows, :] = _modulated_norm(x, n1_ref[...], sc_ref[...], sh_ref[...]).astype(BF16)

    @pl.when((m == 0) & (n == 0))
    def _():
        fetch(0, 0, True)
        fetch(0, 0, False)
        norm_rows(0, slice(None), sc0_ref, sh0_ref)

    @pl.when((n == 0) & has_next)
    def _():
        fetch(m + 1, nxt, True)

    def matmul():
        return jnp.dot(h_ref[cur], w_ref[...], preferred_element_type=F32)

    @pl.when(n < N_Q_TILES)
    def _():
        acc = matmul()
        ag = acc * qg_ref[...]
        partner = _rope_partner(ag, pq_ref[...])
        cos, sin = cos_ref[...], sin_ref[...]
        for h in range(TN_IN // HEAD_DIM):
            sl = slice(h * HEAD_DIM, (h + 1) * HEAD_DIM)
            r = _head_rms(acc[:, sl]) * (ATTN_SCALE * LOG2_E)
            q_ref[:, sl] = ((ag[:, sl] * cos + partner[:, sl] * sin) * r).astype(BF16)

    @pl.when(n == KV_TILE)
    def _():
        acc = matmul()
        k = acc[:, :KV_WIDTH]
        ag = k * kg_ref[...]
        partner = _rope_partner(ag, pk_ref[...])
        cos, sin = cos_ref[...], sin_ref[...]
        for h in range(N_KV_HEADS):
            sl = slice(h * HEAD_DIM, (h + 1) * HEAD_DIM)
            r = _head_rms(k[:, sl])
            kf_ref[:, sl] = ag[:, sl] * r
            kb_ref[:, sl] = ((ag[:, sl] * cos + partner[:, sl] * sin) * r).astype(BF16)
        v = acc[:, KV_WIDTH:]
        vf_ref[...] = v
        vb_ref[...] = v.astype(BF16)

    half = TM // (G_TILE0 - U_TILE0)
    for k in range(G_TILE0 - U_TILE0):
        for parity in range(2):
            @pl.when((n == U_TILE0 + k) & has_next & (cur == parity))
            def _():
                if k == 0:
                    fetch(m + 1, 1 - parity, False)
                u_ref[...] = jnp.dot(h_ref[parity], w_ref[...], preferred_element_type=F32).astype(BF16)
                norm_rows(1 - parity, slice(k * half, (k + 1) * half), sc1_ref, sh1_ref)

        @pl.when((n == U_TILE0 + k) & jnp.logical_not(has_next))
        def _():
            u_ref[...] = matmul().astype(BF16)

    @pl.when(n >= G_TILE0)
    def _():
        acc = matmul()
        for h in range(TN_IN // HEAD_DIM):
            sl = slice(h * HEAD_DIM, (h + 1) * HEAD_DIM)
            a = acc[:, sl]
            gh_ref[:, sl] = (a * _head_rms(a) * sg_ref[:, sl]).astype(BF16)


def _rope_tables():
    n_rows = DEC_SEQ // GRID_W
    rows = jnp.broadcast_to(jnp.arange(n_rows)[:, None], (n_rows, GRID_W)).reshape(-1)
    cols = jnp.broadcast_to(jnp.arange(GRID_W)[None, :], (n_rows, GRID_W)).reshape(-1)
    inv = ROPE_THETA ** (-jnp.arange(0, ROPE_AXIS_DIM, 2, dtype=F32) / ROPE_AXIS_DIM)
    ang_r = rows.astype(F32)[:, None] * inv
    ang_c = cols.astype(F32)[:, None] * inv
    cos = jnp.concatenate([jnp.cos(ang_r), jnp.cos(ang_r), jnp.cos(ang_c), jnp.cos(ang_c)], axis=1)
    sin = jnp.concatenate([-jnp.sin(ang_r), jnp.sin(ang_r), -jnp.sin(ang_c), jnp.sin(ang_c)], axis=1)
    cos = jnp.concatenate([jnp.ones((TM, HEAD_DIM), F32), cos], axis=0)
    sin = jnp.concatenate([jnp.zeros((TM, HEAD_DIM), F32), sin], axis=0)
    return cos, sin


def _partner_matrix(n_heads):
    w = n_heads * HEAD_DIM
    quarter = ROPE_AXIS_DIM // 2
    j = jnp.arange(w)
    partner = jnp.where((j % ROPE_AXIS_DIM) < quarter, j + quarter, j - quarter)
    return (jnp.arange(w)[:, None] == partner[None, :]).astype(BF16)


def _rope_block(m):
    return jnp.where(m < MP // TM, 0, 1 + (m - MP // TM) % (DEC_SEQ // TM))


def _in_projections(x, w_in_bf, mod, norm1_g, q_norm_g, k_norm_g, sgu_norm_g, cos, sin, layer):
    tn = TN_IN
    xa, xb, lat_row0 = _stream_hbm(x)
    row = lambda m, n: m
    next_row = lambda m, n: jnp.minimum(m + 1, M // TM - 1)
    anyspace = pl.BlockSpec(memory_space=pl.ANY)
    const = lambda shape: pl.BlockSpec(shape, lambda m, n: (0,) * len(shape))
    rope_spec = pl.BlockSpec((TM, HEAD_DIM), lambda m, n: (_rope_block(m), 0))
    q_heads = tn // HEAD_DIM
    q_gain = jnp.tile(q_norm_g[layer], q_heads)[None, :]
    k_gain = jnp.tile(k_norm_g[layer], N_KV_HEADS)[None, :]
    g_tile = lambda n: jnp.clip(n - G_TILE0, 0, SGU_WIDTH // tn - 1)
    kv_out = pl.BlockSpec((TM, KV_WIDTH), lambda m, n: (m, 0))
    kv_shape = lambda dt: jax.ShapeDtypeStruct((M, KV_WIDTH), dt)
    return pl.pallas_call(
        functools.partial(_in_proj_kernel, lat_row0=lat_row0),
        grid=(M // TM, N_IN_TILES),
        in_specs=[anyspace, anyspace,
                  pl.BlockSpec((None, 1, D), lambda m, n: (layer, 0, 0)),
                  _mod_spec(layer, 1, TM, row), _mod_spec(layer, 0, TM, row),
                  _mod_spec(layer, 1, TM, next_row), _mod_spec(layer, 0, TM, next_row),
                  pl.BlockSpec((None, D, tn), lambda m, n: (layer, 0, n)),
                  const((1, tn)), const((1, KV_WIDTH)),
                  pl.BlockSpec((None, 1, tn), lambda m, n: (layer, 0, g_tile(n))),
                  const((tn, tn)), const((KV_WIDTH, KV_WIDTH)), rope_spec, rope_spec],
        out_specs=[pl.BlockSpec((TM, tn), lambda m, n: (m, jnp.minimum(n, N_Q_TILES - 1))),
                   kv_out, kv_out, kv_out, kv_out,
                   pl.BlockSpec((TM, tn), lambda m, n: (m, jnp.clip(n - U_TILE0, 0, SGU_WIDTH // tn - 1))),
                   pl.BlockSpec((TM, tn), lambda m, n: (m, g_tile(n)))],
        out_shape=[jax.ShapeDtypeStruct((M, ATTN_WIDTH), BF16),
                   kv_shape(F32), kv_shape(BF16), kv_shape(F32), kv_shape(BF16),
                   jax.ShapeDtypeStruct((M, SGU_WIDTH), BF16),
                   jax.ShapeDtypeStruct((M, SGU_WIDTH), BF16)],
        scratch_shapes=[pltpu.VMEM((2, TM, D), F32), pltpu.VMEM((2, TM, D), BF16),
                        pltpu.SemaphoreType.DMA((2,))],
        compiler_params=_params(2),
        name="in_proj",
    )(xa, xb, norm1_g.reshape(DEPTH, 1, D), mod, mod, mod, mod, w_in_bf, q_gain, k_gain,
      sgu_norm_g.reshape(DEPTH, 1, SGU_WIDTH), _partner_matrix(q_heads), _partner_matrix(N_KV_HEADS),
      cos, sin)


def _qk(q, k):
    return lax.dot_general(q, k, (((1,), (1,)), ((), ())), preferred_element_type=F32)


def _attn_kernel(*refs, has_cache, n_batch, seq, tq):
    def with_ones(v):
        return jnp.concatenate([v, jnp.ones_like(v)], axis=1)

    if has_cache:
        q_ref, k_ref, v_ref, kc_ref, vc_ref, o_ref = refs
    else:
        q_ref, k_ref, v_ref, o_ref = refs
    for b in range(n_batch):
        rows_q = slice(b * tq, (b + 1) * tq)
        rows_k = slice(b * seq, (b + 1) * seq)
        for kv in range(N_KV_HEADS):
            kv_cols = slice(kv * HEAD_DIM, (kv + 1) * HEAD_DIM)
            k = k_ref[rows_k, kv_cols]
            v = with_ones(v_ref[rows_k, kv_cols])
            if has_cache:
                kc = kc_ref[:, kv_cols].astype(BF16)
                vc = with_ones(vc_ref[:, kv_cols].astype(BF16))
            for g in range(Q_PER_KV):
                head = kv * Q_PER_KV + g
                sl = slice(head * HEAD_DIM, (head + 1) * HEAD_DIM)
                q = q_ref[rows_q, sl]
                s = _qk(q, k)
                m = jnp.max(s, axis=-1, keepdims=True)
                if has_cache:
                    sc = _qk(q, kc)
                    m = jnp.maximum(m, jnp.max(sc, axis=-1, keepdims=True))
                o = jnp.dot(jnp.exp2(s - m).astype(BF16), v, preferred_element_type=F32)
                if has_cache:
                    o = o + jnp.dot(jnp.exp2(sc - m).astype(BF16), vc, preferred_element_type=F32)
                o_ref[rows_q, sl] = (o[:, :HEAD_DIM] / o[:, HEAD_DIM:]).astype(BF16)


def _attention(q, kb, vb, cache_k, cache_v, layer, *, batch, seq, row0):
    has_cache = cache_k is not None
    tq = min(T_Q, seq)
    nq = seq // tq
    n_batch = max(1, ATTN_ROWS // seq) if nq == 1 else 1
    q_spec = pl.BlockSpec((n_batch * tq, ATTN_WIDTH), lambda b, i: (row0 // (n_batch * tq) + b * nq + i, 0))
    kv_spec = pl.BlockSpec((n_batch * seq, KV_WIDTH), lambda b, i: (row0 // (n_batch * seq) + b, 0))
    in_specs = [q_spec, kv_spec, kv_spec]
    args = [q, kb, vb]
    if has_cache:
        c_spec = pl.BlockSpec((None, None, PAST_LEN, KV_WIDTH), lambda b, i: (b, layer, 0, 0))
        in_specs += [c_spec, c_spec]
        args += [cache_k.reshape(DEC_BATCH, DEPTH, PAST_LEN, KV_WIDTH),
                 cache_v.reshape(DEC_BATCH, DEPTH, PAST_LEN, KV_WIDTH)]
    return pl.pallas_call(
        functools.partial(_attn_kernel, has_cache=has_cache, n_batch=n_batch, seq=seq, tq=tq),
        grid=(batch // n_batch, nq),
        in_specs=in_specs,
        out_specs=pl.BlockSpec((n_batch * tq, ATTN_WIDTH), lambda b, i: (b * nq + i, 0)),
        out_shape=jax.ShapeDtypeStruct((batch * seq, ATTN_WIDTH), BF16),
        compiler_params=_params(2),
        name="attention_cached" if has_cache else "attention",
    )(*args)


def _sgu_merge_kernel(u_ref, gh_ref, ap_ref, as_ref, ws_ref, bs_ref, gn_ref, o_ref, sgu_ref):
    t = u_ref.shape[0]
    a = _pick(ap_ref, as_ref, 0).astype(F32)
    a = a * lax.rsqrt(jnp.mean(a * a, axis=-1, keepdims=True) + EPS) * gn_ref[:, :ATTN_WIDTH]
    o_ref[:, :ATTN_WIDTH] = a.astype(BF16)
    for h in range(N_SGU_HEADS):
        cs = slice(h * HEAD_DIM, (h + 1) * HEAD_DIM)
        w = ws_ref[h].astype(BF16)
        b = bs_ref[h]
        for c in range(t // CHUNK):
            rs = slice(c * CHUNK, (c + 1) * CHUNK)
            mixed = jnp.dot(w, gh_ref[rs, cs], preferred_element_type=F32) + b
            sgu_ref[rs, cs] = u_ref[rs, cs].astype(F32) * mixed
    s = sgu_ref[...]
    s = s * lax.rsqrt(jnp.mean(s * s, axis=-1, keepdims=True) + EPS) * gn_ref[:, ATTN_WIDTH:]
    o_ref[:, ATTN_WIDTH:] = s.astype(BF16)


def _sgu_merge(u, gh, attn_ctx, attn_lat, w_spatial, b_spatial, out_norm_g, layer):
    t = T_NORM
    bias = jnp.broadcast_to(b_spatial[:, :, :, None], (DEPTH, N_SGU_HEADS, CHUNK, HEAD_DIM))
    row = lambda w: pl.BlockSpec((t, w), lambda i: (i, 0))
    a_args, a_specs = _stream_in((attn_ctx, attn_lat), t, ATTN_WIDTH, lambda i: i, lambda i: 0)
    return pl.pallas_call(
        _sgu_merge_kernel,
        grid=(M // t,),
        in_specs=[row(SGU_WIDTH), row(SGU_WIDTH)] + a_specs + [
            pl.BlockSpec((None, N_SGU_HEADS, CHUNK, CHUNK), lambda i: (layer, 0, 0, 0)),
            pl.BlockSpec((None, N_SGU_HEADS, CHUNK, HEAD_DIM), lambda i: (layer, 0, 0, 0)),
            pl.BlockSpec((None, 1, D), lambda i: (layer, 0, 0))],
        out_specs=row(D),
        out_shape=jax.ShapeDtypeStruct((M, D), BF16),
        scratch_shapes=[pltpu.VMEM((t, SGU_WIDTH), F32)],
        compiler_params=_params(1),
        name="sgu_merge",
    )(u, gh, *a_args, w_spatial, bias, out_norm_g.reshape(DEPTH, 1, D))


def _mm_resid_kernel(a_ref, w_ref, xa_ref, xb_ref, g_ref, o_ref, wbf_ref):
    _cast_weight_once(w_ref, wbf_ref)
    acc = jnp.dot(a_ref[...], wbf_ref[...], preferred_element_type=F32)
    o_ref[...] = _pick(xa_ref, xb_ref, 1) + g_ref[...] * acc


def _mm_resid(a, w, x, mod, layer, w_index, gate_chunk, tm, tn):
    k = a.shape[1]
    row = lambda n, m: m
    col = lambda n, m: n
    x_args, x_specs = _stream_in(x, tm, tn, row, col)
    return pl.pallas_call(
        _mm_resid_kernel,
        grid=(D // tn, M // tm),
        in_specs=[pl.BlockSpec((tm, k), lambda n, m: (m, 0)),
                  pl.BlockSpec((None, k, tn), lambda n, m: (w_index, 0, n))] + x_specs + [
                  _mod_spec(layer, gate_chunk, tm, row, col, tn=tn)],
        out_specs=pl.BlockSpec((tm, tn), lambda n, m: (m, n)),
        out_shape=jax.ShapeDtypeStruct((M, D), F32),
        scratch_shapes=[pltpu.VMEM((k, tn), BF16)],
        compiler_params=_params(2),
        name="mm_resid",
    )(a, w, *x_args, mod)


def _swiglu(a, b):
    return a * jax.nn.sigmoid(a) * b


def _ffn_gu_kernel(x_ref, wg_ref, wu_ref, o_ref, wgb_ref, wub_ref):
    @pl.when(pl.program_id(1) == 0)
    def _():
        _cast_rows(wg_ref, wgb_ref)
        _cast_rows(wu_ref, wub_ref)
    x = x_ref[...]
    a = jnp.dot(x, wgb_ref[...], preferred_element_type=F32)
    b = jnp.dot(x, wub_ref[...], preferred_element_type=F32)
    o_ref[...] = _swiglu(a, b).astype(BF16)


def _ffn_gate_up(h, w_gate, w_up, j):
    tm, tn = TM, 512
    w_spec = pl.BlockSpec((None, D, tn), lambda n, m: (j, 0, n))
    return pl.pallas_call(
        _ffn_gu_kernel,
        grid=(D_FF // tn, M // tm),
        in_specs=[pl.BlockSpec((tm, D), lambda n, m: (m, 0)), w_spec, w_spec],
        out_specs=pl.BlockSpec((tm, tn), lambda n, m: (m, n)),
        out_shape=jax.ShapeDtypeStruct((M, D_FF), BF16),
        scratch_shapes=[pltpu.VMEM((D, tn), BF16), pltpu.VMEM((D, tn), BF16)],
        compiler_params=_params(2),
        name="ffn_gate_up",
    )(h, w_gate, w_up)


def _route_meta(idx):
    t = T_MOE
    experts = jnp.arange(N_EXPERTS, dtype=I32)
    onehot = (idx[:, :, None] == experts[None, None, :]).astype(I32).sum(axis=1)
    csum = jnp.cumsum(onehot, axis=0)
    rank = csum - onehot
    count = csum[-1]
    ntile = (count + t - 1) // t
    tile_end = jnp.cumsum(ntile)
    tile_start = tile_end - ntile
    nused = tile_end[-1]
    pos = (tile_start * t)[idx] + jnp.take_along_axis(rank, idx, axis=1)
    j = jnp.arange(NT_MOE, dtype=I32)
    te_raw = jnp.minimum(jnp.sum(j[:, None] >= tile_end[None, :], axis=1), N_EXPERTS - 1).astype(I32)
    te = jnp.where(j < nused, te_raw, te_raw[nused - 1])
    first = ((j == tile_start[te]) & (j < nused)).astype(I32)
    later = (ntile[None, :] > 0) & (experts[None, :] > te[:, None])
    nxt = jnp.min(jnp.where(later, experts[None, :], N_EXPERTS), axis=1)
    nxt = jnp.where(nxt == N_EXPERTS, -1, nxt).astype(I32)
    pad_start = tile_start * t + count
    pad_len = ntile * t - count
    tail = jnp.stack([nused * t, (NT_MOE - nused) * (t // ZERO_ROWS)])
    zinfo = jnp.concatenate([pad_start, pad_len, tail]).astype(I32)
    valid = jnp.clip(count[te] - (j - tile_start[te]) * t, 0, t)
    nhalf = jnp.where(j < nused, (valid + t // 2 - 1) // (t // 2), 0).astype(I32)
    return pos.astype(I32), zinfo, (te, first, nxt, nused.reshape(1).astype(I32), nhalf)


def _dispatch_kernel(p0_ref, p1_ref, z_ref, h_ref, xs_ref, zero_ref, sem, zsem):
    t = h_ref.shape[0]
    i = pl.program_id(0)
    base = i * t

    def clear_padding(start):
        def go(n, off):
            cp = pltpu.make_async_copy(zero_ref.at[pl.ds(0, n), :], xs_ref.at[pl.ds(off, n), :], zsem.at[0])
            cp.start() if start else cp.wait()

        for e in range(N_EXPERTS):
            off, ln = z_ref[e], z_ref[N_EXPERTS + e]
            end = off + ln
            for b in range(SUBLANE_BITS, PAD_BITS):
                @pl.when(((ln >> b) & 1) == 1)
                def _():
                    go(1 << b, pl.multiple_of(end - ((ln >> b) << b), SUBLANES))
            for k in range(SUBLANES - 1):
                @pl.when(k < (ln & (SUBLANES - 1)))
                def _():
                    go(1, off + k)
        tail0, n_tail = z_ref[2 * N_EXPERTS], z_ref[2 * N_EXPERTS + 1]

        def tail_body(k, carry):
            go(ZERO_ROWS, pl.multiple_of(tail0 + k * ZERO_ROWS, SUBLANES))
            return carry

        lax.fori_loop(0, n_tail, tail_body, 0)

    @pl.when(i == 0)
    def _():
        zero_ref[...] = jnp.zeros(zero_ref.shape, zero_ref.dtype)
        clear_padding(True)

    def issue(r, carry):
        src = h_ref.at[pl.ds(r, 1), :]
        pltpu.make_async_copy(src, xs_ref.at[pl.ds(p0_ref[base + r], 1), :], sem.at[0]).start()
        pltpu.make_async_copy(src, xs_ref.at[pl.ds(p1_ref[base + r], 1), :], sem.at[1]).start()
        return carry

    lax.fori_loop(0, t, issue, 0, unroll=8)
    pltpu.make_async_copy(h_ref, xs_ref.at[pl.ds(0, t), :], sem.at[0]).wait()
    pltpu.make_async_copy(h_ref, xs_ref.at[pl.ds(0, t), :], sem.at[1]).wait()

    @pl.when(i == 0)
    def _():
        clear_padding(False)


def _dispatch(h, pos0, pos1, zinfo):
    t = T_DISPATCH
    return pl.pallas_call(
        _dispatch_kernel,
        grid_spec=pltpu.PrefetchScalarGridSpec(
            num_scalar_prefetch=3,
            grid=(M // t,),
            in_specs=[pl.BlockSpec((t, D), lambda i, p0, p1, z: (i, 0))],
            out_specs=pl.BlockSpec(memory_space=pl.ANY),
            scratch_shapes=[pltpu.VMEM((ZERO_ROWS, D), F32),
                            pltpu.SemaphoreType.DMA((2,)), pltpu.SemaphoreType.DMA((1,))]),
        out_shape=jax.ShapeDtypeStruct((P_MOE, D), F32),
        compiler_params=_params(1),
        name="moe_dispatch",
    )(pos0, pos1, zinfo, h)


def _expert_weight_stream(w_refs, stage_refs, bf_refs, sem, te_ref, first_ref, nxt_ref, tn):
    c = pl.program_id(0)
    j = pl.program_id(1)
    nc = pl.num_programs(0)

    def copies(e, cc):
        col = pl.multiple_of(cc * tn, LANES)
        return [pltpu.make_async_copy(w.at[e, :, pl.ds(col, tn)], st, sem.at[k])
                for k, (w, st) in enumerate(zip(w_refs, stage_refs))]

    def start(e, cc):
        for cp in copies(e, cc):
            cp.start()

    @pl.when((c == 0) & (j == 0))
    def _():
        start(te_ref[0], 0)

    @pl.when(first_ref[j] == 1)
    def _():
        for cp in copies(0, 0):
            cp.wait()
        for st, bf in zip(stage_refs, bf_refs):
            _cast_rows(st, bf)
        ne = nxt_ref[j]

        @pl.when(ne >= 0)
        def _():
            start(ne, c)

        @pl.when((ne < 0) & (c + 1 < nc))
        def _():
            start(te_ref[0], c + 1)


def _live_halves(n_live, o_ref, rows):
    half = T_MOE // 2
    for r in range(2):
        rs = slice(r * half, (r + 1) * half)

        @pl.when(r < n_live)
        def _():
            o_ref[rs, :] = rows(rs)

        @pl.when(r >= n_live)
        def _():
            o_ref[rs, :] = jnp.zeros((half, o_ref.shape[1]), o_ref.dtype)


def _gmm_gate_up_kernel(te_ref, first_ref, nxt_ref, nused_ref, nhalf_ref, xs_ref, wg_ref, wu_ref, o_ref,
                        sg_ref, su_ref, wgb_ref, wub_ref, sem):
    _expert_weight_stream((wg_ref, wu_ref), (sg_ref, su_ref), (wgb_ref, wub_ref), sem,
                          te_ref, first_ref, nxt_ref, TN_GU)

    def rows(rs):
        x = xs_ref[rs, :].astype(BF16)
        a = jnp.dot(x, wgb_ref[...], preferred_element_type=F32)
        b = jnp.dot(x, wub_ref[...], preferred_element_type=F32)
        return _swiglu(a, b).astype(BF16)

    _live_halves(nhalf_ref[pl.program_id(1)], o_ref, rows)


def _gmm_down_kernel(te_ref, first_ref, nxt_ref, nused_ref, nhalf_ref, a_ref, wd_ref, o_ref,
                     sd_ref, wdb_ref, sem):
    _expert_weight_stream((wd_ref,), (sd_ref,), (wdb_ref,), sem, te_ref, first_ref, nxt_ref, TN_DN)

    def rows(rs):
        return jnp.dot(a_ref[rs, :], wdb_ref[...], preferred_element_type=F32)

    _live_halves(nhalf_ref[pl.program_id(1)], o_ref, rows)


def _used_tile(j, nused):
    return jnp.minimum(j, nused[0] - 1)


def _gmm_gate_up(xs, w_gate, w_up, meta):
    te, first, nxt, nused, nhalf = meta
    tn = TN_GU
    return pl.pallas_call(
        _gmm_gate_up_kernel,
        grid_spec=pltpu.PrefetchScalarGridSpec(
            num_scalar_prefetch=5,
            grid=(D_FF_EXPERT // tn, NT_MOE),
            in_specs=[pl.BlockSpec((T_MOE, D), lambda c, j, te, fi, nx, nu, nh: (_used_tile(j, nu), 0)),
                      pl.BlockSpec(memory_space=pl.ANY),
                      pl.BlockSpec(memory_space=pl.ANY)],
            out_specs=pl.BlockSpec((T_MOE, tn), lambda c, j, te, fi, nx, nu, nh: (j, c)),
            scratch_shapes=[pltpu.VMEM((D, tn), F32), pltpu.VMEM((D, tn), F32),
                            pltpu.VMEM((D, tn), BF16), pltpu.VMEM((D, tn), BF16),
                            pltpu.SemaphoreType.DMA((2,))]),
        out_shape=jax.ShapeDtypeStruct((P_MOE, D_FF_EXPERT), BF16),
        compiler_params=_params(2),
        name="moe_gate_up",
    )(te, first, nxt, nused, nhalf, xs, w_gate, w_up)


def _gmm_down(act, w_down, meta):
    te, first, nxt, nused, nhalf = meta
    tn = TN_DN
    return pl.pallas_call(
        _gmm_down_kernel,
        grid_spec=pltpu.PrefetchScalarGridSpec(
            num_scalar_prefetch=5,
            grid=(D // tn, NT_MOE),
            in_specs=[pl.BlockSpec((T_MOE, D_FF_EXPERT), lambda c, j, te, fi, nx, nu, nh: (_used_tile(j, nu), 0)),
                      pl.BlockSpec(memory_space=pl.ANY)],
            out_specs=pl.BlockSpec((T_MOE, tn), lambda c, j, te, fi, nx, nu, nh: (j, c)),
            scratch_shapes=[pltpu.VMEM((D_FF_EXPERT, tn), F32), pltpu.VMEM((D_FF_EXPERT, tn), BF16),
                            pltpu.SemaphoreType.DMA((1,))]),
        out_shape=jax.ShapeDtypeStruct((P_MOE, D), F32),
        compiler_params=_params(2),
        name="moe_down",
    )(te, first, nxt, nused, nhalf, act, w_down)


def _combine_kernel(p0_ref, p1_ref, ys_ref, xa_ref, xb_ref, g_ref, w_ref, oc_ref, ol_ref,
                    a_ref, b_ref, sem):
    t = xa_ref.shape[0]
    i = pl.program_id(0)
    n = pl.num_programs(0)

    def issue(step, slot):
        base = step * t

        def body(r, carry):
            pltpu.make_async_copy(ys_ref.at[pl.ds(p0_ref[base + r], 1), :],
                                  a_ref.at[slot, pl.ds(r, 1), :], sem.at[0, slot]).start()
            pltpu.make_async_copy(ys_ref.at[pl.ds(p1_ref[base + r], 1), :],
                                  b_ref.at[slot, pl.ds(r, 1), :], sem.at[1, slot]).start()
            return carry

        lax.fori_loop(0, t, body, 0, unroll=8)

    @pl.when(i == 0)
    def _():
        issue(0, 0)

    @pl.when(i + 1 < n)
    def _():
        issue(i + 1, (i + 1) % 2)

    slot = i % 2
    pltpu.make_async_copy(ys_ref.at[pl.ds(0, t), :], a_ref.at[slot], sem.at[0, slot]).wait()
    pltpu.make_async_copy(ys_ref.at[pl.ds(0, t), :], b_ref.at[slot], sem.at[1, slot]).wait()
    w = w_ref[...]
    moe = w[:, 0:1] * a_ref[slot] + w[:, 1:2] * b_ref[slot]
    y = _pick(xa_ref, xb_ref, 0) + g_ref[...] * moe
    is_ctx = _is_ctx_tile(i, t)

    @pl.when(is_ctx)
    def _():
        oc_ref[...] = y

    @pl.when(jnp.logical_not(is_ctx))
    def _():
        ol_ref[...] = y


def _combine(ys, x, mod, layer, gate_chunk, wts, pos0, pos1):
    t = T_COMBINE
    n_ctx = MP // t
    row = lambda i, *_: i
    x_args, x_specs = _stream_in(x, t, D, row, lambda i, *_: 0)
    return pl.pallas_call(
        _combine_kernel,
        grid_spec=pltpu.PrefetchScalarGridSpec(
            num_scalar_prefetch=2,
            grid=(M // t,),
            in_specs=[pl.BlockSpec(memory_space=pl.ANY)] + x_specs + [
                _mod_spec(layer, gate_chunk, t, row),
                pl.BlockSpec((t, LANES), lambda i, p0, p1: (i, 0))],
            out_specs=[pl.BlockSpec((t, D), lambda i, p0, p1: (jnp.minimum(i, n_ctx - 1), 0)),
                       pl.BlockSpec((t, D), lambda i, p0, p1: (jnp.maximum(i - n_ctx, 0), 0))],
            scratch_shapes=[pltpu.VMEM((2, t, D), F32), pltpu.VMEM((2, t, D), F32),
                            pltpu.SemaphoreType.DMA((2, 2))]),
        out_shape=[jax.ShapeDtypeStruct((MP, D), F32), jax.ShapeDtypeStruct((MS, D), F32)],
        compiler_params=_params(1),
        name="moe_combine",
    )(pos0, pos1, ys, *x_args, mod, wts)


def _moe(x, h, idx, wts, mod, layer, w_gate, w_up, w_down):
    pos, zinfo, meta = _route_meta(idx[:, :TOP_K])
    pos0, pos1 = pos[:, 0], pos[:, 1]
    xs = _dispatch(h, pos0, pos1, zinfo)
    act = _gmm_gate_up(xs, w_gate, w_up, meta)
    ys = _gmm_down(act, w_down, meta)
    return _combine(ys, x, mod, layer, 5, wts, pos0, pos1)


def kernel(x_prompt, x_sample, cache_k, cache_v, c, c_ctx, w_ada, b_ada, norm1_g, norm2_g, w_in, q_norm_g, k_norm_g, sgu_norm_g, w_spatial, b_spatial, out_norm_g, w_out, ffn_w_gate, ffn_w_up, ffn_w_down, w_router, b_router, moe_w_gate, moe_w_up, moe_w_down):
    assert DEPTH == 2
    x = (x_prompt.reshape(MP, D), x_sample.reshape(MS, D))
    cond = jnp.concatenate([c_ctx[None, :], c, jnp.zeros((N_COND - 1 - DEC_BATCH, D), F32)], axis=0)
    mod = _modulation(cond, w_ada, b_ada).reshape(DEPTH, N_COND, 1, N_MOD * D)
    cos, sin = _rope_tables()
    w_in_bf = _cast_in_weights(w_in)

    new_k, new_v = [], []
    for i in range(DEPTH):
        q, kf, kb, vf, vb, u, gh = _in_projections(x, w_in_bf, mod, norm1_g, q_norm_g, k_norm_g,
                                                   sgu_norm_g, cos, sin, i)
        attn_ctx = _attention(q, kb, vb, None, None, i, batch=BATCH, seq=SEQ, row0=0)
        attn_lat = _attention(q, kb, vb, cache_k, cache_v, i, batch=DEC_BATCH, seq=DEC_SEQ, row0=MP)
        o = _sgu_merge(u, gh, attn_ctx, attn_lat, w_spatial, b_spatial, out_norm_g, i)
        j = i // 2
        if i % 2 == 0:
            x, h2 = _out_proj(o, w_out, x, mod, norm2_g, i)
            act = _ffn_gate_up(h2, ffn_w_gate, ffn_w_up, j)
            x = _mm_resid(act, ffn_w_down, x, mod, i, j, 5, 512, 512)
        else:
            x, h2, idx, wts = _out_proj(o, w_out, x, mod, norm2_g, i, router=(w_router[j], b_router[j]))
            x = _moe(x, h2, idx, wts, mod, i, moe_w_gate[j], moe_w_up[j], moe_w_down[j])
        new_k.append(kf[:MP].reshape(BATCH, SEQ, N_KV_HEADS, HEAD_DIM))
        new_v.append(vf[:MP].reshape(BATCH, SEQ, N_KV_HEADS, HEAD_DIM))

    y_prompt = x[0].reshape(BATCH, SEQ, D)
    y_sample = x[1].reshape(DEC_BATCH, DEC_SEQ, D)
    return (y_prompt, y_sample, jnp.stack(new_k, axis=1), jnp.stack(new_v, axis=1))
```

```python
import functools

import jax
import jax.numpy as jnp
from jax import lax
from jax.experimental import pallas as pl
from jax.experimental.pallas import tpu as pltpu

F32 = jnp.float32
BF16 = jnp.bfloat16
I32 = jnp.int32

D = 2048
BATCH, SEQ = 16, 256
DEC_BATCH, DEC_SEQ = 4, 2048
PAST_LEN = 256
DEPTH = 2
GRID_W = 64
CHUNK = 128
HEAD_DIM = 128
N_Q_HEADS, N_KV_HEADS = 8, 2
Q_PER_KV = N_Q_HEADS // N_KV_HEADS
ATTN_WIDTH = N_Q_HEADS * HEAD_DIM
KV_WIDTH = N_KV_HEADS * HEAD_DIM
N_SGU_HEADS = 8
SGU_WIDTH = N_SGU_HEADS * HEAD_DIM
IN_WIDTH = ATTN_WIDTH + 2 * KV_WIDTH + 2 * SGU_WIDTH
ROPE_THETA = 10000.0
ROPE_AXIS_DIM = HEAD_DIM // 2
D_FF = 5632
N_EXPERTS = 8
TOP_K = 2
D_FF_EXPERT = 2816
N_MOD = 6
EPS = 1e-6
ATTN_SCALE = HEAD_DIM ** -0.5
LOG2_E = 1.4426950408889634

MP = BATCH * SEQ
MS = DEC_BATCH * DEC_SEQ
M = MP + MS
N_COND = 8
LANES = 128

VMEM_LIMIT = 56 * 1024 * 1024

TM = 1024
TN_IN = 2 * KV_WIDTH
T_NORM = 512
T_Q = 512
ATTN_ROWS = 1024
T_MOE = 512
P_MOE = M * TOP_K + N_EXPERTS * T_MOE
NT_MOE = P_MOE // T_MOE
TN_GU = D_FF_EXPERT // 2
TN_DN = D // 2
T_COMBINE = 256


def _params(n_axes):
    return pltpu.CompilerParams(dimension_semantics=("arbitrary",) * n_axes,
                                vmem_limit_bytes=VMEM_LIMIT)


def _cond_row(i, t):
    return jnp.where(i < MP // t, 0, 1 + (i - MP // t) // (DEC_SEQ // t))


def _is_ctx_tile(i, t):
    return i < MP // t


def _stream_in(x, t, width, row_of, col_of):
    n_ctx = MP // t
    pair = isinstance(x, tuple)
    base = 0 if pair else n_ctx
    ctx = pl.BlockSpec((t, width), lambda *g: (jnp.minimum(row_of(*g), n_ctx - 1), col_of(*g)))
    lat = pl.BlockSpec((t, width), lambda *g: (base + jnp.maximum(row_of(*g) - n_ctx, 0), col_of(*g)))
    return (list(x) if pair else [x, x]), [ctx, lat]


def _mod_spec(layer, chunk, t, row_of, col_of=None, tn=D):
    per = D // tn

    def index_map(*g):
        col = chunk * per + (col_of(*g) if col_of is not None else 0)
        return (layer, _cond_row(row_of(*g), t), 0, col)

    return pl.BlockSpec((None, None, 1, tn), index_map)


def _ada_kernel(c_ref, w_ref, b_ref, o_ref):
    c = c_ref[...]
    s = (c * jax.nn.sigmoid(c)).astype(BF16)
    o_ref[...] = jnp.dot(s, w_ref[...].astype(BF16), preferred_element_type=F32) + b_ref[...]


def _modulation(cond, w_ada, b_ada):
    tn = 1024
    width = N_MOD * D
    return pl.pallas_call(
        _ada_kernel,
        grid=(DEPTH, width // tn),
        in_specs=[pl.BlockSpec((N_COND, D), lambda l, n: (0, 0)),
                  pl.BlockSpec((None, D, tn), lambda l, n: (l, 0, n)),
                  pl.BlockSpec((None, 1, tn), lambda l, n: (l, 0, n))],
        out_specs=pl.BlockSpec((None, N_COND, tn), lambda l, n: (l, 0, n)),
        out_shape=jax.ShapeDtypeStruct((DEPTH, N_COND, width), F32),
        compiler_params=_params(2),
        name="modulation",
    )(cond, w_ada, b_ada.reshape(DEPTH, 1, width))


def _modulated_norm(x, g, sc, sh):
    y = x * lax.rsqrt(jnp.mean(x * x, axis=-1, keepdims=True) + EPS)
    return (y * g) * (1.0 + sc) + sh


def _pick(xa_ref, xb_ref, axis):
    t = xa_ref.shape[0]
    return jnp.where(_is_ctx_tile(pl.program_id(axis), t), xa_ref[...], xb_ref[...])


def _stream_tile_copy(xa_ref, xb_ref, lat_row0, buf_ref, sem, tile, slot, start):
    t = buf_ref.shape[1]

    def copy(src_ref, row):
        return pltpu.make_async_copy(src_ref.at[pl.ds(pl.multiple_of(row, t), t), :],
                                     buf_ref.at[slot], sem.at[slot])

    if not start:
        copy(xa_ref, 0).wait()
        return
    is_ctx = _is_ctx_tile(tile, t)

    @pl.when(is_ctx)
    def _():
        copy(xa_ref, tile * t).start()

    @pl.when(jnp.logical_not(is_ctx))
    def _():
        copy(xb_ref, lat_row0 + (tile - MP // t) * t).start()


def _next_stream_tile(xa_ref, xb_ref, lat_row0, buf_ref, sem, tile, n_tiles):
    @pl.when(tile == 0)
    def _():
        _stream_tile_copy(xa_ref, xb_ref, lat_row0, buf_ref, sem, 0, 0, True)

    @pl.when(tile + 1 < n_tiles)
    def _():
        _stream_tile_copy(xa_ref, xb_ref, lat_row0, buf_ref, sem, tile + 1, (tile + 1) % 2, True)

    slot = tile % 2
    _stream_tile_copy(xa_ref, xb_ref, lat_row0, buf_ref, sem, tile, slot, False)
    return slot


def _stream_hbm(x):
    return (x[0], x[1], 0) if isinstance(x, tuple) else (x, x, MP)


def _split_bf16(a):
    hi = a.astype(BF16)
    return hi, (a - hi.astype(F32)).astype(BF16)


def _route_top2(h, wr, br, idx_ref, wt_ref):
    h_hi, h_lo = _split_bf16(h)
    w_hi, w_lo = _split_bf16(wr)
    logits = (jnp.dot(h_hi, w_hi, preferred_element_type=F32)
              + jnp.dot(h_lo, w_hi, preferred_element_type=F32)
              + jnp.dot(h_hi, w_lo, preferred_element_type=F32)) + br
    lane = lax.broadcasted_iota(I32, logits.shape, 1)
    neg = jnp.float32(-jnp.inf)
    lg = jnp.where(lane < N_EXPERTS, logits, neg)
    m1 = jnp.max(lg, axis=-1, keepdims=True)
    i1 = jnp.min(jnp.where(lg == m1, lane, LANES), axis=-1, keepdims=True)
    lg2 = jnp.where(lane == i1, neg, lg)
    m2 = jnp.max(lg2, axis=-1, keepdims=True)
    i2 = jnp.min(jnp.where(lg2 == m2, lane, LANES), axis=-1, keepdims=True)
    e = jnp.exp(m2 - m1)
    w1 = 1.0 / (1.0 + e)
    w2 = e / (1.0 + e)
    idx_ref[...] = jnp.where(lane == 0, i1, jnp.where(lane == 1, i2, 0))
    wt_ref[...] = jnp.where(lane == 0, w1, jnp.where(lane == 1, w2, 0.0))


W_PIECE = 512


def _out_proj_kernel(*refs, layer, lat_row0, route):
    if route:
        (o_ref, w_ref, xa_ref, xb_ref, gate_ref, g_ref, sc_ref, sh_ref, wr_ref, br_ref,
         xn_ref, h_ref, idx_ref, wt_ref, stage_ref, wbf_ref, xbuf_ref, wsem, xsem) = refs
    else:
        (o_ref, w_ref, xa_ref, xb_ref, gate_ref, g_ref, sc_ref, sh_ref,
         xn_ref, h_ref, stage_ref, wbf_ref, xbuf_ref, wsem, xsem) = refs
    i = pl.program_id(0)

    @pl.when(i == 0)
    def _():
        for p in range(D // W_PIECE):
            cols = pl.ds(p * W_PIECE, W_PIECE)
            cp = pltpu.make_async_copy(w_ref.at[layer, :, cols], stage_ref, wsem.at[0])
            cp.start()
            cp.wait()
            _cast_rows(stage_ref, wbf_ref.at[:, cols])

    slot = _next_stream_tile(xa_ref, xb_ref, lat_row0, xbuf_ref, xsem, i, pl.num_programs(0))
    acc = jnp.dot(o_ref[...], wbf_ref[...], preferred_element_type=F32)
    x_new = xbuf_ref[slot] + gate_ref[...] * acc
    xn_ref[...] = x_new
    h = _modulated_norm(x_new, g_ref[...], sc_ref[...], sh_ref[...])
    if route:
        h_ref[...] = h
        _route_top2(h, wr_ref[...], br_ref[...], idx_ref, wt_ref)
    else:
        h_ref[...] = h.astype(BF16)


def _out_proj(o, w_out, x, mod, norm2_g, layer, router=None):
    t = T_NORM
    row = lambda i: i
    xa, xb, lat_row0 = _stream_hbm(x)
    route = router is not None
    anyspace = pl.BlockSpec(memory_space=pl.ANY)
    rows = lambda w: pl.BlockSpec((t, w), lambda i: (i, 0))
    in_specs = [rows(D), anyspace, anyspace, anyspace,
                _mod_spec(layer, 2, t, row),
                pl.BlockSpec((None, 1, D), lambda i: (layer, 0, 0)),
                _mod_spec(layer, 4, t, row), _mod_spec(layer, 3, t, row)]
    args = [o, w_out, xa, xb, mod, norm2_g.reshape(DEPTH, 1, D), mod, mod]
    out_specs = [rows(D), rows(D)]
    out_shape = [jax.ShapeDtypeStruct((M, D), F32), jax.ShapeDtypeStruct((M, D), F32 if route else BF16)]
    if route:
        w_router, b_router = router
        args += [jnp.zeros((D, LANES), F32).at[:, :N_EXPERTS].set(w_router),
                 jnp.zeros((1, LANES), F32).at[0, :N_EXPERTS].set(b_router)]
        in_specs += [pl.BlockSpec((D, LANES), lambda i: (0, 0)), pl.BlockSpec((1, LANES), lambda i: (0, 0))]
        out_specs += [rows(LANES), rows(LANES)]
        out_shape += [jax.ShapeDtypeStruct((M, LANES), I32), jax.ShapeDtypeStruct((M, LANES), F32)]
    return pl.pallas_call(
        functools.partial(_out_proj_kernel, layer=layer, lat_row0=lat_row0, route=route),
        grid=(M // t,),
        in_specs=in_specs,
        out_specs=out_specs,
        out_shape=out_shape,
        scratch_shapes=[pltpu.VMEM((D, W_PIECE), F32), pltpu.VMEM((D, D), BF16),
                        pltpu.VMEM((2, t, D), F32),
                        pltpu.SemaphoreType.DMA((1,)), pltpu.SemaphoreType.DMA((2,))],
        compiler_params=_params(1),
        name="out_proj_router" if route else "out_proj",
    )(*args)


CAST_ROWS = 256


def _cast_rows(src_ref, dst_ref):
    def body(r, carry):
        rs = pl.ds(pl.multiple_of(r * CAST_ROWS, CAST_ROWS), CAST_ROWS)
        dst_ref[rs, :] = src_ref[rs, :].astype(BF16)
        return carry

    lax.fori_loop(0, src_ref.shape[0] // CAST_ROWS, body, 0)


def _cast_weight_once(w_ref, wbf_ref):
    @pl.when(pl.program_id(1) == 0)
    def _():
        _cast_rows(w_ref, wbf_ref)


def _head_rms(a):
    return lax.rsqrt(jnp.mean(a * a, axis=-1, keepdims=True) + EPS)


def _rope_partner(ag, perm):
    hi, lo = _split_bf16(ag)
    return (jnp.dot(hi, perm, preferred_element_type=F32)
            + jnp.dot(lo, perm, preferred_element_type=F32))


def _cast_kernel(w_ref, o_ref):
    o_ref[...] = w_ref[...].astype(BF16)


def _cast_in_weights(w_in):
    spec = pl.BlockSpec((None, D, TN_IN), lambda l, n: (l, 0, n))
    return pl.pallas_call(
        _cast_kernel,
        grid=(DEPTH, IN_WIDTH // TN_IN),
        in_specs=[spec],
        out_specs=spec,
        out_shape=jax.ShapeDtypeStruct(w_in.shape, BF16),
        compiler_params=_params(2),
        name="cast_w_in",
    )(w_in)


N_Q_TILES = ATTN_WIDTH // TN_IN
KV_TILE = N_Q_TILES
U_TILE0 = KV_TILE + 1
G_TILE0 = U_TILE0 + SGU_WIDTH // TN_IN
N_IN_TILES = IN_WIDTH // TN_IN


def _in_proj_kernel(xa_ref, xb_ref, n1_ref, sc0_ref, sh0_ref, sc1_ref, sh1_ref, w_ref,
                    qg_ref, kg_ref, sg_ref, pq_ref, pk_ref, cos_ref, sin_ref,
                    q_ref, kf_ref, kb_ref, vf_ref, vb_ref, u_ref, gh_ref,
                    xbuf_ref, h_ref, xsem, *, lat_row0):
    m = pl.program_id(0)
    n = pl.program_id(1)
    nm = pl.num_programs(0)
    cur = m % 2
    nxt = (m + 1) % 2
    has_next = m + 1 < nm
    fetch = functools.partial(_stream_tile_copy, xa_ref, xb_ref, lat_row0, xbuf_ref, xsem)

    def norm_rows(slot, rows, sc_ref, sh_ref):
        x = xbuf_ref[slot, rows, :]
        h_ref[slot, rows, :] = _modulated_norm(x, n1_ref[...], sc_ref[...], sh_ref[...]).astype(BF16)

    @pl.when((m == 0) & (n == 0))
    def _():
        fetch(0, 0, True)
        fetch(0, 0, False)
        norm_rows(0, slice(None), sc0_ref, sh0_ref)

    @pl.when((n == 0) & has_next)
    def _():
        fetch(m + 1, nxt, True)

    def matmul():
        return jnp.dot(h_ref[cur], w_ref[...], preferred_element_type=F32)

    @pl.when(n < N_Q_TILES)
    def _():
        acc = matmul()
        ag = acc * qg_ref[...]
        partner = _rope_partner(ag, pq_ref[...])
        cos, sin = cos_ref[...], sin_ref[...]
        for h in range(TN_IN // HEAD_DIM):
            sl = slice(h * HEAD_DIM, (h + 1) * HEAD_DIM)
            r = _head_rms(acc[:, sl]) * (ATTN_SCALE * LOG2_E)
            q_ref[:, sl] = ((ag[:, sl] * cos + partner[:, sl] * sin) * r).astype(BF16)

    @pl.when(n == KV_TILE)
    def _():
        acc = matmul()
        k = acc[:, :KV_WIDTH]
        ag = k * kg_ref[...]
        partner = _rope_partner(ag, pk_ref[...])
        cos, sin = cos_ref[...], sin_ref[...]
        for h in range(N_KV_HEADS):
            sl = slice(h * HEAD_DIM, (h + 1) * HEAD_DIM)
            r = _head_rms(k[:, sl])
            kf_ref[:, sl] = ag[:, sl] * r
            kb_ref[:, sl] = ((ag[:, sl] * cos + partner[:, sl] * sin) * r).astype(BF16)
        v = acc[:, KV_WIDTH:]
        vf_ref[...] = v
        vb_ref[...] = v.astype(BF16)

    half = TM // (G_TILE0 - U_TILE0)
    for k in range(G_TILE0 - U_TILE0):
        for parity in range(2):
            @pl.when((n == U_TILE0 + k) & has_next & (cur == parity))
            def _():
                if k == 0:
                    fetch(m + 1, 1 - parity, False)
                u_ref[...] = jnp.dot(h_ref[parity], w_ref[...], preferred_element_type=F32).astype(BF16)
                norm_rows(1 - parity, slice(k * half, (k + 1) * half), sc1_ref, sh1_ref)

        @pl.when((n == U_TILE0 + k) & jnp.logical_not(has_next))
        def _():
            u_ref[...] = matmul().astype(BF16)

    @pl.when(n >= G_TILE0)
    def _():
        acc = matmul()
        for h in range(TN_IN // HEAD_DIM):
            sl = slice(h * HEAD_DIM, (h + 1) * HEAD_DIM)
            a = acc[:, sl]
            gh_ref[:, sl] = (a * _head_rms(a) * sg_ref[:, sl]).astype(BF16)


def _rope_tables():
    n_rows = DEC_SEQ // GRID_W
    rows = jnp.broadcast_to(jnp.arange(n_rows)[:, None], (n_rows, GRID_W)).reshape(-1)
    cols = jnp.broadcast_to(jnp.arange(GRID_W)[None, :], (n_rows, GRID_W)).reshape(-1)
    inv = ROPE_THETA ** (-jnp.arange(0, ROPE_AXIS_DIM, 2, dtype=F32) / ROPE_AXIS_DIM)
    ang_r = rows.astype(F32)[:, None] * inv
    ang_c = cols.astype(F32)[:, None] * inv
    cos = jnp.concatenate([jnp.cos(ang_r), jnp.cos(ang_r), jnp.cos(ang_c), jnp.cos(ang_c)], axis=1)
    sin = jnp.concatenate([-jnp.sin(ang_r), jnp.sin(ang_r), -jnp.sin(ang_c), jnp.sin(ang_c)], axis=1)
    cos = jnp.concatenate([jnp.ones((TM, HEAD_DIM), F32), cos], axis=0)
    sin = jnp.concatenate([jnp.zeros((TM, HEAD_DIM), F32), sin], axis=0)
    return cos, sin


def _partner_matrix(n_heads):
    w = n_heads * HEAD_DIM
    quarter = ROPE_AXIS_DIM // 2
    j = jnp.arange(w)
    partner = jnp.where((j % ROPE_AXIS_DIM) < quarter, j + quarter, j - quarter)
    return (jnp.arange(w)[:, None] == partner[None, :]).astype(BF16)


def _rope_block(m):
    return jnp.where(m < MP // TM, 0, 1 + (m - MP // TM) % (DEC_SEQ // TM))


def _in_projections(x, w_in_bf, mod, norm1_g, q_norm_g, k_norm_g, sgu_norm_g, cos, sin, layer):
    tn = TN_IN
    xa, xb, lat_row0 = _stream_hbm(x)
    row = lambda m, n: m
    next_row = lambda m, n: jnp.minimum(m + 1, M // TM - 1)
    anyspace = pl.BlockSpec(memory_space=pl.ANY)
    const = lambda shape: pl.BlockSpec(shape, lambda m, n: (0,) * len(shape))
    rope_spec = pl.BlockSpec((TM, HEAD_DIM), lambda m, n: (_rope_block(m), 0))
    q_heads = tn // HEAD_DIM
    q_gain = jnp.tile(q_norm_g[layer], q_heads)[None, :]
    k_gain = jnp.tile(k_norm_g[layer], N_KV_HEADS)[None, :]
    g_tile = lambda n: jnp.clip(n - G_TILE0, 0, SGU_WIDTH // tn - 1)
    kv_out = pl.BlockSpec((TM, KV_WIDTH), lambda m, n: (m, 0))
    kv_shape = lambda dt: jax.ShapeDtypeStruct((M, KV_WIDTH), dt)
    return pl.pallas_call(
        functools.partial(_in_proj_kernel, lat_row0=lat_row0),
        grid=(M // TM, N_IN_TILES),
        in_specs=[anyspace, anyspace,
                  pl.BlockSpec((None, 1, D), lambda m, n: (layer, 0, 0)),
                  _mod_spec(layer, 1, TM, row), _mod_spec(layer, 0, TM, row),
                  _mod_spec(layer, 1, TM, next_row), _mod_spec(layer, 0, TM, next_row),
                  pl.BlockSpec((None, D, tn), lambda m, n: (layer, 0, n)),
                  const((1, tn)), const((1, KV_WIDTH)),
                  pl.BlockSpec((None, 1, tn), lambda m, n: (layer, 0, g_tile(n))),
                  const((tn, tn)), const((KV_WIDTH, KV_WIDTH)), rope_spec, rope_spec],
        out_specs=[pl.BlockSpec((TM, tn), lambda m, n: (m, jnp.minimum(n, N_Q_TILES - 1))),
                   kv_out, kv_out, kv_out, kv_out,
                   pl.BlockSpec((TM, tn), lambda m, n: (m, jnp.clip(n - U_TILE0, 0, SGU_WIDTH // tn - 1))),
                   pl.BlockSpec((TM, tn), lambda m, n: (m, g_tile(n)))],
        out_shape=[jax.ShapeDtypeStruct((M, ATTN_WIDTH), BF16),
                   kv_shape(F32), kv_shape(BF16), kv_shape(F32), kv_shape(BF16),
                   jax.ShapeDtypeStruct((M, SGU_WIDTH), BF16),
                   jax.ShapeDtypeStruct((M, SGU_WIDTH), BF16)],
        scratch_shapes=[pltpu.VMEM((2, TM, D), F32), pltpu.VMEM((2, TM, D), BF16),
                        pltpu.SemaphoreType.DMA((2,))],
        compiler_params=_params(2),
        name="in_proj",
    )(xa, xb, norm1_g.reshape(DEPTH, 1, D), mod, mod, mod, mod, w_in_bf, q_gain, k_gain,
      sgu_norm_g.reshape(DEPTH, 1, SGU_WIDTH), _partner_matrix(q_heads), _partner_matrix(N_KV_HEADS),
      cos, sin)


def _qk(q, k):
    return lax.dot_general(q, k, (((1,), (1,)), ((), ())), preferred_element_type=F32)


def _attn_kernel(*refs, has_cache, n_batch, seq, tq):
    def with_ones(v):
        return jnp.concatenate([v, jnp.ones_like(v)], axis=1)

    if has_cache:
        q_ref, k_ref, v_ref, kc_ref, vc_ref, o_ref = refs
    else:
        q_ref, k_ref, v_ref, o_ref = refs
    for b in range(n_batch):
        rows_q = slice(b * tq, (b + 1) * tq)
        rows_k = slice(b * seq, (b + 1) * seq)
        for kv in range(N_KV_HEADS):
            kv_cols = slice(kv * HEAD_DIM, (kv + 1) * HEAD_DIM)
            k = k_ref[rows_k, kv_cols]
            v = with_ones(v_ref[rows_k, kv_cols])
            if has_cache:
                kc = kc_ref[:, kv_cols].astype(BF16)
                vc = with_ones(vc_ref[:, kv_cols].astype(BF16))
            for g in range(Q_PER_KV):
                head = kv * Q_PER_KV + g
                sl = slice(head * HEAD_DIM, (head + 1) * HEAD_DIM)
                q = q_ref[rows_q, sl]
                s = _qk(q, k)
                m = jnp.max(s, axis=-1, keepdims=True)
                if has_cache:
                    sc = _qk(q, kc)
                    m = jnp.maximum(m, jnp.max(sc, axis=-1, keepdims=True))
                o = jnp.dot(jnp.exp2(s - m).astype(BF16), v, preferred_element_type=F32)
                if has_cache:
                    o = o + jnp.dot(jnp.exp2(sc - m).astype(BF16), vc, preferred_element_type=F32)
                o_ref[rows_q, sl] = (o[:, :HEAD_DIM] / o[:, HEAD_DIM:]).astype(BF16)


def _attention(q, kb, vb, cache_k, cache_v, layer, *, batch, seq, row0):
    has_cache = cache_k is not None
    tq = min(T_Q, seq)
    nq = seq // tq
    n_batch = max(1, ATTN_ROWS // seq) if nq == 1 else 1
    q_spec = pl.BlockSpec((n_batch * tq, ATTN_WIDTH), lambda b, i: (row0 // (n_batch * tq) + b * nq + i, 0))
    kv_spec = pl.BlockSpec((n_batch * seq, KV_WIDTH), lambda b, i: (row0 // (n_batch * seq) + b, 0))
    in_specs = [q_spec, kv_spec, kv_spec]
    args = [q, kb, vb]
    if has_cache:
        c_spec = pl.BlockSpec((None, None, PAST_LEN, KV_WIDTH), lambda b, i: (b, layer, 0, 0))
        in_specs += [c_spec, c_spec]
        args += [cache_k.reshape(DEC_BATCH, DEPTH, PAST_LEN, KV_WIDTH),
                 cache_v.reshape(DEC_BATCH, DEPTH, PAST_LEN, KV_WIDTH)]
    return pl.pallas_call(
        functools.partial(_attn_kernel, has_cache=has_cache, n_batch=n_batch, seq=seq, tq=tq),
        grid=(batch // n_batch, nq),
        in_specs=in_specs,
        out_specs=pl.BlockSpec((n_batch * tq, ATTN_WIDTH), lambda b, i: (b * nq + i, 0)),
        out_shape=jax.ShapeDtypeStruct((batch * seq, ATTN_WIDTH), BF16),
        compiler_params=_params(2),
        name="attention_cached" if has_cache else "attention",
    )(*args)


def _sgu_merge_kernel(u_ref, gh_ref, ap_ref, as_ref, ws_ref, bs_ref, gn_ref, o_ref, sgu_ref):
    t = u_ref.shape[0]
    a = _pick(ap_ref, as_ref, 0).astype(F32)
    a = a * lax.rsqrt(jnp.mean(a * a, axis=-1, keepdims=True) + EPS) * gn_ref[:, :ATTN_WIDTH]
    o_ref[:, :ATTN_WIDTH] = a.astype(BF16)
    for h in range(N_SGU_HEADS):
        cs = slice(h * HEAD_DIM, (h + 1) * HEAD_DIM)
        w = ws_ref[h].astype(BF16)
        b = bs_ref[h]
        for c in range(t // CHUNK):
            rs = slice(c * CHUNK, (c + 1) * CHUNK)
            mixed = jnp.dot(w, gh_ref[rs, cs], preferred_element_type=F32) + b
            sgu_ref[rs, cs] = u_ref[rs, cs].astype(F32) * mixed
    s = sgu_ref[...]
    s = s * lax.rsqrt(jnp.mean(s * s, axis=-1, keepdims=True) + EPS) * gn_ref[:, ATTN_WIDTH:]
    o_ref[:, ATTN_WIDTH:] = s.astype(BF16)


def _sgu_merge(u, gh, attn_ctx, attn_lat, w_spatial, b_spatial, out_norm_g, layer):
    t = T_NORM
    bias = jnp.broadcast_to(b_spatial[:, :, :, None], (DEPTH, N_SGU_HEADS, CHUNK, HEAD_DIM))
    row = lambda w: pl.BlockSpec((t, w), lambda i: (i, 0))
    a_args, a_specs = _stream_in((attn_ctx, attn_lat), t, ATTN_WIDTH, lambda i: i, lambda i: 0)
    return pl.pallas_call(
        _sgu_merge_kernel,
        grid=(M // t,),
        in_specs=[row(SGU_WIDTH), row(SGU_WIDTH)] + a_specs + [
            pl.BlockSpec((None, N_SGU_HEADS, CHUNK, CHUNK), lambda i: (layer, 0, 0, 0)),
            pl.BlockSpec((None, N_SGU_HEADS, CHUNK, HEAD_DIM), lambda i: (layer, 0, 0, 0)),
            pl.BlockSpec((None, 1, D), lambda i: (layer, 0, 0))],
        out_specs=row(D),
        out_shape=jax.ShapeDtypeStruct((M, D), BF16),
        scratch_shapes=[pltpu.VMEM((t, SGU_WIDTH), F32)],
        compiler_params=_params(1),
        name="sgu_merge",
    )(u, gh, *a_args, w_spatial, bias, out_norm_g.reshape(DEPTH, 1, D))


def _mm_resid_kernel(a_ref, w_ref, xa_ref, xb_ref, g_ref, o_ref, wbf_ref):
    _cast_weight_once(w_ref, wbf_ref)
    acc = jnp.dot(a_ref[...], wbf_ref[...], preferred_element_type=F32)
    o_ref[...] = _pick(xa_ref, xb_ref, 1) + g_ref[...] * acc


def _mm_resid(a, w, x, mod, layer, w_index, gate_chunk, tm, tn):
    k = a.shape[1]
    row = lambda n, m: m
    col = lambda n, m: n
    x_args, x_specs = _stream_in(x, tm, tn, row, col)
    return pl.pallas_call(
        _mm_resid_kernel,
        grid=(D // tn, M // tm),
        in_specs=[pl.BlockSpec((tm, k), lambda n, m: (m, 0)),
                  pl.BlockSpec((None, k, tn), lambda n, m: (w_index, 0, n))] + x_specs + [
                  _mod_spec(layer, gate_chunk, tm, row, col, tn=tn)],
        out_specs=pl.BlockSpec((tm, tn), lambda n, m: (m, n)),
        out_shape=jax.ShapeDtypeStruct((M, D), F32),
        scratch_shapes=[pltpu.VMEM((k, tn), BF16)],
        compiler_params=_params(2),
        name="mm_resid",
    )(a, w, *x_args, mod)


def _swiglu(a, b):
    return a * jax.nn.sigmoid(a) * b


def _ffn_gu_kernel(x_ref, wg_ref, wu_ref, o_ref, wgb_ref, wub_ref):
    @pl.when(pl.program_id(1) == 0)
    def _():
        _cast_rows(wg_ref, wgb_ref)
        _cast_rows(wu_ref, wub_ref)
    x = x_ref[...]
    a = jnp.dot(x, wgb_ref[...], preferred_element_type=F32)
    b = jnp.dot(x, wub_ref[...], preferred_element_type=F32)
    o_ref[...] = _swiglu(a, b).astype(BF16)


def _ffn_gate_up(h, w_gate, w_up, j):
    tm, tn = TM, 512
    w_spec = pl.BlockSpec((None, D, tn), lambda n, m: (j, 0, n))
    return pl.pallas_call(
        _ffn_gu_kernel,
        grid=(D_FF // tn, M // tm),
        in_specs=[pl.BlockSpec((tm, D), lambda n, m: (m, 0)), w_spec, w_spec],
        out_specs=pl.BlockSpec((tm, tn), lambda n, m: (m, n)),
        out_shape=jax.ShapeDtypeStruct((M, D_FF), BF16),
        scratch_shapes=[pltpu.VMEM((D, tn), BF16), pltpu.VMEM((D, tn), BF16)],
        compiler_params=_params(2),
        name="ffn_gate_up",
    )(h, w_gate, w_up)


def _route_meta(idx):
    t = T_MOE
    experts = jnp.arange(N_EXPERTS, dtype=I32)
    onehot = (idx[:, :, None] == experts[None, None, :]).astype(I32).sum(axis=1)
    csum = jnp.cumsum(onehot, axis=0)
    rank = csum - onehot
    count = csum[-1]
    ntile = (count + t - 1) // t
    tile_end = jnp.cumsum(ntile)
    tile_start = tile_end - ntile
    nused = tile_end[-1]
    pos = (tile_start * t)[idx] + jnp.take_along_axis(rank, idx, axis=1)
    j = jnp.arange(NT_MOE, dtype=I32)
    te_raw = jnp.minimum(jnp.sum(j[:, None] >= tile_end[None, :], axis=1), N_EXPERTS - 1).astype(I32)
    te = jnp.where(j < nused, te_raw, te_raw[nused - 1])
    first = ((j == tile_start[te]) & (j < nused)).astype(I32)
    later = (ntile[None, :] > 0) & (experts[None, :] > te[:, None])
    nxt = jnp.min(jnp.where(later, experts[None, :], N_EXPERTS), axis=1)
    nxt = jnp.where(nxt == N_EXPERTS, -1, nxt).astype(I32)
    return pos.astype(I32), (te, first, nxt, nused.reshape(1).astype(I32))


def _expert_weight_stream(w_refs, stage_refs, bf_refs, sem, te_ref, first_ref, nxt_ref, tn):
    c = pl.program_id(0)
    j = pl.program_id(1)
    nc = pl.num_programs(0)

    def copies(e, cc):
        col = pl.multiple_of(cc * tn, LANES)
        return [pltpu.make_async_copy(w.at[e, :, pl.ds(col, tn)], st, sem.at[k])
                for k, (w, st) in enumerate(zip(w_refs, stage_refs))]

    def start(e, cc):
        for cp in copies(e, cc):
            cp.start()

    @pl.when((c == 0) & (j == 0))
    def _():
        start(te_ref[0], 0)

    @pl.when(first_ref[j] == 1)
    def _():
        for cp in copies(0, 0):
            cp.wait()
        for st, bf in zip(stage_refs, bf_refs):
            _cast_rows(st, bf)
        ne = nxt_ref[j]

        @pl.when(ne >= 0)
        def _():
            start(ne, c)

        @pl.when((ne < 0) & (c + 1 < nc))
        def _():
            start(te_ref[0], c + 1)


def _gmm_down_kernel(te_ref, first_ref, nxt_ref, nused_ref, a_ref, wd_ref, o_ref,
                     sd_ref, wdb_ref, sem):
    _expert_weight_stream((wd_ref,), (sd_ref,), (wdb_ref,), sem, te_ref, first_ref, nxt_ref, TN_DN)

    @pl.when(pl.program_id(1) < nused_ref[0])
    def _():
        o_ref[...] = jnp.dot(a_ref[...], wdb_ref[...], preferred_element_type=F32)

    @pl.when(pl.program_id(1) >= nused_ref[0])
    def _():
        o_ref[...] = jnp.zeros(o_ref.shape, o_ref.dtype)


def _used_tile(j, nused):
    return jnp.minimum(j, nused[0] - 1)


def _gmm_down(act, w_down, meta):
    te, first, nxt, nused = meta
    tn = TN_DN
    return pl.pallas_call(
        _gmm_down_kernel,
        grid_spec=pltpu.PrefetchScalarGridSpec(
            num_scalar_prefetch=4,
            grid=(D // tn, NT_MOE),
            in_specs=[pl.BlockSpec((T_MOE, D_FF_EXPERT), lambda c, j, te, fi, nx, nu: (_used_tile(j, nu), 0)),
                      pl.BlockSpec(memory_space=pl.ANY)],
            out_specs=pl.BlockSpec((T_MOE, tn), lambda c, j, te, fi, nx, nu: (j, c)),
            scratch_shapes=[pltpu.VMEM((D_FF_EXPERT, tn), F32), pltpu.VMEM((D_FF_EXPERT, tn), BF16),
                            pltpu.SemaphoreType.DMA((1,))]),
        out_shape=jax.ShapeDtypeStruct((P_MOE, D), F32),
        compiler_params=_params(2),
        name="moe_down",
    )(te, first, nxt, nused, act, w_down)


def _combine_kernel(p0_ref, p1_ref, ys_ref, xa_ref, xb_ref, g_ref, w_ref, oc_ref, ol_ref,
                    a_ref, b_ref, sem):
    t = xa_ref.shape[0]
    i = pl.program_id(0)
    n = pl.num_programs(0)

    def issue(step, slot):
        base = step * t

        def body(r, carry):
            pltpu.make_async_copy(ys_ref.at[pl.ds(p0_ref[base + r], 1), :],
                                  a_ref.at[slot, pl.ds(r, 1), :], sem.at[0, slot]).start()
            pltpu.make_async_copy(ys_ref.at[pl.ds(p1_ref[base + r], 1), :],
                                  b_ref.at[slot, pl.ds(r, 1), :], sem.at[1, slot]).start()
            return carry

        lax.fori_loop(0, t, body, 0, unroll=8)

    @pl.when(i == 0)
    def _():
        issue(0, 0)

    @pl.when(i + 1 < n)
    def _():
        issue(i + 1, (i + 1) % 2)

    slot = i % 2
    pltpu.make_async_copy(ys_ref.at[pl.ds(0, t), :], a_ref.at[slot], sem.at[0, slot]).wait()
    pltpu.make_async_copy(ys_ref.at[pl.ds(0, t), :], b_ref.at[slot], sem.at[1, slot]).wait()
    w = w_ref[...]
    moe = w[:, 0:1] * a_ref[slot] + w[:, 1:2] * b_ref[slot]
    y = _pick(xa_ref, xb_ref, 0) + g_ref[...] * moe
    is_ctx = _is_ctx_tile(i, t)

    @pl.when(is_ctx)
    def _():
        oc_ref[...] = y

    @pl.when(jnp.logical_not(is_ctx))
    def _():
        ol_ref[...] = y


def _combine(ys, x, mod, layer, gate_chunk, wts, pos0, pos1):
    t = T_COMBINE
    n_ctx = MP // t
    row = lambda i, *_: i
    x_args, x_specs = _stream_in(x, t, D, row, lambda i, *_: 0)
    return pl.pallas_call(
        _combine_kernel,
        grid_spec=pltpu.PrefetchScalarGridSpec(
            num_scalar_prefetch=2,
            grid=(M // t,),
            in_specs=[pl.BlockSpec(memory_space=pl.ANY)] + x_specs + [
                _mod_spec(layer, gate_chunk, t, row),
                pl.BlockSpec((t, LANES), lambda i, p0, p1: (i, 0))],
            out_specs=[pl.BlockSpec((t, D), lambda i, p0, p1: (jnp.minimum(i, n_ctx - 1), 0)),
                       pl.BlockSpec((t, D), lambda i, p0, p1: (jnp.maximum(i - n_ctx, 0), 0))],
            scratch_shapes=[pltpu.VMEM((2, t, D), F32), pltpu.VMEM((2, t, D), F32),
                            pltpu.SemaphoreType.DMA((2, 2))]),
        out_shape=[jax.ShapeDtypeStruct((MP, D), F32), jax.ShapeDtypeStruct((MS, D), F32)],
        compiler_params=_params(1),
        name="moe_combine",
    )(pos0, pos1, ys, *x_args, mod, wts)


def _gather_gate_up_kernel(te_ref, first_ref, nxt_ref, nused_ref, src_ref, h_ref, wg_ref, wu_ref, o_ref,
                           x0_ref, x1_ref, sg_ref, su_ref, wgb_ref, wub_ref, wsem, xsem):
    _expert_weight_stream((wg_ref, wu_ref), (sg_ref, su_ref), (wgb_ref, wub_ref), wsem,
                          te_ref, first_ref, nxt_ref, TN_GU)
    c = pl.program_id(0)
    j = pl.program_id(1)
    nused = nused_ref[0]
    bufs = (x0_ref, x1_ref)

    def row_copy(tile, r, slot):
        p = tile * T_MOE + r
        return pltpu.make_async_copy(h_ref.at[pl.ds(src_ref[p], 1), :],
                                     bufs[slot].at[pl.ds(r, 1), :], xsem.at[slot])

    def wait_tile(slot):
        pltpu.make_async_copy(h_ref.at[pl.ds(0, T_MOE), :], bufs[slot], xsem.at[slot]).wait()

    @pl.when((c == 0) & (j == 0))
    def _():
        def body(r, carry):
            row_copy(0, r, 0).start()
            return carry

        lax.fori_loop(0, T_MOE, body, 0, unroll=8)

    used = j < nused
    step = c * nused + j
    nxt_tile = jnp.where(j + 1 < nused, j + 1, 0)
    last = (c + 1 == pl.num_programs(0)) & (j + 1 == nused)
    for slot in range(2):
        @pl.when(used & (step % 2 == slot))
        def _():
            wait_tile(slot)
            for r in range(T_MOE):
                row_copy(nxt_tile, r, 1 - slot).start()
            half = T_MOE // 2
            for r in range(2):
                rs = slice(r * half, (r + 1) * half)
                x = bufs[slot][rs, :].astype(BF16)
                a = jnp.dot(x, wgb_ref[...], preferred_element_type=F32)
                b = jnp.dot(x, wub_ref[...], preferred_element_type=F32)
                o_ref[rs, :] = _swiglu(a, b).astype(BF16)

        @pl.when(used & (step % 2 == slot) & last)
        def _():
            wait_tile(1 - slot)

    @pl.when(jnp.logical_not(used))
    def _():
        o_ref[...] = jnp.zeros(o_ref.shape, o_ref.dtype)


def _gather_gate_up(h, src, w_gate, w_up, meta):
    te, first, nxt, nused = meta
    tn = TN_GU
    anyspace = pl.BlockSpec(memory_space=pl.ANY)
    return pl.pallas_call(
        _gather_gate_up_kernel,
        grid_spec=pltpu.PrefetchScalarGridSpec(
            num_scalar_prefetch=5,
            grid=(D_FF_EXPERT // tn, NT_MOE),
            in_specs=[anyspace, anyspace, anyspace],
            out_specs=pl.BlockSpec((T_MOE, tn), lambda c, j, te, fi, nx, nu, sr: (j, c)),
            scratch_shapes=[pltpu.VMEM((T_MOE, D), F32), pltpu.VMEM((T_MOE, D), F32),
                            pltpu.VMEM((D, tn), F32), pltpu.VMEM((D, tn), F32),
                            pltpu.VMEM((D, tn), BF16), pltpu.VMEM((D, tn), BF16),
                            pltpu.SemaphoreType.DMA((2,)), pltpu.SemaphoreType.DMA((2,))]),
        out_shape=jax.ShapeDtypeStruct((P_MOE, D_FF_EXPERT), BF16),
        compiler_params=_params(2),
        name="moe_gate_up",
    )(te, first, nxt, nused, src, h, w_gate, w_up)


def _moe(x, h, idx, wts, mod, layer, w_gate, w_up, w_down):
    pos, meta = _route_meta(idx[:, :TOP_K])
    pos0, pos1 = pos[:, 0], pos[:, 1]
    src = jnp.zeros((P_MOE,), I32).at[pos.reshape(-1)].set(jnp.repeat(jnp.arange(M, dtype=I32), TOP_K))
    act = _gather_gate_up(h, src, w_gate, w_up, meta)
    ys = _gmm_down(act, w_down, meta)
    return _combine(ys, x, mod, layer, 5, wts, pos0, pos1)


def kernel(x_prompt, x_sample, cache_k, cache_v, c, c_ctx, w_ada, b_ada, norm1_g, norm2_g, w_in, q_norm_g, k_norm_g, sgu_norm_g, w_spatial, b_spatial, out_norm_g, w_out, ffn_w_gate, ffn_w_up, ffn_w_down, w_router, b_router, moe_w_gate, moe_w_up, moe_w_down):
    assert DEPTH == 2
    x = (x_prompt.reshape(MP, D), x_sample.reshape(MS, D))
    cond = jnp.concatenate([c_ctx[None, :], c, jnp.zeros((N_COND - 1 - DEC_BATCH, D), F32)], axis=0)
    mod = _modulation(cond, w_ada, b_ada).reshape(DEPTH, N_COND, 1, N_MOD * D)
    cos, sin = _rope_tables()
    w_in_bf = _cast_in_weights(w_in)

    new_k, new_v = [], []
    for i in range(DEPTH):
        q, kf, kb, vf, vb, u, gh = _in_projections(x, w_in_bf, mod, norm1_g, q_norm_g, k_norm_g,
                                                   sgu_norm_g, cos, sin, i)
        attn_ctx = _attention(q, kb, vb, None, None, i, batch=BATCH, seq=SEQ, row0=0)
        attn_lat = _attention(q, kb, vb, cache_k, cache_v, i, batch=DEC_BATCH, seq=DEC_SEQ, row0=MP)
        o = _sgu_merge(u, gh, attn_ctx, attn_lat, w_spatial, b_spatial, out_norm_g, i)
        j = i // 2
        if i % 2 == 0:
            x, h2 = _out_proj(o, w_out, x, mod, norm2_g, i)
            act = _ffn_gate_up(h2, ffn_w_gate, ffn_w_up, j)
            x = _mm_resid(act, ffn_w_down, x, mod, i, j, 5, 512, 512)
        else:
            x, h2, idx, wts = _out_proj(o, w_out, x, mod, norm2_g, i, router=(w_router[j], b_router[j]))
            x = _moe(x, h2, idx, wts, mod, i, moe_w_gate[j], moe_w_up[j], moe_w_down[j])
        new_k.append(kf[:MP].reshape(BATCH, SEQ, N_KV_HEADS, HEAD_DIM))
        new_v.append(vf[:MP].reshape(BATCH, SEQ, N_KV_HEADS, HEAD_DIM))

    y_prompt = x[0].reshape(BATCH, SEQ, D)
    y_sample = x[1].reshape(DEC_BATCH, DEC_SEQ, D)
    return (y_prompt, y_sample, jnp.stack(new_k, axis=1), jnp.stack(new_v, axis=1))
```

```python
import functools

import jax
import jax.numpy as jnp
from jax import lax
from jax.experimental import pallas as pl
from jax.experimental.pallas import tpu as pltpu

F32 = jnp.float32
BF16 = jnp.bfloat16
I32 = jnp.int32

D = 2048
BATCH, SEQ = 16, 256
DEC_BATCH, DEC_SEQ = 4, 2048
PAST_LEN = 256
DEPTH = 2
GRID_W = 64
CHUNK = 128
HEAD_DIM = 128
N_Q_HEADS, N_KV_HEADS = 8, 2
Q_PER_KV = N_Q_HEADS // N_KV_HEADS
ATTN_WIDTH = N_Q_HEADS * HEAD_DIM
KV_WIDTH = N_KV_HEADS * HEAD_DIM
N_SGU_HEADS = 8
SGU_WIDTH = N_SGU_HEADS * HEAD_DIM
IN_WIDTH = ATTN_WIDTH + 2 * KV_WIDTH + 2 * SGU_WIDTH
ROPE_THETA = 10000.0
ROPE_AXIS_DIM = HEAD_DIM // 2
D_FF = 5632
N_EXPERTS = 8
TOP_K = 2
D_FF_EXPERT = 2816
N_MOD = 6
EPS = 1e-6
ATTN_SCALE = HEAD_DIM ** -0.5
LOG2_E = 1.4426950408889634

MP = BATCH * SEQ
MS = DEC_BATCH * DEC_SEQ
M = MP + MS
N_COND = 8
LANES = 128
SUBLANES = 8
SUBLANE_BITS = 3

VMEM_LIMIT = 56 * 1024 * 1024

TM = 1024
TN_IN = 2 * KV_WIDTH
T_NORM = 512
T_Q = 512
ATTN_ROWS = 1024
T_MOE = 512
P_MOE = M * TOP_K + N_EXPERTS * T_MOE
NT_MOE = P_MOE // T_MOE
TN_GU = D_FF_EXPERT // 2
TN_DN = D // 2
T_DISPATCH = 1024
T_COMBINE = 256
ZERO_ROWS = T_MOE // 2
PAD_BITS = ZERO_ROWS.bit_length()


def _params(n_axes):
    return pltpu.CompilerParams(dimension_semantics=("arbitrary",) * n_axes,
                                vmem_limit_bytes=VMEM_LIMIT)


def _cond_row(i, t):
    return jnp.where(i < MP // t, 0, 1 + (i - MP // t) // (DEC_SEQ // t))


def _is_ctx_tile(i, t):
    return i < MP // t


def _stream_in(x, t, width, row_of, col_of):
    n_ctx = MP // t
    pair = isinstance(x, tuple)
    base = 0 if pair else n_ctx
    ctx = pl.BlockSpec((t, width), lambda *g: (jnp.minimum(row_of(*g), n_ctx - 1), col_of(*g)))
    lat = pl.BlockSpec((t, width), lambda *g: (base + jnp.maximum(row_of(*g) - n_ctx, 0), col_of(*g)))
    return (list(x) if pair else [x, x]), [ctx, lat]


def _mod_spec(layer, chunk, t, row_of, col_of=None, tn=D):
    per = D // tn

    def index_map(*g):
        col = chunk * per + (col_of(*g) if col_of is not None else 0)
        return (layer, _cond_row(row_of(*g), t), 0, col)

    return pl.BlockSpec((None, None, 1, tn), index_map)


def _ada_kernel(c_ref, w_ref, b_ref, o_ref):
    c = c_ref[...]
    s = (c * jax.nn.sigmoid(c)).astype(BF16)
    o_ref[...] = jnp.dot(s, w_ref[...].astype(BF16), preferred_element_type=F32) + b_ref[...]


def _modulation(cond, w_ada, b_ada):
    tn = 1024
    width = N_MOD * D
    return pl.pallas_call(
        _ada_kernel,
        grid=(DEPTH, width // tn),
        in_specs=[pl.BlockSpec((N_COND, D), lambda l, n: (0, 0)),
                  pl.BlockSpec((None, D, tn), lambda l, n: (l, 0, n)),
                  pl.BlockSpec((None, 1, tn), lambda l, n: (l, 0, n))],
        out_specs=pl.BlockSpec((None, N_COND, tn), lambda l, n: (l, 0, n)),
        out_shape=jax.ShapeDtypeStruct((DEPTH, N_COND, width), F32),
        compiler_params=_params(2),
        name="modulation",
    )(cond, w_ada, b_ada.reshape(DEPTH, 1, width))


def _modulated_norm(x, g, sc, sh):
    y = x * lax.rsqrt(jnp.mean(x * x, axis=-1, keepdims=True) + EPS)
    return (y * g) * (1.0 + sc) + sh


def _pick(xa_ref, xb_ref, axis):
    t = xa_ref.shape[0]
    return jnp.where(_is_ctx_tile(pl.program_id(axis), t), xa_ref[...], xb_ref[...])


def _stream_tile_copy(xa_ref, xb_ref, lat_row0, buf_ref, sem, tile, slot, start):
    t = buf_ref.shape[1]

    def copy(src_ref, row):
        return pltpu.make_async_copy(src_ref.at[pl.ds(pl.multiple_of(row, t), t), :],
                                     buf_ref.at[slot], sem.at[slot])

    if not start:
        copy(xa_ref, 0).wait()
        return
    is_ctx = _is_ctx_tile(tile, t)

    @pl.when(is_ctx)
    def _():
        copy(xa_ref, tile * t).start()

    @pl.when(jnp.logical_not(is_ctx))
    def _():
        copy(xb_ref, lat_row0 + (tile - MP // t) * t).start()


def _next_stream_tile(xa_ref, xb_ref, lat_row0, buf_ref, sem, tile, n_tiles):
    @pl.when(tile == 0)
    def _():
        _stream_tile_copy(xa_ref, xb_ref, lat_row0, buf_ref, sem, 0, 0, True)

    @pl.when(tile + 1 < n_tiles)
    def _():
        _stream_tile_copy(xa_ref, xb_ref, lat_row0, buf_ref, sem, tile + 1, (tile + 1) % 2, True)

    slot = tile % 2
    _stream_tile_copy(xa_ref, xb_ref, lat_row0, buf_ref, sem, tile, slot, False)
    return slot


def _stream_hbm(x):
    return (x[0], x[1], 0) if isinstance(x, tuple) else (x, x, MP)


def _split_bf16(a):
    hi = a.astype(BF16)
    return hi, (a - hi.astype(F32)).astype(BF16)


def _route_top2(h, wr, br, idx_ref, wt_ref):
    h_hi, h_lo = _split_bf16(h)
    w_hi, w_lo = _split_bf16(wr)
    logits = (jnp.dot(h_hi, w_hi, preferred_element_type=F32)
              + jnp.dot(h_lo, w_hi, preferred_element_type=F32)
              + jnp.dot(h_hi, w_lo, preferred_element_type=F32)) + br
    lane = lax.broadcasted_iota(I32, logits.shape, 1)
    neg = jnp.float32(-jnp.inf)
    lg = jnp.where(lane < N_EXPERTS, logits, neg)
    m1 = jnp.max(lg, axis=-1, keepdims=True)
    i1 = jnp.min(jnp.where(lg == m1, lane, LANES), axis=-1, keepdims=True)
    lg2 = jnp.where(lane == i1, neg, lg)
    m2 = jnp.max(lg2, axis=-1, keepdims=True)
    i2 = jnp.min(jnp.where(lg2 == m2, lane, LANES), axis=-1, keepdims=True)
    e = jnp.exp(m2 - m1)
    w1 = 1.0 / (1.0 + e)
    w2 = e / (1.0 + e)
    idx_ref[...] = jnp.where(lane == 0, i1, jnp.where(lane == 1, i2, 0))
    wt_ref[...] = jnp.where(lane == 0, w1, jnp.where(lane == 1, w2, 0.0))


W_PIECE = 512


def _out_proj_kernel(*refs, layer, lat_row0, route):
    if route:
        (o_ref, w_ref, xa_ref, xb_ref, gate_ref, g_ref, sc_ref, sh_ref, wr_ref, br_ref,
         xn_ref, h_ref, idx_ref, wt_ref, stage_ref, wbf_ref, xbuf_ref, wsem, xsem) = refs
    else:
        (o_ref, w_ref, xa_ref, xb_ref, gate_ref, g_ref, sc_ref, sh_ref,
         xn_ref, h_ref, stage_ref, wbf_ref, xbuf_ref, wsem, xsem) = refs
    i = pl.program_id(0)

    @pl.when(i == 0)
    def _():
        for p in range(D // W_PIECE):
            cols = pl.ds(p * W_PIECE, W_PIECE)
            cp = pltpu.make_async_copy(w_ref.at[layer, :, cols], stage_ref, wsem.at[0])
            cp.start()
            cp.wait()
            _cast_rows(stage_ref, wbf_ref.at[:, cols])

    slot = _next_stream_tile(xa_ref, xb_ref, lat_row0, xbuf_ref, xsem, i, pl.num_programs(0))
    acc = jnp.dot(o_ref[...], wbf_ref[...], preferred_element_type=F32)
    x_new = xbuf_ref[slot] + gate_ref[...] * acc
    xn_ref[...] = x_new
    h = _modulated_norm(x_new, g_ref[...], sc_ref[...], sh_ref[...])
    if route:
        h_ref[...] = h
        _route_top2(h, wr_ref[...], br_ref[...], idx_ref, wt_ref)
    else:
        h_ref[...] = h.astype(BF16)


def _out_proj(o, w_out, x, mod, norm2_g, layer, router=None):
    t = T_NORM
    row = lambda i: i
    xa, xb, lat_row0 = _stream_hbm(x)
    route = router is not None
    anyspace = pl.BlockSpec(memory_space=pl.ANY)
    rows = lambda w: pl.BlockSpec((t, w), lambda i: (i, 0))
    in_specs = [rows(D), anyspace, anyspace, anyspace,
                _mod_spec(layer, 2, t, row),
                pl.BlockSpec((None, 1, D), lambda i: (layer, 0, 0)),
                _mod_spec(layer, 4, t, row), _mod_spec(layer, 3, t, row)]
    args = [o, w_out, xa, xb, mod, norm2_g.reshape(DEPTH, 1, D), mod, mod]
    out_specs = [rows(D), rows(D)]
    out_shape = [jax.ShapeDtypeStruct((M, D), F32), jax.ShapeDtypeStruct((M, D), F32 if route else BF16)]
    if route:
        w_router, b_router = router
        args += [jnp.zeros((D, LANES), F32).at[:, :N_EXPERTS].set(w_router),
                 jnp.zeros((1, LANES), F32).at[0, :N_EXPERTS].set(b_router)]
        in_specs += [pl.BlockSpec((D, LANES), lambda i: (0, 0)), pl.BlockSpec((1, LANES), lambda i: (0, 0))]
        out_specs += [rows(LANES), rows(LANES)]
        out_shape += [jax.ShapeDtypeStruct((M, LANES), I32), jax.ShapeDtypeStruct((M, LANES), F32)]
    return pl.pallas_call(
        functools.partial(_out_proj_kernel, layer=layer, lat_row0=lat_row0, route=route),
        grid=(M // t,),
        in_specs=in_specs,
        out_specs=out_specs,
        out_shape=out_shape,
        scratch_shapes=[pltpu.VMEM((D, W_PIECE), F32), pltpu.VMEM((D, D), BF16),
                        pltpu.VMEM((2, t, D), F32),
                        pltpu.SemaphoreType.DMA((1,)), pltpu.SemaphoreType.DMA((2,))],
        compiler_params=_params(1),
        name="out_proj_router" if route else "out_proj",
    )(*args)


CAST_ROWS = 256


def _cast_rows(src_ref, dst_ref):
    def body(r, carry):
        rs = pl.ds(pl.multiple_of(r * CAST_ROWS, CAST_ROWS), CAST_ROWS)
        dst_ref[rs, :] = src_ref[rs, :].astype(BF16)
        return carry

    lax.fori_loop(0, src_ref.shape[0] // CAST_ROWS, body, 0)


def _cast_weight_once(w_ref, wbf_ref):
    @pl.when(pl.program_id(1) == 0)
    def _():
        _cast_rows(w_ref, wbf_ref)


def _head_rms(a):
    return lax.rsqrt(jnp.mean(a * a, axis=-1, keepdims=True) + EPS)


def _rope_partner(ag, perm):
    hi, lo = _split_bf16(ag)
    return (jnp.dot(hi, perm, preferred_element_type=F32)
            + jnp.dot(lo, perm, preferred_element_type=F32))


def _cast_kernel(w_ref, o_ref):
    o_ref[...] = w_ref[...].astype(BF16)


def _cast_in_weights(w_in):
    spec = pl.BlockSpec((None, D, TN_IN), lambda l, n: (l, 0, n))
    return pl.pallas_call(
        _cast_kernel,
        grid=(DEPTH, IN_WIDTH // TN_IN),
        in_specs=[spec],
        out_specs=spec,
        out_shape=jax.ShapeDtypeStruct(w_in.shape, BF16),
        compiler_params=_params(2),
        name="cast_w_in",
    )(w_in)


N_Q_TILES = ATTN_WIDTH // TN_IN
KV_TILE = N_Q_TILES
U_TILE0 = KV_TILE + 1
G_TILE0 = U_TILE0 + SGU_WIDTH // TN_IN
N_IN_TILES = IN_WIDTH // TN_IN


def _in_proj_kernel(xa_ref, xb_ref, n1_ref, sc0_ref, sh0_ref, sc1_ref, sh1_ref, w_ref,
                    qg_ref, kg_ref, sg_ref, pq_ref, pk_ref, cos_ref, sin_ref,
                    q_ref, kf_ref, kb_ref, vf_ref, vb_ref, u_ref, gh_ref,
                    xbuf_ref, h_ref, xsem, *, lat_row0):
    m = pl.program_id(0)
    n = pl.program_id(1)
    nm = pl.num_programs(0)
    cur = m % 2
    nxt = (m + 1) % 2
    has_next = m + 1 < nm
    fetch = functools.partial(_stream_tile_copy, xa_ref, xb_ref, lat_row0, xbuf_ref, xsem)

    def norm_rows(slot, rows, sc_ref, sh_ref):
        x = xbuf_ref[slot, rows, :]
        h_ref[slot, rows, :] = _modulated_norm(x, n1_ref[...], sc_ref[...], sh_ref[...]).astype(BF16)

    @pl.when((m == 0) & (n == 0))
    def _():
        fetch(0, 0, True)
        fetch(0, 0, False)
        norm_rows(0, slice(None), sc0_ref, sh0_ref)

    @pl.when((n == 0) & has_next)
    def _():
        fetch(m + 1, nxt, True)

    def matmul():
        return jnp.dot(h_ref[cur], w_ref[...], preferred_element_type=F32)

    @pl.when(n < N_Q_TILES)
    def _():
        acc = matmul()
        ag = acc * qg_ref[...]
        partner = _rope_partner(ag, pq_ref[...])
        cos, sin = cos_ref[...], sin_ref[...]
        for h in range(TN_IN // HEAD_DIM):
            sl = slice(h * HEAD_DIM, (h + 1) * HEAD_DIM)
            r = _head_rms(acc[:, sl]) * (ATTN_SCALE * LOG2_E)
            q_ref[:, sl] = ((ag[:, sl] * cos + partner[:, sl] * sin) * r).astype(BF16)

    @pl.when(n == KV_TILE)
    def _():
        acc = matmul()
        k = acc[:, :KV_WIDTH]
        ag = k * kg_ref[...]
        partner = _rope_partner(ag, pk_ref[...])
        cos, sin = cos_ref[...], sin_ref[...]
        for h in range(N_KV_HEADS):
            sl = slice(h * HEAD_DIM, (h + 1) * HEAD_DIM)
            r = _head_rms(k[:, sl])
            kf_ref[:, sl] = ag[:, sl] * r
            kb_ref[:, sl] = ((ag[:, sl] * cos + partner[:, sl] * sin) * r).astype(BF16)
        v = acc[:, KV_WIDTH:]
        vf_ref[...] = v
        vb_ref[...] = v.astype(BF16)

    half = TM // (G_TILE0 - U_TILE0)
    for k in range(G_TILE0 - U_TILE0):
        for parity in range(2):
            @pl.when((n == U_TILE0 + k) & has_next & (cur == parity))
            def _():
                if k == 0:
                    fetch(m + 1, 1 - parity, False)
                u_ref[...] = jnp.dot(h_ref[parity], w_ref[...], preferred_element_type=F32).astype(BF16)
                norm_rows(1 - parity, slice(k * half, (k + 1) * half), sc1_ref, sh1_ref)

        @pl.when((n == U_TILE0 + k) & jnp.logical_not(has_next))
        def _():
            u_ref[...] = matmul().astype(BF16)

    @pl.when(n >= G_TILE0)
    def _():
        acc = matmul()
        for h in range(TN_IN // HEAD_DIM):
            sl = slice(h * HEAD_DIM, (h + 1) * HEAD_DIM)
            a = acc[:, sl]
            gh_ref[:, sl] = (a * _head_rms(a) * sg_ref[:, sl]).astype(BF16)


def _rope_tables():
    n_rows = DEC_SEQ // GRID_W
    rows = jnp.broadcast_to(jnp.arange(n_rows)[:, None], (n_rows, GRID_W)).reshape(-1)
    cols = jnp.broadcast_to(jnp.arange(GRID_W)[None, :], (n_rows, GRID_W)).reshape(-1)
    inv = ROPE_THETA ** (-jnp.arange(0, ROPE_AXIS_DIM, 2, dtype=F32) / ROPE_AXIS_DIM)
    ang_r = rows.astype(F32)[:, None] * inv
    ang_c = cols.astype(F32)[:, None] * inv
    cos = jnp.concatenate([jnp.cos(ang_r), jnp.cos(ang_r), jnp.cos(ang_c), jnp.cos(ang_c)], axis=1)
    sin = jnp.concatenate([-jnp.sin(ang_r), jnp.sin(ang_r), -jnp.sin(ang_c), jnp.sin(ang_c)], axis=1)
    cos = jnp.concatenate([jnp.ones((TM, HEAD_DIM), F32), cos], axis=0)
    sin = jnp.concatenate([jnp.zeros((TM, HEAD_DIM), F32), sin], axis=0)
    return cos, sin


def _partner_matrix(n_heads):
    w = n_heads * HEAD_DIM
    quarter = ROPE_AXIS_DIM // 2
    j = jnp.arange(w)
    partner = jnp.where((j % ROPE_AXIS_DIM) < quarter, j + quarter, j - quarter)
    return (jnp.arange(w)[:, None] == partner[None, :]).astype(BF16)


def _rope_block(m):
    return jnp.where(m < MP // TM, 0, 1 + (m - MP // TM) % (DEC_SEQ // TM))


def _in_projections(x, w_in_bf, mod, norm1_g, q_norm_g, k_norm_g, sgu_norm_g, cos, sin, layer):
    tn = TN_IN
    xa, xb, lat_row0 = _stream_hbm(x)
    row = lambda m, n: m
    next_row = lambda m, n: jnp.minimum(m + 1, M // TM - 1)
    anyspace = pl.BlockSpec(memory_space=pl.ANY)
    const = lambda shape: pl.BlockSpec(shape, lambda m, n: (0,) * len(shape))
    rope_spec = pl.BlockSpec((TM, HEAD_DIM), lambda m, n: (_rope_block(m), 0))
    q_heads = tn // HEAD_DIM
    q_gain = jnp.tile(q_norm_g[layer], q_heads)[None, :]
    k_gain = jnp.tile(k_norm_g[layer], N_KV_HEADS)[None, :]
    g_tile = lambda n: jnp.clip(n - G_TILE0, 0, SGU_WIDTH // tn - 1)
    kv_out = pl.BlockSpec((TM, KV_WIDTH), lambda m, n: (m, 0))
    kv_shape = lambda dt: jax.ShapeDtypeStruct((M, KV_WIDTH), dt)
    return pl.pallas_call(
        functools.partial(_in_proj_kernel, lat_row0=lat_row0),
        grid=(M // TM, N_IN_TILES),
        in_specs=[anyspace, anyspace,
                  pl.BlockSpec((None, 1, D), lambda m, n: (layer, 0, 0)),
                  _mod_spec(layer, 1, TM, row), _mod_spec(layer, 0, TM, row),
                  _mod_spec(layer, 1, TM, next_row), _mod_spec(layer, 0, TM, next_row),
                  pl.BlockSpec((None, D, tn), lambda m, n: (layer, 0, n)),
                  const((1, tn)), const((1, KV_WIDTH)),
                  pl.BlockSpec((None, 1, tn), lambda m, n: (layer, 0, g_tile(n))),
                  const((tn, tn)), const((KV_WIDTH, KV_WIDTH)), rope_spec, rope_spec],
        out_specs=[pl.BlockSpec((TM, tn), lambda m, n: (m, jnp.minimum(n, N_Q_TILES - 1))),
                   kv_out, kv_out, kv_out, kv_out,
                   pl.BlockSpec((TM, tn), lambda m, n: (m, jnp.clip(n - U_TILE0, 0, SGU_WIDTH // tn - 1))),
                   pl.BlockSpec((TM, tn), lambda m, n: (m, g_tile(n)))],
        out_shape=[jax.ShapeDtypeStruct((M, ATTN_WIDTH), BF16),
                   kv_shape(F32), kv_shape(BF16), kv_shape(F32), kv_shape(BF16),
                   jax.ShapeDtypeStruct((M, SGU_WIDTH), BF16),
                   jax.ShapeDtypeStruct((M, SGU_WIDTH), BF16)],
        scratch_shapes=[pltpu.VMEM((2, TM, D), F32), pltpu.VMEM((2, TM, D), BF16),
                        pltpu.SemaphoreType.DMA((2,))],
        compiler_params=_params(2),
        name="in_proj",
    )(xa, xb, norm1_g.reshape(DEPTH, 1, D), mod, mod, mod, mod, w_in_bf, q_gain, k_gain,
      sgu_norm_g.reshape(DEPTH, 1, SGU_WIDTH), _partner_matrix(q_heads), _partner_matrix(N_KV_HEADS),
      cos, sin)


def _qk(q, k):
    return lax.dot_general(q, k, (((1,), (1,)), ((), ())), preferred_element_type=F32)


def _attn_kernel(*refs, has_cache, n_batch, seq, tq):
    def with_ones(v):
        return jnp.concatenate([v, jnp.ones_like(v)], axis=1)

    if has_cache:
        q_ref, k_ref, v_ref, kc_ref, vc_ref, o_ref = refs
    else:
        q_ref, k_ref, v_ref, o_ref = refs
    for b in range(n_batch):
        rows_q = slice(b * tq, (b + 1) * tq)
        rows_k = slice(b * seq, (b + 1) * seq)
        for kv in range(N_KV_HEADS):
            kv_cols = slice(kv * HEAD_DIM, (kv + 1) * HEAD_DIM)
            k = k_ref[rows_k, kv_cols]
            v = with_ones(v_ref[rows_k, kv_cols])
            if has_cache:
                kc = kc_ref[:, kv_cols].astype(BF16)
                vc = with_ones(vc_ref[:, kv_cols].astype(BF16))
            for g in range(Q_PER_KV):
                head = kv * Q_PER_KV + g
                sl = slice(head * HEAD_DIM, (head + 1) * HEAD_DIM)
                q = q_ref[rows_q, sl]
                s = _qk(q, k)
                m = jnp.max(s, axis=-1, keepdims=True)
                if has_cache:
                    sc = _qk(q, kc)
                    m = jnp.maximum(m, jnp.max(sc, axis=-1, keepdims=True))
                o = jnp.dot(jnp.exp2(s - m).astype(BF16), v, preferred_element_type=F32)
                if has_cache:
                    o = o + jnp.dot(jnp.exp2(sc - m).astype(BF16), vc, preferred_element_type=F32)
                o_ref[rows_q, sl] = (o[:, :HEAD_DIM] / o[:, HEAD_DIM:]).astype(BF16)


def _attention(q, kb, vb, cache_k, cache_v, layer, *, batch, seq, row0):
    has_cache = cache_k is not None
    tq = min(T_Q, seq)
    nq = seq // tq
    n_batch = max(1, ATTN_ROWS // seq) if nq == 1 else 1
    q_spec = pl.BlockSpec((n_batch * tq, ATTN_WIDTH), lambda b, i: (row0 // (n_batch * tq) + b * nq + i, 0))
    kv_spec = pl.BlockSpec((n_batch * seq, KV_WIDTH), lambda b, i: (row0 // (n_batch * seq) + b, 0))
    in_specs = [q_spec, kv_spec, kv_spec]
    args = [q, kb, vb]
    if has_cache:
        c_spec = pl.BlockSpec((None, None, PAST_LEN, KV_WIDTH), lambda b, i: (b, layer, 0, 0))
        in_specs += [c_spec, c_spec]
        args += [cache_k.reshape(DEC_BATCH, DEPTH, PAST_LEN, KV_WIDTH),
                 cache_v.reshape(DEC_BATCH, DEPTH, PAST_LEN, KV_WIDTH)]
    return pl.pallas_call(
        functools.partial(_attn_kernel, has_cache=has_cache, n_batch=n_batch, seq=seq, tq=tq),
        grid=(batch // n_batch, nq),
        in_specs=in_specs,
        out_specs=pl.BlockSpec((n_batch * tq, ATTN_WIDTH), lambda b, i: (b * nq + i, 0)),
        out_shape=jax.ShapeDtypeStruct((batch * seq, ATTN_WIDTH), BF16),
        compiler_params=_params(2),
        name="attention_cached" if has_cache else "attention",
    )(*args)


def _sgu_merge_kernel(u_ref, gh_ref, ap_ref, as_ref, ws_ref, bs_ref, gn_ref, o_ref, sgu_ref):
    t = u_ref.shape[0]
    a = _pick(ap_ref, as_ref, 0).astype(F32)
    a = a * lax.rsqrt(jnp.mean(a * a, axis=-1, keepdims=True) + EPS) * gn_ref[:, :ATTN_WIDTH]
    o_ref[:, :ATTN_WIDTH] = a.astype(BF16)
    for h in range(N_SGU_HEADS):
        cs = slice(h * HEAD_DIM, (h + 1) * HEAD_DIM)
        w = ws_ref[h].astype(BF16)
        b = bs_ref[h]
        for c in range(t // CHUNK):
            rs = slice(c * CHUNK, (c + 1) * CHUNK)
            mixed = jnp.dot(w, gh_ref[rs, cs], preferred_element_type=F32) + b
            sgu_ref[rs, cs] = u_ref[rs, cs].astype(F32) * mixed
    s = sgu_ref[...]
    s = s * lax.rsqrt(jnp.mean(s * s, axis=-1, keepdims=True) + EPS) * gn_ref[:, ATTN_WIDTH:]
    o_ref[:, ATTN_WIDTH:] = s.astype(BF16)


def _sgu_merge(u, gh, attn_ctx, attn_lat, w_spatial, b_spatial, out_norm_g, layer):
    t = T_NORM
    bias = jnp.broadcast_to(b_spatial[:, :, :, None], (DEPTH, N_SGU_HEADS, CHUNK, HEAD_DIM))
    row = lambda w: pl.BlockSpec((t, w), lambda i: (i, 0))
    a_args, a_specs = _stream_in((attn_ctx, attn_lat), t, ATTN_WIDTH, lambda i: i, lambda i: 0)
    return pl.pallas_call(
        _sgu_merge_kernel,
        grid=(M // t,),
        in_specs=[row(SGU_WIDTH), row(SGU_WIDTH)] + a_specs + [
            pl.BlockSpec((None, N_SGU_HEADS, CHUNK, CHUNK), lambda i: (layer, 0, 0, 0)),
            pl.BlockSpec((None, N_SGU_HEADS, CHUNK, HEAD_DIM), lambda i: (layer, 0, 0, 0)),
            pl.BlockSpec((None, 1, D), lambda i: (layer, 0, 0))],
        out_specs=row(D),
        out_shape=jax.ShapeDtypeStruct((M, D), BF16),
        scratch_shapes=[pltpu.VMEM((t, SGU_WIDTH), F32)],
        compiler_params=_params(1),
        name="sgu_merge",
    )(u, gh, *a_args, w_spatial, bias, out_norm_g.reshape(DEPTH, 1, D))


def _mm_resid_kernel(a_ref, w_ref, xa_ref, xb_ref, g_ref, o_ref, wbf_ref):
    _cast_weight_once(w_ref, wbf_ref)
    acc = jnp.dot(a_ref[...], wbf_ref[...], preferred_element_type=F32)
    o_ref[...] = _pick(xa_ref, xb_ref, 1) + g_ref[...] * acc


def _mm_resid(a, w, x, mod, layer, w_index, gate_chunk, tm, tn):
    k = a.shape[1]
    row = lambda n, m: m
    col = lambda n, m: n
    x_args, x_specs = _stream_in(x, tm, tn, row, col)
    return pl.pallas_call(
        _mm_resid_kernel,
        grid=(D // tn, M // tm),
        in_specs=[pl.BlockSpec((tm, k), lambda n, m: (m, 0)),
                  pl.BlockSpec((None, k, tn), lambda n, m: (w_index, 0, n))] + x_specs + [
                  _mod_spec(layer, gate_chunk, tm, row, col, tn=tn)],
        out_specs=pl.BlockSpec((tm, tn), lambda n, m: (m, n)),
        out_shape=jax.ShapeDtypeStruct((M, D), F32),
        scratch_shapes=[pltpu.VMEM((k, tn), BF16)],
        compiler_params=_params(2),
        name="mm_resid",
    )(a, w, *x_args, mod)


def _swiglu(a, b):
    return a * jax.nn.sigmoid(a) * b


def _ffn_gu_kernel(x_ref, wg_ref, wu_ref, o_ref, wgb_ref, wub_ref):
    @pl.when(pl.program_id(1) == 0)
    def _():
        _cast_rows(wg_ref, wgb_ref)
        _cast_rows(wu_ref, wub_ref)
    x = x_ref[...]
    a = jnp.dot(x, wgb_ref[...], preferred_element_type=F32)
    b = jnp.dot(x, wub_ref[...], preferred_element_type=F32)
    o_ref[...] = _swiglu(a, b).astype(BF16)


def _ffn_gate_up(h, w_gate, w_up, j):
    tm, tn = TM, 512
    w_spec = pl.BlockSpec((None, D, tn), lambda n, m: (j, 0, n))
    return pl.pallas_call(
        _ffn_gu_kernel,
        grid=(D_FF // tn, M // tm),
        in_specs=[pl.BlockSpec((tm, D), lambda n, m: (m, 0)), w_spec, w_spec],
        out_specs=pl.BlockSpec((tm, tn), lambda n, m: (m, n)),
        out_shape=jax.ShapeDtypeStruct((M, D_FF), BF16),
        scratch_shapes=[pltpu.VMEM((D, tn), BF16), pltpu.VMEM((D, tn), BF16)],
        compiler_params=_params(2),
        name="ffn_gate_up",
    )(h, w_gate, w_up)


def _route_meta(idx):
    t = T_MOE
    experts = jnp.arange(N_EXPERTS, dtype=I32)
    onehot = (idx[:, :, None] == experts[None, None, :]).astype(I32).sum(axis=1)
    csum = jnp.cumsum(onehot, axis=0)
    rank = csum - onehot
    count = csum[-1]
    ntile = (count + t - 1) // t
    tile_end = jnp.cumsum(ntile)
    tile_start = tile_end - ntile
    nused = tile_end[-1]
    pos = (tile_start * t)[idx] + jnp.take_along_axis(rank, idx, axis=1)
    j = jnp.arange(NT_MOE, dtype=I32)
    te_raw = jnp.minimum(jnp.sum(j[:, None] >= tile_end[None, :], axis=1), N_EXPERTS - 1).astype(I32)
    te = jnp.where(j < nused, te_raw, te_raw[nused - 1])
    first = ((j == tile_start[te]) & (j < nused)).astype(I32)
    later = (ntile[None, :] > 0) & (experts[None, :] > te[:, None])
    nxt = jnp.min(jnp.where(later, experts[None, :], N_EXPERTS), axis=1)
    nxt = jnp.where(nxt == N_EXPERTS, -1, nxt).astype(I32)
    pad_start = tile_start * t + count
    pad_len = ntile * t - count
    tail = jnp.stack([nused * t, (NT_MOE - nused) * (t // ZERO_ROWS)])
    zinfo = jnp.concatenate([pad_start, pad_len, tail]).astype(I32)
    return pos.astype(I32), zinfo, (te, first, nxt, nused.reshape(1).astype(I32))


def _dispatch_kernel(p0_ref, p1_ref, z_ref, h_ref, xs_ref, zero_ref, sem, zsem):
    t = h_ref.shape[0]
    i = pl.program_id(0)
    base = i * t

    def clear_padding(start):
        def go(n, off):
            cp = pltpu.make_async_copy(zero_ref.at[pl.ds(0, n), :], xs_ref.at[pl.ds(off, n), :], zsem.at[0])
            cp.start() if start else cp.wait()

        for e in range(N_EXPERTS):
            off, ln = z_ref[e], z_ref[N_EXPERTS + e]
            end = off + ln
            for b in range(SUBLANE_BITS, PAD_BITS):
                @pl.when(((ln >> b) & 1) == 1)
                def _():
                    go(1 << b, pl.multiple_of(end - ((ln >> b) << b), SUBLANES))
            for k in range(SUBLANES - 1):
                @pl.when(k < (ln & (SUBLANES - 1)))
                def _():
                    go(1, off + k)
        tail0, n_tail = z_ref[2 * N_EXPERTS], z_ref[2 * N_EXPERTS + 1]

        def tail_body(k, carry):
            go(ZERO_ROWS, pl.multiple_of(tail0 + k * ZERO_ROWS, SUBLANES))
            return carry

        lax.fori_loop(0, n_tail, tail_body, 0)

    @pl.when(i == 0)
    def _():
        zero_ref[...] = jnp.zeros(zero_ref.shape, zero_ref.dtype)
        clear_padding(True)

    def issue(r, carry):
        src = h_ref.at[pl.ds(r, 1), :]
        pltpu.make_async_copy(src, xs_ref.at[pl.ds(p0_ref[base + r], 1), :], sem.at[0]).start()
        pltpu.make_async_copy(src, xs_ref.at[pl.ds(p1_ref[base + r], 1), :], sem.at[1]).start()
        return carry

    lax.fori_loop(0, t, issue, 0, unroll=8)
    pltpu.make_async_copy(h_ref, xs_ref.at[pl.ds(0, t), :], sem.at[0]).wait()
    pltpu.make_async_copy(h_ref, xs_ref.at[pl.ds(0, t), :], sem.at[1]).wait()

    @pl.when(i == 0)
    def _():
        clear_padding(False)


def _dispatch(h, pos0, pos1, zinfo):
    t = T_DISPATCH
    return pl.pallas_call(
        _dispatch_kernel,
        grid_spec=pltpu.PrefetchScalarGridSpec(
            num_scalar_prefetch=3,
            grid=(M // t,),
            in_specs=[pl.BlockSpec((t, D), lambda i, p0, p1, z: (i, 0))],
            out_specs=pl.BlockSpec(memory_space=pl.ANY),
            scratch_shapes=[pltpu.VMEM((ZERO_ROWS, D), F32),
                            pltpu.SemaphoreType.DMA((2,)), pltpu.SemaphoreType.DMA((1,))]),
        out_shape=jax.ShapeDtypeStruct((P_MOE, D), F32),
        compiler_params=_params(1),
        name="moe_dispatch",
    )(pos0, pos1, zinfo, h)


def _expert_weight_stream(w_refs, stage_refs, bf_refs, sem, te_ref, first_ref, nxt_ref, tn):
    c = pl.program_id(0)
    j = pl.program_id(1)
    nc = pl.num_programs(0)

    def copies(e, cc):
        col = pl.multiple_of(cc * tn, LANES)
        return [pltpu.make_async_copy(w.at[e, :, pl.ds(col, tn)], st, sem.at[k])
                for k, (w, st) in enumerate(zip(w_refs, stage_refs))]

    def start(e, cc):
        for cp in copies(e, cc):
            cp.start()

    @pl.when((c == 0) & (j == 0))
    def _():
        start(te_ref[0], 0)

    @pl.when(first_ref[j] == 1)
    def _():
        for cp in copies(0, 0):
            cp.wait()
        for st, bf in zip(stage_refs, bf_refs):
            _cast_rows(st, bf)
        ne = nxt_ref[j]

        @pl.when(ne >= 0)
        def _():
            start(ne, c)

        @pl.when((ne < 0) & (c + 1 < nc))
        def _():
            start(te_ref[0], c + 1)


def _gmm_gate_up_kernel(te_ref, first_ref, nxt_ref, nused_ref, xs_ref, wg_ref, wu_ref, o_ref,
                        sg_ref, su_ref, wgb_ref, wub_ref, sem):
    _expert_weight_stream((wg_ref, wu_ref), (sg_ref, su_ref), (wgb_ref, wub_ref), sem,
                          te_ref, first_ref, nxt_ref, TN_GU)

    @pl.when(pl.program_id(1) < nused_ref[0])
    def _():
        half = T_MOE // 2
        for r in range(2):
            rs = slice(r * half, (r + 1) * half)
            x = xs_ref[rs, :].astype(BF16)
            a = jnp.dot(x, wgb_ref[...], preferred_element_type=F32)
            b = jnp.dot(x, wub_ref[...], preferred_element_type=F32)
            o_ref[rs, :] = _swiglu(a, b).astype(BF16)

    @pl.when(pl.program_id(1) >= nused_ref[0])
    def _():
        o_ref[...] = jnp.zeros(o_ref.shape, o_ref.dtype)


def _gmm_down_kernel(te_ref, first_ref, nxt_ref, nused_ref, a_ref, wd_ref, o_ref,
                     sd_ref, wdb_ref, sem):
    _expert_weight_stream((wd_ref,), (sd_ref,), (wdb_ref,), sem, te_ref, first_ref, nxt_ref, TN_DN)

    @pl.when(pl.program_id(1) < nused_ref[0])
    def _():
        o_ref[...] = jnp.dot(a_ref[...], wdb_ref[...], preferred_element_type=F32)

    @pl.when(pl.program_id(1) >= nused_ref[0])
    def _():
        o_ref[...] = jnp.zeros(o_ref.shape, o_ref.dtype)


def _used_tile(j, nused):
    return jnp.minimum(j, nused[0] - 1)


def _gmm_gate_up(xs, w_gate, w_up, meta):
    te, first, nxt, nused = meta
    tn = TN_GU
    return pl.pallas_call(
        _gmm_gate_up_kernel,
        grid_spec=pltpu.PrefetchScalarGridSpec(
            num_scalar_prefetch=4,
            grid=(D_FF_EXPERT // tn, NT_MOE),
            in_specs=[pl.BlockSpec((T_MOE, D), lambda c, j, te, fi, nx, nu: (_used_tile(j, nu), 0)),
                      pl.BlockSpec(memory_space=pl.ANY),
                      pl.BlockSpec(memory_space=pl.ANY)],
            out_specs=pl.BlockSpec((T_MOE, tn), lambda c, j, te, fi, nx, nu: (j, c)),
            scratch_shapes=[pltpu.VMEM((D, tn), F32), pltpu.VMEM((D, tn), F32),
                            pltpu.VMEM((D, tn), BF16), pltpu.VMEM((D, tn), BF16),
                            pltpu.SemaphoreType.DMA((2,))]),
        out_shape=jax.ShapeDtypeStruct((P_MOE, D_FF_EXPERT), BF16),
        compiler_params=_params(2),
        name="moe_gate_up",
    )(te, first, nxt, nused, xs, w_gate, w_up)


def _gmm_down(act, w_down, meta):
    te, first, nxt, nused = meta
    tn = TN_DN
    return pl.pallas_call(
        _gmm_down_kernel,
        grid_spec=pltpu.PrefetchScalarGridSpec(
            num_scalar_prefetch=4,
            grid=(D // tn, NT_MOE),
            in_specs=[pl.BlockSpec((T_MOE, D_FF_EXPERT), lambda c, j, te, fi, nx, nu: (_used_tile(j, nu), 0)),
                      pl.BlockSpec(memory_space=pl.ANY)],
            out_specs=pl.BlockSpec((T_MOE, tn), lambda c, j, te, fi, nx, nu: (j, c)),
            scratch_shapes=[pltpu.VMEM((D_FF_EXPERT, tn), F32), pltpu.VMEM((D_FF_EXPERT, tn), BF16),
                            pltpu.SemaphoreType.DMA((1,))]),
        out_shape=jax.ShapeDtypeStruct((P_MOE, D), F32),
        compiler_params=_params(2),
        name="moe_down",
    )(te, first, nxt, nused, act, w_down)


def _combine_kernel(p0_ref, p1_ref, ys_ref, xa_ref, xb_ref, g_ref, w_ref, oc_ref, ol_ref,
                    a_ref, b_ref, sem):
    t = xa_ref.shape[0]
    i = pl.program_id(0)
    n = pl.num_programs(0)

    def issue(step, slot):
        base = step * t

        def body(r, carry):
            pltpu.make_async_copy(ys_ref.at[pl.ds(p0_ref[base + r], 1), :],
                                  a_ref.at[slot, pl.ds(r, 1), :], sem.at[0, slot]).start()
            pltpu.make_async_copy(ys_ref.at[pl.ds(p1_ref[base + r], 1), :],
                                  b_ref.at[slot, pl.ds(r, 1), :], sem.at[1, slot]).start()
            return carry

        lax.fori_loop(0, t, body, 0, unroll=8)

    @pl.when(i == 0)
    def _():
        issue(0, 0)

    @pl.when(i + 1 < n)
    def _():
        issue(i + 1, (i + 1) % 2)

    slot = i % 2
    pltpu.make_async_copy(ys_ref.at[pl.ds(0, t), :], a_ref.at[slot], sem.at[0, slot]).wait()
    pltpu.make_async_copy(ys_ref.at[pl.ds(0, t), :], b_ref.at[slot], sem.at[1, slot]).wait()
    w = w_ref[...]
    moe = w[:, 0:1] * a_ref[slot] + w[:, 1:2] * b_ref[slot]
    y = _pick(xa_ref, xb_ref, 0) + g_ref[...] * moe
    is_ctx = _is_ctx_tile(i, t)

    @pl.when(is_ctx)
    def _():
        oc_ref[...] = y

    @pl.when(jnp.logical_not(is_ctx))
    def _():
        ol_ref[...] = y


def _combine(ys, x, mod, layer, gate_chunk, wts, pos0, pos1):
    t = T_COMBINE
    n_ctx = MP // t
    row = lambda i, *_: i
    x_args, x_specs = _stream_in(x, t, D, row, lambda i, *_: 0)
    return pl.pallas_call(
        _combine_kernel,
        grid_spec=pltpu.PrefetchScalarGridSpec(
            num_scalar_prefetch=2,
            grid=(M // t,),
            in_specs=[pl.BlockSpec(memory_space=pl.ANY)] + x_specs + [
                _mod_spec(layer, gate_chunk, t, row),
                pl.BlockSpec((t, LANES), lambda i, p0, p1: (i, 0))],
            out_specs=[pl.BlockSpec((t, D), lambda i, p0, p1: (jnp.minimum(i, n_ctx - 1), 0)),
                       pl.BlockSpec((t, D), lambda i, p0, p1: (jnp.maximum(i - n_ctx, 0), 0))],
            scratch_shapes=[pltpu.VMEM((2, t, D), F32), pltpu.VMEM((2, t, D), F32),
                            pltpu.SemaphoreType.DMA((2, 2))]),
        out_shape=[jax.ShapeDtypeStruct((MP, D), F32), jax.ShapeDtypeStruct((MS, D), F32)],
        compiler_params=_params(1),
        name="moe_combine",
    )(pos0, pos1, ys, *x_args, mod, wts)


def _moe(x, h, idx, wts, mod, layer, w_gate, w_up, w_down):
    pos, zinfo, meta = _route_meta(idx[:, :TOP_K])
    pos0, pos1 = pos[:, 0], pos[:, 1]
    xs = _dispatch(h, pos0, pos1, zinfo)
    act = _gmm_gate_up(xs, w_gate, w_up, meta)
    ys = _gmm_down(act, w_down, meta)
    return _combine(ys, x, mod, layer, 5, wts, pos0, pos1)


def kernel(x_prompt, x_sample, cache_k, cache_v, c, c_ctx, w_ada, b_ada, norm1_g, norm2_g, w_in, q_norm_g, k_norm_g, sgu_norm_g, w_spatial, b_spatial, out_norm_g, w_out, ffn_w_gate, ffn_w_up, ffn_w_down, w_router, b_router, moe_w_gate, moe_w_up, moe_w_down):
    assert DEPTH == 2
    x = (x_prompt.reshape(MP, D), x_sample.reshape(MS, D))
    cond = jnp.concatenate([c_ctx[None, :], c, jnp.zeros((N_COND - 1 - DEC_BATCH, D), F32)], axis=0)
    mod = _modulation(cond, w_ada, b_ada).reshape(DEPTH, N_COND, 1, N_MOD * D)
    cos, sin = _rope_tables()
    w_in_bf = _cast_in_weights(w_in)

    new_k, new_v = [], []
    for i in range(DEPTH):
        q, kf, kb, vf, vb, u, gh = _in_projections(x, w_in_bf, mod, norm1_g, q_norm_g, k_norm_g,
                                                   sgu_norm_g, cos, sin, i)
        attn_ctx = _attention(q, kb, vb, None, None, i, batch=BATCH, seq=SEQ, row0=0)
        attn_lat = _attention(q, kb, vb, cache_k, cache_v, i, batch=DEC_BATCH, seq=DEC_SEQ, row0=MP)
        o = _sgu_merge(u, gh, attn_ctx, attn_lat, w_spatial, b_spatial, out_norm_g, i)
        j = i // 2
        if i % 2 == 0:
            x, h2 = _out_proj(o, w_out, x, mod, norm2_g, i)
            act = _ffn_gate_up(h2, ffn_w_gate, ffn_w_up, j)
            x = _mm_resid(act, ffn_w_down, x, mod, i, j, 5, 512, 512)
        else:
            x, h2, idx, wts = _out_proj(o, w_out, x, mod, norm2_g, i, router=(w_router[j], b_router[j]))
            x = _moe(x, h2, idx, wts, mod, i, moe_w_gate[j], moe_w_up[j], moe_w_down[j])
        new_k.append(kf[:MP].reshape(BATCH, SEQ, N_KV_HEADS, HEAD_DIM))
        new_v.append(vf[:MP].reshape(BATCH, SEQ, N_KV_HEADS, HEAD_DIM))

    y_prompt = x[0].reshape(BATCH, SEQ, D)
    y_sample = x[1].reshape(DEC_BATCH, DEC_SEQ, D)
    return (y_prompt, y_sample, jnp.stack(new_k, axis=1), jnp.stack(new_v, axis=1))
```

```python
import functools

import jax
import jax.numpy as jnp
from jax import lax
from jax.experimental import pallas as pl
from jax.experimental.pallas import tpu as pltpu

F32 = jnp.float32
BF16 = jnp.bfloat16
I32 = jnp.int32

D = 2048
BATCH, SEQ = 16, 256
DEC_BATCH, DEC_SEQ = 4, 2048
PAST_LEN = 256
DEPTH = 2
GRID_W = 64
CHUNK = 128
HEAD_DIM = 128
N_Q_HEADS, N_KV_HEADS = 8, 2
Q_PER_KV = N_Q_HEADS // N_KV_HEADS
ATTN_WIDTH = N_Q_HEADS * HEAD_DIM
KV_WIDTH = N_KV_HEADS * HEAD_DIM
N_SGU_HEADS = 8
SGU_WIDTH = N_SGU_HEADS * HEAD_DIM
IN_WIDTH = ATTN_WIDTH + 2 * KV_WIDTH + 2 * SGU_WIDTH
ROPE_THETA = 10000.0
ROPE_AXIS_DIM = HEAD_DIM // 2
D_FF = 5632
N_EXPERTS = 8
TOP_K = 2
D_FF_EXPERT = 2816
N_MOD = 6
EPS = 1e-6
ATTN_SCALE = HEAD_DIM ** -0.5
LOG2_E = 1.4426950408889634

MP = BATCH * SEQ
MS = DEC_BATCH * DEC_SEQ
M = MP + MS
N_COND = 8
LANES = 128
SUBLANES = 8
SUBLANE_BITS = 3

VMEM_LIMIT = 56 * 1024 * 1024

TM = 1024
TN_IN = 2 * KV_WIDTH
T_NORM = 512
T_Q = 512
ATTN_ROWS = 1024
T_MOE = 512
P_MOE = M * TOP_K + N_EXPERTS * T_MOE
NT_MOE = P_MOE // T_MOE
TN_GU = D_FF_EXPERT // 2
TN_DN = D // 2
T_DISPATCH = 1024
T_COMBINE = 256
ZERO_ROWS = T_MOE // 2
PAD_BITS = ZERO_ROWS.bit_length()


def _params(n_axes):
    return pltpu.CompilerParams(dimension_semantics=("arbitrary",) * n_axes,
                                vmem_limit_bytes=VMEM_LIMIT)


def _cond_row(i, t):
    return jnp.where(i < MP // t, 0, 1 + (i - MP // t) // (DEC_SEQ // t))


def _is_ctx_tile(i, t):
    return i < MP // t


def _stream_in(x, t, width, row_of, col_of):
    n_ctx = MP // t
    pair = isinstance(x, tuple)
    base = 0 if pair else n_ctx
    ctx = pl.BlockSpec((t, width), lambda *g: (jnp.minimum(row_of(*g), n_ctx - 1), col_of(*g)))
    lat = pl.BlockSpec((t, width), lambda *g: (base + jnp.maximum(row_of(*g) - n_ctx, 0), col_of(*g)))
    return (list(x) if pair else [x, x]), [ctx, lat]


def _mod_spec(layer, chunk, t, row_of, col_of=None, tn=D):
    per = D // tn

    def index_map(*g):
        col = chunk * per + (col_of(*g) if col_of is not None else 0)
        return (layer, _cond_row(row_of(*g), t), 0, col)

    return pl.BlockSpec((None, None, 1, tn), index_map)


def _ada_kernel(c_ref, w_ref, b_ref, o_ref):
    c = c_ref[...]
    s = (c * jax.nn.sigmoid(c)).astype(BF16)
    o_ref[...] = jnp.dot(s, w_ref[...].astype(BF16), preferred_element_type=F32) + b_ref[...]


def _modulation(cond, w_ada, b_ada):
    tn = 1024
    width = N_MOD * D
    return pl.pallas_call(
        _ada_kernel,
        grid=(DEPTH, width // tn),
        in_specs=[pl.BlockSpec((N_COND, D), lambda l, n: (0, 0)),
                  pl.BlockSpec((None, D, tn), lambda l, n: (l, 0, n)),
                  pl.BlockSpec((None, 1, tn), lambda l, n: (l, 0, n))],
        out_specs=pl.BlockSpec((None, N_COND, tn), lambda l, n: (l, 0, n)),
        out_shape=jax.ShapeDtypeStruct((DEPTH, N_COND, width), F32),
        compiler_params=_params(2),
        name="modulation",
    )(cond, w_ada, b_ada.reshape(DEPTH, 1, width))


def _modulated_norm(x, g, sc, sh):
    y = x * lax.rsqrt(jnp.mean(x * x, axis=-1, keepdims=True) + EPS)
    return y * (g * (1.0 + sc)) + sh


def _pick(xa_ref, xb_ref, axis):
    t = xa_ref.shape[0]
    return jnp.where(_is_ctx_tile(pl.program_id(axis), t), xa_ref[...], xb_ref[...])


def _stream_tile_copy(xa_ref, xb_ref, lat_row0, buf_ref, sem, tile, slot, start):
    t = buf_ref.shape[1]

    def copy(src_ref, row):
        return pltpu.make_async_copy(src_ref.at[pl.ds(pl.multiple_of(row, t), t), :],
                                     buf_ref.at[slot], sem.at[slot])

    if not start:
        copy(xa_ref, 0).wait()
        return
    is_ctx = _is_ctx_tile(tile, t)

    @pl.when(is_ctx)
    def _():
        copy(xa_ref, tile * t).start()

    @pl.when(jnp.logical_not(is_ctx))
    def _():
        copy(xb_ref, lat_row0 + (tile - MP // t) * t).start()


def _next_stream_tile(xa_ref, xb_ref, lat_row0, buf_ref, sem, tile, n_tiles):
    @pl.when(tile == 0)
    def _():
        _stream_tile_copy(xa_ref, xb_ref, lat_row0, buf_ref, sem, 0, 0, True)

    @pl.when(tile + 1 < n_tiles)
    def _():
        _stream_tile_copy(xa_ref, xb_ref, lat_row0, buf_ref, sem, tile + 1, (tile + 1) % 2, True)

    slot = tile % 2
    _stream_tile_copy(xa_ref, xb_ref, lat_row0, buf_ref, sem, tile, slot, False)
    return slot


def _stream_hbm(x):
    return (x[0], x[1], 0) if isinstance(x, tuple) else (x, x, MP)


def _split_bf16(a):
    hi = a.astype(BF16)
    return hi, (a - hi.astype(F32)).astype(BF16)


def _route_top2(h, wr, br, idx_ref, wt_ref):
    h_hi, h_lo = _split_bf16(h)
    w_hi, w_lo = _split_bf16(wr)
    logits = (jnp.dot(h_hi, w_hi, preferred_element_type=F32)
              + jnp.dot(h_lo, w_hi, preferred_element_type=F32)
              + jnp.dot(h_hi, w_lo, preferred_element_type=F32)) + br
    lane = lax.broadcasted_iota(I32, logits.shape, 1)
    neg = jnp.float32(-jnp.inf)
    lg = jnp.where(lane < N_EXPERTS, logits, neg)
    m1 = jnp.max(lg, axis=-1, keepdims=True)
    i1 = jnp.min(jnp.where(lg == m1, lane, LANES), axis=-1, keepdims=True)
    lg2 = jnp.where(lane == i1, neg, lg)
    m2 = jnp.max(lg2, axis=-1, keepdims=True)
    i2 = jnp.min(jnp.where(lg2 == m2, lane, LANES), axis=-1, keepdims=True)
    e = jnp.exp(m2 - m1)
    w1 = 1.0 / (1.0 + e)
    w2 = e / (1.0 + e)
    idx_ref[...] = jnp.where(lane == 0, i1, jnp.where(lane == 1, i2, 0))
    wt_ref[...] = jnp.where(lane == 0, w1, jnp.where(lane == 1, w2, 0.0))


W_PIECE = 512


def _out_proj_kernel(*refs, layer, lat_row0, route):
    if route:
        (o_ref, w_ref, xa_ref, xb_ref, gate_ref, g_ref, sc_ref, sh_ref, wr_ref, br_ref,
         xn_ref, h_ref, idx_ref, wt_ref, stage_ref, wbf_ref, xbuf_ref, wsem, xsem) = refs
    else:
        (o_ref, w_ref, xa_ref, xb_ref, gate_ref, g_ref, sc_ref, sh_ref,
         xn_ref, h_ref, stage_ref, wbf_ref, xbuf_ref, wsem, xsem) = refs
    i = pl.program_id(0)

    @pl.when(i == 0)
    def _():
        for p in range(D // W_PIECE):
            cols = pl.ds(p * W_PIECE, W_PIECE)
            cp = pltpu.make_async_copy(w_ref.at[layer, :, cols], stage_ref, wsem.at[0])
            cp.start()
            cp.wait()
            _cast_rows(stage_ref, wbf_ref.at[:, cols])

    slot = _next_stream_tile(xa_ref, xb_ref, lat_row0, xbuf_ref, xsem, i, pl.num_programs(0))
    acc = jnp.dot(o_ref[...], wbf_ref[...], preferred_element_type=F32)
    x_new = xbuf_ref[slot] + gate_ref[...] * acc
    xn_ref[...] = x_new
    h = _modulated_norm(x_new, g_ref[...], sc_ref[...], sh_ref[...])
    if route:
        h_ref[...] = h
        _route_top2(h, wr_ref[...], br_ref[...], idx_ref, wt_ref)
    else:
        h_ref[...] = h.astype(BF16)


def _out_proj(o, w_out, x, mod, norm2_g, layer, router=None):
    t = T_NORM
    row = lambda i: i
    xa, xb, lat_row0 = _stream_hbm(x)
    route = router is not None
    anyspace = pl.BlockSpec(memory_space=pl.ANY)
    rows = lambda w: pl.BlockSpec((t, w), lambda i: (i, 0))
    in_specs = [rows(D), anyspace, anyspace, anyspace,
                _mod_spec(layer, 2, t, row),
                pl.BlockSpec((None, 1, D), lambda i: (layer, 0, 0)),
                _mod_spec(layer, 4, t, row), _mod_spec(layer, 3, t, row)]
    args = [o, w_out, xa, xb, mod, norm2_g.reshape(DEPTH, 1, D), mod, mod]
    out_specs = [rows(D), rows(D)]
    out_shape = [jax.ShapeDtypeStruct((M, D), F32), jax.ShapeDtypeStruct((M, D), F32 if route else BF16)]
    if route:
        w_router, b_router = router
        args += [jnp.zeros((D, LANES), F32).at[:, :N_EXPERTS].set(w_router),
                 jnp.zeros((1, LANES), F32).at[0, :N_EXPERTS].set(b_router)]
        in_specs += [pl.BlockSpec((D, LANES), lambda i: (0, 0)), pl.BlockSpec((1, LANES), lambda i: (0, 0))]
        out_specs += [rows(LANES), rows(LANES)]
        out_shape += [jax.ShapeDtypeStruct((M, LANES), I32), jax.ShapeDtypeStruct((M, LANES), F32)]
    return pl.pallas_call(
        functools.partial(_out_proj_kernel, layer=layer, lat_row0=lat_row0, route=route),
        grid=(M // t,),
        in_specs=in_specs,
        out_specs=out_specs,
        out_shape=out_shape,
        scratch_shapes=[pltpu.VMEM((D, W_PIECE), F32), pltpu.VMEM((D, D), BF16),
                        pltpu.VMEM((2, t, D), F32),
                        pltpu.SemaphoreType.DMA((1,)), pltpu.SemaphoreType.DMA((2,))],
        compiler_params=_params(1),
        name="out_proj_router" if route else "out_proj",
    )(*args)


CAST_ROWS = 256


def _cast_rows(src_ref, dst_ref):
    def body(r, carry):
        rs = pl.ds(pl.multiple_of(r * CAST_ROWS, CAST_ROWS), CAST_ROWS)
        dst_ref[rs, :] = src_ref[rs, :].astype(BF16)
        return carry

    lax.fori_loop(0, src_ref.shape[0] // CAST_ROWS, body, 0)


def _cast_weight_once(w_ref, wbf_ref):
    @pl.when(pl.program_id(1) == 0)
    def _():
        _cast_rows(w_ref, wbf_ref)


def _head_rms(a):
    return lax.rsqrt(jnp.mean(a * a, axis=-1, keepdims=True) + EPS)


def _rope_partner(ag, perm):
    hi, lo = _split_bf16(ag)
    return (jnp.dot(hi, perm, preferred_element_type=F32)
            + jnp.dot(lo, perm, preferred_element_type=F32))


def _cast_kernel(w_ref, o_ref):
    o_ref[...] = w_ref[...].astype(BF16)


def _cast_in_weights(w_in):
    spec = pl.BlockSpec((None, D, TN_IN), lambda l, n: (l, 0, n))
    return pl.pallas_call(
        _cast_kernel,
        grid=(DEPTH, IN_WIDTH // TN_IN),
        in_specs=[spec],
        out_specs=spec,
        out_shape=jax.ShapeDtypeStruct(w_in.shape, BF16),
        compiler_params=_params(2),
        name="cast_w_in",
    )(w_in)


N_Q_TILES = ATTN_WIDTH // TN_IN
KV_TILE = N_Q_TILES
U_TILE0 = KV_TILE + 1
G_TILE0 = U_TILE0 + SGU_WIDTH // TN_IN
N_IN_TILES = IN_WIDTH // TN_IN


def _in_proj_kernel(xa_ref, xb_ref, n1_ref, sc0_ref, sh0_ref, sc1_ref, sh1_ref, w_ref,
                    qg_ref, kg_ref, sg_ref, pq_ref, pk_ref, cos_ref, sin_ref,
                    q_ref, kf_ref, kb_ref, vf_ref, vb_ref, u_ref, gh_ref,
                    xbuf_ref, h_ref, xsem, *, lat_row0):
    m = pl.program_id(0)
    n = pl.program_id(1)
    nm = pl.num_programs(0)
    cur = m % 2
    nxt = (m + 1) % 2
    has_next = m + 1 < nm
    fetch = functools.partial(_stream_tile_copy, xa_ref, xb_ref, lat_row0, xbuf_ref, xsem)

    def norm_rows(slot, rows, sc_ref, sh_ref):
        x = xbuf_ref[slot, rows, :]
        h_ref[slot, rows, :] = _modulated_norm(x, n1_ref[...], sc_ref[...], sh_ref[...]).astype(BF16)

    @pl.when((m == 0) & (n == 0))
    def _():
        fetch(0, 0, True)
        fetch(0, 0, False)
        norm_rows(0, slice(None), sc0_ref, sh0_ref)

    @pl.when((n == 0) & has_next)
    def _():
        fetch(m + 1, nxt, True)

    def matmul():
        return jnp.dot(h_ref[cur], w_ref[...], preferred_element_type=F32)

    for rope in (False, True):
        positioned = jnp.logical_not(_is_ctx_tile(m, TM)) if rope else _is_ctx_tile(m, TM)

        @pl.when((n < N_Q_TILES) & positioned)
        def _():
            acc = matmul()
            ag = acc * qg_ref[...]
            if rope:
                partner = _rope_partner(ag, pq_ref[...])
                cos, sin = cos_ref[...], sin_ref[...]
            for h in range(TN_IN // HEAD_DIM):
                sl = slice(h * HEAD_DIM, (h + 1) * HEAD_DIM)
                r = _head_rms(acc[:, sl]) * (ATTN_SCALE * LOG2_E)
                qh = ag[:, sl] * cos + partner[:, sl] * sin if rope else ag[:, sl]
                q_ref[:, sl] = (qh * r).astype(BF16)

        @pl.when((n == KV_TILE) & positioned)
        def _():
            acc = matmul()
            k = acc[:, :KV_WIDTH]
            ag = k * kg_ref[...]
            if rope:
                partner = _rope_partner(ag, pk_ref[...])
                cos, sin = cos_ref[...], sin_ref[...]
            for h in range(N_KV_HEADS):
                sl = slice(h * HEAD_DIM, (h + 1) * HEAD_DIM)
                r = _head_rms(k[:, sl])
                kf = ag[:, sl] * r
                kf_ref[:, sl] = kf
                kb_ref[:, sl] = ((ag[:, sl] * cos + partner[:, sl] * sin) * r if rope else kf).astype(BF16)
            v = acc[:, KV_WIDTH:]
            vf_ref[...] = v
            vb_ref[...] = v.astype(BF16)

    half = TM // (G_TILE0 - U_TILE0)
    for k in range(G_TILE0 - U_TILE0):
        for parity in range(2):
            @pl.when((n == U_TILE0 + k) & has_next & (cur == parity))
            def _():
                if k == 0:
                    fetch(m + 1, 1 - parity, False)
                u_ref[...] = jnp.dot(h_ref[parity], w_ref[...], preferred_element_type=F32).astype(BF16)
                norm_rows(1 - parity, slice(k * half, (k + 1) * half), sc1_ref, sh1_ref)

        @pl.when((n == U_TILE0 + k) & jnp.logical_not(has_next))
        def _():
            u_ref[...] = matmul().astype(BF16)

    @pl.when(n >= G_TILE0)
    def _():
        acc = matmul()
        for h in range(TN_IN // HEAD_DIM):
            sl = slice(h * HEAD_DIM, (h + 1) * HEAD_DIM)
            a = acc[:, sl]
            gh_ref[:, sl] = (a * _head_rms(a) * sg_ref[:, sl]).astype(BF16)


def _rope_tables():
    n_rows = DEC_SEQ // GRID_W
    rows = jnp.broadcast_to(jnp.arange(n_rows)[:, None], (n_rows, GRID_W)).reshape(-1)
    cols = jnp.broadcast_to(jnp.arange(GRID_W)[None, :], (n_rows, GRID_W)).reshape(-1)
    inv = ROPE_THETA ** (-jnp.arange(0, ROPE_AXIS_DIM, 2, dtype=F32) / ROPE_AXIS_DIM)
    ang_r = rows.astype(F32)[:, None] * inv
    ang_c = cols.astype(F32)[:, None] * inv
    cos = jnp.concatenate([jnp.cos(ang_r), jnp.cos(ang_r), jnp.cos(ang_c), jnp.cos(ang_c)], axis=1)
    sin = jnp.concatenate([-jnp.sin(ang_r), jnp.sin(ang_r), -jnp.sin(ang_c), jnp.sin(ang_c)], axis=1)
    return cos, sin


def _partner_matrix(n_heads):
    w = n_heads * HEAD_DIM
    quarter = ROPE_AXIS_DIM // 2
    j = jnp.arange(w)
    partner = jnp.where((j % ROPE_AXIS_DIM) < quarter, j + quarter, j - quarter)
    return (jnp.arange(w)[:, None] == partner[None, :]).astype(BF16)


def _rope_block(m):
    return jnp.maximum(m - MP // TM, 0) % (DEC_SEQ // TM)


def _in_projections(x, w_in_bf, mod, norm1_g, q_norm_g, k_norm_g, sgu_norm_g, cos, sin, layer):
    tn = TN_IN
    xa, xb, lat_row0 = _stream_hbm(x)
    row = lambda m, n: m
    next_row = lambda m, n: jnp.minimum(m + 1, M // TM - 1)
    anyspace = pl.BlockSpec(memory_space=pl.ANY)
    const = lambda shape: pl.BlockSpec(shape, lambda m, n: (0,) * len(shape))
    rope_spec = pl.BlockSpec((TM, HEAD_DIM), lambda m, n: (_rope_block(m), 0))
    q_heads = tn // HEAD_DIM
    q_gain = jnp.tile(q_norm_g[layer], q_heads)[None, :]
    k_gain = jnp.tile(k_norm_g[layer], N_KV_HEADS)[None, :]
    g_tile = lambda n: jnp.clip(n - G_TILE0, 0, SGU_WIDTH // tn - 1)
    kv_out = pl.BlockSpec((TM, KV_WIDTH), lambda m, n: (m, 0))
    kv_shape = lambda dt: jax.ShapeDtypeStruct((M, KV_WIDTH), dt)
    return pl.pallas_call(
        functools.partial(_in_proj_kernel, lat_row0=lat_row0),
        grid=(M // TM, N_IN_TILES),
        in_specs=[anyspace, anyspace,
                  pl.BlockSpec((None, 1, D), lambda m, n: (layer, 0, 0)),
                  _mod_spec(layer, 1, TM, row), _mod_spec(layer, 0, TM, row),
                  _mod_spec(layer, 1, TM, next_row), _mod_spec(layer, 0, TM, next_row),
                  pl.BlockSpec((None, D, tn), lambda m, n: (layer, 0, n)),
                  const((1, tn)), const((1, KV_WIDTH)),
                  pl.BlockSpec((None, 1, tn), lambda m, n: (layer, 0, g_tile(n))),
                  const((tn, tn)), const((KV_WIDTH, KV_WIDTH)), rope_spec, rope_spec],
        out_specs=[pl.BlockSpec((TM, tn), lambda m, n: (m, jnp.minimum(n, N_Q_TILES - 1))),
                   kv_out, kv_out, kv_out, kv_out,
                   pl.BlockSpec((TM, tn), lambda m, n: (m, jnp.clip(n - U_TILE0, 0, SGU_WIDTH // tn - 1))),
                   pl.BlockSpec((TM, tn), lambda m, n: (m, g_tile(n)))],
        out_shape=[jax.ShapeDtypeStruct((M, ATTN_WIDTH), BF16),
                   kv_shape(F32), kv_shape(BF16), kv_shape(F32), kv_shape(BF16),
                   jax.ShapeDtypeStruct((M, SGU_WIDTH), BF16),
                   jax.ShapeDtypeStruct((M, SGU_WIDTH), BF16)],
        scratch_shapes=[pltpu.VMEM((2, TM, D), F32), pltpu.VMEM((2, TM, D), BF16),
                        pltpu.SemaphoreType.DMA((2,))],
        compiler_params=_params(2),
        name="in_proj",
    )(xa, xb, norm1_g.reshape(DEPTH, 1, D), mod, mod, mod, mod, w_in_bf, q_gain, k_gain,
      sgu_norm_g.reshape(DEPTH, 1, SGU_WIDTH), _partner_matrix(q_heads), _partner_matrix(N_KV_HEADS),
      cos, sin)


def _qk(q, k):
    return lax.dot_general(q, k, (((1,), (1,)), ((), ())), preferred_element_type=F32)


def _attn_kernel(*refs, has_cache, n_batch, seq, tq):
    def with_ones(v):
        return jnp.concatenate([v, jnp.ones_like(v)], axis=1)

    if has_cache:
        q_ref, k_ref, v_ref, kc_ref, vc_ref, o_ref = refs
    else:
        q_ref, k_ref, v_ref, o_ref = refs
    for b in range(n_batch):
        rows_q = slice(b * tq, (b + 1) * tq)
        rows_k = slice(b * seq, (b + 1) * seq)
        for kv in range(N_KV_HEADS):
            kv_cols = slice(kv * HEAD_DIM, (kv + 1) * HEAD_DIM)
            k = k_ref[rows_k, kv_cols]
            v = with_ones(v_ref[rows_k, kv_cols])
            if has_cache:
                kc = kc_ref[:, kv_cols].astype(BF16)
                vc = with_ones(vc_ref[:, kv_cols].astype(BF16))
            for g in range(Q_PER_KV):
                head = kv * Q_PER_KV + g
                sl = slice(head * HEAD_DIM, (head + 1) * HEAD_DIM)
                q = q_ref[rows_q, sl]
                s = _qk(q, k)
                m = jnp.max(s, axis=-1, keepdims=True)
                if has_cache:
                    sc = _qk(q, kc)
                    m = jnp.maximum(m, jnp.max(sc, axis=-1, keepdims=True))
                o = jnp.dot(jnp.exp2(s - m).astype(BF16), v, preferred_element_type=F32)
                if has_cache:
                    o = o + jnp.dot(jnp.exp2(sc - m).astype(BF16), vc, preferred_element_type=F32)
                o_ref[rows_q, sl] = (o[:, :HEAD_DIM] / o[:, HEAD_DIM:]).astype(BF16)


def _attention(q, kb, vb, cache_k, cache_v, layer, *, batch, seq, row0):
    has_cache = cache_k is not None
    tq = min(T_Q, seq)
    nq = seq // tq
    n_batch = max(1, ATTN_ROWS // seq) if nq == 1 else 1
    q_spec = pl.BlockSpec((n_batch * tq, ATTN_WIDTH), lambda b, i: (row0 // (n_batch * tq) + b * nq + i, 0))
    kv_spec = pl.BlockSpec((n_batch * seq, KV_WIDTH), lambda b, i: (row0 // (n_batch * seq) + b, 0))
    in_specs = [q_spec, kv_spec, kv_spec]
    args = [q, kb, vb]
    if has_cache:
        c_spec = pl.BlockSpec((None, None, PAST_LEN, KV_WIDTH), lambda b, i: (b, layer, 0, 0))
        in_specs += [c_spec, c_spec]
        args += [cache_k.reshape(DEC_BATCH, DEPTH, PAST_LEN, KV_WIDTH),
                 cache_v.reshape(DEC_BATCH, DEPTH, PAST_LEN, KV_WIDTH)]
    return pl.pallas_call(
        functools.partial(_attn_kernel, has_cache=has_cache, n_batch=n_batch, seq=seq, tq=tq),
        grid=(batch // n_batch, nq),
        in_specs=in_specs,
        out_specs=pl.BlockSpec((n_batch * tq, ATTN_WIDTH), lambda b, i: (b * nq + i, 0)),
        out_shape=jax.ShapeDtypeStruct((batch * seq, ATTN_WIDTH), BF16),
        compiler_params=_params(2),
        name="attention_cached" if has_cache else "attention",
    )(*args)


def _sgu_merge_kernel(u_ref, gh_ref, ap_ref, as_ref, ws_ref, bs_ref, gn_ref, o_ref, sgu_ref):
    t = u_ref.shape[0]
    a = _pick(ap_ref, as_ref, 0).astype(F32)
    a = a * lax.rsqrt(jnp.mean(a * a, axis=-1, keepdims=True) + EPS) * gn_ref[:, :ATTN_WIDTH]
    o_ref[:, :ATTN_WIDTH] = a.astype(BF16)
    for h in range(N_SGU_HEADS):
        cs = slice(h * HEAD_DIM, (h + 1) * HEAD_DIM)
        w = ws_ref[h].astype(BF16)
        b = bs_ref[h]
        for c in range(t // CHUNK):
            rs = slice(c * CHUNK, (c + 1) * CHUNK)
            mixed = jnp.dot(w, gh_ref[rs, cs], preferred_element_type=F32) + b
            sgu_ref[rs, cs] = u_ref[rs, cs].astype(F32) * mixed
    s = sgu_ref[...]
    s = s * lax.rsqrt(jnp.mean(s * s, axis=-1, keepdims=True) + EPS) * gn_ref[:, ATTN_WIDTH:]
    o_ref[:, ATTN_WIDTH:] = s.astype(BF16)


def _sgu_merge(u, gh, attn_ctx, attn_lat, w_spatial, b_spatial, out_norm_g, layer):
    t = T_NORM
    bias = jnp.broadcast_to(b_spatial[:, :, :, None], (DEPTH, N_SGU_HEADS, CHUNK, HEAD_DIM))
    row = lambda w: pl.BlockSpec((t, w), lambda i: (i, 0))
    a_args, a_specs = _stream_in((attn_ctx, attn_lat), t, ATTN_WIDTH, lambda i: i, lambda i: 0)
    return pl.pallas_call(
        _sgu_merge_kernel,
        grid=(M // t,),
        in_specs=[row(SGU_WIDTH), row(SGU_WIDTH)] + a_specs + [
            pl.BlockSpec((None, N_SGU_HEADS, CHUNK, CHUNK), lambda i: (layer, 0, 0, 0)),
            pl.BlockSpec((None, N_SGU_HEADS, CHUNK, HEAD_DIM), lambda i: (layer, 0, 0, 0)),
            pl.BlockSpec((None, 1, D), lambda i: (layer, 0, 0))],
        out_specs=row(D),
        out_shape=jax.ShapeDtypeStruct((M, D), BF16),
        scratch_shapes=[pltpu.VMEM((t, SGU_WIDTH), F32)],
        compiler_params=_params(1),
        name="sgu_merge",
    )(u, gh, *a_args, w_spatial, bias, out_norm_g.reshape(DEPTH, 1, D))


def _mm_resid_kernel(a_ref, w_ref, xa_ref, xb_ref, g_ref, o_ref, wbf_ref):
    _cast_weight_once(w_ref, wbf_ref)
    acc = jnp.dot(a_ref[...], wbf_ref[...], preferred_element_type=F32)
    o_ref[...] = _pick(xa_ref, xb_ref, 1) + g_ref[...] * acc


def _mm_resid(a, w, x, mod, layer, w_index, gate_chunk, tm, tn):
    k = a.shape[1]
    row = lambda n, m: m
    col = lambda n, m: n
    x_args, x_specs = _stream_in(x, tm, tn, row, col)
    return pl.pallas_call(
        _mm_resid_kernel,
        grid=(D // tn, M // tm),
        in_specs=[pl.BlockSpec((tm, k), lambda n, m: (m, 0)),
                  pl.BlockSpec((None, k, tn), lambda n, m: (w_index, 0, n))] + x_specs + [
                  _mod_spec(layer, gate_chunk, tm, row, col, tn=tn)],
        out_specs=pl.BlockSpec((tm, tn), lambda n, m: (m, n)),
        out_shape=jax.ShapeDtypeStruct((M, D), F32),
        scratch_shapes=[pltpu.VMEM((k, tn), BF16)],
        compiler_params=_params(2),
        name="mm_resid",
    )(a, w, *x_args, mod)


def _swiglu(a, b):
    return a * jax.nn.sigmoid(a) * b


def _ffn_gu_kernel(x_ref, wg_ref, wu_ref, o_ref, wgb_ref, wub_ref):
    @pl.when(pl.program_id(1) == 0)
    def _():
        _cast_rows(wg_ref, wgb_ref)
        _cast_rows(wu_ref, wub_ref)
    x = x_ref[...]
    a = jnp.dot(x, wgb_ref[...], preferred_element_type=F32)
    b = jnp.dot(x, wub_ref[...], preferred_element_type=F32)
    o_ref[...] = _swiglu(a, b).astype(BF16)


def _ffn_gate_up(h, w_gate, w_up, j):
    tm, tn = TM, 512
    w_spec = pl.BlockSpec((None, D, tn), lambda n, m: (j, 0, n))
    return pl.pallas_call(
        _ffn_gu_kernel,
        grid=(D_FF // tn, M // tm),
        in_specs=[pl.BlockSpec((tm, D), lambda n, m: (m, 0)), w_spec, w_spec],
        out_specs=pl.BlockSpec((tm, tn), lambda n, m: (m, n)),
        out_shape=jax.ShapeDtypeStruct((M, D_FF), BF16),
        scratch_shapes=[pltpu.VMEM((D, tn), BF16), pltpu.VMEM((D, tn), BF16)],
        compiler_params=_params(2),
        name="ffn_gate_up",
    )(h, w_gate, w_up)


def _route_meta(idx):
    t = T_MOE
    experts = jnp.arange(N_EXPERTS, dtype=I32)
    onehot = (idx[:, :, None] == experts[None, None, :]).astype(I32).sum(axis=1)
    csum = jnp.cumsum(onehot, axis=0)
    rank = csum - onehot
    count = csum[-1]
    ntile = (count + t - 1) // t
    tile_end = jnp.cumsum(ntile)
    tile_start = tile_end - ntile
    nused = tile_end[-1]
    pos = (tile_start * t)[idx] + jnp.take_along_axis(rank, idx, axis=1)
    j = jnp.arange(NT_MOE, dtype=I32)
    te_raw = jnp.minimum(jnp.sum(j[:, None] >= tile_end[None, :], axis=1), N_EXPERTS - 1).astype(I32)
    te = jnp.where(j < nused, te_raw, te_raw[nused - 1])
    first = ((j == tile_start[te]) & (j < nused)).astype(I32)
    later = (ntile[None, :] > 0) & (experts[None, :] > te[:, None])
    nxt = jnp.min(jnp.where(later, experts[None, :], N_EXPERTS), axis=1)
    nxt = jnp.where(nxt == N_EXPERTS, -1, nxt).astype(I32)
    pad_start = tile_start * t + count
    pad_len = ntile * t - count
    tail = jnp.stack([nused * t, (NT_MOE - nused) * (t // ZERO_ROWS)])
    zinfo = jnp.concatenate([pad_start, pad_len, tail]).astype(I32)
    return pos.astype(I32), zinfo, (te, first, nxt, nused.reshape(1).astype(I32))


def _dispatch_kernel(p0_ref, p1_ref, z_ref, h_ref, xs_ref, zero_ref, sem, zsem):
    t = h_ref.shape[0]
    i = pl.program_id(0)
    base = i * t

    def clear_padding(start):
        def go(n, off):
            cp = pltpu.make_async_copy(zero_ref.at[pl.ds(0, n), :], xs_ref.at[pl.ds(off, n), :], zsem.at[0])
            cp.start() if start else cp.wait()

        for e in range(N_EXPERTS):
            off, ln = z_ref[e], z_ref[N_EXPERTS + e]
            end = off + ln
            for b in range(SUBLANE_BITS, PAD_BITS):
                @pl.when(((ln >> b) & 1) == 1)
                def _():
                    go(1 << b, pl.multiple_of(end - ((ln >> b) << b), SUBLANES))
            for k in range(SUBLANES - 1):
                @pl.when(k < (ln & (SUBLANES - 1)))
                def _():
                    go(1, off + k)
        tail0, n_tail = z_ref[2 * N_EXPERTS], z_ref[2 * N_EXPERTS + 1]

        def tail_body(k, carry):
            go(ZERO_ROWS, pl.multiple_of(tail0 + k * ZERO_ROWS, SUBLANES))
            return carry

        lax.fori_loop(0, n_tail, tail_body, 0)

    @pl.when(i == 0)
    def _():
        zero_ref[...] = jnp.zeros(zero_ref.shape, zero_ref.dtype)
        clear_padding(True)

    def issue(r, carry):
        src = h_ref.at[pl.ds(r, 1), :]
        pltpu.make_async_copy(src, xs_ref.at[pl.ds(p0_ref[base + r], 1), :], sem.at[0]).start()
        pltpu.make_async_copy(src, xs_ref.at[pl.ds(p1_ref[base + r], 1), :], sem.at[1]).start()
        return carry

    lax.fori_loop(0, t, issue, 0, unroll=8)
    pltpu.make_async_copy(h_ref, xs_ref.at[pl.ds(0, t), :], sem.at[0]).wait()
    pltpu.make_async_copy(h_ref, xs_ref.at[pl.ds(0, t), :], sem.at[1]).wait()

    @pl.when(i == 0)
    def _():
        clear_padding(False)


def _dispatch(h, pos0, pos1, zinfo):
    t = T_DISPATCH
    return pl.pallas_call(
        _dispatch_kernel,
        grid_spec=pltpu.PrefetchScalarGridSpec(
            num_scalar_prefetch=3,
            grid=(M // t,),
            in_specs=[pl.BlockSpec((t, D), lambda i, p0, p1, z: (i, 0))],
            out_specs=pl.BlockSpec(memory_space=pl.ANY),
            scratch_shapes=[pltpu.VMEM((ZERO_ROWS, D), F32),
                            pltpu.SemaphoreType.DMA((2,)), pltpu.SemaphoreType.DMA((1,))]),
        out_shape=jax.ShapeDtypeStruct((P_MOE, D), F32),
        compiler_params=_params(1),
        name="moe_dispatch",
    )(pos0, pos1, zinfo, h)


def _expert_weight_stream(w_refs, stage_refs, bf_refs, sem, te_ref, first_ref, nxt_ref, tn):
    c = pl.program_id(0)
    j = pl.program_id(1)
    nc = pl.num_programs(0)

    def copies(e, cc):
        col = pl.multiple_of(cc * tn, LANES)
        return [pltpu.make_async_copy(w.at[e, :, pl.ds(col, tn)], st, sem.at[k])
                for k, (w, st) in enumerate(zip(w_refs, stage_refs))]

    def start(e, cc):
        for cp in copies(e, cc):
            cp.start()

    @pl.when((c == 0) & (j == 0))
    def _():
        start(te_ref[0], 0)

    @pl.when(first_ref[j] == 1)
    def _():
        for cp in copies(0, 0):
            cp.wait()
        for st, bf in zip(stage_refs, bf_refs):
            _cast_rows(st, bf)
        ne = nxt_ref[j]

        @pl.when(ne >= 0)
        def _():
            start(ne, c)

        @pl.when((ne < 0) & (c + 1 < nc))
        def _():
            start(te_ref[0], c + 1)


def _gmm_gate_up_kernel(te_ref, first_ref, nxt_ref, nused_ref, xs_ref, wg_ref, wu_ref, o_ref,
                        sg_ref, su_ref, wgb_ref, wub_ref, sem):
    _expert_weight_stream((wg_ref, wu_ref), (sg_ref, su_ref), (wgb_ref, wub_ref), sem,
                          te_ref, first_ref, nxt_ref, TN_GU)

    @pl.when(pl.program_id(1) < nused_ref[0])
    def _():
        half = T_MOE // 2
        for r in range(2):
            rs = slice(r * half, (r + 1) * half)
            x = xs_ref[rs, :].astype(BF16)
            a = jnp.dot(x, wgb_ref[...], preferred_element_type=F32)
            b = jnp.dot(x, wub_ref[...], preferred_element_type=F32)
            o_ref[rs, :] = _swiglu(a, b).astype(BF16)

    @pl.when(pl.program_id(1) >= nused_ref[0])
    def _():
        o_ref[...] = jnp.zeros(o_ref.shape, o_ref.dtype)


def _gmm_down_kernel(te_ref, first_ref, nxt_ref, nused_ref, a_ref, wd_ref, o_ref,
                     sd_ref, wdb_ref, sem):
    _expert_weight_stream((wd_ref,), (sd_ref,), (wdb_ref,), sem, te_ref, first_ref, nxt_ref, TN_DN)

    @pl.when(pl.program_id(1) < nused_ref[0])
    def _():
        o_ref[...] = jnp.dot(a_ref[...], wdb_ref[...], preferred_element_type=F32)

    @pl.when(pl.program_id(1) >= nused_ref[0])
    def _():
        o_ref[...] = jnp.zeros(o_ref.shape, o_ref.dtype)


def _used_tile(j, nused):
    return jnp.minimum(j, nused[0] - 1)


def _gmm_gate_up(xs, w_gate, w_up, meta):
    te, first, nxt, nused = meta
    tn = TN_GU
    return pl.pallas_call(
        _gmm_gate_up_kernel,
        grid_spec=pltpu.PrefetchScalarGridSpec(
            num_scalar_prefetch=4,
            grid=(D_FF_EXPERT // tn, NT_MOE),
            in_specs=[pl.BlockSpec((T_MOE, D), lambda c, j, te, fi, nx, nu: (_used_tile(j, nu), 0)),
                      pl.BlockSpec(memory_space=pl.ANY),
                      pl.BlockSpec(memory_space=pl.ANY)],
            out_specs=pl.BlockSpec((T_MOE, tn), lambda c, j, te, fi, nx, nu: (j, c)),
            scratch_shapes=[pltpu.VMEM((D, tn), F32), pltpu.VMEM((D, tn), F32),
                            pltpu.VMEM((D, tn), BF16), pltpu.VMEM((D, tn), BF16),
                            pltpu.SemaphoreType.DMA((2,))]),
        out_shape=jax.ShapeDtypeStruct((P_MOE, D_FF_EXPERT), BF16),
        compiler_params=_params(2),
        name="moe_gate_up",
    )(te, first, nxt, nused, xs, w_gate, w_up)


def _gmm_down(act, w_down, meta):
    te, first, nxt, nused = meta
    tn = TN_DN
    return pl.pallas_call(
        _gmm_down_kernel,
        grid_spec=pltpu.PrefetchScalarGridSpec(
            num_scalar_prefetch=4,
            grid=(D // tn, NT_MOE),
            in_specs=[pl.BlockSpec((T_MOE, D_FF_EXPERT), lambda c, j, te, fi, nx, nu: (_used_tile(j, nu), 0)),
                      pl.BlockSpec(memory_space=pl.ANY)],
            out_specs=pl.BlockSpec((T_MOE, tn), lambda c, j, te, fi, nx, nu: (j, c)),
            scratch_shapes=[pltpu.VMEM((D_FF_EXPERT, tn), F32), pltpu.VMEM((D_FF_EXPERT, tn), BF16),
                            pltpu.SemaphoreType.DMA((1,))]),
        out_shape=jax.ShapeDtypeStruct((P_MOE, D), F32),
        compiler_params=_params(2),
        name="moe_down",
    )(te, first, nxt, nused, act, w_down)


def _combine_kernel(p0_ref, p1_ref, ys_ref, xa_ref, xb_ref, g_ref, w_ref, oc_ref, ol_ref,
                    a_ref, b_ref, sem):
    t = xa_ref.shape[0]
    i = pl.program_id(0)
    n = pl.num_programs(0)

    def issue(step, slot):
        base = step * t

        def body(r, carry):
            pltpu.make_async_copy(ys_ref.at[pl.ds(p0_ref[base + r], 1), :],
                                  a_ref.at[slot, pl.ds(r, 1), :], sem.at[0, slot]).start()
            pltpu.make_async_copy(ys_ref.at[pl.ds(p1_ref[base + r], 1), :],
                                  b_ref.at[slot, pl.ds(r, 1), :], sem.at[1, slot]).start()
            return carry

        lax.fori_loop(0, t, body, 0, unroll=8)

    @pl.when(i == 0)
    def _():
        issue(0, 0)

    @pl.when(i + 1 < n)
    def _():
        issue(i + 1, (i + 1) % 2)

    slot = i % 2
    pltpu.make_async_copy(ys_ref.at[pl.ds(0, t), :], a_ref.at[slot], sem.at[0, slot]).wait()
    pltpu.make_async_copy(ys_ref.at[pl.ds(0, t), :], b_ref.at[slot], sem.at[1, slot]).wait()
    w = w_ref[...]
    moe = w[:, 0:1] * a_ref[slot] + w[:, 1:2] * b_ref[slot]
    y = _pick(xa_ref, xb_ref, 0) + g_ref[...] * moe
    is_ctx = _is_ctx_tile(i, t)

    @pl.when(is_ctx)
    def _():
        oc_ref[...] = y

    @pl.when(jnp.logical_not(is_ctx))
    def _():
        ol_ref[...] = y


def _combine(ys, x, mod, layer, gate_chunk, wts, pos0, pos1):
    t = T_COMBINE
    n_ctx = MP // t
    row = lambda i, *_: i
    x_args, x_specs = _stream_in(x, t, D, row, lambda i, *_: 0)
    return pl.pallas_call(
        _combine_kernel,
        grid_spec=pltpu.PrefetchScalarGridSpec(
            num_scalar_prefetch=2,
            grid=(M // t,),
            in_specs=[pl.BlockSpec(memory_space=pl.ANY)] + x_specs + [
                _mod_spec(layer, gate_chunk, t, row),
                pl.BlockSpec((t, LANES), lambda i, p0, p1: (i, 0))],
            out_specs=[pl.BlockSpec((t, D), lambda i, p0, p1: (jnp.minimum(i, n_ctx - 1), 0)),
                       pl.BlockSpec((t, D), lambda i, p0, p1: (jnp.maximum(i - n_ctx, 0), 0))],
            scratch_shapes=[pltpu.VMEM((2, t, D), F32), pltpu.VMEM((2, t, D), F32),
                            pltpu.SemaphoreType.DMA((2, 2))]),
        out_shape=[jax.ShapeDtypeStruct((MP, D), F32), jax.ShapeDtypeStruct((MS, D), F32)],
        compiler_params=_params(1),
        name="moe_combine",
    )(pos0, pos1, ys, *x_args, mod, wts)


def _moe(x, h, idx, wts, mod, layer, w_gate, w_up, w_down):
    pos, zinfo, meta = _route_meta(idx[:, :TOP_K])
    pos0, pos1 = pos[:, 0], pos[:, 1]
    xs = _dispatch(h, pos0, pos1, zinfo)
    act = _gmm_gate_up(xs, w_gate, w_up, meta)
    ys = _gmm_down(act, w_down, meta)
    return _combine(ys, x, mod, layer, 5, wts, pos0, pos1)


def kernel(x_prompt, x_sample, cache_k, cache_v, c, c_ctx, w_ada, b_ada, norm1_g, norm2_g, w_in, q_norm_g, k_norm_g, sgu_norm_g, w_spatial, b_spatial, out_norm_g, w_out, ffn_w_gate, ffn_w_up, ffn_w_down, w_router, b_router, moe_w_gate, moe_w_up, moe_w_down):
    assert DEPTH == 2
    x = (x_prompt.reshape(MP, D), x_sample.reshape(MS, D))
    cond = jnp.concatenate([c_ctx[None, :], c, jnp.zeros((N_COND - 1 - DEC_BATCH, D), F32)], axis=0)
    mod = _modulation(cond, w_ada, b_ada).reshape(DEPTH, N_COND, 1, N_MOD * D)
    cos, sin = _rope_tables()
    w_in_bf = _cast_in_weights(w_in)

    new_k, new_v = [], []
    for i in range(DEPTH):
        q, kf, kb, vf, vb, u, gh = _in_projections(x, w_in_bf, mod, norm1_g, q_norm_g, k_norm_g,
                                                   sgu_norm_g, cos, sin, i)
        attn_ctx = _attention(q, kb, vb, None, None, i, batch=BATCH, seq=SEQ, row0=0)
        attn_lat = _attention(q, kb, vb, cache_k, cache_v, i, batch=DEC_BATCH, seq=DEC_SEQ, row0=MP)
        o = _sgu_merge(u, gh, attn_ctx, attn_lat, w_spatial, b_spatial, out_norm_g, i)
        j = i // 2
        if i % 2 == 0:
            x, h2 = _out_proj(o, w_out, x, mod, norm2_g, i)
            act = _ffn_gate_up(h2, ffn_w_gate, ffn_w_up, j)
            x = _mm_resid(act, ffn_w_down, x, mod, i, j, 5, 512, 512)
        else:
            x, h2, idx, wts = _out_proj(o, w_out, x, mod, norm2_g, i, router=(w_router[j], b_router[j]))
            x = _moe(x, h2, idx, wts, mod, i, moe_w_gate[j], moe_w_up[j], moe_w_down[j])
        new_k.append(kf[:MP].reshape(BATCH, SEQ, N_KV_HEADS, HEAD_DIM))
        new_v.append(vf[:MP].reshape(BATCH, SEQ, N_KV_HEADS, HEAD_DIM))

    y_prompt = x[0].reshape(BATCH, SEQ, D)
    y_sample = x[1].reshape(DEC_BATCH, DEC_SEQ, D)
    return (y_prompt, y_sample, jnp.stack(new_k, axis=1), jnp.stack(new_v, axis=1))
```

```python
import functools

import jax
import jax.numpy as jnp
from jax import lax
from jax.experimental import pallas as pl
from jax.experimental.pallas import tpu as pltpu

F32 = jnp.float32
BF16 = jnp.bfloat16
I32 = jnp.int32

D = 2048
BATCH, SEQ = 16, 256
DEC_BATCH, DEC_SEQ = 4, 2048
PAST_LEN = 256
DEPTH = 2
GRID_W = 64
CHUNK = 128
HEAD_DIM = 128
N_Q_HEADS, N_KV_HEADS = 8, 2
Q_PER_KV = N_Q_HEADS // N_KV_HEADS
ATTN_WIDTH = N_Q_HEADS * HEAD_DIM
KV_WIDTH = N_KV_HEADS * HEAD_DIM
N_SGU_HEADS = 8
SGU_WIDTH = N_SGU_HEADS * HEAD_DIM
IN_WIDTH = ATTN_WIDTH + 2 * KV_WIDTH + 2 * SGU_WIDTH
ROPE_THETA = 10000.0
ROPE_AXIS_DIM = HEAD_DIM // 2
D_FF = 5632
N_EXPERTS = 8
TOP_K = 2
D_FF_EXPERT = 2816
N_MOD = 6
EPS = 1e-6
ATTN_SCALE = HEAD_DIM ** -0.5
LOG2_E = 1.4426950408889634

MP = BATCH * SEQ
MS = DEC_BATCH * DEC_SEQ
M = MP + MS
N_COND = 8
LANES = 128
SUBLANES = 8
SUBLANE_BITS = 3

VMEM_LIMIT = 56 * 1024 * 1024

TM = 1024
TN_IN = 2 * KV_WIDTH
T_NORM = 512
T_Q = 512
ATTN_ROWS = 1024
T_MOE = 512
P_MOE = M * TOP_K + N_EXPERTS * T_MOE
NT_MOE = P_MOE // T_MOE
TN_GU = D_FF_EXPERT // 2
TN_DN = D
T_DISPATCH = 1024
T_COMBINE = 512
T_SGU = 1024
ZERO_ROWS = T_MOE // 2
PAD_BITS = ZERO_ROWS.bit_length()


def _params(n_axes):
    return pltpu.CompilerParams(dimension_semantics=("arbitrary",) * n_axes,
                                vmem_limit_bytes=VMEM_LIMIT)


def _cond_row(i, t):
    return jnp.where(i < MP // t, 0, 1 + (i - MP // t) // (DEC_SEQ // t))


def _is_ctx_tile(i, t):
    return i < MP // t


def _stream_in(x, t, width, row_of, col_of):
    n_ctx = MP // t
    pair = isinstance(x, tuple)
    base = 0 if pair else n_ctx
    ctx = pl.BlockSpec((t, width), lambda *g: (jnp.minimum(row_of(*g), n_ctx - 1), col_of(*g)))
    lat = pl.BlockSpec((t, width), lambda *g: (base + jnp.maximum(row_of(*g) - n_ctx, 0), col_of(*g)))
    return (list(x) if pair else [x, x]), [ctx, lat]


def _mod_spec(layer, chunk, t, row_of, col_of=None, tn=D):
    per = D // tn

    def index_map(*g):
        col = chunk * per + (col_of(*g) if col_of is not None else 0)
        return (layer, _cond_row(row_of(*g), t), 0, col)

    return pl.BlockSpec((None, None, 1, tn), index_map)


def _ada_kernel(c_ref, w_ref, b_ref, o_ref):
    c = c_ref[...]
    s = (c * jax.nn.sigmoid(c)).astype(BF16)
    o_ref[...] = jnp.dot(s, w_ref[...].astype(BF16), preferred_element_type=F32) + b_ref[...]


def _modulation(cond, w_ada, b_ada):
    tn = 1024
    width = N_MOD * D
    return pl.pallas_call(
        _ada_kernel,
        grid=(DEPTH, width // tn),
        in_specs=[pl.BlockSpec((N_COND, D), lambda l, n: (0, 0)),
                  pl.BlockSpec((None, D, tn), lambda l, n: (l, 0, n)),
                  pl.BlockSpec((None, 1, tn), lambda l, n: (l, 0, n))],
        out_specs=pl.BlockSpec((None, N_COND, tn), lambda l, n: (l, 0, n)),
        out_shape=jax.ShapeDtypeStruct((DEPTH, N_COND, width), F32),
        compiler_params=_params(2),
        name="modulation",
    )(cond, w_ada, b_ada.reshape(DEPTH, 1, width))


def _modulated_norm(x, g, sc, sh):
    y = x * lax.rsqrt(jnp.mean(x * x, axis=-1, keepdims=True) + EPS)
    return y * (g * (1.0 + sc)) + sh


def _pick(xa_ref, xb_ref, axis):
    t = xa_ref.shape[0]
    return jnp.where(_is_ctx_tile(pl.program_id(axis), t), xa_ref[...], xb_ref[...])


def _stream_tile_copy(xa_ref, xb_ref, lat_row0, buf_ref, sem, tile, slot, start):
    t = buf_ref.shape[1]

    def copy(src_ref, row):
        return pltpu.make_async_copy(src_ref.at[pl.ds(pl.multiple_of(row, t), t), :],
                                     buf_ref.at[slot], sem.at[slot])

    if not start:
        copy(xa_ref, 0).wait()
        return
    is_ctx = _is_ctx_tile(tile, t)

    @pl.when(is_ctx)
    def _():
        copy(xa_ref, tile * t).start()

    @pl.when(jnp.logical_not(is_ctx))
    def _():
        copy(xb_ref, lat_row0 + (tile - MP // t) * t).start()


def _next_stream_tile(xa_ref, xb_ref, lat_row0, buf_ref, sem, tile, n_tiles):
    @pl.when(tile == 0)
    def _():
        _stream_tile_copy(xa_ref, xb_ref, lat_row0, buf_ref, sem, 0, 0, True)

    @pl.when(tile + 1 < n_tiles)
    def _():
        _stream_tile_copy(xa_ref, xb_ref, lat_row0, buf_ref, sem, tile + 1, (tile + 1) % 2, True)

    slot = tile % 2
    _stream_tile_copy(xa_ref, xb_ref, lat_row0, buf_ref, sem, tile, slot, False)
    return slot


def _stream_hbm(x):
    return (x[0], x[1], 0) if isinstance(x, tuple) else (x, x, MP)


def _split_bf16(a):
    hi = a.astype(BF16)
    return hi, (a - hi.astype(F32)).astype(BF16)


def _route_top2(h, wr, br, idx_ref, wt_ref):
    h_hi, h_lo = _split_bf16(h)
    w_hi, w_lo = _split_bf16(wr)
    logits = (jnp.dot(h_hi, w_hi, preferred_element_type=F32)
              + jnp.dot(h_lo, w_hi, preferred_element_type=F32)
              + jnp.dot(h_hi, w_lo, preferred_element_type=F32)) + br
    lane = lax.broadcasted_iota(I32, logits.shape, 1)
    neg = jnp.float32(-jnp.inf)
    lg = jnp.where(lane < N_EXPERTS, logits, neg)
    m1 = jnp.max(lg, axis=-1, keepdims=True)
    i1 = jnp.min(jnp.where(lg == m1, lane, LANES), axis=-1, keepdims=True)
    lg2 = jnp.where(lane == i1, neg, lg)
    m2 = jnp.max(lg2, axis=-1, keepdims=True)
    i2 = jnp.min(jnp.where(lg2 == m2, lane, LANES), axis=-1, keepdims=True)
    e = jnp.exp(m2 - m1)
    w1 = 1.0 / (1.0 + e)
    w2 = e / (1.0 + e)
    idx_ref[...] = jnp.where(lane == 0, i1, jnp.where(lane == 1, i2, 0))
    wt_ref[...] = jnp.where(lane == 0, w1, jnp.where(lane == 1, w2, 0.0))


W_PIECE = 512


def _out_proj_kernel(*refs, layer, lat_row0, route):
    if route:
        (o_ref, w_ref, xa_ref, xb_ref, gate_ref, g_ref, sc_ref, sh_ref, wr_ref, br_ref,
         xn_ref, h_ref, idx_ref, wt_ref, stage_ref, wbf_ref, xbuf_ref, wsem, xsem) = refs
    else:
        (o_ref, w_ref, xa_ref, xb_ref, gate_ref, g_ref, sc_ref, sh_ref,
         xn_ref, h_ref, stage_ref, wbf_ref, xbuf_ref, wsem, xsem) = refs
    i = pl.program_id(0)

    @pl.when(i == 0)
    def _():
        for p in range(D // W_PIECE):
            cols = pl.ds(p * W_PIECE, W_PIECE)
            cp = pltpu.make_async_copy(w_ref.at[layer, :, cols], stage_ref, wsem.at[0])
            cp.start()
            cp.wait()
            _cast_rows(stage_ref, wbf_ref.at[:, cols])

    slot = _next_stream_tile(xa_ref, xb_ref, lat_row0, xbuf_ref, xsem, i, pl.num_programs(0))
    acc = jnp.dot(o_ref[...], wbf_ref[...], preferred_element_type=F32)
    x_new = xbuf_ref[slot] + gate_ref[...] * acc
    xn_ref[...] = x_new
    h = _modulated_norm(x_new, g_ref[...], sc_ref[...], sh_ref[...])
    if route:
        h_ref[...] = h
        _route_top2(h, wr_ref[...], br_ref[...], idx_ref, wt_ref)
    else:
        h_ref[...] = h.astype(BF16)


def _out_proj(o, w_out, x, mod, norm2_g, layer, router=None):
    t = T_NORM
    row = lambda i: i
    xa, xb, lat_row0 = _stream_hbm(x)
    route = router is not None
    anyspace = pl.BlockSpec(memory_space=pl.ANY)
    rows = lambda w: pl.BlockSpec((t, w), lambda i: (i, 0))
    in_specs = [rows(D), anyspace, anyspace, anyspace,
                _mod_spec(layer, 2, t, row),
                pl.BlockSpec((None, 1, D), lambda i: (layer, 0, 0)),
                _mod_spec(layer, 4, t, row), _mod_spec(layer, 3, t, row)]
    args = [o, w_out, xa, xb, mod, norm2_g.reshape(DEPTH, 1, D), mod, mod]
    out_specs = [rows(D), rows(D)]
    out_shape = [jax.ShapeDtypeStruct((M, D), F32), jax.ShapeDtypeStruct((M, D), F32 if route else BF16)]
    if route:
        w_router, b_router = router
        args += [jnp.zeros((D, LANES), F32).at[:, :N_EXPERTS].set(w_router),
                 jnp.zeros((1, LANES), F32).at[0, :N_EXPERTS].set(b_router)]
        in_specs += [pl.BlockSpec((D, LANES), lambda i: (0, 0)), pl.BlockSpec((1, LANES), lambda i: (0, 0))]
        out_specs += [rows(LANES), rows(LANES)]
        out_shape += [jax.ShapeDtypeStruct((M, LANES), I32), jax.ShapeDtypeStruct((M, LANES), F32)]
    return pl.pallas_call(
        functools.partial(_out_proj_kernel, layer=layer, lat_row0=lat_row0, route=route),
        grid=(M // t,),
        in_specs=in_specs,
        out_specs=out_specs,
        out_shape=out_shape,
        scratch_shapes=[pltpu.VMEM((D, W_PIECE), F32), pltpu.VMEM((D, D), BF16),
                        pltpu.VMEM((2, t, D), F32),
                        pltpu.SemaphoreType.DMA((1,)), pltpu.SemaphoreType.DMA((2,))],
        compiler_params=_params(1),
        name="out_proj_router" if route else "out_proj",
    )(*args)


CAST_ROWS = 256


def _cast_rows(src_ref, dst_ref):
    def body(r, carry):
        rs = pl.ds(pl.multiple_of(r * CAST_ROWS, CAST_ROWS), CAST_ROWS)
        dst_ref[rs, :] = src_ref[rs, :].astype(BF16)
        return carry

    lax.fori_loop(0, src_ref.shape[0] // CAST_ROWS, body, 0)


def _cast_weight_once(w_ref, wbf_ref):
    @pl.when(pl.program_id(1) == 0)
    def _():
        _cast_rows(w_ref, wbf_ref)


def _head_rms(a):
    return lax.rsqrt(jnp.mean(a * a, axis=-1, keepdims=True) + EPS)


def _rope_partner(ag, perm):
    hi, lo = _split_bf16(ag)
    return (jnp.dot(hi, perm, preferred_element_type=F32)
            + jnp.dot(lo, perm, preferred_element_type=F32))


def _cast_kernel(w_ref, o_ref):
    o_ref[...] = w_ref[...].astype(BF16)


def _cast_in_weights(w_in):
    spec = pl.BlockSpec((None, D, TN_IN), lambda l, n: (l, 0, n))
    return pl.pallas_call(
        _cast_kernel,
        grid=(DEPTH, IN_WIDTH // TN_IN),
        in_specs=[spec],
        out_specs=spec,
        out_shape=jax.ShapeDtypeStruct(w_in.shape, BF16),
        compiler_params=_params(2),
        name="cast_w_in",
    )(w_in)


N_Q_TILES = ATTN_WIDTH // TN_IN
KV_TILE = N_Q_TILES
U_TILE0 = KV_TILE + 1
G_TILE0 = U_TILE0 + SGU_WIDTH // TN_IN
N_IN_TILES = IN_WIDTH // TN_IN


def _in_proj_kernel(xa_ref, xb_ref, n1_ref, sc0_ref, sh0_ref, sc1_ref, sh1_ref, w_ref,
                    qg_ref, kg_ref, sg_ref, pq_ref, pk_ref, cos_ref, sin_ref,
                    q_ref, kf_ref, kb_ref, vf_ref, vb_ref, u_ref, gh_ref,
                    xbuf_ref, h_ref, xsem, *, lat_row0):
    m = pl.program_id(0)
    n = pl.program_id(1)
    nm = pl.num_programs(0)
    cur = m % 2
    nxt = (m + 1) % 2
    has_next = m + 1 < nm
    fetch = functools.partial(_stream_tile_copy, xa_ref, xb_ref, lat_row0, xbuf_ref, xsem)

    def norm_rows(slot, rows, sc_ref, sh_ref):
        x = xbuf_ref[slot, rows, :]
        h_ref[slot, rows, :] = _modulated_norm(x, n1_ref[...], sc_ref[...], sh_ref[...]).astype(BF16)

    @pl.when((m == 0) & (n == 0))
    def _():
        fetch(0, 0, True)
        fetch(0, 0, False)
        norm_rows(0, slice(None), sc0_ref, sh0_ref)

    @pl.when((n == 0) & has_next)
    def _():
        fetch(m + 1, nxt, True)

    def matmul():
        return jnp.dot(h_ref[cur], w_ref[...], preferred_element_type=F32)

    for rope in (False, True):
        positioned = jnp.logical_not(_is_ctx_tile(m, TM)) if rope else _is_ctx_tile(m, TM)

        @pl.when((n < N_Q_TILES) & positioned)
        def _():
            acc = matmul()
            ag = acc * qg_ref[...]
            if rope:
                partner = _rope_partner(ag, pq_ref[...])
                cos, sin = cos_ref[...], sin_ref[...]
            for h in range(TN_IN // HEAD_DIM):
                sl = slice(h * HEAD_DIM, (h + 1) * HEAD_DIM)
                r = _head_rms(acc[:, sl]) * (ATTN_SCALE * LOG2_E)
                qh = ag[:, sl] * cos + partner[:, sl] * sin if rope else ag[:, sl]
                q_ref[:, sl] = (qh * r).astype(BF16)

        @pl.when((n == KV_TILE) & positioned)
        def _():
            acc = matmul()
            k = acc[:, :KV_WIDTH]
            ag = k * kg_ref[...]
            if rope:
                partner = _rope_partner(ag, pk_ref[...])
                cos, sin = cos_ref[...], sin_ref[...]
            for h in range(N_KV_HEADS):
                sl = slice(h * HEAD_DIM, (h + 1) * HEAD_DIM)
                r = _head_rms(k[:, sl])
                kf = ag[:, sl] * r
                kf_ref[:, sl] = kf
                kb_ref[:, sl] = ((ag[:, sl] * cos + partner[:, sl] * sin) * r if rope else kf).astype(BF16)
            v = acc[:, KV_WIDTH:]
            vf_ref[...] = v
            vb_ref[...] = v.astype(BF16)

    half = TM // (G_TILE0 - U_TILE0)
    for k in range(G_TILE0 - U_TILE0):
        for parity in range(2):
            @pl.when((n == U_TILE0 + k) & has_next & (cur == parity))
            def _():
                if k == 0:
                    fetch(m + 1, 1 - parity, False)
                u_ref[...] = jnp.dot(h_ref[parity], w_ref[...], preferred_element_type=F32).astype(BF16)
                norm_rows(1 - parity, slice(k * half, (k + 1) * half), sc1_ref, sh1_ref)

        @pl.when((n == U_TILE0 + k) & jnp.logical_not(has_next))
        def _():
            u_ref[...] = matmul().astype(BF16)

    @pl.when(n >= G_TILE0)
    def _():
        acc = matmul()
        for h in range(TN_IN // HEAD_DIM):
            sl = slice(h * HEAD_DIM, (h + 1) * HEAD_DIM)
            a = acc[:, sl]
            gh_ref[:, sl] = (a * _head_rms(a) * sg_ref[:, sl]).astype(BF16)


def _rope_tables():
    n_rows = DEC_SEQ // GRID_W
    rows = jnp.broadcast_to(jnp.arange(n_rows)[:, None], (n_rows, GRID_W)).reshape(-1)
    cols = jnp.broadcast_to(jnp.arange(GRID_W)[None, :], (n_rows, GRID_W)).reshape(-1)
    inv = ROPE_THETA ** (-jnp.arange(0, ROPE_AXIS_DIM, 2, dtype=F32) / ROPE_AXIS_DIM)
    ang_r = rows.astype(F32)[:, None] * inv
    ang_c = cols.astype(F32)[:, None] * inv
    cos = jnp.concatenate([jnp.cos(ang_r), jnp.cos(ang_r), jnp.cos(ang_c), jnp.cos(ang_c)], axis=1)
    sin = jnp.concatenate([-jnp.sin(ang_r), jnp.sin(ang_r), -jnp.sin(ang_c), jnp.sin(ang_c)], axis=1)
    return cos, sin


def _partner_matrix(n_heads):
    w = n_heads * HEAD_DIM
    quarter = ROPE_AXIS_DIM // 2
    j = jnp.arange(w)
    partner = jnp.where((j % ROPE_AXIS_DIM) < quarter, j + quarter, j - quarter)
    return (jnp.arange(w)[:, None] == partner[None, :]).astype(BF16)


def _rope_block(m):
    return jnp.maximum(m - MP // TM, 0) % (DEC_SEQ // TM)


def _in_projections(x, w_in_bf, mod, norm1_g, q_norm_g, k_norm_g, sgu_norm_g, cos, sin, layer):
    tn = TN_IN
    xa, xb, lat_row0 = _stream_hbm(x)
    row = lambda m, n: m
    next_row = lambda m, n: jnp.minimum(m + 1, M // TM - 1)
    anyspace = pl.BlockSpec(memory_space=pl.ANY)
    const = lambda shape: pl.BlockSpec(shape, lambda m, n: (0,) * len(shape))
    rope_spec = pl.BlockSpec((TM, HEAD_DIM), lambda m, n: (_rope_block(m), 0))
    q_heads = tn // HEAD_DIM
    q_gain = jnp.tile(q_norm_g[layer], q_heads)[None, :]
    k_gain = jnp.tile(k_norm_g[layer], N_KV_HEADS)[None, :]
    g_tile = lambda n: jnp.clip(n - G_TILE0, 0, SGU_WIDTH // tn - 1)
    kv_out = pl.BlockSpec((TM, KV_WIDTH), lambda m, n: (m, 0))
    kv_shape = lambda dt: jax.ShapeDtypeStruct((M, KV_WIDTH), dt)
    return pl.pallas_call(
        functools.partial(_in_proj_kernel, lat_row0=lat_row0),
        grid=(M // TM, N_IN_TILES),
        in_specs=[anyspace, anyspace,
                  pl.BlockSpec((None, 1, D), lambda m, n: (layer, 0, 0)),
                  _mod_spec(layer, 1, TM, row), _mod_spec(layer, 0, TM, row),
                  _mod_spec(layer, 1, TM, next_row), _mod_spec(layer, 0, TM, next_row),
                  pl.BlockSpec((None, D, tn), lambda m, n: (layer, 0, n)),
                  const((1, tn)), const((1, KV_WIDTH)),
                  pl.BlockSpec((None, 1, tn), lambda m, n: (layer, 0, g_tile(n))),
                  const((tn, tn)), const((KV_WIDTH, KV_WIDTH)), rope_spec, rope_spec],
        out_specs=[pl.BlockSpec((TM, tn), lambda m, n: (m, jnp.minimum(n, N_Q_TILES - 1))),
                   kv_out, kv_out, kv_out, kv_out,
                   pl.BlockSpec((TM, tn), lambda m, n: (m, jnp.clip(n - U_TILE0, 0, SGU_WIDTH // tn - 1))),
                   pl.BlockSpec((TM, tn), lambda m, n: (m, g_tile(n)))],
        out_shape=[jax.ShapeDtypeStruct((M, ATTN_WIDTH), BF16),
                   kv_shape(F32), kv_shape(BF16), kv_shape(F32), kv_shape(BF16),
                   jax.ShapeDtypeStruct((M, SGU_WIDTH), BF16),
                   jax.ShapeDtypeStruct((M, SGU_WIDTH), BF16)],
        scratch_shapes=[pltpu.VMEM((2, TM, D), F32), pltpu.VMEM((2, TM, D), BF16),
                        pltpu.SemaphoreType.DMA((2,))],
        compiler_params=_params(2),
        name="in_proj",
    )(xa, xb, norm1_g.reshape(DEPTH, 1, D), mod, mod, mod, mod, w_in_bf, q_gain, k_gain,
      sgu_norm_g.reshape(DEPTH, 1, SGU_WIDTH), _partner_matrix(q_heads), _partner_matrix(N_KV_HEADS),
      cos, sin)


def _qk(q, k):
    return lax.dot_general(q, k, (((1,), (1,)), ((), ())), preferred_element_type=F32)


def _attn_kernel(*refs, has_cache, n_batch, seq, tq):
    def with_ones(v):
        return jnp.concatenate([v, jnp.ones_like(v)], axis=1)

    if has_cache:
        q_ref, k_ref, v_ref, kc_ref, vc_ref, o_ref = refs
    else:
        q_ref, k_ref, v_ref, o_ref = refs
    for b in range(n_batch):
        rows_q = slice(b * tq, (b + 1) * tq)
        rows_k = slice(b * seq, (b + 1) * seq)
        for kv in range(N_KV_HEADS):
            kv_cols = slice(kv * HEAD_DIM, (kv + 1) * HEAD_DIM)
            k = k_ref[rows_k, kv_cols]
            v = with_ones(v_ref[rows_k, kv_cols])
            if has_cache:
                kc = kc_ref[:, kv_cols].astype(BF16)
                vc = with_ones(vc_ref[:, kv_cols].astype(BF16))
            for g in range(Q_PER_KV):
                head = kv * Q_PER_KV + g
                sl = slice(head * HEAD_DIM, (head + 1) * HEAD_DIM)
                q = q_ref[rows_q, sl]
                s = _qk(q, k)
                m = jnp.max(s, axis=-1, keepdims=True)
                if has_cache:
                    sc = _qk(q, kc)
                    m = jnp.maximum(m, jnp.max(sc, axis=-1, keepdims=True))
                o = jnp.dot(jnp.exp2(s - m).astype(BF16), v, preferred_element_type=F32)
                if has_cache:
                    o = o + jnp.dot(jnp.exp2(sc - m).astype(BF16), vc, preferred_element_type=F32)
                o_ref[rows_q, sl] = (o[:, :HEAD_DIM] / o[:, HEAD_DIM:]).astype(BF16)


def _attention(q, kb, vb, cache_k, cache_v, layer, *, batch, seq, row0):
    has_cache = cache_k is not None
    tq = min(T_Q, seq)
    nq = seq // tq
    n_batch = max(1, ATTN_ROWS // seq) if nq == 1 else 1
    q_spec = pl.BlockSpec((n_batch * tq, ATTN_WIDTH), lambda b, i: (row0 // (n_batch * tq) + b * nq + i, 0))
    kv_spec = pl.BlockSpec((n_batch * seq, KV_WIDTH), lambda b, i: (row0 // (n_batch * seq) + b, 0))
    in_specs = [q_spec, kv_spec, kv_spec]
    args = [q, kb, vb]
    if has_cache:
        c_spec = pl.BlockSpec((None, None, PAST_LEN, KV_WIDTH), lambda b, i: (b, layer, 0, 0))
        in_specs += [c_spec, c_spec]
        args += [cache_k.reshape(DEC_BATCH, DEPTH, PAST_LEN, KV_WIDTH),
                 cache_v.reshape(DEC_BATCH, DEPTH, PAST_LEN, KV_WIDTH)]
    return pl.pallas_call(
        functools.partial(_attn_kernel, has_cache=has_cache, n_batch=n_batch, seq=seq, tq=tq),
        grid=(batch // n_batch, nq),
        in_specs=in_specs,
        out_specs=pl.BlockSpec((n_batch * tq, ATTN_WIDTH), lambda b, i: (b * nq + i, 0)),
        out_shape=jax.ShapeDtypeStruct((batch * seq, ATTN_WIDTH), BF16),
        compiler_params=_params(2),
        name="attention_cached" if has_cache else "attention",
    )(*args)


def _sgu_merge_kernel(u_ref, gh_ref, ap_ref, as_ref, ws_ref, bs_ref, gn_ref, o_ref, sgu_ref):
    t = u_ref.shape[0]
    a = _pick(ap_ref, as_ref, 0).astype(F32)
    a = a * lax.rsqrt(jnp.mean(a * a, axis=-1, keepdims=True) + EPS) * gn_ref[:, :ATTN_WIDTH]
    o_ref[:, :ATTN_WIDTH] = a.astype(BF16)
    for h in range(N_SGU_HEADS):
        cs = slice(h * HEAD_DIM, (h + 1) * HEAD_DIM)
        w = ws_ref[h].astype(BF16)
        b = bs_ref[h]
        for c in range(t // CHUNK):
            rs = slice(c * CHUNK, (c + 1) * CHUNK)
            mixed = jnp.dot(w, gh_ref[rs, cs], preferred_element_type=F32) + b
            sgu_ref[rs, cs] = u_ref[rs, cs].astype(F32) * mixed
    s = sgu_ref[...]
    s = s * lax.rsqrt(jnp.mean(s * s, axis=-1, keepdims=True) + EPS) * gn_ref[:, ATTN_WIDTH:]
    o_ref[:, ATTN_WIDTH:] = s.astype(BF16)


def _sgu_merge(u, gh, attn_ctx, attn_lat, w_spatial, b_spatial, out_norm_g, layer):
    t = T_SGU
    bias = jnp.broadcast_to(b_spatial[:, :, :, None], (DEPTH, N_SGU_HEADS, CHUNK, HEAD_DIM))
    row = lambda w: pl.BlockSpec((t, w), lambda i: (i, 0))
    a_args, a_specs = _stream_in((attn_ctx, attn_lat), t, ATTN_WIDTH, lambda i: i, lambda i: 0)
    return pl.pallas_call(
        _sgu_merge_kernel,
        grid=(M // t,),
        in_specs=[row(SGU_WIDTH), row(SGU_WIDTH)] + a_specs + [
            pl.BlockSpec((None, N_SGU_HEADS, CHUNK, CHUNK), lambda i: (layer, 0, 0, 0)),
            pl.BlockSpec((None, N_SGU_HEADS, CHUNK, HEAD_DIM), lambda i: (layer, 0, 0, 0)),
            pl.BlockSpec((None, 1, D), lambda i: (layer, 0, 0))],
        out_specs=row(D),
        out_shape=jax.ShapeDtypeStruct((M, D), BF16),
        scratch_shapes=[pltpu.VMEM((t, SGU_WIDTH), F32)],
        compiler_params=_params(1),
        name="sgu_merge",
    )(u, gh, *a_args, w_spatial, bias, out_norm_g.reshape(DEPTH, 1, D))


def _mm_resid_kernel(a_ref, w_ref, xa_ref, xb_ref, g_ref, o_ref, wbf_ref):
    _cast_weight_once(w_ref, wbf_ref)
    acc = jnp.dot(a_ref[...], wbf_ref[...], preferred_element_type=F32)
    o_ref[...] = _pick(xa_ref, xb_ref, 1) + g_ref[...] * acc


def _mm_resid(a, w, x, mod, layer, w_index, gate_chunk, tm, tn):
    k = a.shape[1]
    row = lambda n, m: m
    col = lambda n, m: n
    x_args, x_specs = _stream_in(x, tm, tn, row, col)
    return pl.pallas_call(
        _mm_resid_kernel,
        grid=(D // tn, M // tm),
        in_specs=[pl.BlockSpec((tm, k), lambda n, m: (m, 0)),
                  pl.BlockSpec((None, k, tn), lambda n, m: (w_index, 0, n))] + x_specs + [
                  _mod_spec(layer, gate_chunk, tm, row, col, tn=tn)],
        out_specs=pl.BlockSpec((tm, tn), lambda n, m: (m, n)),
        out_shape=jax.ShapeDtypeStruct((M, D), F32),
        scratch_shapes=[pltpu.VMEM((k, tn), BF16)],
        compiler_params=_params(2),
        name="mm_resid",
    )(a, w, *x_args, mod)


def _swiglu(a, b):
    return a * jax.nn.sigmoid(a) * b


def _ffn_gu_kernel(x_ref, wg_ref, wu_ref, o_ref, wgb_ref, wub_ref):
    @pl.when(pl.program_id(1) == 0)
    def _():
        _cast_rows(wg_ref, wgb_ref)
        _cast_rows(wu_ref, wub_ref)
    x = x_ref[...]
    a = jnp.dot(x, wgb_ref[...], preferred_element_type=F32)
    b = jnp.dot(x, wub_ref[...], preferred_element_type=F32)
    o_ref[...] = _swiglu(a, b).astype(BF16)


def _ffn_gate_up(h, w_gate, w_up, j):
    tm, tn = TM, 512
    w_spec = pl.BlockSpec((None, D, tn), lambda n, m: (j, 0, n))
    return pl.pallas_call(
        _ffn_gu_kernel,
        grid=(D_FF // tn, M // tm),
        in_specs=[pl.BlockSpec((tm, D), lambda n, m: (m, 0)), w_spec, w_spec],
        out_specs=pl.BlockSpec((tm, tn), lambda n, m: (m, n)),
        out_shape=jax.ShapeDtypeStruct((M, D_FF), BF16),
        scratch_shapes=[pltpu.VMEM((D, tn), BF16), pltpu.VMEM((D, tn), BF16)],
        compiler_params=_params(2),
        name="ffn_gate_up",
    )(h, w_gate, w_up)


def _route_meta(idx):
    t = T_MOE
    experts = jnp.arange(N_EXPERTS, dtype=I32)
    onehot = (idx[:, :, None] == experts[None, None, :]).astype(I32).sum(axis=1)
    csum = jnp.cumsum(onehot, axis=0)
    rank = csum - onehot
    count = csum[-1]
    ntile = (count + t - 1) // t
    tile_end = jnp.cumsum(ntile)
    tile_start = tile_end - ntile
    nused = tile_end[-1]
    pos = (tile_start * t)[idx] + jnp.take_along_axis(rank, idx, axis=1)
    j = jnp.arange(NT_MOE, dtype=I32)
    te_raw = jnp.minimum(jnp.sum(j[:, None] >= tile_end[None, :], axis=1), N_EXPERTS - 1).astype(I32)
    te = jnp.where(j < nused, te_raw, te_raw[nused - 1])
    first = ((j == tile_start[te]) & (j < nused)).astype(I32)
    later = (ntile[None, :] > 0) & (experts[None, :] > te[:, None])
    nxt = jnp.min(jnp.where(later, experts[None, :], N_EXPERTS), axis=1)
    nxt = jnp.where(nxt == N_EXPERTS, -1, nxt).astype(I32)
    pad_start = tile_start * t + count
    pad_len = ntile * t - count
    tail = jnp.stack([nused * t, (NT_MOE - nused) * (t // ZERO_ROWS)])
    zinfo = jnp.concatenate([pad_start, pad_len, tail]).astype(I32)
    return pos.astype(I32), zinfo, (te, first, nxt, nused.reshape(1).astype(I32))


def _dispatch_kernel(p0_ref, p1_ref, z_ref, h_ref, xs_ref, zero_ref, sem, zsem):
    t = h_ref.shape[0]
    i = pl.program_id(0)
    base = i * t

    def clear_padding(start):
        def go(n, off):
            cp = pltpu.make_async_copy(zero_ref.at[pl.ds(0, n), :], xs_ref.at[pl.ds(off, n), :], zsem.at[0])
            cp.start() if start else cp.wait()

        for e in range(N_EXPERTS):
            off, ln = z_ref[e], z_ref[N_EXPERTS + e]
            end = off + ln
            for b in range(SUBLANE_BITS, PAD_BITS):
                @pl.when(((ln >> b) & 1) == 1)
                def _():
                    go(1 << b, pl.multiple_of(end - ((ln >> b) << b), SUBLANES))
            for k in range(SUBLANES - 1):
                @pl.when(k < (ln & (SUBLANES - 1)))
                def _():
                    go(1, off + k)
        tail0, n_tail = z_ref[2 * N_EXPERTS], z_ref[2 * N_EXPERTS + 1]

        def tail_body(k, carry):
            go(ZERO_ROWS, pl.multiple_of(tail0 + k * ZERO_ROWS, SUBLANES))
            return carry

        lax.fori_loop(0, n_tail, tail_body, 0)

    @pl.when(i == 0)
    def _():
        zero_ref[...] = jnp.zeros(zero_ref.shape, zero_ref.dtype)
        clear_padding(True)

    def issue(r, carry):
        src = h_ref.at[pl.ds(r, 1), :]
        pltpu.make_async_copy(src, xs_ref.at[pl.ds(p0_ref[base + r], 1), :], sem.at[0]).start()
        pltpu.make_async_copy(src, xs_ref.at[pl.ds(p1_ref[base + r], 1), :], sem.at[1]).start()
        return carry

    lax.fori_loop(0, t, issue, 0, unroll=8)
    pltpu.make_async_copy(h_ref, xs_ref.at[pl.ds(0, t), :], sem.at[0]).wait()
    pltpu.make_async_copy(h_ref, xs_ref.at[pl.ds(0, t), :], sem.at[1]).wait()

    @pl.when(i == 0)
    def _():
        clear_padding(False)


def _dispatch(h, pos0, pos1, zinfo):
    t = T_DISPATCH
    return pl.pallas_call(
        _dispatch_kernel,
        grid_spec=pltpu.PrefetchScalarGridSpec(
            num_scalar_prefetch=3,
            grid=(M // t,),
            in_specs=[pl.BlockSpec((t, D), lambda i, p0, p1, z: (i, 0))],
            out_specs=pl.BlockSpec(memory_space=pl.ANY),
            scratch_shapes=[pltpu.VMEM((ZERO_ROWS, D), F32),
                            pltpu.SemaphoreType.DMA((2,)), pltpu.SemaphoreType.DMA((1,))]),
        out_shape=jax.ShapeDtypeStruct((P_MOE, D), F32),
        compiler_params=_params(1),
        name="moe_dispatch",
    )(pos0, pos1, zinfo, h)


def _expert_weight_stream(w_refs, stage_refs, bf_refs, sem, te_ref, first_ref, nxt_ref, tn):
    c = pl.program_id(0)
    j = pl.program_id(1)
    nc = pl.num_programs(0)

    def copies(e, cc):
        col = pl.multiple_of(cc * tn, LANES)
        return [pltpu.make_async_copy(w.at[e, :, pl.ds(col, tn)], st, sem.at[k])
                for k, (w, st) in enumerate(zip(w_refs, stage_refs))]

    def start(e, cc):
        for cp in copies(e, cc):
            cp.start()

    @pl.when((c == 0) & (j == 0))
    def _():
        start(te_ref[0], 0)

    @pl.when(first_ref[j] == 1)
    def _():
        for cp in copies(0, 0):
            cp.wait()
        for st, bf in zip(stage_refs, bf_refs):
            _cast_rows(st, bf)
        ne = nxt_ref[j]

        @pl.when(ne >= 0)
        def _():
            start(ne, c)

        @pl.when((ne < 0) & (c + 1 < nc))
        def _():
            start(te_ref[0], c + 1)


def _gmm_gate_up_kernel(te_ref, first_ref, nxt_ref, nused_ref, xs_ref, wg_ref, wu_ref, o_ref,
                        sg_ref, su_ref, wgb_ref, wub_ref, sem):
    _expert_weight_stream((wg_ref, wu_ref), (sg_ref, su_ref), (wgb_ref, wub_ref), sem,
                          te_ref, first_ref, nxt_ref, TN_GU)

    @pl.when(pl.program_id(1) < nused_ref[0])
    def _():
        half = T_MOE // 2
        for r in range(2):
            rs = slice(r * half, (r + 1) * half)
            x = xs_ref[rs, :].astype(BF16)
            a = jnp.dot(x, wgb_ref[...], preferred_element_type=F32)
            b = jnp.dot(x, wub_ref[...], preferred_element_type=F32)
            o_ref[rs, :] = _swiglu(a, b).astype(BF16)

    @pl.when(pl.program_id(1) >= nused_ref[0])
    def _():
        o_ref[...] = jnp.zeros(o_ref.shape, o_ref.dtype)


def _gmm_down_kernel(te_ref, first_ref, nxt_ref, nused_ref, a_ref, wd_ref, o_ref,
                     sd_ref, wdb_ref, sem):
    _expert_weight_stream((wd_ref,), (sd_ref,), (wdb_ref,), sem, te_ref, first_ref, nxt_ref, TN_DN)

    @pl.when(pl.program_id(1) < nused_ref[0])
    def _():
        o_ref[...] = jnp.dot(a_ref[...], wdb_ref[...], preferred_element_type=F32)

    @pl.when(pl.program_id(1) >= nused_ref[0])
    def _():
        o_ref[...] = jnp.zeros(o_ref.shape, o_ref.dtype)


def _used_tile(j, nused):
    return jnp.minimum(j, nused[0] - 1)


def _gmm_gate_up(xs, w_gate, w_up, meta):
    te, first, nxt, nused = meta
    tn = TN_GU
    return pl.pallas_call(
        _gmm_gate_up_kernel,
        grid_spec=pltpu.PrefetchScalarGridSpec(
            num_scalar_prefetch=4,
            grid=(D_FF_EXPERT // tn, NT_MOE),
            in_specs=[pl.BlockSpec((T_MOE, D), lambda c, j, te, fi, nx, nu: (_used_tile(j, nu), 0)),
                      pl.BlockSpec(memory_space=pl.ANY),
                      pl.BlockSpec(memory_space=pl.ANY)],
            out_specs=pl.BlockSpec((T_MOE, tn), lambda c, j, te, fi, nx, nu: (j, c)),
            scratch_shapes=[pltpu.VMEM((D, tn), F32), pltpu.VMEM((D, tn), F32),
                            pltpu.VMEM((D, tn), BF16), pltpu.VMEM((D, tn), BF16),
                            pltpu.SemaphoreType.DMA((2,))]),
        out_shape=jax.ShapeDtypeStruct((P_MOE, D_FF_EXPERT), BF16),
        compiler_params=_params(2),
        name="moe_gate_up",
    )(te, first, nxt, nused, xs, w_gate, w_up)


def _gmm_down(act, w_down, meta):
    te, first, nxt, nused = meta
    tn = TN_DN
    return pl.pallas_call(
        _gmm_down_kernel,
        grid_spec=pltpu.PrefetchScalarGridSpec(
            num_scalar_prefetch=4,
            grid=(D // tn, NT_MOE),
            in_specs=[pl.BlockSpec((T_MOE, D_FF_EXPERT), lambda c, j, te, fi, nx, nu: (_used_tile(j, nu), 0)),
                      pl.BlockSpec(memory_space=pl.ANY)],
            out_specs=pl.BlockSpec((T_MOE, tn), lambda c, j, te, fi, nx, nu: (j, c)),
            scratch_shapes=[pltpu.VMEM((D_FF_EXPERT, tn), F32), pltpu.VMEM((D_FF_EXPERT, tn), BF16),
                            pltpu.SemaphoreType.DMA((1,))]),
        out_shape=jax.ShapeDtypeStruct((P_MOE, D), F32),
        compiler_params=_params(2),
        name="moe_down",
    )(te, first, nxt, nused, act, w_down)


def _combine_kernel(p0_ref, p1_ref, ys_ref, xa_ref, xb_ref, g_ref, w_ref, oc_ref, ol_ref,
                    a_ref, b_ref, sem):
    t = xa_ref.shape[0]
    i = pl.program_id(0)
    n = pl.num_programs(0)

    def issue(step, slot):
        base = step * t

        def body(r, carry):
            pltpu.make_async_copy(ys_ref.at[pl.ds(p0_ref[base + r], 1), :],
                                  a_ref.at[slot, pl.ds(r, 1), :], sem.at[0, slot]).start()
            pltpu.make_async_copy(ys_ref.at[pl.ds(p1_ref[base + r], 1), :],
                                  b_ref.at[slot, pl.ds(r, 1), :], sem.at[1, slot]).start()
            return carry

        lax.fori_loop(0, t, body, 0, unroll=8)

    @pl.when(i == 0)
    def _():
        issue(0, 0)

    @pl.when(i + 1 < n)
    def _():
        issue(i + 1, (i + 1) % 2)

    slot = i % 2
    pltpu.make_async_copy(ys_ref.at[pl.ds(0, t), :], a_ref.at[slot], sem.at[0, slot]).wait()
    pltpu.make_async_copy(ys_ref.at[pl.ds(0, t), :], b_ref.at[slot], sem.at[1, slot]).wait()
    w = w_ref[...]
    moe = w[:, 0:1] * a_ref[slot] + w[:, 1:2] * b_ref[slot]
    y = _pick(xa_ref, xb_ref, 0) + g_ref[...] * moe
    is_ctx = _is_ctx_tile(i, t)

    @pl.when(is_ctx)
    def _():
        oc_ref[...] = y

    @pl.when(jnp.logical_not(is_ctx))
    def _():
        ol_ref[...] = y


def _combine(ys, x, mod, layer, gate_chunk, wts, pos0, pos1):
    t = T_COMBINE
    n_ctx = MP // t
    row = lambda i, *_: i
    x_args, x_specs = _stream_in(x, t, D, row, lambda i, *_: 0)
    return pl.pallas_call(
        _combine_kernel,
        grid_spec=pltpu.PrefetchScalarGridSpec(
            num_scalar_prefetch=2,
            grid=(M // t,),
            in_specs=[pl.BlockSpec(memory_space=pl.ANY)] + x_specs + [
                _mod_spec(layer, gate_chunk, t, row),
                pl.BlockSpec((t, LANES), lambda i, p0, p1: (i, 0))],
            out_specs=[pl.BlockSpec((t, D), lambda i, p0, p1: (jnp.minimum(i, n_ctx - 1), 0)),
                       pl.BlockSpec((t, D), lambda i, p0, p1: (jnp.maximum(i - n_ctx, 0), 0))],
            scratch_shapes=[pltpu.VMEM((2, t, D), F32), pltpu.VMEM((2, t, D), F32),
                            pltpu.SemaphoreType.DMA((2, 2))]),
        out_shape=[jax.ShapeDtypeStruct((MP, D), F32), jax.ShapeDtypeStruct((MS, D), F32)],
        compiler_params=_params(1),
        name="moe_combine",
    )(pos0, pos1, ys, *x_args, mod, wts)


def _moe(x, h, idx, wts, mod, layer, w_gate, w_up, w_down):
    pos, zinfo, meta = _route_meta(idx[:, :TOP_K])
    pos0, pos1 = pos[:, 0], pos[:, 1]
    xs = _dispatch(h, pos0, pos1, zinfo)
    act = _gmm_gate_up(xs, w_gate, w_up, meta)
    ys = _gmm_down(act, w_down, meta)
    return _combine(ys, x, mod, layer, 5, wts, pos0, pos1)


def kernel(x_prompt, x_sample, cache_k, cache_v, c, c_ctx, w_ada, b_ada, norm1_g, norm2_g, w_in, q_norm_g, k_norm_g, sgu_norm_g, w_spatial, b_spatial, out_norm_g, w_out, ffn_w_gate, ffn_w_up, ffn_w_down, w_router, b_router, moe_w_gate, moe_w_up, moe_w_down):
    assert DEPTH == 2
    x = (x_prompt.reshape(MP, D), x_sample.reshape(MS, D))
    cond = jnp.concatenate([c_ctx[None, :], c, jnp.zeros((N_COND - 1 - DEC_BATCH, D), F32)], axis=0)
    mod = _modulation(cond, w_ada, b_ada).reshape(DEPTH, N_COND, 1, N_MOD * D)
    cos, sin = _rope_tables()
    w_in_bf = _cast_in_weights(w_in)

    new_k, new_v = [], []
    for i in range(DEPTH):
        q, kf, kb, vf, vb, u, gh = _in_projections(x, w_in_bf, mod, norm1_g, q_norm_g, k_norm_g,
                                                   sgu_norm_g, cos, sin, i)
        attn_ctx = _attention(q, kb, vb, None, None, i, batch=BATCH, seq=SEQ, row0=0)
        attn_lat = _attention(q, kb, vb, cache_k, cache_v, i, batch=DEC_BATCH, seq=DEC_SEQ, row0=MP)
        o = _sgu_merge(u, gh, attn_ctx, attn_lat, w_spatial, b_spatial, out_norm_g, i)
        j = i // 2
        if i % 2 == 0:
            x, h2 = _out_proj(o, w_out, x, mod, norm2_g, i)
            act = _ffn_gate_up(h2, ffn_w_gate, ffn_w_up, j)
            x = _mm_resid(act, ffn_w_down, x, mod, i, j, 5, 512, 512)
        else:
            x, h2, idx, wts = _out_proj(o, w_out, x, mod, norm2_g, i, router=(w_router[j], b_router[j]))
            x = _moe(x, h2, idx, wts, mod, i, moe_w_gate[j], moe_w_up[j], moe_w_down[j])
        new_k.append(kf[:MP].reshape(BATCH, SEQ, N_KV_HEADS, HEAD_DIM))
        new_v.append(vf[:MP].reshape(BATCH, SEQ, N_KV_HEADS, HEAD_DIM))

    y_prompt = x[0].reshape(BATCH, SEQ, D)
    y_sample = x[1].reshape(DEC_BATCH, DEC_SEQ, D)
    return (y_prompt, y_sample, jnp.stack(new_k, axis=1), jnp.stack(new_v, axis=1))
```

```python
import functools

import jax
import jax.numpy as jnp
from jax import lax
from jax.experimental import pallas as pl
from jax.experimental.pallas import tpu as pltpu

F32 = jnp.float32
BF16 = jnp.bfloat16
I32 = jnp.int32

D = 2048
BATCH, SEQ = 16, 256
DEC_BATCH, DEC_SEQ = 4, 2048
PAST_LEN = 256
DEPTH = 2
GRID_W = 64
CHUNK = 128
HEAD_DIM = 128
N_Q_HEADS, N_KV_HEADS = 8, 2
Q_PER_KV = N_Q_HEADS // N_KV_HEADS
ATTN_WIDTH = N_Q_HEADS * HEAD_DIM
KV_WIDTH = N_KV_HEADS * HEAD_DIM
N_SGU_HEADS = 8
SGU_WIDTH = N_SGU_HEADS * HEAD_DIM
IN_WIDTH = ATTN_WIDTH + 2 * KV_WIDTH + 2 * SGU_WIDTH
ROPE_THETA = 10000.0
ROPE_AXIS_DIM = HEAD_DIM // 2
D_FF = 5632
N_EXPERTS = 8
TOP_K = 2
D_FF_EXPERT = 2816
N_MOD = 6
EPS = 1e-6
ATTN_SCALE = HEAD_DIM ** -0.5
LOG2_E = 1.4426950408889634

MP = BATCH * SEQ
MS = DEC_BATCH * DEC_SEQ
M = MP + MS
N_COND = 8
LANES = 128
SUBLANES = 8
SUBLANE_BITS = 3

VMEM_LIMIT = 56 * 1024 * 1024

TM = 1024
TN_IN = 2 * KV_WIDTH
T_NORM = 512
T_Q = 512
ATTN_ROWS = 1024
T_MOE = 512
P_MOE = M * TOP_K + N_EXPERTS * T_MOE
NT_MOE = P_MOE // T_MOE
TN_GU = D_FF_EXPERT // 2
TN_DN = D
T_DISPATCH = 1024
T_COMBINE = 256
T_SGU = 1024
ZERO_ROWS = T_MOE // 2
PAD_BITS = ZERO_ROWS.bit_length()


def _params(n_axes):
    return pltpu.CompilerParams(dimension_semantics=("arbitrary",) * n_axes,
                                vmem_limit_bytes=VMEM_LIMIT)


def _cond_row(i, t):
    return jnp.where(i < MP // t, 0, 1 + (i - MP // t) // (DEC_SEQ // t))


def _is_ctx_tile(i, t):
    return i < MP // t


def _stream_in(x, t, width, row_of, col_of):
    n_ctx = MP // t
    pair = isinstance(x, tuple)
    base = 0 if pair else n_ctx
    ctx = pl.BlockSpec((t, width), lambda *g: (jnp.minimum(row_of(*g), n_ctx - 1), col_of(*g)))
    lat = pl.BlockSpec((t, width), lambda *g: (base + jnp.maximum(row_of(*g) - n_ctx, 0), col_of(*g)))
    return (list(x) if pair else [x, x]), [ctx, lat]


def _mod_spec(layer, chunk, t, row_of, col_of=None, tn=D):
    per = D // tn

    def index_map(*g):
        col = chunk * per + (col_of(*g) if col_of is not None else 0)
        return (layer, _cond_row(row_of(*g), t), 0, col)

    return pl.BlockSpec((None, None, 1, tn), index_map)


def _ada_kernel(c_ref, w_ref, b_ref, o_ref):
    c = c_ref[...]
    s = (c * jax.nn.sigmoid(c)).astype(BF16)
    o_ref[...] = jnp.dot(s, w_ref[...].astype(BF16), preferred_element_type=F32) + b_ref[...]


def _modulation(cond, w_ada, b_ada):
    tn = 1024
    width = N_MOD * D
    return pl.pallas_call(
        _ada_kernel,
        grid=(DEPTH, width // tn),
        in_specs=[pl.BlockSpec((N_COND, D), lambda l, n: (0, 0)),
                  pl.BlockSpec((None, D, tn), lambda l, n: (l, 0, n)),
                  pl.BlockSpec((None, 1, tn), lambda l, n: (l, 0, n))],
        out_specs=pl.BlockSpec((None, N_COND, tn), lambda l, n: (l, 0, n)),
        out_shape=jax.ShapeDtypeStruct((DEPTH, N_COND, width), F32),
        compiler_params=_params(2),
        name="modulation",
    )(cond, w_ada, b_ada.reshape(DEPTH, 1, width))


def _modulated_norm(x, g, sc, sh):
    y = x * lax.rsqrt(jnp.mean(x * x, axis=-1, keepdims=True) + EPS)
    return y * (g * (1.0 + sc)) + sh


def _pick(xa_ref, xb_ref, axis):
    t = xa_ref.shape[0]
    return jnp.where(_is_ctx_tile(pl.program_id(axis), t), xa_ref[...], xb_ref[...])


def _stream_tile_copy(xa_ref, xb_ref, lat_row0, buf_ref, sem, tile, slot, start):
    t = buf_ref.shape[1]

    def copy(src_ref, row):
        return pltpu.make_async_copy(src_ref.at[pl.ds(pl.multiple_of(row, t), t), :],
                                     buf_ref.at[slot], sem.at[slot])

    if not start:
        copy(xa_ref, 0).wait()
        return
    is_ctx = _is_ctx_tile(tile, t)

    @pl.when(is_ctx)
    def _():
        copy(xa_ref, tile * t).start()

    @pl.when(jnp.logical_not(is_ctx))
    def _():
        copy(xb_ref, lat_row0 + (tile - MP // t) * t).start()


def _next_stream_tile(xa_ref, xb_ref, lat_row0, buf_ref, sem, tile, n_tiles):
    @pl.when(tile == 0)
    def _():
        _stream_tile_copy(xa_ref, xb_ref, lat_row0, buf_ref, sem, 0, 0, True)

    @pl.when(tile + 1 < n_tiles)
    def _():
        _stream_tile_copy(xa_ref, xb_ref, lat_row0, buf_ref, sem, tile + 1, (tile + 1) % 2, True)

    slot = tile % 2
    _stream_tile_copy(xa_ref, xb_ref, lat_row0, buf_ref, sem, tile, slot, False)
    return slot


def _stream_hbm(x):
    return (x[0], x[1], 0) if isinstance(x, tuple) else (x, x, MP)


def _split_bf16(a):
    hi = a.astype(BF16)
    return hi, (a - hi.astype(F32)).astype(BF16)


def _route_top2(h, wr, br, idx_ref, wt_ref):
    h_hi, h_lo = _split_bf16(h)
    w_hi, w_lo = _split_bf16(wr)
    logits = (jnp.dot(h_hi, w_hi, preferred_element_type=F32)
              + jnp.dot(h_lo, w_hi, preferred_element_type=F32)
              + jnp.dot(h_hi, w_lo, preferred_element_type=F32)) + br
    lane = lax.broadcasted_iota(I32, logits.shape, 1)
    neg = jnp.float32(-jnp.inf)
    lg = jnp.where(lane < N_EXPERTS, logits, neg)
    m1 = jnp.max(lg, axis=-1, keepdims=True)
    i1 = jnp.min(jnp.where(lg == m1, lane, LANES), axis=-1, keepdims=True)
    lg2 = jnp.where(lane == i1, neg, lg)
    m2 = jnp.max(lg2, axis=-1, keepdims=True)
    i2 = jnp.min(jnp.where(lg2 == m2, lane, LANES), axis=-1, keepdims=True)
    e = jnp.exp(m2 - m1)
    w1 = 1.0 / (1.0 + e)
    w2 = e / (1.0 + e)
    idx_ref[...] = jnp.where(lane == 0, i1, jnp.where(lane == 1, i2, 0))
    wt_ref[...] = jnp.where(lane == 0, w1, jnp.where(lane == 1, w2, 0.0))


W_PIECE = 512


def _out_proj_kernel(*refs, layer, lat_row0, route):
    if route:
        (o_ref, w_ref, xa_ref, xb_ref, gate_ref, g_ref, sc_ref, sh_ref, wr_ref, br_ref,
         xn_ref, h_ref, idx_ref, wt_ref, stage_ref, wbf_ref, xbuf_ref, wsem, xsem) = refs
    else:
        (o_ref, w_ref, xa_ref, xb_ref, gate_ref, g_ref, sc_ref, sh_ref,
         xn_ref, h_ref, stage_ref, wbf_ref, xbuf_ref, wsem, xsem) = refs
    i = pl.program_id(0)

    @pl.when(i == 0)
    def _():
        for p in range(D // W_PIECE):
            cols = pl.ds(p * W_PIECE, W_PIECE)
            cp = pltpu.make_async_copy(w_ref.at[layer, :, cols], stage_ref, wsem.at[0])
            cp.start()
            cp.wait()
            _cast_rows(stage_ref, wbf_ref.at[:, cols])

    slot = _next_stream_tile(xa_ref, xb_ref, lat_row0, xbuf_ref, xsem, i, pl.num_programs(0))
    acc = jnp.dot(o_ref[...], wbf_ref[...], preferred_element_type=F32)
    x_new = xbuf_ref[slot] + gate_ref[...] * acc
    xn_ref[...] = x_new
    h = _modulated_norm(x_new, g_ref[...], sc_ref[...], sh_ref[...])
    if route:
        h_ref[...] = h
        _route_top2(h, wr_ref[...], br_ref[...], idx_ref, wt_ref)
    else:
        h_ref[...] = h.astype(BF16)


def _out_proj(o, w_out, x, mod, norm2_g, layer, router=None):
    t = T_NORM
    row = lambda i: i
    xa, xb, lat_row0 = _stream_hbm(x)
    route = router is not None
    anyspace = pl.BlockSpec(memory_space=pl.ANY)
    rows = lambda w: pl.BlockSpec((t, w), lambda i: (i, 0))
    in_specs = [rows(D), anyspace, anyspace, anyspace,
                _mod_spec(layer, 2, t, row),
                pl.BlockSpec((None, 1, D), lambda i: (layer, 0, 0)),
                _mod_spec(layer, 4, t, row), _mod_spec(layer, 3, t, row)]
    args = [o, w_out, xa, xb, mod, norm2_g.reshape(DEPTH, 1, D), mod, mod]
    out_specs = [rows(D), rows(D)]
    out_shape = [jax.ShapeDtypeStruct((M, D), F32), jax.ShapeDtypeStruct((M, D), F32 if route else BF16)]
    if route:
        w_router, b_router = router
        args += [jnp.zeros((D, LANES), F32).at[:, :N_EXPERTS].set(w_router),
                 jnp.zeros((1, LANES), F32).at[0, :N_EXPERTS].set(b_router)]
        in_specs += [pl.BlockSpec((D, LANES), lambda i: (0, 0)), pl.BlockSpec((1, LANES), lambda i: (0, 0))]
        out_specs += [rows(LANES), rows(LANES)]
        out_shape += [jax.ShapeDtypeStruct((M, LANES), I32), jax.ShapeDtypeStruct((M, LANES), F32)]
    return pl.pallas_call(
        functools.partial(_out_proj_kernel, layer=layer, lat_row0=lat_row0, route=route),
        grid=(M // t,),
        in_specs=in_specs,
        out_specs=out_specs,
        out_shape=out_shape,
        scratch_shapes=[pltpu.VMEM((D, W_PIECE), F32), pltpu.VMEM((D, D), BF16),
                        pltpu.VMEM((2, t, D), F32),
                        pltpu.SemaphoreType.DMA((1,)), pltpu.SemaphoreType.DMA((2,))],
        compiler_params=_params(1),
        name="out_proj_router" if route else "out_proj",
    )(*args)


CAST_ROWS = 256


def _cast_rows(src_ref, dst_ref):
    def body(r, carry):
        rs = pl.ds(pl.multiple_of(r * CAST_ROWS, CAST_ROWS), CAST_ROWS)
        dst_ref[rs, :] = src_ref[rs, :].astype(BF16)
        return carry

    lax.fori_loop(0, src_ref.shape[0] // CAST_ROWS, body, 0)


def _head_rms(a):
    return lax.rsqrt(jnp.mean(a * a, axis=-1, keepdims=True) + EPS)


def _rope_partner(ag, perm):
    hi, lo = _split_bf16(ag)
    return (jnp.dot(hi, perm, preferred_element_type=F32)
            + jnp.dot(lo, perm, preferred_element_type=F32))


def _cast_kernel(w_ref, o_ref):
    o_ref[...] = w_ref[...].astype(BF16)


def _cast_in_weights(w_in):
    spec = pl.BlockSpec((None, D, TN_IN), lambda l, n: (l, 0, n))
    return pl.pallas_call(
        _cast_kernel,
        grid=(DEPTH, IN_WIDTH // TN_IN),
        in_specs=[spec],
        out_specs=spec,
        out_shape=jax.ShapeDtypeStruct(w_in.shape, BF16),
        compiler_params=_params(2),
        name="cast_w_in",
    )(w_in)


N_Q_TILES = ATTN_WIDTH // TN_IN
KV_TILE = N_Q_TILES
U_TILE0 = KV_TILE + 1
G_TILE0 = U_TILE0 + SGU_WIDTH // TN_IN
N_IN_TILES = IN_WIDTH // TN_IN


def _in_proj_kernel(xa_ref, xb_ref, n1_ref, sc0_ref, sh0_ref, sc1_ref, sh1_ref, w_ref,
                    qg_ref, kg_ref, sg_ref, pq_ref, pk_ref, cos_ref, sin_ref,
                    q_ref, kf_ref, kb_ref, vf_ref, vb_ref, u_ref, gh_ref,
                    xbuf_ref, h_ref, xsem, *, lat_row0):
    m = pl.program_id(0)
    n = pl.program_id(1)
    nm = pl.num_programs(0)
    cur = m % 2
    nxt = (m + 1) % 2
    has_next = m + 1 < nm
    fetch = functools.partial(_stream_tile_copy, xa_ref, xb_ref, lat_row0, xbuf_ref, xsem)

    def norm_rows(slot, rows, sc_ref, sh_ref):
        x = xbuf_ref[slot, rows, :]
        h_ref[slot, rows, :] = _modulated_norm(x, n1_ref[...], sc_ref[...], sh_ref[...]).astype(BF16)

    @pl.when((m == 0) & (n == 0))
    def _():
        fetch(0, 0, True)
        fetch(0, 0, False)
        norm_rows(0, slice(None), sc0_ref, sh0_ref)

    @pl.when((n == 0) & has_next)
    def _():
        fetch(m + 1, nxt, True)

    def matmul():
        return jnp.dot(h_ref[cur], w_ref[...], preferred_element_type=F32)

    for rope in (False, True):
        positioned = jnp.logical_not(_is_ctx_tile(m, TM)) if rope else _is_ctx_tile(m, TM)

        @pl.when((n < N_Q_TILES) & positioned)
        def _():
            acc = matmul()
            ag = acc * qg_ref[...]
            if rope:
                partner = _rope_partner(ag, pq_ref[...])
                cos, sin = cos_ref[...], sin_ref[...]
            for h in range(TN_IN // HEAD_DIM):
                sl = slice(h * HEAD_DIM, (h + 1) * HEAD_DIM)
                r = _head_rms(acc[:, sl]) * (ATTN_SCALE * LOG2_E)
                qh = ag[:, sl] * cos + partner[:, sl] * sin if rope else ag[:, sl]
                q_ref[:, sl] = (qh * r).astype(BF16)

        @pl.when((n == KV_TILE) & positioned)
        def _():
            acc = matmul()
            k = acc[:, :KV_WIDTH]
            ag = k * kg_ref[...]
            if rope:
                partner = _rope_partner(ag, pk_ref[...])
                cos, sin = cos_ref[...], sin_ref[...]
            for h in range(N_KV_HEADS):
                sl = slice(h * HEAD_DIM, (h + 1) * HEAD_DIM)
                r = _head_rms(k[:, sl])
                kf = ag[:, sl] * r
                kf_ref[:, sl] = kf
                kb_ref[:, sl] = ((ag[:, sl] * cos + partner[:, sl] * sin) * r if rope else kf).astype(BF16)
            v = acc[:, KV_WIDTH:]
            vf_ref[...] = v
            vb_ref[...] = v.astype(BF16)

    half = TM // (G_TILE0 - U_TILE0)
    for k in range(G_TILE0 - U_TILE0):
        for parity in range(2):
            @pl.when((n == U_TILE0 + k) & has_next & (cur == parity))
            def _():
                if k == 0:
                    fetch(m + 1, 1 - parity, False)
                u_ref[...] = jnp.dot(h_ref[parity], w_ref[...], preferred_element_type=F32).astype(BF16)
                norm_rows(1 - parity, slice(k * half, (k + 1) * half), sc1_ref, sh1_ref)

        @pl.when((n == U_TILE0 + k) & jnp.logical_not(has_next))
        def _():
            u_ref[...] = matmul().astype(BF16)

    @pl.when(n >= G_TILE0)
    def _():
        acc = matmul()
        for h in range(TN_IN // HEAD_DIM):
            sl = slice(h * HEAD_DIM, (h + 1) * HEAD_DIM)
            a = acc[:, sl]
            gh_ref[:, sl] = (a * _head_rms(a) * sg_ref[:, sl]).astype(BF16)


def _rope_tables():
    n_rows = DEC_SEQ // GRID_W
    rows = jnp.broadcast_to(jnp.arange(n_rows)[:, None], (n_rows, GRID_W)).reshape(-1)
    cols = jnp.broadcast_to(jnp.arange(GRID_W)[None, :], (n_rows, GRID_W)).reshape(-1)
    inv = ROPE_THETA ** (-jnp.arange(0, ROPE_AXIS_DIM, 2, dtype=F32) / ROPE_AXIS_DIM)
    ang_r = rows.astype(F32)[:, None] * inv
    ang_c = cols.astype(F32)[:, None] * inv
    cos = jnp.concatenate([jnp.cos(ang_r), jnp.cos(ang_r), jnp.cos(ang_c), jnp.cos(ang_c)], axis=1)
    sin = jnp.concatenate([-jnp.sin(ang_r), jnp.sin(ang_r), -jnp.sin(ang_c), jnp.sin(ang_c)], axis=1)
    return cos, sin


def _partner_matrix(n_heads):
    w = n_heads * HEAD_DIM
    quarter = ROPE_AXIS_DIM // 2
    j = jnp.arange(w)
    partner = jnp.where((j % ROPE_AXIS_DIM) < quarter, j + quarter, j - quarter)
    return (jnp.arange(w)[:, None] == partner[None, :]).astype(BF16)


def _rope_block(m):
    return jnp.maximum(m - MP // TM, 0) % (DEC_SEQ // TM)


def _in_projections(x, w_in_bf, mod, norm1_g, q_norm_g, k_norm_g, sgu_norm_g, cos, sin, layer):
    tn = TN_IN
    xa, xb, lat_row0 = _stream_hbm(x)
    row = lambda m, n: m
    next_row = lambda m, n: jnp.minimum(m + 1, M // TM - 1)
    anyspace = pl.BlockSpec(memory_space=pl.ANY)
    const = lambda shape: pl.BlockSpec(shape, lambda m, n: (0,) * len(shape))
    rope_spec = pl.BlockSpec((TM, HEAD_DIM), lambda m, n: (_rope_block(m), 0))
    q_heads = tn // HEAD_DIM
    q_gain = jnp.tile(q_norm_g[layer], q_heads)[None, :]
    k_gain = jnp.tile(k_norm_g[layer], N_KV_HEADS)[None, :]
    g_tile = lambda n: jnp.clip(n - G_TILE0, 0, SGU_WIDTH // tn - 1)
    kv_out = pl.BlockSpec((TM, KV_WIDTH), lambda m, n: (m, 0))
    kv_shape = lambda dt: jax.ShapeDtypeStruct((M, KV_WIDTH), dt)
    return pl.pallas_call(
        functools.partial(_in_proj_kernel, lat_row0=lat_row0),
        grid=(M // TM, N_IN_TILES),
        in_specs=[anyspace, anyspace,
                  pl.BlockSpec((None, 1, D), lambda m, n: (layer, 0, 0)),
                  _mod_spec(layer, 1, TM, row), _mod_spec(layer, 0, TM, row),
                  _mod_spec(layer, 1, TM, next_row), _mod_spec(layer, 0, TM, next_row),
                  pl.BlockSpec((None, D, tn), lambda m, n: (layer, 0, n)),
                  const((1, tn)), const((1, KV_WIDTH)),
                  pl.BlockSpec((None, 1, tn), lambda m, n: (layer, 0, g_tile(n))),
                  const((tn, tn)), const((KV_WIDTH, KV_WIDTH)), rope_spec, rope_spec],
        out_specs=[pl.BlockSpec((TM, tn), lambda m, n: (m, jnp.minimum(n, N_Q_TILES - 1))),
                   kv_out, kv_out, kv_out, kv_out,
                   pl.BlockSpec((TM, tn), lambda m, n: (m, jnp.clip(n - U_TILE0, 0, SGU_WIDTH // tn - 1))),
                   pl.BlockSpec((TM, tn), lambda m, n: (m, g_tile(n)))],
        out_shape=[jax.ShapeDtypeStruct((M, ATTN_WIDTH), BF16),
                   kv_shape(F32), kv_shape(BF16), kv_shape(F32), kv_shape(BF16),
                   jax.ShapeDtypeStruct((M, SGU_WIDTH), BF16),
                   jax.ShapeDtypeStruct((M, SGU_WIDTH), BF16)],
        scratch_shapes=[pltpu.VMEM((2, TM, D), F32), pltpu.VMEM((2, TM, D), BF16),
                        pltpu.SemaphoreType.DMA((2,))],
        compiler_params=_params(2),
        name="in_proj",
    )(xa, xb, norm1_g.reshape(DEPTH, 1, D), mod, mod, mod, mod, w_in_bf, q_gain, k_gain,
      sgu_norm_g.reshape(DEPTH, 1, SGU_WIDTH), _partner_matrix(q_heads), _partner_matrix(N_KV_HEADS),
      cos, sin)


def _qk(q, k):
    return lax.dot_general(q, k, (((1,), (1,)), ((), ())), preferred_element_type=F32)


def _attn_kernel(*refs, has_cache, n_batch, seq, tq):
    def with_ones(v):
        return jnp.concatenate([v, jnp.ones_like(v)], axis=1)

    if has_cache:
        q_ref, k_ref, v_ref, kc_ref, vc_ref, o_ref = refs
    else:
        q_ref, k_ref, v_ref, o_ref = refs
    for b in range(n_batch):
        rows_q = slice(b * tq, (b + 1) * tq)
        rows_k = slice(b * seq, (b + 1) * seq)
        for kv in range(N_KV_HEADS):
            kv_cols = slice(kv * HEAD_DIM, (kv + 1) * HEAD_DIM)
            k = k_ref[rows_k, kv_cols]
            v = with_ones(v_ref[rows_k, kv_cols])
            if has_cache:
                kc = kc_ref[:, kv_cols].astype(BF16)
                vc = with_ones(vc_ref[:, kv_cols].astype(BF16))
            for g in range(Q_PER_KV):
                head = kv * Q_PER_KV + g
                sl = slice(head * HEAD_DIM, (head + 1) * HEAD_DIM)
                q = q_ref[rows_q, sl]
                s = _qk(q, k)
                m = jnp.max(s, axis=-1, keepdims=True)
                if has_cache:
                    sc = _qk(q, kc)
                    m = jnp.maximum(m, jnp.max(sc, axis=-1, keepdims=True))
                o = jnp.dot(jnp.exp2(s - m).astype(BF16), v, preferred_element_type=F32)
                if has_cache:
                    o = o + jnp.dot(jnp.exp2(sc - m).astype(BF16), vc, preferred_element_type=F32)
                o_ref[rows_q, sl] = (o[:, :HEAD_DIM] / o[:, HEAD_DIM:]).astype(BF16)


def _attention(q, kb, vb, cache_k, cache_v, layer, *, batch, seq, row0):
    has_cache = cache_k is not None
    tq = min(T_Q, seq)
    nq = seq // tq
    n_batch = max(1, ATTN_ROWS // seq) if nq == 1 else 1
    q_spec = pl.BlockSpec((n_batch * tq, ATTN_WIDTH), lambda b, i: (row0 // (n_batch * tq) + b * nq + i, 0))
    kv_spec = pl.BlockSpec((n_batch * seq, KV_WIDTH), lambda b, i: (row0 // (n_batch * seq) + b, 0))
    in_specs = [q_spec, kv_spec, kv_spec]
    args = [q, kb, vb]
    if has_cache:
        c_spec = pl.BlockSpec((None, None, PAST_LEN, KV_WIDTH), lambda b, i: (b, layer, 0, 0))
        in_specs += [c_spec, c_spec]
        args += [cache_k.reshape(DEC_BATCH, DEPTH, PAST_LEN, KV_WIDTH),
                 cache_v.reshape(DEC_BATCH, DEPTH, PAST_LEN, KV_WIDTH)]
    return pl.pallas_call(
        functools.partial(_attn_kernel, has_cache=has_cache, n_batch=n_batch, seq=seq, tq=tq),
        grid=(batch // n_batch, nq),
        in_specs=in_specs,
        out_specs=pl.BlockSpec((n_batch * tq, ATTN_WIDTH), lambda b, i: (b * nq + i, 0)),
        out_shape=jax.ShapeDtypeStruct((batch * seq, ATTN_WIDTH), BF16),
        compiler_params=_params(2),
        name="attention_cached" if has_cache else "attention",
    )(*args)


def _sgu_merge_kernel(u_ref, gh_ref, ap_ref, as_ref, ws_ref, bs_ref, gn_ref, o_ref, sgu_ref):
    t = u_ref.shape[0]
    a = _pick(ap_ref, as_ref, 0).astype(F32)
    a = a * lax.rsqrt(jnp.mean(a * a, axis=-1, keepdims=True) + EPS) * gn_ref[:, :ATTN_WIDTH]
    o_ref[:, :ATTN_WIDTH] = a.astype(BF16)
    for h in range(N_SGU_HEADS):
        cs = slice(h * HEAD_DIM, (h + 1) * HEAD_DIM)
        w = ws_ref[h].astype(BF16)
        b = bs_ref[h]
        for c in range(t // CHUNK):
            rs = slice(c * CHUNK, (c + 1) * CHUNK)
            mixed = jnp.dot(w, gh_ref[rs, cs], preferred_element_type=F32) + b
            sgu_ref[rs, cs] = u_ref[rs, cs].astype(F32) * mixed
    s = sgu_ref[...]
    s = s * lax.rsqrt(jnp.mean(s * s, axis=-1, keepdims=True) + EPS) * gn_ref[:, ATTN_WIDTH:]
    o_ref[:, ATTN_WIDTH:] = s.astype(BF16)


def _sgu_merge(u, gh, attn_ctx, attn_lat, w_spatial, b_spatial, out_norm_g, layer):
    t = T_SGU
    bias = jnp.broadcast_to(b_spatial[:, :, :, None], (DEPTH, N_SGU_HEADS, CHUNK, HEAD_DIM))
    row = lambda w: pl.BlockSpec((t, w), lambda i: (i, 0))
    a_args, a_specs = _stream_in((attn_ctx, attn_lat), t, ATTN_WIDTH, lambda i: i, lambda i: 0)
    return pl.pallas_call(
        _sgu_merge_kernel,
        grid=(M // t,),
        in_specs=[row(SGU_WIDTH), row(SGU_WIDTH)] + a_specs + [
            pl.BlockSpec((None, N_SGU_HEADS, CHUNK, CHUNK), lambda i: (layer, 0, 0, 0)),
            pl.BlockSpec((None, N_SGU_HEADS, CHUNK, HEAD_DIM), lambda i: (layer, 0, 0, 0)),
            pl.BlockSpec((None, 1, D), lambda i: (layer, 0, 0))],
        out_specs=row(D),
        out_shape=jax.ShapeDtypeStruct((M, D), BF16),
        scratch_shapes=[pltpu.VMEM((t, SGU_WIDTH), F32)],
        compiler_params=_params(1),
        name="sgu_merge",
    )(u, gh, *a_args, w_spatial, bias, out_norm_g.reshape(DEPTH, 1, D))


def _mm_resid_kernel(a_ref, w_ref, x_ref, g_ref, o_ref, stage_ref, wbf_ref, sem, *, w_index):
    n = pl.program_id(0)
    tn = stage_ref.shape[1]

    def weight_copy(col_block):
        cols = pl.ds(pl.multiple_of(col_block * tn, LANES), tn)
        return pltpu.make_async_copy(w_ref.at[w_index, :, cols], stage_ref, sem.at[0])

    @pl.when(pl.program_id(1) == 0)
    def _():
        @pl.when(n == 0)
        def _():
            weight_copy(0).start()

        weight_copy(0).wait()
        _cast_rows(stage_ref, wbf_ref)

        @pl.when(n + 1 < pl.num_programs(0))
        def _():
            weight_copy(n + 1).start()

    acc = jnp.dot(a_ref[...], wbf_ref[...], preferred_element_type=F32)
    o_ref[...] = x_ref[...] + g_ref[...] * acc


def _mm_resid(a, w, x, mod, layer, w_index, gate_chunk, tm, tn):
    k = a.shape[1]
    row = lambda n, m: m
    col = lambda n, m: n
    return pl.pallas_call(
        functools.partial(_mm_resid_kernel, w_index=w_index),
        grid=(D // tn, M // tm),
        in_specs=[pl.BlockSpec((tm, k), lambda n, m: (m, 0)),
                  pl.BlockSpec(memory_space=pl.ANY),
                  pl.BlockSpec((tm, tn), lambda n, m: (m, n)),
                  _mod_spec(layer, gate_chunk, tm, row, col, tn=tn)],
        out_specs=pl.BlockSpec((tm, tn), lambda n, m: (m, n)),
        out_shape=jax.ShapeDtypeStruct((M, D), F32),
        scratch_shapes=[pltpu.VMEM((k, tn), F32), pltpu.VMEM((k, tn), BF16),
                        pltpu.SemaphoreType.DMA((1,))],
        compiler_params=_params(2),
        name="mm_resid",
    )(a, w, x, mod)


def _swiglu(a, b):
    return a * jax.nn.sigmoid(a) * b


def _ffn_gu_kernel(x_ref, wg_ref, wu_ref, o_ref, wgb_ref, wub_ref):
    @pl.when(pl.program_id(1) == 0)
    def _():
        _cast_rows(wg_ref, wgb_ref)
        _cast_rows(wu_ref, wub_ref)
    for r in range(x_ref.shape[0] // TM):
        rs = slice(r * TM, (r + 1) * TM)
        x = x_ref[rs, :]
        a = jnp.dot(x, wgb_ref[...], preferred_element_type=F32)
        b = jnp.dot(x, wub_ref[...], preferred_element_type=F32)
        o_ref[rs, :] = _swiglu(a, b).astype(BF16)


def _ffn_gate_up(h, w_gate, w_up, j):
    tm, tn = 2 * TM, 512
    w_spec = pl.BlockSpec((None, D, tn), lambda n, m: (j, 0, n))
    return pl.pallas_call(
        _ffn_gu_kernel,
        grid=(D_FF // tn, M // tm),
        in_specs=[pl.BlockSpec((tm, D), lambda n, m: (m, 0)), w_spec, w_spec],
        out_specs=pl.BlockSpec((tm, tn), lambda n, m: (m, n)),
        out_shape=jax.ShapeDtypeStruct((M, D_FF), BF16),
        scratch_shapes=[pltpu.VMEM((D, tn), BF16), pltpu.VMEM((D, tn), BF16)],
        compiler_params=_params(2),
        name="ffn_gate_up",
    )(h, w_gate, w_up)


def _route_meta(idx):
    t = T_MOE
    experts = jnp.arange(N_EXPERTS, dtype=I32)
    onehot = (idx[:, :, None] == experts[None, None, :]).astype(I32).sum(axis=1)
    csum = jnp.cumsum(onehot, axis=0)
    rank = csum - onehot
    count = csum[-1]
    ntile = (count + t - 1) // t
    tile_end = jnp.cumsum(ntile)
    tile_start = tile_end - ntile
    nused = tile_end[-1]
    pos = (tile_start * t)[idx] + jnp.take_along_axis(rank, idx, axis=1)
    j = jnp.arange(NT_MOE, dtype=I32)
    te_raw = jnp.minimum(jnp.sum(j[:, None] >= tile_end[None, :], axis=1), N_EXPERTS - 1).astype(I32)
    te = jnp.where(j < nused, te_raw, te_raw[nused - 1])
    first = ((j == tile_start[te]) & (j < nused)).astype(I32)
    later = (ntile[None, :] > 0) & (experts[None, :] > te[:, None])
    nxt = jnp.min(jnp.where(later, experts[None, :], N_EXPERTS), axis=1)
    nxt = jnp.where(nxt == N_EXPERTS, -1, nxt).astype(I32)
    pad_start = tile_start * t + count
    pad_len = ntile * t - count
    tail = jnp.stack([nused * t, (NT_MOE - nused) * (t // ZERO_ROWS)])
    zinfo = jnp.concatenate([pad_start, pad_len, tail]).astype(I32)
    return pos.astype(I32), zinfo, (te, first, nxt, nused.reshape(1).astype(I32))


def _dispatch_kernel(p0_ref, p1_ref, z_ref, h_ref, xs_ref, zero_ref, sem, zsem):
    t = h_ref.shape[0]
    i = pl.program_id(0)
    base = i * t

    def clear_padding(start):
        def go(n, off):
            cp = pltpu.make_async_copy(zero_ref.at[pl.ds(0, n), :], xs_ref.at[pl.ds(off, n), :], zsem.at[0])
            cp.start() if start else cp.wait()

        for e in range(N_EXPERTS):
            off, ln = z_ref[e], z_ref[N_EXPERTS + e]
            end = off + ln
            for b in range(SUBLANE_BITS, PAD_BITS):
                @pl.when(((ln >> b) & 1) == 1)
                def _():
                    go(1 << b, pl.multiple_of(end - ((ln >> b) << b), SUBLANES))
            for k in range(SUBLANES - 1):
                @pl.when(k < (ln & (SUBLANES - 1)))
                def _():
                    go(1, off + k)
        tail0, n_tail = z_ref[2 * N_EXPERTS], z_ref[2 * N_EXPERTS + 1]

        def tail_body(k, carry):
            go(ZERO_ROWS, pl.multiple_of(tail0 + k * ZERO_ROWS, SUBLANES))
            return carry

        lax.fori_loop(0, n_tail, tail_body, 0)

    @pl.when(i == 0)
    def _():
        zero_ref[...] = jnp.zeros(zero_ref.shape, zero_ref.dtype)
        clear_padding(True)

    def issue(r, carry):
        src = h_ref.at[pl.ds(r, 1), :]
        pltpu.make_async_copy(src, xs_ref.at[pl.ds(p0_ref[base + r], 1), :], sem.at[0]).start()
        pltpu.make_async_copy(src, xs_ref.at[pl.ds(p1_ref[base + r], 1), :], sem.at[1]).start()
        return carry

    lax.fori_loop(0, t, issue, 0, unroll=8)
    pltpu.make_async_copy(h_ref, xs_ref.at[pl.ds(0, t), :], sem.at[0]).wait()
    pltpu.make_async_copy(h_ref, xs_ref.at[pl.ds(0, t), :], sem.at[1]).wait()

    @pl.when(i == 0)
    def _():
        clear_padding(False)


def _dispatch(h, pos0, pos1, zinfo):
    t = T_DISPATCH
    return pl.pallas_call(
        _dispatch_kernel,
        grid_spec=pltpu.PrefetchScalarGridSpec(
            num_scalar_prefetch=3,
            grid=(M // t,),
            in_specs=[pl.BlockSpec((t, D), lambda i, p0, p1, z: (i, 0))],
            out_specs=pl.BlockSpec(memory_space=pl.ANY),
            scratch_shapes=[pltpu.VMEM((ZERO_ROWS, D), F32),
                            pltpu.SemaphoreType.DMA((2,)), pltpu.SemaphoreType.DMA((1,))]),
        out_shape=jax.ShapeDtypeStruct((P_MOE, D), F32),
        compiler_params=_params(1),
        name="moe_dispatch",
    )(pos0, pos1, zinfo, h)


def _expert_weight_stream(w_refs, stage_refs, bf_refs, sem, te_ref, first_ref, nxt_ref, tn):
    c = pl.program_id(0)
    j = pl.program_id(1)
    nc = pl.num_programs(0)

    def copies(e, cc):
        col = pl.multiple_of(cc * tn, LANES)
        return [pltpu.make_async_copy(w.at[e, :, pl.ds(col, tn)], st, sem.at[k])
                for k, (w, st) in enumerate(zip(w_refs, stage_refs))]

    def start(e, cc):
        for cp in copies(e, cc):
            cp.start()

    @pl.when((c == 0) & (j == 0))
    def _():
        start(te_ref[0], 0)

    @pl.when(first_ref[j] == 1)
    def _():
        for cp in copies(0, 0):
            cp.wait()
        for st, bf in zip(stage_refs, bf_refs):
            _cast_rows(st, bf)
        ne = nxt_ref[j]

        @pl.when(ne >= 0)
        def _():
            start(ne, c)

        @pl.when((ne < 0) & (c + 1 < nc))
        def _():
            start(te_ref[0], c + 1)


def _gmm_gate_up_kernel(te_ref, first_ref, nxt_ref, nused_ref, xs_ref, wg_ref, wu_ref, o_ref,
                        sg_ref, su_ref, wgb_ref, wub_ref, sem):
    _expert_weight_stream((wg_ref, wu_ref), (sg_ref, su_ref), (wgb_ref, wub_ref), sem,
                          te_ref, first_ref, nxt_ref, TN_GU)

    @pl.when(pl.program_id(1) < nused_ref[0])
    def _():
        half = T_MOE // 2
        for r in range(2):
            rs = slice(r * half, (r + 1) * half)
            x = xs_ref[rs, :].astype(BF16)
            a = jnp.dot(x, wgb_ref[...], preferred_element_type=F32)
            b = jnp.dot(x, wub_ref[...], preferred_element_type=F32)
            o_ref[rs, :] = _swiglu(a, b).astype(BF16)

    @pl.when(pl.program_id(1) >= nused_ref[0])
    def _():
        o_ref[...] = jnp.zeros(o_ref.shape, o_ref.dtype)


def _gmm_down_kernel(te_ref, first_ref, nxt_ref, nused_ref, a_ref, wd_ref, o_ref,
                     sd_ref, wdb_ref, sem):
    _expert_weight_stream((wd_ref,), (sd_ref,), (wdb_ref,), sem, te_ref, first_ref, nxt_ref, TN_DN)

    @pl.when(pl.program_id(1) < nused_ref[0])
    def _():
        o_ref[...] = jnp.dot(a_ref[...], wdb_ref[...], preferred_element_type=F32)

    @pl.when(pl.program_id(1) >= nused_ref[0])
    def _():
        o_ref[...] = jnp.zeros(o_ref.shape, o_ref.dtype)


def _used_tile(j, nused):
    return jnp.minimum(j, nused[0] - 1)


def _gmm_gate_up(xs, w_gate, w_up, meta):
    te, first, nxt, nused = meta
    tn = TN_GU
    return pl.pallas_call(
        _gmm_gate_up_kernel,
        grid_spec=pltpu.PrefetchScalarGridSpec(
            num_scalar_prefetch=4,
            grid=(D_FF_EXPERT // tn, NT_MOE),
            in_specs=[pl.BlockSpec((T_MOE, D), lambda c, j, te, fi, nx, nu: (_used_tile(j, nu), 0)),
                      pl.BlockSpec(memory_space=pl.ANY),
                      pl.BlockSpec(memory_space=pl.ANY)],
            out_specs=pl.BlockSpec((T_MOE, tn), lambda c, j, te, fi, nx, nu: (j, c)),
            scratch_shapes=[pltpu.VMEM((D, tn), F32), pltpu.VMEM((D, tn), F32),
                            pltpu.VMEM((D, tn), BF16), pltpu.VMEM((D, tn), BF16),
                            pltpu.SemaphoreType.DMA((2,))]),
        out_shape=jax.ShapeDtypeStruct((P_MOE, D_FF_EXPERT), BF16),
        compiler_params=_params(2),
        name="moe_gate_up",
    )(te, first, nxt, nused, xs, w_gate, w_up)


def _gmm_down(act, w_down, meta):
    te, first, nxt, nused = meta
    tn = TN_DN
    return pl.pallas_call(
        _gmm_down_kernel,
        grid_spec=pltpu.PrefetchScalarGridSpec(
            num_scalar_prefetch=4,
            grid=(D // tn, NT_MOE),
            in_specs=[pl.BlockSpec((T_MOE, D_FF_EXPERT), lambda c, j, te, fi, nx, nu: (_used_tile(j, nu), 0)),
                      pl.BlockSpec(memory_space=pl.ANY)],
            out_specs=pl.BlockSpec((T_MOE, tn), lambda c, j, te, fi, nx, nu: (j, c)),
            scratch_shapes=[pltpu.VMEM((D_FF_EXPERT, tn), F32), pltpu.VMEM((D_FF_EXPERT, tn), BF16),
                            pltpu.SemaphoreType.DMA((1,))]),
        out_shape=jax.ShapeDtypeStruct((P_MOE, D), F32),
        compiler_params=_params(2),
        name="moe_down",
    )(te, first, nxt, nused, act, w_down)


def _combine_kernel(p0_ref, p1_ref, ys_ref, xa_ref, xb_ref, g_ref, w_ref, oc_ref, ol_ref,
                    a_ref, b_ref, sem):
    t = xa_ref.shape[0]
    i = pl.program_id(0)
    n = pl.num_programs(0)

    def issue(step, slot):
        base = step * t

        def body(r, carry):
            pltpu.make_async_copy(ys_ref.at[pl.ds(p0_ref[base + r], 1), :],
                                  a_ref.at[slot, pl.ds(r, 1), :], sem.at[0, slot]).start()
            pltpu.make_async_copy(ys_ref.at[pl.ds(p1_ref[base + r], 1), :],
                                  b_ref.at[slot, pl.ds(r, 1), :], sem.at[1, slot]).start()
            return carry

        lax.fori_loop(0, t, body, 0, unroll=8)

    @pl.when(i == 0)
    def _():
        issue(0, 0)

    @pl.when(i + 1 < n)
    def _():
        issue(i + 1, (i + 1) % 2)

    slot = i % 2
    pltpu.make_async_copy(ys_ref.at[pl.ds(0, t), :], a_ref.at[slot], sem.at[0, slot]).wait()
    pltpu.make_async_copy(ys_ref.at[pl.ds(0, t), :], b_ref.at[slot], sem.at[1, slot]).wait()
    w = w_ref[...]
    moe = w[:, 0:1] * a_ref[slot] + w[:, 1:2] * b_ref[slot]
    y = _pick(xa_ref, xb_ref, 0) + g_ref[...] * moe
    is_ctx = _is_ctx_tile(i, t)

    @pl.when(is_ctx)
    def _():
        oc_ref[...] = y

    @pl.when(jnp.logical_not(is_ctx))
    def _():
        ol_ref[...] = y


def _combine(ys, x, mod, layer, gate_chunk, wts, pos0, pos1):
    t = T_COMBINE
    n_ctx = MP // t
    row = lambda i, *_: i
    x_args, x_specs = _stream_in(x, t, D, row, lambda i, *_: 0)
    return pl.pallas_call(
        _combine_kernel,
        grid_spec=pltpu.PrefetchScalarGridSpec(
            num_scalar_prefetch=2,
            grid=(M // t,),
            in_specs=[pl.BlockSpec(memory_space=pl.ANY)] + x_specs + [
                _mod_spec(layer, gate_chunk, t, row),
                pl.BlockSpec((t, LANES), lambda i, p0, p1: (i, 0))],
            out_specs=[pl.BlockSpec((t, D), lambda i, p0, p1: (jnp.minimum(i, n_ctx - 1), 0)),
                       pl.BlockSpec((t, D), lambda i, p0, p1: (jnp.maximum(i - n_ctx, 0), 0))],
            scratch_shapes=[pltpu.VMEM((2, t, D), F32), pltpu.VMEM((2, t, D), F32),
                            pltpu.SemaphoreType.DMA((2, 2))]),
        out_shape=[jax.ShapeDtypeStruct((MP, D), F32), jax.ShapeDtypeStruct((MS, D), F32)],
        compiler_params=_params(1),
        name="moe_combine",
    )(pos0, pos1, ys, *x_args, mod, wts)


def _moe(x, h, idx, wts, mod, layer, w_gate, w_up, w_down):
    pos, zinfo, meta = _route_meta(idx[:, :TOP_K])
    pos0, pos1 = pos[:, 0], pos[:, 1]
    xs = _dispatch(h, pos0, pos1, zinfo)
    act = _gmm_gate_up(xs, w_gate, w_up, meta)
    ys = _gmm_down(act, w_down, meta)
    return _combine(ys, x, mod, layer, 5, wts, pos0, pos1)


def kernel(x_prompt, x_sample, cache_k, cache_v, c, c_ctx, w_ada, b_ada, norm1_g, norm2_g, w_in, q_norm_g, k_norm_g, sgu_norm_g, w_spatial, b_spatial, out_norm_g, w_out, ffn_w_gate, ffn_w_up, ffn_w_down, w_router, b_router, moe_w_gate, moe_w_up, moe_w_down):
    assert DEPTH == 2
    x = (x_prompt.reshape(MP, D), x_sample.reshape(MS, D))
    cond = jnp.concatenate([c_ctx[None, :], c, jnp.zeros((N_COND - 1 - DEC_BATCH, D), F32)], axis=0)
    mod = _modulation(cond, w_ada, b_ada).reshape(DEPTH, N_COND, 1, N_MOD * D)
    cos, sin = _rope_tables()
    w_in_bf = _cast_in_weights(w_in)

    new_k, new_v = [], []
    for i in range(DEPTH):
        q, kf, kb, vf, vb, u, gh = _in_projections(x, w_in_bf, mod, norm1_g, q_norm_g, k_norm_g,
                                                   sgu_norm_g, cos, sin, i)
        attn_ctx = _attention(q, kb, vb, None, None, i, batch=BATCH, seq=SEQ, row0=0)
        attn_lat = _attention(q, kb, vb, cache_k, cache_v, i, batch=DEC_BATCH, seq=DEC_SEQ, row0=MP)
        o = _sgu_merge(u, gh, attn_ctx, attn_lat, w_spatial, b_spatial, out_norm_g, i)
        j = i // 2
        if i % 2 == 0:
            x, h2 = _out_proj(o, w_out, x, mod, norm2_g, i)
            act = _ffn_gate_up(h2, ffn_w_gate, ffn_w_up, j)
            x = _mm_resid(act, ffn_w_down, x, mod, i, j, 5, TM, 512)
        else:
            x, h2, idx, wts = _out_proj(o, w_out, x, mod, norm2_g, i, router=(w_router[j], b_router[j]))
            x = _moe(x, h2, idx, wts, mod, i, moe_w_gate[j], moe_w_up[j], moe_w_down[j])
        new_k.append(kf[:MP].reshape(BATCH, SEQ, N_KV_HEADS, HEAD_DIM))
        new_v.append(vf[:MP].reshape(BATCH, SEQ, N_KV_HEADS, HEAD_DIM))

    y_prompt = x[0].reshape(BATCH, SEQ, D)
    y_sample = x[1].reshape(DEC_BATCH, DEC_SEQ, D)
    return (y_prompt, y_sample, jnp.stack(new_k, axis=1), jnp.stack(new_v, axis=1))
```

```python
import functools

import jax
import jax.numpy as jnp
from jax import lax
from jax.experimental import pallas as pl
from jax.experimental.pallas import tpu as pltpu

F32 = jnp.float32
BF16 = jnp.bfloat16
I32 = jnp.int32

D = 2048
BATCH, SEQ = 16, 256
DEC_BATCH, DEC_SEQ = 4, 2048
PAST_LEN = 256
DEPTH = 2
GRID_W = 64
CHUNK = 128
HEAD_DIM = 128
N_Q_HEADS, N_KV_HEADS = 8, 2
Q_PER_KV = N_Q_HEADS // N_KV_HEADS
ATTN_WIDTH = N_Q_HEADS * HEAD_DIM
KV_WIDTH = N_KV_HEADS * HEAD_DIM
N_SGU_HEADS = 8
SGU_WIDTH = N_SGU_HEADS * HEAD_DIM
IN_WIDTH = ATTN_WIDTH + 2 * KV_WIDTH + 2 * SGU_WIDTH
ROPE_THETA = 10000.0
ROPE_AXIS_DIM = HEAD_DIM // 2
D_FF = 5632
N_EXPERTS = 8
TOP_K = 2
D_FF_EXPERT = 2816
N_MOD = 6
EPS = 1e-6
ATTN_SCALE = HEAD_DIM ** -0.5
LOG2_E = 1.4426950408889634

MP = BATCH * SEQ
MS = DEC_BATCH * DEC_SEQ
M = MP + MS
N_COND = 8
LANES = 128
SUBLANES = 8
SUBLANE_BITS = 3

VMEM_LIMIT = 56 * 1024 * 1024

TM = 1024
TN_IN = 2 * KV_WIDTH
T_NORM = 512
T_Q = 512
ATTN_ROWS = 1024
T_MOE = 512
P_MOE = M * TOP_K + N_EXPERTS * T_MOE
NT_MOE = P_MOE // T_MOE
TN_GU = D_FF_EXPERT // 2
TN_DN = D
T_DISPATCH = 1024
T_COMBINE = 256
T_SGU = 1024
ZERO_ROWS = T_MOE // 2
PAD_BITS = ZERO_ROWS.bit_length()


def _params(n_axes):
    return pltpu.CompilerParams(dimension_semantics=("arbitrary",) * n_axes,
                                vmem_limit_bytes=VMEM_LIMIT)


def _cond_row(i, t):
    return jnp.where(i < MP // t, 0, 1 + (i - MP // t) // (DEC_SEQ // t))


def _is_ctx_tile(i, t):
    return i < MP // t


def _stream_in(x, t, width, row_of, col_of):
    n_ctx = MP // t
    pair = isinstance(x, tuple)
    base = 0 if pair else n_ctx
    ctx = pl.BlockSpec((t, width), lambda *g: (jnp.minimum(row_of(*g), n_ctx - 1), col_of(*g)))
    lat = pl.BlockSpec((t, width), lambda *g: (base + jnp.maximum(row_of(*g) - n_ctx, 0), col_of(*g)))
    return (list(x) if pair else [x, x]), [ctx, lat]


def _mod_spec(layer, chunk, t, row_of, col_of=None, tn=D):
    per = D // tn

    def index_map(*g):
        col = chunk * per + (col_of(*g) if col_of is not None else 0)
        return (layer, _cond_row(row_of(*g), t), 0, col)

    return pl.BlockSpec((None, None, 1, tn), index_map)


def _ada_kernel(c_ref, w_ref, b_ref, o_ref):
    c = c_ref[...]
    s = (c * jax.nn.sigmoid(c)).astype(BF16)
    o_ref[...] = jnp.dot(s, w_ref[...].astype(BF16), preferred_element_type=F32) + b_ref[...]


def _modulation(cond, w_ada, b_ada):
    tn = 1024
    width = N_MOD * D
    return pl.pallas_call(
        _ada_kernel,
        grid=(DEPTH, width // tn),
        in_specs=[pl.BlockSpec((N_COND, D), lambda l, n: (0, 0)),
                  pl.BlockSpec((None, D, tn), lambda l, n: (l, 0, n)),
                  pl.BlockSpec((None, 1, tn), lambda l, n: (l, 0, n))],
        out_specs=pl.BlockSpec((None, N_COND, tn), lambda l, n: (l, 0, n)),
        out_shape=jax.ShapeDtypeStruct((DEPTH, N_COND, width), F32),
        compiler_params=_params(2),
        name="modulation",
    )(cond, w_ada, b_ada.reshape(DEPTH, 1, width))


def _modulated_norm(x, g, sc, sh):
    y = x * lax.rsqrt(jnp.mean(x * x, axis=-1, keepdims=True) + EPS)
    return y * (g * (1.0 + sc)) + sh


def _pick(xa_ref, xb_ref, axis):
    t = xa_ref.shape[0]
    return jnp.where(_is_ctx_tile(pl.program_id(axis), t), xa_ref[...], xb_ref[...])


def _stream_tile_copy(xa_ref, xb_ref, lat_row0, buf_ref, sem, tile, slot, start):
    t = buf_ref.shape[1]

    def copy(src_ref, row):
        return pltpu.make_async_copy(src_ref.at[pl.ds(pl.multiple_of(row, t), t), :],
                                     buf_ref.at[slot], sem.at[slot])

    if not start:
        copy(xa_ref, 0).wait()
        return
    is_ctx = _is_ctx_tile(tile, t)

    @pl.when(is_ctx)
    def _():
        copy(xa_ref, tile * t).start()

    @pl.when(jnp.logical_not(is_ctx))
    def _():
        copy(xb_ref, lat_row0 + (tile - MP // t) * t).start()


def _next_stream_tile(xa_ref, xb_ref, lat_row0, buf_ref, sem, tile, n_tiles):
    @pl.when(tile == 0)
    def _():
        _stream_tile_copy(xa_ref, xb_ref, lat_row0, buf_ref, sem, 0, 0, True)

    @pl.when(tile + 1 < n_tiles)
    def _():
        _stream_tile_copy(xa_ref, xb_ref, lat_row0, buf_ref, sem, tile + 1, (tile + 1) % 2, True)

    slot = tile % 2
    _stream_tile_copy(xa_ref, xb_ref, lat_row0, buf_ref, sem, tile, slot, False)
    return slot


def _stream_hbm(x):
    return (x[0], x[1], 0) if isinstance(x, tuple) else (x, x, MP)


def _split_bf16(a):
    hi = a.astype(BF16)
    return hi, (a - hi.astype(F32)).astype(BF16)


def _route_top2(h, wr, br, idx_ref, wt_ref):
    h_hi, h_lo = _split_bf16(h)
    w_hi, w_lo = _split_bf16(wr)
    logits = (jnp.dot(h_hi, w_hi, preferred_element_type=F32)
              + jnp.dot(h_lo, w_hi, preferred_element_type=F32)
              + jnp.dot(h_hi, w_lo, preferred_element_type=F32)) + br
    lane = lax.broadcasted_iota(I32, logits.shape, 1)
    neg = jnp.float32(-jnp.inf)
    lg = jnp.where(lane < N_EXPERTS, logits, neg)
    m1 = jnp.max(lg, axis=-1, keepdims=True)
    i1 = jnp.min(jnp.where(lg == m1, lane, LANES), axis=-1, keepdims=True)
    lg2 = jnp.where(lane == i1, neg, lg)
    m2 = jnp.max(lg2, axis=-1, keepdims=True)
    i2 = jnp.min(jnp.where(lg2 == m2, lane, LANES), axis=-1, keepdims=True)
    e = jnp.exp(m2 - m1)
    w1 = 1.0 / (1.0 + e)
    w2 = e / (1.0 + e)
    idx_ref[...] = jnp.where(lane == 0, i1, jnp.where(lane == 1, i2, 0))
    wt_ref[...] = jnp.where(lane == 0, w1, jnp.where(lane == 1, w2, 0.0))


W_PIECE = 512


def _out_proj_kernel(*refs, layer, lat_row0, route):
    if route:
        (o_ref, w_ref, xa_ref, xb_ref, gate_ref, g_ref, sc_ref, sh_ref, wr_ref, br_ref,
         xn_ref, h_ref, idx_ref, wt_ref, stage_ref, wbf_ref, xbuf_ref, wsem, xsem) = refs
    else:
        (o_ref, w_ref, xa_ref, xb_ref, gate_ref, g_ref, sc_ref, sh_ref,
         xn_ref, h_ref, stage_ref, wbf_ref, xbuf_ref, wsem, xsem) = refs
    i = pl.program_id(0)

    @pl.when(i == 0)
    def _():
        for p in range(D // W_PIECE):
            cols = pl.ds(p * W_PIECE, W_PIECE)
            cp = pltpu.make_async_copy(w_ref.at[layer, :, cols], stage_ref, wsem.at[0])
            cp.start()
            cp.wait()
            _cast_rows(stage_ref, wbf_ref.at[:, cols])

    slot = _next_stream_tile(xa_ref, xb_ref, lat_row0, xbuf_ref, xsem, i, pl.num_programs(0))
    acc = jnp.dot(o_ref[...], wbf_ref[...], preferred_element_type=F32)
    x_new = xbuf_ref[slot] + gate_ref[...] * acc
    xn_ref[...] = x_new
    h = _modulated_norm(x_new, g_ref[...], sc_ref[...], sh_ref[...])
    if route:
        h_ref[...] = h
        _route_top2(h, wr_ref[...], br_ref[...], idx_ref, wt_ref)
    else:
        h_ref[...] = h.astype(BF16)


def _out_proj(o, w_out, x, mod, norm2_g, layer, router=None):
    t = T_NORM
    row = lambda i: i
    xa, xb, lat_row0 = _stream_hbm(x)
    route = router is not None
    anyspace = pl.BlockSpec(memory_space=pl.ANY)
    rows = lambda w: pl.BlockSpec((t, w), lambda i: (i, 0))
    in_specs = [rows(D), anyspace, anyspace, anyspace,
                _mod_spec(layer, 2, t, row),
                pl.BlockSpec((None, 1, D), lambda i: (layer, 0, 0)),
                _mod_spec(layer, 4, t, row), _mod_spec(layer, 3, t, row)]
    args = [o, w_out, xa, xb, mod, norm2_g.reshape(DEPTH, 1, D), mod, mod]
    out_specs = [rows(D), rows(D)]
    out_shape = [jax.ShapeDtypeStruct((M, D), F32), jax.ShapeDtypeStruct((M, D), F32 if route else BF16)]
    if route:
        w_router, b_router = router
        args += [jnp.zeros((D, LANES), F32).at[:, :N_EXPERTS].set(w_router),
                 jnp.zeros((1, LANES), F32).at[0, :N_EXPERTS].set(b_router)]
        in_specs += [pl.BlockSpec((D, LANES), lambda i: (0, 0)), pl.BlockSpec((1, LANES), lambda i: (0, 0))]
        out_specs += [rows(LANES), rows(LANES)]
        out_shape += [jax.ShapeDtypeStruct((M, LANES), I32), jax.ShapeDtypeStruct((M, LANES), F32)]
    return pl.pallas_call(
        functools.partial(_out_proj_kernel, layer=layer, lat_row0=lat_row0, route=route),
        grid=(M // t,),
        in_specs=in_specs,
        out_specs=out_specs,
        out_shape=out_shape,
        scratch_shapes=[pltpu.VMEM((D, W_PIECE), F32), pltpu.VMEM((D, D), BF16),
                        pltpu.VMEM((2, t, D), F32),
                        pltpu.SemaphoreType.DMA((1,)), pltpu.SemaphoreType.DMA((2,))],
        compiler_params=_params(1),
        name="out_proj_router" if route else "out_proj",
    )(*args)


CAST_ROWS = 256


def _cast_rows(src_ref, dst_ref):
    def body(r, carry):
        rs = pl.ds(pl.multiple_of(r * CAST_ROWS, CAST_ROWS), CAST_ROWS)
        dst_ref[rs, :] = src_ref[rs, :].astype(BF16)
        return carry

    lax.fori_loop(0, src_ref.shape[0] // CAST_ROWS, body, 0)


def _head_rms(a):
    return lax.rsqrt(jnp.mean(a * a, axis=-1, keepdims=True) + EPS)


def _rope_partner(ag, perm):
    hi, lo = _split_bf16(ag)
    return (jnp.dot(hi, perm, preferred_element_type=F32)
            + jnp.dot(lo, perm, preferred_element_type=F32))


def _cast_kernel(w_ref, o_ref):
    o_ref[...] = w_ref[...].astype(BF16)


def _cast_in_weights(w_in):
    spec = pl.BlockSpec((None, D, TN_IN), lambda l, n: (l, 0, n))
    return pl.pallas_call(
        _cast_kernel,
        grid=(DEPTH, IN_WIDTH // TN_IN),
        in_specs=[spec],
        out_specs=spec,
        out_shape=jax.ShapeDtypeStruct(w_in.shape, BF16),
        compiler_params=_params(2),
        name="cast_w_in",
    )(w_in)


N_Q_TILES = ATTN_WIDTH // TN_IN
KV_TILE = N_Q_TILES
U_TILE0 = KV_TILE + 1
G_TILE0 = U_TILE0 + SGU_WIDTH // TN_IN
N_IN_TILES = IN_WIDTH // TN_IN


def _in_proj_kernel(xa_ref, xb_ref, n1_ref, sc0_ref, sh0_ref, sc1_ref, sh1_ref, w_ref,
                    qg_ref, kg_ref, sg_ref, pq_ref, pk_ref, cos_ref, sin_ref,
                    q_ref, kf_ref, kb_ref, vf_ref, vb_ref, u_ref, gh_ref,
                    xbuf_ref, h_ref, xsem, *, lat_row0):
    m = pl.program_id(0)
    n = pl.program_id(1)
    nm = pl.num_programs(0)
    cur = m % 2
    nxt = (m + 1) % 2
    has_next = m + 1 < nm
    fetch = functools.partial(_stream_tile_copy, xa_ref, xb_ref, lat_row0, xbuf_ref, xsem)

    def norm_rows(slot, rows, sc_ref, sh_ref):
        x = xbuf_ref[slot, rows, :]
        h_ref[slot, rows, :] = _modulated_norm(x, n1_ref[...], sc_ref[...], sh_ref[...]).astype(BF16)

    @pl.when((m == 0) & (n == 0))
    def _():
        fetch(0, 0, True)
        fetch(0, 0, False)
        norm_rows(0, slice(None), sc0_ref, sh0_ref)

    @pl.when((n == 0) & has_next)
    def _():
        fetch(m + 1, nxt, True)

    def matmul():
        return jnp.dot(h_ref[cur], w_ref[...], preferred_element_type=F32)

    for rope in (False, True):
        positioned = jnp.logical_not(_is_ctx_tile(m, TM)) if rope else _is_ctx_tile(m, TM)

        @pl.when((n < N_Q_TILES) & positioned)
        def _():
            acc = matmul()
            ag = acc * qg_ref[...]
            if rope:
                partner = _rope_partner(ag, pq_ref[...])
                cos, sin = cos_ref[...], sin_ref[...]
            for h in range(TN_IN // HEAD_DIM):
                sl = slice(h * HEAD_DIM, (h + 1) * HEAD_DIM)
                r = _head_rms(acc[:, sl]) * (ATTN_SCALE * LOG2_E)
                qh = ag[:, sl] * cos + partner[:, sl] * sin if rope else ag[:, sl]
                q_ref[:, sl] = (qh * r).astype(BF16)

        @pl.when((n == KV_TILE) & positioned)
        def _():
            acc = matmul()
            k = acc[:, :KV_WIDTH]
            ag = k * kg_ref[...]
            if rope:
                partner = _rope_partner(ag, pk_ref[...])
                cos, sin = cos_ref[...], sin_ref[...]
            for h in range(N_KV_HEADS):
                sl = slice(h * HEAD_DIM, (h + 1) * HEAD_DIM)
                r = _head_rms(k[:, sl])
                kf = ag[:, sl] * r
                if not rope:
                    kf_ref[:, sl] = kf
                kb_ref[:, sl] = ((ag[:, sl] * cos + partner[:, sl] * sin) * r if rope else kf).astype(BF16)
            v = acc[:, KV_WIDTH:]
            if not rope:
                vf_ref[...] = v
            vb_ref[...] = v.astype(BF16)

    half = TM // (G_TILE0 - U_TILE0)
    for k in range(G_TILE0 - U_TILE0):
        for parity in range(2):
            @pl.when((n == U_TILE0 + k) & has_next & (cur == parity))
            def _():
                if k == 0:
                    fetch(m + 1, 1 - parity, False)
                u_ref[...] = jnp.dot(h_ref[parity], w_ref[...], preferred_element_type=F32).astype(BF16)
                norm_rows(1 - parity, slice(k * half, (k + 1) * half), sc1_ref, sh1_ref)

        @pl.when((n == U_TILE0 + k) & jnp.logical_not(has_next))
        def _():
            u_ref[...] = matmul().astype(BF16)

    @pl.when(n >= G_TILE0)
    def _():
        acc = matmul()
        for h in range(TN_IN // HEAD_DIM):
            sl = slice(h * HEAD_DIM, (h + 1) * HEAD_DIM)
            a = acc[:, sl]
            gh_ref[:, sl] = (a * _head_rms(a) * sg_ref[:, sl]).astype(BF16)


def _rope_tables():
    n_rows = DEC_SEQ // GRID_W
    rows = jnp.broadcast_to(jnp.arange(n_rows)[:, None], (n_rows, GRID_W)).reshape(-1)
    cols = jnp.broadcast_to(jnp.arange(GRID_W)[None, :], (n_rows, GRID_W)).reshape(-1)
    inv = ROPE_THETA ** (-jnp.arange(0, ROPE_AXIS_DIM, 2, dtype=F32) / ROPE_AXIS_DIM)
    ang_r = rows.astype(F32)[:, None] * inv
    ang_c = cols.astype(F32)[:, None] * inv
    cos = jnp.concatenate([jnp.cos(ang_r), jnp.cos(ang_r), jnp.cos(ang_c), jnp.cos(ang_c)], axis=1)
    sin = jnp.concatenate([-jnp.sin(ang_r), jnp.sin(ang_r), -jnp.sin(ang_c), jnp.sin(ang_c)], axis=1)
    return cos, sin


def _partner_matrix(n_heads):
    w = n_heads * HEAD_DIM
    quarter = ROPE_AXIS_DIM // 2
    j = jnp.arange(w)
    partner = jnp.where((j % ROPE_AXIS_DIM) < quarter, j + quarter, j - quarter)
    return (jnp.arange(w)[:, None] == partner[None, :]).astype(BF16)


def _rope_block(m):
    return jnp.maximum(m - MP // TM, 0) % (DEC_SEQ // TM)


def _in_projections(x, w_in_bf, mod, norm1_g, q_norm_g, k_norm_g, sgu_norm_g, cos, sin, layer):
    tn = TN_IN
    xa, xb, lat_row0 = _stream_hbm(x)
    row = lambda m, n: m
    next_row = lambda m, n: jnp.minimum(m + 1, M // TM - 1)
    anyspace = pl.BlockSpec(memory_space=pl.ANY)
    const = lambda shape: pl.BlockSpec(shape, lambda m, n: (0,) * len(shape))
    rope_spec = pl.BlockSpec((TM, HEAD_DIM), lambda m, n: (_rope_block(m), 0))
    q_heads = tn // HEAD_DIM
    q_gain = jnp.tile(q_norm_g[layer], q_heads)[None, :]
    k_gain = jnp.tile(k_norm_g[layer], N_KV_HEADS)[None, :]
    g_tile = lambda n: jnp.clip(n - G_TILE0, 0, SGU_WIDTH // tn - 1)
    kv_out = pl.BlockSpec((TM, KV_WIDTH), lambda m, n: (m, 0))
    kv_ctx = pl.BlockSpec((TM, KV_WIDTH), lambda m, n: (jnp.minimum(m, MP // TM - 1), 0))
    kv_shape = lambda rows, dt: jax.ShapeDtypeStruct((rows, KV_WIDTH), dt)
    return pl.pallas_call(
        functools.partial(_in_proj_kernel, lat_row0=lat_row0),
        grid=(M // TM, N_IN_TILES),
        in_specs=[anyspace, anyspace,
                  pl.BlockSpec((None, 1, D), lambda m, n: (layer, 0, 0)),
                  _mod_spec(layer, 1, TM, row), _mod_spec(layer, 0, TM, row),
                  _mod_spec(layer, 1, TM, next_row), _mod_spec(layer, 0, TM, next_row),
                  pl.BlockSpec((None, D, tn), lambda m, n: (layer, 0, n)),
                  const((1, tn)), const((1, KV_WIDTH)),
                  pl.BlockSpec((None, 1, tn), lambda m, n: (layer, 0, g_tile(n))),
                  const((tn, tn)), const((KV_WIDTH, KV_WIDTH)), rope_spec, rope_spec],
        out_specs=[pl.BlockSpec((TM, tn), lambda m, n: (m, jnp.minimum(n, N_Q_TILES - 1))),
                   kv_ctx, kv_out, kv_ctx, kv_out,
                   pl.BlockSpec((TM, tn), lambda m, n: (m, jnp.clip(n - U_TILE0, 0, SGU_WIDTH // tn - 1))),
                   pl.BlockSpec((TM, tn), lambda m, n: (m, g_tile(n)))],
        out_shape=[jax.ShapeDtypeStruct((M, ATTN_WIDTH), BF16),
                   kv_shape(MP, F32), kv_shape(M, BF16), kv_shape(MP, F32), kv_shape(M, BF16),
                   jax.ShapeDtypeStruct((M, SGU_WIDTH), BF16),
                   jax.ShapeDtypeStruct((M, SGU_WIDTH), BF16)],
        scratch_shapes=[pltpu.VMEM((2, TM, D), F32), pltpu.VMEM((2, TM, D), BF16),
                        pltpu.SemaphoreType.DMA((2,))],
        compiler_params=_params(2),
        name="in_proj",
    )(xa, xb, norm1_g.reshape(DEPTH, 1, D), mod, mod, mod, mod, w_in_bf, q_gain, k_gain,
      sgu_norm_g.reshape(DEPTH, 1, SGU_WIDTH), _partner_matrix(q_heads), _partner_matrix(N_KV_HEADS),
      cos, sin)


def _qk(q, k):
    return lax.dot_general(q, k, (((1,), (1,)), ((), ())), preferred_element_type=F32)


def _attn_kernel(*refs, has_cache, n_batch, seq, tq):
    def with_ones(v):
        return jnp.concatenate([v, jnp.ones_like(v)], axis=1)

    if has_cache:
        q_ref, k_ref, v_ref, kc_ref, vc_ref, o_ref = refs
    else:
        q_ref, k_ref, v_ref, o_ref = refs
    for b in range(n_batch):
        rows_q = slice(b * tq, (b + 1) * tq)
        rows_k = slice(b * seq, (b + 1) * seq)
        for kv in range(N_KV_HEADS):
            kv_cols = slice(kv * HEAD_DIM, (kv + 1) * HEAD_DIM)
            k = k_ref[rows_k, kv_cols]
            v = with_ones(v_ref[rows_k, kv_cols])
            if has_cache:
                kc = kc_ref[:, kv_cols].astype(BF16)
                vc = with_ones(vc_ref[:, kv_cols].astype(BF16))
            for g in range(Q_PER_KV):
                head = kv * Q_PER_KV + g
                sl = slice(head * HEAD_DIM, (head + 1) * HEAD_DIM)
                q = q_ref[rows_q, sl]
                s = _qk(q, k)
                m = jnp.max(s, axis=-1, keepdims=True)
                if has_cache:
                    sc = _qk(q, kc)
                    m = jnp.maximum(m, jnp.max(sc, axis=-1, keepdims=True))
                o = jnp.dot(jnp.exp2(s - m).astype(BF16), v, preferred_element_type=F32)
                if has_cache:
                    o = o + jnp.dot(jnp.exp2(sc - m).astype(BF16), vc, preferred_element_type=F32)
                o_ref[rows_q, sl] = (o[:, :HEAD_DIM] / o[:, HEAD_DIM:]).astype(BF16)


def _attention(q, kb, vb, cache_k, cache_v, layer, *, batch, seq, row0):
    has_cache = cache_k is not None
    tq = min(T_Q, seq)
    nq = seq // tq
    n_batch = max(1, ATTN_ROWS // seq) if nq == 1 else 1
    q_spec = pl.BlockSpec((n_batch * tq, ATTN_WIDTH), lambda b, i: (row0 // (n_batch * tq) + b * nq + i, 0))
    kv_spec = pl.BlockSpec((n_batch * seq, KV_WIDTH), lambda b, i: (row0 // (n_batch * seq) + b, 0))
    in_specs = [q_spec, kv_spec, kv_spec]
    args = [q, kb, vb]
    if has_cache:
        c_spec = pl.BlockSpec((None, None, PAST_LEN, KV_WIDTH), lambda b, i: (b, layer, 0, 0))
        in_specs += [c_spec, c_spec]
        args += [cache_k.reshape(DEC_BATCH, DEPTH, PAST_LEN, KV_WIDTH),
                 cache_v.reshape(DEC_BATCH, DEPTH, PAST_LEN, KV_WIDTH)]
    return pl.pallas_call(
        functools.partial(_attn_kernel, has_cache=has_cache, n_batch=n_batch, seq=seq, tq=tq),
        grid=(batch // n_batch, nq),
        in_specs=in_specs,
        out_specs=pl.BlockSpec((n_batch * tq, ATTN_WIDTH), lambda b, i: (b * nq + i, 0)),
        out_shape=jax.ShapeDtypeStruct((batch * seq, ATTN_WIDTH), BF16),
        compiler_params=_params(2),
        name="attention_cached" if has_cache else "attention",
    )(*args)


def _sgu_merge_kernel(u_ref, gh_ref, ap_ref, as_ref, ws_ref, bs_ref, gn_ref, o_ref, sgu_ref):
    t = u_ref.shape[0]
    a = _pick(ap_ref, as_ref, 0).astype(F32)
    a = a * lax.rsqrt(jnp.mean(a * a, axis=-1, keepdims=True) + EPS) * gn_ref[:, :ATTN_WIDTH]
    o_ref[:, :ATTN_WIDTH] = a.astype(BF16)
    for h in range(N_SGU_HEADS):
        cs = slice(h * HEAD_DIM, (h + 1) * HEAD_DIM)
        w = ws_ref[h].astype(BF16)
        b = bs_ref[h]
        for c in range(t // CHUNK):
            rs = slice(c * CHUNK, (c + 1) * CHUNK)
            mixed = jnp.dot(w, gh_ref[rs, cs], preferred_element_type=F32) + b
            sgu_ref[rs, cs] = u_ref[rs, cs].astype(F32) * mixed
    s = sgu_ref[...]
    s = s * lax.rsqrt(jnp.mean(s * s, axis=-1, keepdims=True) + EPS) * gn_ref[:, ATTN_WIDTH:]
    o_ref[:, ATTN_WIDTH:] = s.astype(BF16)


def _sgu_merge(u, gh, attn_ctx, attn_lat, w_spatial, b_spatial, out_norm_g, layer):
    t = T_SGU
    bias = jnp.broadcast_to(b_spatial[:, :, :, None], (DEPTH, N_SGU_HEADS, CHUNK, HEAD_DIM))
    row = lambda w: pl.BlockSpec((t, w), lambda i: (i, 0))
    a_args, a_specs = _stream_in((attn_ctx, attn_lat), t, ATTN_WIDTH, lambda i: i, lambda i: 0)
    return pl.pallas_call(
        _sgu_merge_kernel,
        grid=(M // t,),
        in_specs=[row(SGU_WIDTH), row(SGU_WIDTH)] + a_specs + [
            pl.BlockSpec((None, N_SGU_HEADS, CHUNK, CHUNK), lambda i: (layer, 0, 0, 0)),
            pl.BlockSpec((None, N_SGU_HEADS, CHUNK, HEAD_DIM), lambda i: (layer, 0, 0, 0)),
            pl.BlockSpec((None, 1, D), lambda i: (layer, 0, 0))],
        out_specs=row(D),
        out_shape=jax.ShapeDtypeStruct((M, D), BF16),
        scratch_shapes=[pltpu.VMEM((t, SGU_WIDTH), F32)],
        compiler_params=_params(1),
        name="sgu_merge",
    )(u, gh, *a_args, w_spatial, bias, out_norm_g.reshape(DEPTH, 1, D))


def _mm_resid_kernel(a_ref, w_ref, x_ref, g_ref, o_ref, stage_ref, wbf_ref, sem, *, w_index):
    n = pl.program_id(0)
    tn = stage_ref.shape[1]

    def weight_copy(col_block):
        cols = pl.ds(pl.multiple_of(col_block * tn, LANES), tn)
        return pltpu.make_async_copy(w_ref.at[w_index, :, cols], stage_ref, sem.at[0])

    @pl.when(pl.program_id(1) == 0)
    def _():
        @pl.when(n == 0)
        def _():
            weight_copy(0).start()

        weight_copy(0).wait()
        _cast_rows(stage_ref, wbf_ref)

        @pl.when(n + 1 < pl.num_programs(0))
        def _():
            weight_copy(n + 1).start()

    acc = jnp.dot(a_ref[...], wbf_ref[...], preferred_element_type=F32)
    o_ref[...] = x_ref[...] + g_ref[...] * acc


def _mm_resid(a, w, x, mod, layer, w_index, gate_chunk, tm, tn):
    k = a.shape[1]
    row = lambda n, m: m
    col = lambda n, m: n
    return pl.pallas_call(
        functools.partial(_mm_resid_kernel, w_index=w_index),
        grid=(D // tn, M // tm),
        in_specs=[pl.BlockSpec((tm, k), lambda n, m: (m, 0)),
                  pl.BlockSpec(memory_space=pl.ANY),
                  pl.BlockSpec((tm, tn), lambda n, m: (m, n)),
                  _mod_spec(layer, gate_chunk, tm, row, col, tn=tn)],
        out_specs=pl.BlockSpec((tm, tn), lambda n, m: (m, n)),
        out_shape=jax.ShapeDtypeStruct((M, D), F32),
        scratch_shapes=[pltpu.VMEM((k, tn), F32), pltpu.VMEM((k, tn), BF16),
                        pltpu.SemaphoreType.DMA((1,))],
        compiler_params=_params(2),
        name="mm_resid",
    )(a, w, x, mod)


def _swiglu(a, b):
    return a * jax.nn.sigmoid(a) * b


def _ffn_gu_kernel(x_ref, wg_ref, wu_ref, o_ref, wgb_ref, wub_ref):
    @pl.when(pl.program_id(1) == 0)
    def _():
        _cast_rows(wg_ref, wgb_ref)
        _cast_rows(wu_ref, wub_ref)
    for r in range(x_ref.shape[0] // TM):
        rs = slice(r * TM, (r + 1) * TM)
        x = x_ref[rs, :]
        a = jnp.dot(x, wgb_ref[...], preferred_element_type=F32)
        b = jnp.dot(x, wub_ref[...], preferred_element_type=F32)
        o_ref[rs, :] = _swiglu(a, b).astype(BF16)


def _ffn_gate_up(h, w_gate, w_up, j):
    tm, tn = 2 * TM, 512
    w_spec = pl.BlockSpec((None, D, tn), lambda n, m: (j, 0, n))
    return pl.pallas_call(
        _ffn_gu_kernel,
        grid=(D_FF // tn, M // tm),
        in_specs=[pl.BlockSpec((tm, D), lambda n, m: (m, 0)), w_spec, w_spec],
        out_specs=pl.BlockSpec((tm, tn), lambda n, m: (m, n)),
        out_shape=jax.ShapeDtypeStruct((M, D_FF), BF16),
        scratch_shapes=[pltpu.VMEM((D, tn), BF16), pltpu.VMEM((D, tn), BF16)],
        compiler_params=_params(2),
        name="ffn_gate_up",
    )(h, w_gate, w_up)


def _route_meta(idx):
    t = T_MOE
    experts = jnp.arange(N_EXPERTS, dtype=I32)
    sel = idx.T[:, None, :] == experts[None, :, None]
    onehot = (sel[0] | sel[1]).astype(I32)
    csum = jnp.cumsum(onehot, axis=1)
    rank = csum - onehot
    count = csum[:, -1]
    ntile = (count + t - 1) // t
    tile_end = jnp.cumsum(ntile)
    tile_start = tile_end - ntile
    nused = tile_end[-1]
    row = (tile_start * t)[:, None] + rank
    pos = jnp.sum(jnp.where(sel, row[None], 0), axis=1)
    j = jnp.arange(NT_MOE, dtype=I32)
    te_raw = jnp.minimum(jnp.sum(j[:, None] >= tile_end[None, :], axis=1), N_EXPERTS - 1).astype(I32)
    te = jnp.where(j < nused, te_raw, te_raw[nused - 1])
    first = ((j == tile_start[te]) & (j < nused)).astype(I32)
    later = (ntile[None, :] > 0) & (experts[None, :] > te[:, None])
    nxt = jnp.min(jnp.where(later, experts[None, :], N_EXPERTS), axis=1)
    nxt = jnp.where(nxt == N_EXPERTS, -1, nxt).astype(I32)
    pad_start = tile_start * t + count
    pad_len = ntile * t - count
    tail = jnp.stack([nused * t, (NT_MOE - nused) * (t // ZERO_ROWS)])
    zinfo = jnp.concatenate([pad_start, pad_len, tail]).astype(I32)
    return pos.astype(I32), zinfo, (te, first, nxt, nused.reshape(1).astype(I32))


def _dispatch_kernel(p0_ref, p1_ref, z_ref, h_ref, xs_ref, zero_ref, sem, zsem):
    t = h_ref.shape[0]
    i = pl.program_id(0)
    base = i * t

    def clear_padding(start):
        def go(n, off):
            cp = pltpu.make_async_copy(zero_ref.at[pl.ds(0, n), :], xs_ref.at[pl.ds(off, n), :], zsem.at[0])
            cp.start() if start else cp.wait()

        for e in range(N_EXPERTS):
            off, ln = z_ref[e], z_ref[N_EXPERTS + e]
            end = off + ln
            for b in range(SUBLANE_BITS, PAD_BITS):
                @pl.when(((ln >> b) & 1) == 1)
                def _():
                    go(1 << b, pl.multiple_of(end - ((ln >> b) << b), SUBLANES))
            for k in range(SUBLANES - 1):
                @pl.when(k < (ln & (SUBLANES - 1)))
                def _():
                    go(1, off + k)
        tail0, n_tail = z_ref[2 * N_EXPERTS], z_ref[2 * N_EXPERTS + 1]

        def tail_body(k, carry):
            go(ZERO_ROWS, pl.multiple_of(tail0 + k * ZERO_ROWS, SUBLANES))
            return carry

        lax.fori_loop(0, n_tail, tail_body, 0)

    @pl.when(i == 0)
    def _():
        zero_ref[...] = jnp.zeros(zero_ref.shape, zero_ref.dtype)
        clear_padding(True)

    def issue(r, carry):
        src = h_ref.at[pl.ds(r, 1), :]
        pltpu.make_async_copy(src, xs_ref.at[pl.ds(p0_ref[base + r], 1), :], sem.at[0]).start()
        pltpu.make_async_copy(src, xs_ref.at[pl.ds(p1_ref[base + r], 1), :], sem.at[1]).start()
        return carry

    lax.fori_loop(0, t, issue, 0, unroll=8)
    pltpu.make_async_copy(h_ref, xs_ref.at[pl.ds(0, t), :], sem.at[0]).wait()
    pltpu.make_async_copy(h_ref, xs_ref.at[pl.ds(0, t), :], sem.at[1]).wait()

    @pl.when(i == 0)
    def _():
        clear_padding(False)


def _dispatch(h, pos0, pos1, zinfo):
    t = T_DISPATCH
    return pl.pallas_call(
        _dispatch_kernel,
        grid_spec=pltpu.PrefetchScalarGridSpec(
            num_scalar_prefetch=3,
            grid=(M // t,),
            in_specs=[pl.BlockSpec((t, D), lambda i, p0, p1, z: (i, 0))],
            out_specs=pl.BlockSpec(memory_space=pl.ANY),
            scratch_shapes=[pltpu.VMEM((ZERO_ROWS, D), F32),
                            pltpu.SemaphoreType.DMA((2,)), pltpu.SemaphoreType.DMA((1,))]),
        out_shape=jax.ShapeDtypeStruct((P_MOE, D), F32),
        compiler_params=_params(1),
        name="moe_dispatch",
    )(pos0, pos1, zinfo, h)


def _expert_weight_stream(w_refs, stage_refs, bf_refs, sem, te_ref, first_ref, nxt_ref, tn):
    c = pl.program_id(0)
    j = pl.program_id(1)
    nc = pl.num_programs(0)

    def copies(e, cc):
        col = pl.multiple_of(cc * tn, LANES)
        return [pltpu.make_async_copy(w.at[e, :, pl.ds(col, tn)], st, sem.at[k])
                for k, (w, st) in enumerate(zip(w_refs, stage_refs))]

    def start(e, cc):
        for cp in copies(e, cc):
            cp.start()

    @pl.when((c == 0) & (j == 0))
    def _():
        start(te_ref[0], 0)

    @pl.when(first_ref[j] == 1)
    def _():
        for cp in copies(0, 0):
            cp.wait()
        for st, bf in zip(stage_refs, bf_refs):
            _cast_rows(st, bf)
        ne = nxt_ref[j]

        @pl.when(ne >= 0)
        def _():
            start(ne, c)

        @pl.when((ne < 0) & (c + 1 < nc))
        def _():
            start(te_ref[0], c + 1)


def _gmm_gate_up_kernel(te_ref, first_ref, nxt_ref, nused_ref, xs_ref, wg_ref, wu_ref, o_ref,
                        sg_ref, su_ref, wgb_ref, wub_ref, sem):
    _expert_weight_stream((wg_ref, wu_ref), (sg_ref, su_ref), (wgb_ref, wub_ref), sem,
                          te_ref, first_ref, nxt_ref, TN_GU)

    @pl.when(pl.program_id(1) < nused_ref[0])
    def _():
        half = T_MOE // 2
        for r in range(2):
            rs = slice(r * half, (r + 1) * half)
            x = xs_ref[rs, :].astype(BF16)
            a = jnp.dot(x, wgb_ref[...], preferred_element_type=F32)
            b = jnp.dot(x, wub_ref[...], preferred_element_type=F32)
            o_ref[rs, :] = _swiglu(a, b).astype(BF16)

    @pl.when(pl.program_id(1) >= nused_ref[0])
    def _():
        o_ref[...] = jnp.zeros(o_ref.shape, o_ref.dtype)


def _gmm_down_kernel(te_ref, first_ref, nxt_ref, nused_ref, a_ref, wd_ref, o_ref,
                     sd_ref, wdb_ref, sem):
    _expert_weight_stream((wd_ref,), (sd_ref,), (wdb_ref,), sem, te_ref, first_ref, nxt_ref, TN_DN)

    @pl.when(pl.program_id(1) < nused_ref[0])
    def _():
        o_ref[...] = jnp.dot(a_ref[...], wdb_ref[...], preferred_element_type=F32)

    @pl.when(pl.program_id(1) >= nused_ref[0])
    def _():
        o_ref[...] = jnp.zeros(o_ref.shape, o_ref.dtype)


def _used_tile(j, nused):
    return jnp.minimum(j, nused[0] - 1)


def _gmm_gate_up(xs, w_gate, w_up, meta):
    te, first, nxt, nused = meta
    tn = TN_GU
    return pl.pallas_call(
        _gmm_gate_up_kernel,
        grid_spec=pltpu.PrefetchScalarGridSpec(
            num_scalar_prefetch=4,
            grid=(D_FF_EXPERT // tn, NT_MOE),
            in_specs=[pl.BlockSpec((T_MOE, D), lambda c, j, te, fi, nx, nu: (_used_tile(j, nu), 0)),
                      pl.BlockSpec(memory_space=pl.ANY),
                      pl.BlockSpec(memory_space=pl.ANY)],
            out_specs=pl.BlockSpec((T_MOE, tn), lambda c, j, te, fi, nx, nu: (j, c)),
            scratch_shapes=[pltpu.VMEM((D, tn), F32), pltpu.VMEM((D, tn), F32),
                            pltpu.VMEM((D, tn), BF16), pltpu.VMEM((D, tn), BF16),
                            pltpu.SemaphoreType.DMA((2,))]),
        out_shape=jax.ShapeDtypeStruct((P_MOE, D_FF_EXPERT), BF16),
        compiler_params=_params(2),
        name="moe_gate_up",
    )(te, first, nxt, nused, xs, w_gate, w_up)


def _gmm_down(act, w_down, meta):
    te, first, nxt, nused = meta
    tn = TN_DN
    return pl.pallas_call(
        _gmm_down_kernel,
        grid_spec=pltpu.PrefetchScalarGridSpec(
            num_scalar_prefetch=4,
            grid=(D // tn, NT_MOE),
            in_specs=[pl.BlockSpec((T_MOE, D_FF_EXPERT), lambda c, j, te, fi, nx, nu: (_used_tile(j, nu), 0)),
                      pl.BlockSpec(memory_space=pl.ANY)],
            out_specs=pl.BlockSpec((T_MOE, tn), lambda c, j, te, fi, nx, nu: (j, c)),
            scratch_shapes=[pltpu.VMEM((D_FF_EXPERT, tn), F32), pltpu.VMEM((D_FF_EXPERT, tn), BF16),
                            pltpu.SemaphoreType.DMA((1,))]),
        out_shape=jax.ShapeDtypeStruct((P_MOE, D), F32),
        compiler_params=_params(2),
        name="moe_down",
    )(te, first, nxt, nused, act, w_down)


def _combine_kernel(p0_ref, p1_ref, ys_ref, xa_ref, xb_ref, g_ref, w_ref, oc_ref, ol_ref,
                    a_ref, b_ref, sem):
    t = xa_ref.shape[0]
    i = pl.program_id(0)
    n = pl.num_programs(0)

    def issue(step, slot):
        base = step * t

        def body(r, carry):
            pltpu.make_async_copy(ys_ref.at[pl.ds(p0_ref[base + r], 1), :],
                                  a_ref.at[slot, pl.ds(r, 1), :], sem.at[0, slot]).start()
            pltpu.make_async_copy(ys_ref.at[pl.ds(p1_ref[base + r], 1), :],
                                  b_ref.at[slot, pl.ds(r, 1), :], sem.at[1, slot]).start()
            return carry

        lax.fori_loop(0, t, body, 0, unroll=8)

    @pl.when(i == 0)
    def _():
        issue(0, 0)

    @pl.when(i + 1 < n)
    def _():
        issue(i + 1, (i + 1) % 2)

    slot = i % 2
    pltpu.make_async_copy(ys_ref.at[pl.ds(0, t), :], a_ref.at[slot], sem.at[0, slot]).wait()
    pltpu.make_async_copy(ys_ref.at[pl.ds(0, t), :], b_ref.at[slot], sem.at[1, slot]).wait()
    w = w_ref[...]
    moe = w[:, 0:1] * a_ref[slot] + w[:, 1:2] * b_ref[slot]
    y = _pick(xa_ref, xb_ref, 0) + g_ref[...] * moe
    is_ctx = _is_ctx_tile(i, t)

    @pl.when(is_ctx)
    def _():
        oc_ref[...] = y

    @pl.when(jnp.logical_not(is_ctx))
    def _():
        ol_ref[...] = y


def _combine(ys, x, mod, layer, gate_chunk, wts, pos0, pos1):
    t = T_COMBINE
    n_ctx = MP // t
    row = lambda i, *_: i
    x_args, x_specs = _stream_in(x, t, D, row, lambda i, *_: 0)
    return pl.pallas_call(
        _combine_kernel,
        grid_spec=pltpu.PrefetchScalarGridSpec(
            num_scalar_prefetch=2,
            grid=(M // t,),
            in_specs=[pl.BlockSpec(memory_space=pl.ANY)] + x_specs + [
                _mod_spec(layer, gate_chunk, t, row),
                pl.BlockSpec((t, LANES), lambda i, p0, p1: (i, 0))],
            out_specs=[pl.BlockSpec((t, D), lambda i, p0, p1: (jnp.minimum(i, n_ctx - 1), 0)),
                       pl.BlockSpec((t, D), lambda i, p0, p1: (jnp.maximum(i - n_ctx, 0), 0))],
            scratch_shapes=[pltpu.VMEM((2, t, D), F32), pltpu.VMEM((2, t, D), F32),
                            pltpu.SemaphoreType.DMA((2, 2))]),
        out_shape=[jax.ShapeDtypeStruct((MP, D), F32), jax.ShapeDtypeStruct((MS, D), F32)],
        compiler_params=_params(1),
        name="moe_combine",
    )(pos0, pos1, ys, *x_args, mod, wts)


def _moe(x, h, idx, wts, mod, layer, w_gate, w_up, w_down):
    pos, zinfo, meta = _route_meta(idx[:, :TOP_K])
    pos0, pos1 = pos[0], pos[1]
    xs = _dispatch(h, pos0, pos1, zinfo)
    act = _gmm_gate_up(xs, w_gate, w_up, meta)
    ys = _gmm_down(act, w_down, meta)
    return _combine(ys, x, mod, layer, 5, wts, pos0, pos1)


def kernel(x_prompt, x_sample, cache_k, cache_v, c, c_ctx, w_ada, b_ada, norm1_g, norm2_g, w_in, q_norm_g, k_norm_g, sgu_norm_g, w_spatial, b_spatial, out_norm_g, w_out, ffn_w_gate, ffn_w_up, ffn_w_down, w_router, b_router, moe_w_gate, moe_w_up, moe_w_down):
    assert DEPTH == 2
    x = (x_prompt.reshape(MP, D), x_sample.reshape(MS, D))
    cond = jnp.concatenate([c_ctx[None, :], c, jnp.zeros((N_COND - 1 - DEC_BATCH, D), F32)], axis=0)
    mod = _modulation(cond, w_ada, b_ada).reshape(DEPTH, N_COND, 1, N_MOD * D)
    cos, sin = _rope_tables()
    w_in_bf = _cast_in_weights(w_in)

    new_k, new_v = [], []
    for i in range(DEPTH):
        q, kf, kb, vf, vb, u, gh = _in_projections(x, w_in_bf, mod, norm1_g, q_norm_g, k_norm_g,
                                                   sgu_norm_g, cos, sin, i)
        attn_ctx = _attention(q, kb, vb, None, None, i, batch=BATCH, seq=SEQ, row0=0)
        attn_lat = _attention(q, kb, vb, cache_k, cache_v, i, batch=DEC_BATCH, seq=DEC_SEQ, row0=MP)
        o = _sgu_merge(u, gh, attn_ctx, attn_lat, w_spatial, b_spatial, out_norm_g, i)
        j = i // 2
        if i % 2 == 0:
            x, h2 = _out_proj(o, w_out, x, mod, norm2_g, i)
            act = _ffn_gate_up(h2, ffn_w_gate, ffn_w_up, j)
            x = _mm_resid(act, ffn_w_down, x, mod, i, j, 5, TM, 512)
        else:
            x, h2, idx, wts = _out_proj(o, w_out, x, mod, norm2_g, i, router=(w_router[j], b_router[j]))
            x = _moe(x, h2, idx, wts, mod, i, moe_w_gate[j], moe_w_up[j], moe_w_down[j])
        new_k.append(kf.reshape(BATCH, SEQ, N_KV_HEADS, HEAD_DIM))
        new_v.append(vf.reshape(BATCH, SEQ, N_KV_HEADS, HEAD_DIM))

    y_prompt = x[0].reshape(BATCH, SEQ, D)
    y_sample = x[1].reshape(DEC_BATCH, DEC_SEQ, D)
    return (y_prompt, y_sample, jnp.stack(new_k, axis=1), jnp.stack(new_v, axis=1))
```

```python
import functools

import jax
import jax.numpy as jnp
from jax import lax
from jax.experimental import pallas as pl
from jax.experimental.pallas import tpu as pltpu

F32 = jnp.float32
BF16 = jnp.bfloat16
I32 = jnp.int32

D = 2048
BATCH, SEQ = 16, 256
DEC_BATCH, DEC_SEQ = 4, 2048
PAST_LEN = 256
DEPTH = 2
GRID_W = 64
CHUNK = 128
HEAD_DIM = 128
N_Q_HEADS, N_KV_HEADS = 8, 2
Q_PER_KV = N_Q_HEADS // N_KV_HEADS
ATTN_WIDTH = N_Q_HEADS * HEAD_DIM
KV_WIDTH = N_KV_HEADS * HEAD_DIM
N_SGU_HEADS = 8
SGU_WIDTH = N_SGU_HEADS * HEAD_DIM
IN_WIDTH = ATTN_WIDTH + 2 * KV_WIDTH + 2 * SGU_WIDTH
ROPE_THETA = 10000.0
ROPE_AXIS_DIM = HEAD_DIM // 2
D_FF = 5632
N_EXPERTS = 8
TOP_K = 2
D_FF_EXPERT = 2816
N_MOD = 6
EPS = 1e-6
ATTN_SCALE = HEAD_DIM ** -0.5
LOG2_E = 1.4426950408889634

MP = BATCH * SEQ
MS = DEC_BATCH * DEC_SEQ
M = MP + MS
N_COND = 8
LANES = 128
SUBLANES = 8
SUBLANE_BITS = 3

VMEM_LIMIT = 56 * 1024 * 1024

TM = 1024
TN_IN = 2 * KV_WIDTH
T_NORM = 512
T_Q = 512
ATTN_ROWS = 1024
T_MOE = 512
P_MOE = M * TOP_K + N_EXPERTS * T_MOE
NT_MOE = P_MOE // T_MOE
TN_GU = D_FF_EXPERT // 2
TN_DN = D
T_DISPATCH = 1024
T_COMBINE = 256
T_SGU = 1024
ZERO_ROWS = T_MOE // 2
PAD_BITS = ZERO_ROWS.bit_length()


def _params(n_axes):
    return pltpu.CompilerParams(dimension_semantics=("arbitrary",) * n_axes,
                                vmem_limit_bytes=VMEM_LIMIT)


def _cond_row(i, t):
    return jnp.where(i < MP // t, 0, 1 + (i - MP // t) // (DEC_SEQ // t))


def _is_ctx_tile(i, t):
    return i < MP // t


def _stream_in(x, t, width, row_of, col_of):
    n_ctx = MP // t
    pair = isinstance(x, tuple)
    base = 0 if pair else n_ctx
    ctx = pl.BlockSpec((t, width), lambda *g: (jnp.minimum(row_of(*g), n_ctx - 1), col_of(*g)))
    lat = pl.BlockSpec((t, width), lambda *g: (base + jnp.maximum(row_of(*g) - n_ctx, 0), col_of(*g)))
    return (list(x) if pair else [x, x]), [ctx, lat]


def _mod_spec(layer, chunk, t, row_of, col_of=None, tn=D):
    per = D // tn

    def index_map(*g):
        col = chunk * per + (col_of(*g) if col_of is not None else 0)
        return (layer, _cond_row(row_of(*g), t), 0, col)

    return pl.BlockSpec((None, None, 1, tn), index_map)


def _ada_kernel(c_ref, w_ref, b_ref, o_ref):
    c = c_ref[...]
    s = (c * jax.nn.sigmoid(c)).astype(BF16)
    o_ref[...] = jnp.dot(s, w_ref[...].astype(BF16), preferred_element_type=F32) + b_ref[...]


def _modulation(cond, w_ada, b_ada):
    tn = 1024
    width = N_MOD * D
    return pl.pallas_call(
        _ada_kernel,
        grid=(DEPTH, width // tn),
        in_specs=[pl.BlockSpec((N_COND, D), lambda l, n: (0, 0)),
                  pl.BlockSpec((None, D, tn), lambda l, n: (l, 0, n)),
                  pl.BlockSpec((None, 1, tn), lambda l, n: (l, 0, n))],
        out_specs=pl.BlockSpec((None, N_COND, tn), lambda l, n: (l, 0, n)),
        out_shape=jax.ShapeDtypeStruct((DEPTH, N_COND, width), F32),
        compiler_params=_params(2),
        name="modulation",
    )(cond, w_ada, b_ada.reshape(DEPTH, 1, width))


def _modulated_norm(x, g, sc, sh):
    y = x * lax.rsqrt(jnp.mean(x * x, axis=-1, keepdims=True) + EPS)
    return y * (g * (1.0 + sc)) + sh


def _pick(xa_ref, xb_ref, axis):
    t = xa_ref.shape[0]
    return jnp.where(_is_ctx_tile(pl.program_id(axis), t), xa_ref[...], xb_ref[...])


def _stream_tile_copy(xa_ref, xb_ref, lat_row0, buf_ref, sem, tile, slot, start):
    t = buf_ref.shape[1]

    def copy(src_ref, row):
        return pltpu.make_async_copy(src_ref.at[pl.ds(pl.multiple_of(row, t), t), :],
                                     buf_ref.at[slot], sem.at[slot])

    if not start:
        copy(xa_ref, 0).wait()
        return
    is_ctx = _is_ctx_tile(tile, t)

    @pl.when(is_ctx)
    def _():
        copy(xa_ref, tile * t).start()

    @pl.when(jnp.logical_not(is_ctx))
    def _():
        copy(xb_ref, lat_row0 + (tile - MP // t) * t).start()


def _next_stream_tile(xa_ref, xb_ref, lat_row0, buf_ref, sem, tile, n_tiles):
    @pl.when(tile == 0)
    def _():
        _stream_tile_copy(xa_ref, xb_ref, lat_row0, buf_ref, sem, 0, 0, True)

    @pl.when(tile + 1 < n_tiles)
    def _():
        _stream_tile_copy(xa_ref, xb_ref, lat_row0, buf_ref, sem, tile + 1, (tile + 1) % 2, True)

    slot = tile % 2
    _stream_tile_copy(xa_ref, xb_ref, lat_row0, buf_ref, sem, tile, slot, False)
    return slot


def _stream_hbm(x):
    return (x[0], x[1], 0) if isinstance(x, tuple) else (x, x, MP)


def _split_bf16(a):
    hi = a.astype(BF16)
    return hi, (a - hi.astype(F32)).astype(BF16)


def _route_top2(h, wr, br, idx_ref, wt_ref):
    h_hi, h_lo = _split_bf16(h)
    w_hi, w_lo = _split_bf16(wr)
    logits = (jnp.dot(h_hi, w_hi, preferred_element_type=F32)
              + jnp.dot(h_lo, w_hi, preferred_element_type=F32)
              + jnp.dot(h_hi, w_lo, preferred_element_type=F32)) + br
    lane = lax.broadcasted_iota(I32, logits.shape, 1)
    neg = jnp.float32(-jnp.inf)
    lg = jnp.where(lane < N_EXPERTS, logits, neg)
    m1 = jnp.max(lg, axis=-1, keepdims=True)
    i1 = jnp.min(jnp.where(lg == m1, lane, LANES), axis=-1, keepdims=True)
    lg2 = jnp.where(lane == i1, neg, lg)
    m2 = jnp.max(lg2, axis=-1, keepdims=True)
    i2 = jnp.min(jnp.where(lg2 == m2, lane, LANES), axis=-1, keepdims=True)
    e = jnp.exp(m2 - m1)
    w1 = 1.0 / (1.0 + e)
    w2 = e / (1.0 + e)
    idx_ref[...] = jnp.where(lane == 0, i1, jnp.where(lane == 1, i2, 0))
    wt_ref[...] = jnp.where(lane == 0, w1, jnp.where(lane == 1, w2, 0.0))


W_PIECE = 512


def _out_proj_kernel(*refs, layer, lat_row0, route):
    if route:
        (o_ref, w_ref, xa_ref, xb_ref, gate_ref, g_ref, sc_ref, sh_ref, wr_ref, br_ref,
         xn_ref, h_ref, idx_ref, wt_ref, stage_ref, wbf_ref, xbuf_ref, wsem, xsem) = refs
    else:
        (o_ref, w_ref, xa_ref, xb_ref, gate_ref, g_ref, sc_ref, sh_ref,
         xn_ref, h_ref, stage_ref, wbf_ref, xbuf_ref, wsem, xsem) = refs
    i = pl.program_id(0)

    @pl.when(i == 0)
    def _():
        for p in range(D // W_PIECE):
            cols = pl.ds(p * W_PIECE, W_PIECE)
            cp = pltpu.make_async_copy(w_ref.at[layer, :, cols], stage_ref, wsem.at[0])
            cp.start()
            cp.wait()
            _cast_rows(stage_ref, wbf_ref.at[:, cols])

    slot = _next_stream_tile(xa_ref, xb_ref, lat_row0, xbuf_ref, xsem, i, pl.num_programs(0))
    acc = jnp.dot(o_ref[...], wbf_ref[...], preferred_element_type=F32)
    x_new = xbuf_ref[slot] + gate_ref[...] * acc
    xn_ref[...] = x_new
    h = _modulated_norm(x_new, g_ref[...], sc_ref[...], sh_ref[...])
    if route:
        h_ref[...] = h
        _route_top2(h, wr_ref[...], br_ref[...], idx_ref, wt_ref)
    else:
        h_ref[...] = h.astype(BF16)


def _out_proj(o, w_out, x, mod, norm2_g, layer, router=None):
    t = T_NORM
    row = lambda i: i
    xa, xb, lat_row0 = _stream_hbm(x)
    route = router is not None
    anyspace = pl.BlockSpec(memory_space=pl.ANY)
    rows = lambda w: pl.BlockSpec((t, w), lambda i: (i, 0))
    in_specs = [rows(D), anyspace, anyspace, anyspace,
                _mod_spec(layer, 2, t, row),
                pl.BlockSpec((None, 1, D), lambda i: (layer, 0, 0)),
                _mod_spec(layer, 4, t, row), _mod_spec(layer, 3, t, row)]
    args = [o, w_out, xa, xb, mod, norm2_g.reshape(DEPTH, 1, D), mod, mod]
    out_specs = [rows(D), rows(D)]
    out_shape = [jax.ShapeDtypeStruct((M, D), F32), jax.ShapeDtypeStruct((M, D), F32 if route else BF16)]
    if route:
        w_router, b_router = router
        args += [jnp.zeros((D, LANES), F32).at[:, :N_EXPERTS].set(w_router),
                 jnp.zeros((1, LANES), F32).at[0, :N_EXPERTS].set(b_router)]
        in_specs += [pl.BlockSpec((D, LANES), lambda i: (0, 0)), pl.BlockSpec((1, LANES), lambda i: (0, 0))]
        out_specs += [rows(LANES), rows(LANES)]
        out_shape += [jax.ShapeDtypeStruct((M, LANES), I32), jax.ShapeDtypeStruct((M, LANES), F32)]
    return pl.pallas_call(
        functools.partial(_out_proj_kernel, layer=layer, lat_row0=lat_row0, route=route),
        grid=(M // t,),
        in_specs=in_specs,
        out_specs=out_specs,
        out_shape=out_shape,
        scratch_shapes=[pltpu.VMEM((D, W_PIECE), F32), pltpu.VMEM((D, D), BF16),
                        pltpu.VMEM((2, t, D), F32),
                        pltpu.SemaphoreType.DMA((1,)), pltpu.SemaphoreType.DMA((2,))],
        compiler_params=_params(1),
        name="out_proj_router" if route else "out_proj",
    )(*args)


CAST_ROWS = 256


def _cast_rows(src_ref, dst_ref):
    def body(r, carry):
        rs = pl.ds(pl.multiple_of(r * CAST_ROWS, CAST_ROWS), CAST_ROWS)
        dst_ref[rs, :] = src_ref[rs, :].astype(BF16)
        return carry

    lax.fori_loop(0, src_ref.shape[0] // CAST_ROWS, body, 0)


def _head_rms(a):
    return lax.rsqrt(jnp.mean(a * a, axis=-1, keepdims=True) + EPS)


def _rope_partner(ag, perm):
    hi, lo = _split_bf16(ag)
    return (jnp.dot(hi, perm, preferred_element_type=F32)
            + jnp.dot(lo, perm, preferred_element_type=F32))


def _cast_kernel(w_ref, o_ref):
    o_ref[...] = w_ref[...].astype(BF16)


def _cast_in_weights(w_in):
    spec = pl.BlockSpec((None, D, TN_IN), lambda l, n: (l, 0, n))
    return pl.pallas_call(
        _cast_kernel,
        grid=(DEPTH, IN_WIDTH // TN_IN),
        in_specs=[spec],
        out_specs=spec,
        out_shape=jax.ShapeDtypeStruct(w_in.shape, BF16),
        compiler_params=_params(2),
        name="cast_w_in",
    )(w_in)


N_Q_TILES = ATTN_WIDTH // TN_IN
KV_TILE = N_Q_TILES
U_TILE0 = KV_TILE + 1
G_TILE0 = U_TILE0 + SGU_WIDTH // TN_IN
N_IN_TILES = IN_WIDTH // TN_IN


def _in_proj_kernel(xa_ref, xb_ref, n1_ref, sc0_ref, sh0_ref, sc1_ref, sh1_ref, w_ref,
                    qg_ref, kg_ref, sg_ref, pq_ref, pk_ref, cos_ref, sin_ref,
                    q_ref, kf_ref, kb_ref, vf_ref, vb_ref, u_ref, gh_ref,
                    xbuf_ref, h_ref, xsem, *, lat_row0):
    m = pl.program_id(0)
    n = pl.program_id(1)
    nm = pl.num_programs(0)
    cur = m % 2
    nxt = (m + 1) % 2
    has_next = m + 1 < nm
    fetch = functools.partial(_stream_tile_copy, xa_ref, xb_ref, lat_row0, xbuf_ref, xsem)

    def norm_rows(slot, rows, sc_ref, sh_ref):
        x = xbuf_ref[slot, rows, :]
        h_ref[slot, rows, :] = _modulated_norm(x, n1_ref[...], sc_ref[...], sh_ref[...]).astype(BF16)

    @pl.when((m == 0) & (n == 0))
    def _():
        fetch(0, 0, True)
        fetch(0, 0, False)
        norm_rows(0, slice(None), sc0_ref, sh0_ref)

    @pl.when((n == 0) & has_next)
    def _():
        fetch(m + 1, nxt, True)

    def matmul():
        return jnp.dot(h_ref[cur], w_ref[...], preferred_element_type=F32)

    for rope in (False, True):
        positioned = jnp.logical_not(_is_ctx_tile(m, TM)) if rope else _is_ctx_tile(m, TM)

        @pl.when((n < N_Q_TILES) & positioned)
        def _():
            acc = matmul()
            ag = acc * qg_ref[...]
            if rope:
                partner = _rope_partner(ag, pq_ref[...])
                cos, sin = cos_ref[...], sin_ref[...]
            for h in range(TN_IN // HEAD_DIM):
                sl = slice(h * HEAD_DIM, (h + 1) * HEAD_DIM)
                r = _head_rms(acc[:, sl]) * (ATTN_SCALE * LOG2_E)
                qh = ag[:, sl] * cos + partner[:, sl] * sin if rope else ag[:, sl]
                q_ref[:, sl] = (qh * r).astype(BF16)

        @pl.when((n == KV_TILE) & positioned)
        def _():
            acc = matmul()
            k = acc[:, :KV_WIDTH]
            ag = k * kg_ref[...]
            if rope:
                partner = _rope_partner(ag, pk_ref[...])
                cos, sin = cos_ref[...], sin_ref[...]
            for h in range(N_KV_HEADS):
                sl = slice(h * HEAD_DIM, (h + 1) * HEAD_DIM)
                r = _head_rms(k[:, sl])
                kf = ag[:, sl] * r
                if not rope:
                    kf_ref[:, sl] = kf
                kb_ref[:, sl] = ((ag[:, sl] * cos + partner[:, sl] * sin) * r if rope else kf).astype(BF16)
            v = acc[:, KV_WIDTH:]
            if not rope:
                vf_ref[...] = v
            vb_ref[...] = v.astype(BF16)

    half = TM // (G_TILE0 - U_TILE0)
    for k in range(G_TILE0 - U_TILE0):
        for parity in range(2):
            @pl.when((n == U_TILE0 + k) & has_next & (cur == parity))
            def _():
                if k == 0:
                    fetch(m + 1, 1 - parity, False)
                u_ref[...] = jnp.dot(h_ref[parity], w_ref[...], preferred_element_type=F32).astype(BF16)
                norm_rows(1 - parity, slice(k * half, (k + 1) * half), sc1_ref, sh1_ref)

        @pl.when((n == U_TILE0 + k) & jnp.logical_not(has_next))
        def _():
            u_ref[...] = matmul().astype(BF16)

    @pl.when(n >= G_TILE0)
    def _():
        acc = matmul()
        for h in range(TN_IN // HEAD_DIM):
            sl = slice(h * HEAD_DIM, (h + 1) * HEAD_DIM)
            a = acc[:, sl]
            gh_ref[:, sl] = (a * _head_rms(a) * sg_ref[:, sl]).astype(BF16)


def _rope_tables():
    n_rows = DEC_SEQ // GRID_W
    rows = jnp.broadcast_to(jnp.arange(n_rows)[:, None], (n_rows, GRID_W)).reshape(-1)
    cols = jnp.broadcast_to(jnp.arange(GRID_W)[None, :], (n_rows, GRID_W)).reshape(-1)
    inv = ROPE_THETA ** (-jnp.arange(0, ROPE_AXIS_DIM, 2, dtype=F32) / ROPE_AXIS_DIM)
    ang_r = rows.astype(F32)[:, None] * inv
    ang_c = cols.astype(F32)[:, None] * inv
    cos = jnp.concatenate([jnp.cos(ang_r), jnp.cos(ang_r), jnp.cos(ang_c), jnp.cos(ang_c)], axis=1)
    sin = jnp.concatenate([-jnp.sin(ang_r), jnp.sin(ang_r), -jnp.sin(ang_c), jnp.sin(ang_c)], axis=1)
    return cos, sin


def _partner_matrix(n_heads):
    w = n_heads * HEAD_DIM
    quarter = ROPE_AXIS_DIM // 2
    j = jnp.arange(w)
    partner = jnp.where((j % ROPE_AXIS_DIM) < quarter, j + quarter, j - quarter)
    return (jnp.arange(w)[:, None] == partner[None, :]).astype(BF16)


def _rope_block(m):
    return jnp.maximum(m - MP // TM, 0) % (DEC_SEQ // TM)


def _in_projections(x, w_in_bf, mod, norm1_g, q_norm_g, k_norm_g, sgu_norm_g, cos, sin, layer):
    tn = TN_IN
    xa, xb, lat_row0 = _stream_hbm(x)
    row = lambda m, n: m
    next_row = lambda m, n: jnp.minimum(m + 1, M // TM - 1)
    anyspace = pl.BlockSpec(memory_space=pl.ANY)
    const = lambda shape: pl.BlockSpec(shape, lambda m, n: (0,) * len(shape))
    rope_spec = pl.BlockSpec((TM, HEAD_DIM), lambda m, n: (_rope_block(m), 0))
    q_heads = tn // HEAD_DIM
    q_gain = jnp.tile(q_norm_g[layer], q_heads)[None, :]
    k_gain = jnp.tile(k_norm_g[layer], N_KV_HEADS)[None, :]
    g_tile = lambda n: jnp.clip(n - G_TILE0, 0, SGU_WIDTH // tn - 1)
    kv_out = pl.BlockSpec((TM, KV_WIDTH), lambda m, n: (m, 0))
    kv_ctx = pl.BlockSpec((TM, KV_WIDTH), lambda m, n: (jnp.minimum(m, MP // TM - 1), 0))
    kv_shape = lambda rows, dt: jax.ShapeDtypeStruct((rows, KV_WIDTH), dt)
    return pl.pallas_call(
        functools.partial(_in_proj_kernel, lat_row0=lat_row0),
        grid=(M // TM, N_IN_TILES),
        in_specs=[anyspace, anyspace,
                  pl.BlockSpec((None, 1, D), lambda m, n: (layer, 0, 0)),
                  _mod_spec(layer, 1, TM, row), _mod_spec(layer, 0, TM, row),
                  _mod_spec(layer, 1, TM, next_row), _mod_spec(layer, 0, TM, next_row),
                  pl.BlockSpec((None, D, tn), lambda m, n: (layer, 0, n)),
                  const((1, tn)), const((1, KV_WIDTH)),
                  pl.BlockSpec((None, 1, tn), lambda m, n: (layer, 0, g_tile(n))),
                  const((tn, tn)), const((KV_WIDTH, KV_WIDTH)), rope_spec, rope_spec],
        out_specs=[pl.BlockSpec((TM, tn), lambda m, n: (m, jnp.minimum(n, N_Q_TILES - 1))),
                   kv_ctx, kv_out, kv_ctx, kv_out,
                   pl.BlockSpec((TM, tn), lambda m, n: (m, jnp.clip(n - U_TILE0, 0, SGU_WIDTH // tn - 1))),
                   pl.BlockSpec((TM, tn), lambda m, n: (m, g_tile(n)))],
        out_shape=[jax.ShapeDtypeStruct((M, ATTN_WIDTH), BF16),
                   kv_shape(MP, F32), kv_shape(M, BF16), kv_shape(MP, F32), kv_shape(M, BF16),
                   jax.ShapeDtypeStruct((M, SGU_WIDTH), BF16),
                   jax.ShapeDtypeStruct((M, SGU_WIDTH), BF16)],
        scratch_shapes=[pltpu.VMEM((2, TM, D), F32), pltpu.VMEM((2, TM, D), BF16),
                        pltpu.SemaphoreType.DMA((2,))],
        compiler_params=_params(2),
        name="in_proj",
    )(xa, xb, norm1_g.reshape(DEPTH, 1, D), mod, mod, mod, mod, w_in_bf, q_gain, k_gain,
      sgu_norm_g.reshape(DEPTH, 1, SGU_WIDTH), _partner_matrix(q_heads), _partner_matrix(N_KV_HEADS),
      cos, sin)


def _qk(q, k):
    return lax.dot_general(q, k, (((1,), (1,)), ((), ())), preferred_element_type=F32)


def _attn_kernel(*refs, has_cache, n_batch, seq, tq):
    def with_ones(v):
        return jnp.concatenate([v, jnp.ones_like(v)], axis=1)

    if has_cache:
        q_ref, k_ref, v_ref, kc_ref, vc_ref, o_ref = refs
    else:
        q_ref, k_ref, v_ref, o_ref = refs
    for b in range(n_batch):
        rows_q = slice(b * tq, (b + 1) * tq)
        rows_k = slice(b * seq, (b + 1) * seq)
        for kv in range(N_KV_HEADS):
            kv_cols = slice(kv * HEAD_DIM, (kv + 1) * HEAD_DIM)
            k = k_ref[rows_k, kv_cols]
            v = with_ones(v_ref[rows_k, kv_cols])
            if has_cache:
                kc = kc_ref[:, kv_cols].astype(BF16)
                vc = with_ones(vc_ref[:, kv_cols].astype(BF16))
            for g in range(Q_PER_KV):
                head = kv * Q_PER_KV + g
                sl = slice(head * HEAD_DIM, (head + 1) * HEAD_DIM)
                q = q_ref[rows_q, sl]
                s = _qk(q, k)
                m = jnp.max(s, axis=-1, keepdims=True)
                if has_cache:
                    sc = _qk(q, kc)
                    m = jnp.maximum(m, jnp.max(sc, axis=-1, keepdims=True))
                o = jnp.dot(jnp.exp2(s - m).astype(BF16), v, preferred_element_type=F32)
                if has_cache:
                    o = o + jnp.dot(jnp.exp2(sc - m).astype(BF16), vc, preferred_element_type=F32)
                o_ref[rows_q, sl] = (o[:, :HEAD_DIM] / o[:, HEAD_DIM:]).astype(BF16)


def _attention(q, kb, vb, cache_k, cache_v, layer, *, batch, seq, row0):
    has_cache = cache_k is not None
    tq = min(T_Q, seq)
    nq = seq // tq
    n_batch = max(1, ATTN_ROWS // seq) if nq == 1 else 1
    q_spec = pl.BlockSpec((n_batch * tq, ATTN_WIDTH), lambda b, i: (row0 // (n_batch * tq) + b * nq + i, 0))
    kv_spec = pl.BlockSpec((n_batch * seq, KV_WIDTH), lambda b, i: (row0 // (n_batch * seq) + b, 0))
    in_specs = [q_spec, kv_spec, kv_spec]
    args = [q, kb, vb]
    if has_cache:
        c_spec = pl.BlockSpec((None, None, PAST_LEN, KV_WIDTH), lambda b, i: (b, layer, 0, 0))
        in_specs += [c_spec, c_spec]
        args += [cache_k.reshape(DEC_BATCH, DEPTH, PAST_LEN, KV_WIDTH),
                 cache_v.reshape(DEC_BATCH, DEPTH, PAST_LEN, KV_WIDTH)]
    return pl.pallas_call(
        functools.partial(_attn_kernel, has_cache=has_cache, n_batch=n_batch, seq=seq, tq=tq),
        grid=(batch // n_batch, nq),
        in_specs=in_specs,
        out_specs=pl.BlockSpec((n_batch * tq, ATTN_WIDTH), lambda b, i: (b * nq + i, 0)),
        out_shape=jax.ShapeDtypeStruct((batch * seq, ATTN_WIDTH), BF16),
        compiler_params=_params(2),
        name="attention_cached" if has_cache else "attention",
    )(*args)


def _sgu_merge_kernel(u_ref, gh_ref, ap_ref, as_ref, ws_ref, bs_ref, gn_ref, o_ref, sgu_ref):
    t = u_ref.shape[0]
    a = _pick(ap_ref, as_ref, 0).astype(F32)
    a = a * lax.rsqrt(jnp.mean(a * a, axis=-1, keepdims=True) + EPS) * gn_ref[:, :ATTN_WIDTH]
    o_ref[:, :ATTN_WIDTH] = a.astype(BF16)
    for h in range(N_SGU_HEADS):
        cs = slice(h * HEAD_DIM, (h + 1) * HEAD_DIM)
        w = ws_ref[h].astype(BF16)
        b = bs_ref[h]
        for c in range(t // CHUNK):
            rs = slice(c * CHUNK, (c + 1) * CHUNK)
            mixed = jnp.dot(w, gh_ref[rs, cs], preferred_element_type=F32) + b
            sgu_ref[rs, cs] = u_ref[rs, cs].astype(F32) * mixed
    s = sgu_ref[...]
    s = s * lax.rsqrt(jnp.mean(s * s, axis=-1, keepdims=True) + EPS) * gn_ref[:, ATTN_WIDTH:]
    o_ref[:, ATTN_WIDTH:] = s.astype(BF16)


def _sgu_merge(u, gh, attn_ctx, attn_lat, w_spatial, b_spatial, out_norm_g, layer):
    t = T_SGU
    bias = jnp.broadcast_to(b_spatial[:, :, :, None], (DEPTH, N_SGU_HEADS, CHUNK, HEAD_DIM))
    row = lambda w: pl.BlockSpec((t, w), lambda i: (i, 0))
    a_args, a_specs = _stream_in((attn_ctx, attn_lat), t, ATTN_WIDTH, lambda i: i, lambda i: 0)
    return pl.pallas_call(
        _sgu_merge_kernel,
        grid=(M // t,),
        in_specs=[row(SGU_WIDTH), row(SGU_WIDTH)] + a_specs + [
            pl.BlockSpec((None, N_SGU_HEADS, CHUNK, CHUNK), lambda i: (layer, 0, 0, 0)),
            pl.BlockSpec((None, N_SGU_HEADS, CHUNK, HEAD_DIM), lambda i: (layer, 0, 0, 0)),
            pl.BlockSpec((None, 1, D), lambda i: (layer, 0, 0))],
        out_specs=row(D),
        out_shape=jax.ShapeDtypeStruct((M, D), BF16),
        scratch_shapes=[pltpu.VMEM((t, SGU_WIDTH), F32)],
        compiler_params=_params(1),
        name="sgu_merge",
    )(u, gh, *a_args, w_spatial, bias, out_norm_g.reshape(DEPTH, 1, D))


def _mm_resid_kernel(a_ref, w_ref, x_ref, g_ref, o_ref, stage_ref, wbf_ref, sem, *, w_index):
    n = pl.program_id(0)
    tn = stage_ref.shape[1]

    def weight_copy(col_block):
        cols = pl.ds(pl.multiple_of(col_block * tn, LANES), tn)
        return pltpu.make_async_copy(w_ref.at[w_index, :, cols], stage_ref, sem.at[0])

    @pl.when(pl.program_id(1) == 0)
    def _():
        @pl.when(n == 0)
        def _():
            weight_copy(0).start()

        weight_copy(0).wait()
        _cast_rows(stage_ref, wbf_ref)

        @pl.when(n + 1 < pl.num_programs(0))
        def _():
            weight_copy(n + 1).start()

    acc = jnp.dot(a_ref[...], wbf_ref[...], preferred_element_type=F32)
    o_ref[...] = x_ref[...] + g_ref[...] * acc


def _mm_resid(a, w, x, mod, layer, w_index, gate_chunk, tm, tn):
    k = a.shape[1]
    row = lambda n, m: m
    col = lambda n, m: n
    return pl.pallas_call(
        functools.partial(_mm_resid_kernel, w_index=w_index),
        grid=(D // tn, M // tm),
        in_specs=[pl.BlockSpec((tm, k), lambda n, m: (m, 0)),
                  pl.BlockSpec(memory_space=pl.ANY),
                  pl.BlockSpec((tm, tn), lambda n, m: (m, n)),
                  _mod_spec(layer, gate_chunk, tm, row, col, tn=tn)],
        out_specs=pl.BlockSpec((tm, tn), lambda n, m: (m, n)),
        out_shape=jax.ShapeDtypeStruct((M, D), F32),
        scratch_shapes=[pltpu.VMEM((k, tn), F32), pltpu.VMEM((k, tn), BF16),
                        pltpu.SemaphoreType.DMA((1,))],
        compiler_params=_params(2),
        name="mm_resid",
    )(a, w, x, mod)


def _swiglu(a, b):
    return a * jax.nn.sigmoid(a) * b


def _ffn_gu_kernel(x_ref, wg_ref, wu_ref, o_ref, wgb_ref, wub_ref):
    @pl.when(pl.program_id(1) == 0)
    def _():
        _cast_rows(wg_ref, wgb_ref)
        _cast_rows(wu_ref, wub_ref)
    for r in range(x_ref.shape[0] // TM):
        rs = slice(r * TM, (r + 1) * TM)
        x = x_ref[rs, :]
        a = jnp.dot(x, wgb_ref[...], preferred_element_type=F32)
        b = jnp.dot(x, wub_ref[...], preferred_element_type=F32)
        o_ref[rs, :] = _swiglu(a, b).astype(BF16)


def _ffn_gate_up(h, w_gate, w_up, j):
    tm, tn = 2 * TM, 512
    w_spec = pl.BlockSpec((None, D, tn), lambda n, m: (j, 0, n))
    return pl.pallas_call(
        _ffn_gu_kernel,
        grid=(D_FF // tn, M // tm),
        in_specs=[pl.BlockSpec((tm, D), lambda n, m: (m, 0)), w_spec, w_spec],
        out_specs=pl.BlockSpec((tm, tn), lambda n, m: (m, n)),
        out_shape=jax.ShapeDtypeStruct((M, D_FF), BF16),
        scratch_shapes=[pltpu.VMEM((D, tn), BF16), pltpu.VMEM((D, tn), BF16)],
        compiler_params=_params(2),
        name="ffn_gate_up",
    )(h, w_gate, w_up)


def _route_meta(idx):
    t = T_MOE
    experts = jnp.arange(N_EXPERTS, dtype=I32)
    sel = idx.T[:, None, :] == experts[None, :, None]
    onehot = (sel[0] | sel[1]).astype(I32)
    csum = jnp.cumsum(onehot, axis=1)
    rank = csum - onehot
    count = csum[:, -1]
    ntile = (count + t - 1) // t
    tile_end = jnp.cumsum(ntile)
    tile_start = tile_end - ntile
    nused = tile_end[-1]
    row = (tile_start * t)[:, None] + rank
    pos = jnp.sum(jnp.where(sel, row[None], 0), axis=1)
    j = jnp.arange(NT_MOE, dtype=I32)
    te_raw = jnp.minimum(jnp.sum(j[:, None] >= tile_end[None, :], axis=1), N_EXPERTS - 1).astype(I32)
    te = jnp.where(j < nused, te_raw, te_raw[nused - 1])
    first = ((j == tile_start[te]) & (j < nused)).astype(I32)
    later = (ntile[None, :] > 0) & (experts[None, :] > te[:, None])
    nxt = jnp.min(jnp.where(later, experts[None, :], N_EXPERTS), axis=1)
    nxt = jnp.where(nxt == N_EXPERTS, -1, nxt).astype(I32)
    pad_start = tile_start * t + count
    pad_len = ntile * t - count
    tail = jnp.stack([nused * t, (NT_MOE - nused) * (t // ZERO_ROWS)])
    zinfo = jnp.concatenate([pad_start, pad_len, tail]).astype(I32)
    return pos.astype(I32), zinfo, (te, first, nxt, nused.reshape(1).astype(I32))


def _dispatch_kernel(p0_ref, p1_ref, z_ref, h_ref, xs_ref, zero_ref, sem, zsem):
    t = h_ref.shape[0]
    i = pl.program_id(0)
    base = i * t

    def clear_padding(start):
        def go(n, off):
            cp = pltpu.make_async_copy(zero_ref.at[pl.ds(0, n), :], xs_ref.at[pl.ds(off, n), :], zsem.at[0])
            cp.start() if start else cp.wait()

        for e in range(N_EXPERTS):
            off, ln = z_ref[e], z_ref[N_EXPERTS + e]
            end = off + ln
            for b in range(SUBLANE_BITS, PAD_BITS):
                @pl.when(((ln >> b) & 1) == 1)
                def _():
                    go(1 << b, pl.multiple_of(end - ((ln >> b) << b), SUBLANES))
            for k in range(SUBLANES - 1):
                @pl.when(k < (ln & (SUBLANES - 1)))
                def _():
                    go(1, off + k)
        tail0, n_tail = z_ref[2 * N_EXPERTS], z_ref[2 * N_EXPERTS + 1]

        def tail_body(k, carry):
            go(ZERO_ROWS, pl.multiple_of(tail0 + k * ZERO_ROWS, SUBLANES))
            return carry

        lax.fori_loop(0, n_tail, tail_body, 0)

    @pl.when(i == 0)
    def _():
        zero_ref[...] = jnp.zeros(zero_ref.shape, zero_ref.dtype)
        clear_padding(True)

    def issue(r, carry):
        src = h_ref.at[pl.ds(r, 1), :]
        pltpu.make_async_copy(src, xs_ref.at[pl.ds(p0_ref[base + r], 1), :], sem.at[0]).start()
        pltpu.make_async_copy(src, xs_ref.at[pl.ds(p1_ref[base + r], 1), :], sem.at[1]).start(priority=1)
        return carry

    lax.fori_loop(0, t, issue, 0, unroll=8)
    pltpu.make_async_copy(h_ref, xs_ref.at[pl.ds(0, t), :], sem.at[0]).wait()
    pltpu.make_async_copy(h_ref, xs_ref.at[pl.ds(0, t), :], sem.at[1]).wait()

    @pl.when(i == 0)
    def _():
        clear_padding(False)


def _dispatch(h, pos0, pos1, zinfo):
    t = T_DISPATCH
    return pl.pallas_call(
        _dispatch_kernel,
        grid_spec=pltpu.PrefetchScalarGridSpec(
            num_scalar_prefetch=3,
            grid=(M // t,),
            in_specs=[pl.BlockSpec((t, D), lambda i, p0, p1, z: (i, 0))],
            out_specs=pl.BlockSpec(memory_space=pl.ANY),
            scratch_shapes=[pltpu.VMEM((ZERO_ROWS, D), F32),
                            pltpu.SemaphoreType.DMA((2,)), pltpu.SemaphoreType.DMA((1,))]),
        out_shape=jax.ShapeDtypeStruct((P_MOE, D), F32),
        compiler_params=_params(1),
        name="moe_dispatch",
    )(pos0, pos1, zinfo, h)


def _expert_weight_stream(w_refs, stage_refs, bf_refs, sem, te_ref, first_ref, nxt_ref, tn):
    c = pl.program_id(0)
    j = pl.program_id(1)
    nc = pl.num_programs(0)

    def copies(e, cc):
        col = pl.multiple_of(cc * tn, LANES)
        return [pltpu.make_async_copy(w.at[e, :, pl.ds(col, tn)], st, sem.at[k])
                for k, (w, st) in enumerate(zip(w_refs, stage_refs))]

    def start(e, cc):
        for cp in copies(e, cc):
            cp.start()

    @pl.when((c == 0) & (j == 0))
    def _():
        start(te_ref[0], 0)

    @pl.when(first_ref[j] == 1)
    def _():
        for cp in copies(0, 0):
            cp.wait()
        for st, bf in zip(stage_refs, bf_refs):
            _cast_rows(st, bf)
        ne = nxt_ref[j]

        @pl.when(ne >= 0)
        def _():
            start(ne, c)

        @pl.when((ne < 0) & (c + 1 < nc))
        def _():
            start(te_ref[0], c + 1)


def _gmm_gate_up_kernel(te_ref, first_ref, nxt_ref, nused_ref, xs_ref, wg_ref, wu_ref, o_ref,
                        sg_ref, su_ref, wgb_ref, wub_ref, sem):
    _expert_weight_stream((wg_ref, wu_ref), (sg_ref, su_ref), (wgb_ref, wub_ref), sem,
                          te_ref, first_ref, nxt_ref, TN_GU)

    @pl.when(pl.program_id(1) < nused_ref[0])
    def _():
        half = T_MOE // 2
        for r in range(2):
            rs = slice(r * half, (r + 1) * half)
            x = xs_ref[rs, :].astype(BF16)
            a = jnp.dot(x, wgb_ref[...], preferred_element_type=F32)
            b = jnp.dot(x, wub_ref[...], preferred_element_type=F32)
            o_ref[rs, :] = _swiglu(a, b).astype(BF16)

    @pl.when(pl.program_id(1) >= nused_ref[0])
    def _():
        o_ref[...] = jnp.zeros(o_ref.shape, o_ref.dtype)


def _gmm_down_kernel(te_ref, first_ref, nxt_ref, nused_ref, a_ref, wd_ref, o_ref,
                     sd_ref, wdb_ref, sem):
    _expert_weight_stream((wd_ref,), (sd_ref,), (wdb_ref,), sem, te_ref, first_ref, nxt_ref, TN_DN)

    @pl.when(pl.program_id(1) < nused_ref[0])
    def _():
        o_ref[...] = jnp.dot(a_ref[...], wdb_ref[...], preferred_element_type=F32)

    @pl.when(pl.program_id(1) >= nused_ref[0])
    def _():
        o_ref[...] = jnp.zeros(o_ref.shape, o_ref.dtype)


def _used_tile(j, nused):
    return jnp.minimum(j, nused[0] - 1)


def _gmm_gate_up(xs, w_gate, w_up, meta):
    te, first, nxt, nused = meta
    tn = TN_GU
    return pl.pallas_call(
        _gmm_gate_up_kernel,
        grid_spec=pltpu.PrefetchScalarGridSpec(
            num_scalar_prefetch=4,
            grid=(D_FF_EXPERT // tn, NT_MOE),
            in_specs=[pl.BlockSpec((T_MOE, D), lambda c, j, te, fi, nx, nu: (_used_tile(j, nu), 0)),
                      pl.BlockSpec(memory_space=pl.ANY),
                      pl.BlockSpec(memory_space=pl.ANY)],
            out_specs=pl.BlockSpec((T_MOE, tn), lambda c, j, te, fi, nx, nu: (j, c)),
            scratch_shapes=[pltpu.VMEM((D, tn), F32), pltpu.VMEM((D, tn), F32),
                            pltpu.VMEM((D, tn), BF16), pltpu.VMEM((D, tn), BF16),
                            pltpu.SemaphoreType.DMA((2,))]),
        out_shape=jax.ShapeDtypeStruct((P_MOE, D_FF_EXPERT), BF16),
        compiler_params=_params(2),
        name="moe_gate_up",
    )(te, first, nxt, nused, xs, w_gate, w_up)


def _gmm_down(act, w_down, meta):
    te, first, nxt, nused = meta
    tn = TN_DN
    return pl.pallas_call(
        _gmm_down_kernel,
        grid_spec=pltpu.PrefetchScalarGridSpec(
            num_scalar_prefetch=4,
            grid=(D // tn, NT_MOE),
            in_specs=[pl.BlockSpec((T_MOE, D_FF_EXPERT), lambda c, j, te, fi, nx, nu: (_used_tile(j, nu), 0)),
                      pl.BlockSpec(memory_space=pl.ANY)],
            out_specs=pl.BlockSpec((T_MOE, tn), lambda c, j, te, fi, nx, nu: (j, c)),
            scratch_shapes=[pltpu.VMEM((D_FF_EXPERT, tn), F32), pltpu.VMEM((D_FF_EXPERT, tn), BF16),
                            pltpu.SemaphoreType.DMA((1,))]),
        out_shape=jax.ShapeDtypeStruct((P_MOE, D), F32),
        compiler_params=_params(2),
        name="moe_down",
    )(te, first, nxt, nused, act, w_down)


def _combine_kernel(p0_ref, p1_ref, ys_ref, xa_ref, xb_ref, g_ref, w_ref, oc_ref, ol_ref,
                    a_ref, b_ref, sem):
    t = xa_ref.shape[0]
    i = pl.program_id(0)
    n = pl.num_programs(0)

    def issue(step, slot):
        base = step * t

        def body(r, carry):
            pltpu.make_async_copy(ys_ref.at[pl.ds(p0_ref[base + r], 1), :],
                                  a_ref.at[slot, pl.ds(r, 1), :], sem.at[0, slot]).start()
            pltpu.make_async_copy(ys_ref.at[pl.ds(p1_ref[base + r], 1), :],
                                  b_ref.at[slot, pl.ds(r, 1), :], sem.at[1, slot]).start(priority=1)
            return carry

        lax.fori_loop(0, t, body, 0, unroll=8)

    @pl.when(i == 0)
    def _():
        issue(0, 0)

    @pl.when(i + 1 < n)
    def _():
        issue(i + 1, (i + 1) % 2)

    slot = i % 2
    pltpu.make_async_copy(ys_ref.at[pl.ds(0, t), :], a_ref.at[slot], sem.at[0, slot]).wait()
    pltpu.make_async_copy(ys_ref.at[pl.ds(0, t), :], b_ref.at[slot], sem.at[1, slot]).wait()
    w = w_ref[...]
    moe = w[:, 0:1] * a_ref[slot] + w[:, 1:2] * b_ref[slot]
    y = _pick(xa_ref, xb_ref, 0) + g_ref[...] * moe
    is_ctx = _is_ctx_tile(i, t)

    @pl.when(is_ctx)
    def _():
        oc_ref[...] = y

    @pl.when(jnp.logical_not(is_ctx))
    def _():
        ol_ref[...] = y


def _combine(ys, x, mod, layer, gate_chunk, wts, pos0, pos1):
    t = T_COMBINE
    n_ctx = MP // t
    row = lambda i, *_: i
    x_args, x_specs = _stream_in(x, t, D, row, lambda i, *_: 0)
    return pl.pallas_call(
        _combine_kernel,
        grid_spec=pltpu.PrefetchScalarGridSpec(
            num_scalar_prefetch=2,
            grid=(M // t,),
            in_specs=[pl.BlockSpec(memory_space=pl.ANY)] + x_specs + [
                _mod_spec(layer, gate_chunk, t, row),
                pl.BlockSpec((t, LANES), lambda i, p0, p1: (i, 0))],
            out_specs=[pl.BlockSpec((t, D), lambda i, p0, p1: (jnp.minimum(i, n_ctx - 1), 0)),
                       pl.BlockSpec((t, D), lambda i, p0, p1: (jnp.maximum(i - n_ctx, 0), 0))],
            scratch_shapes=[pltpu.VMEM((2, t, D), F32), pltpu.VMEM((2, t, D), F32),
                            pltpu.SemaphoreType.DMA((2, 2))]),
        out_shape=[jax.ShapeDtypeStruct((MP, D), F32), jax.ShapeDtypeStruct((MS, D), F32)],
        compiler_params=_params(1),
        name="moe_combine",
    )(pos0, pos1, ys, *x_args, mod, wts)


def _moe(x, h, idx, wts, mod, layer, w_gate, w_up, w_down):
    pos, zinfo, meta = _route_meta(idx[:, :TOP_K])
    pos0, pos1 = pos[0], pos[1]
    xs = _dispatch(h, pos0, pos1, zinfo)
    act = _gmm_gate_up(xs, w_gate, w_up, meta)
    ys = _gmm_down(act, w_down, meta)
    return _combine(ys, x, mod, layer, 5, wts, pos0, pos1)


def kernel(x_prompt, x_sample, cache_k, cache_v, c, c_ctx, w_ada, b_ada, norm1_g, norm2_g, w_in, q_norm_g, k_norm_g, sgu_norm_g, w_spatial, b_spatial, out_norm_g, w_out, ffn_w_gate, ffn_w_up, ffn_w_down, w_router, b_router, moe_w_gate, moe_w_up, moe_w_down):
    assert DEPTH == 2
    x = (x_prompt.reshape(MP, D), x_sample.reshape(MS, D))
    cond = jnp.concatenate([c_ctx[None, :], c, jnp.zeros((N_COND - 1 - DEC_BATCH, D), F32)], axis=0)
    mod = _modulation(cond, w_ada, b_ada).reshape(DEPTH, N_COND, 1, N_MOD * D)
    cos, sin = _rope_tables()
    w_in_bf = _cast_in_weights(w_in)

    new_k, new_v = [], []
    for i in range(DEPTH):
        q, kf, kb, vf, vb, u, gh = _in_projections(x, w_in_bf, mod, norm1_g, q_norm_g, k_norm_g,
                                                   sgu_norm_g, cos, sin, i)
        attn_ctx = _attention(q, kb, vb, None, None, i, batch=BATCH, seq=SEQ, row0=0)
        attn_lat = _attention(q, kb, vb, cache_k, cache_v, i, batch=DEC_BATCH, seq=DEC_SEQ, row0=MP)
        o = _sgu_merge(u, gh, attn_ctx, attn_lat, w_spatial, b_spatial, out_norm_g, i)
        j = i // 2
        if i % 2 == 0:
            x, h2 = _out_proj(o, w_out, x, mod, norm2_g, i)
            act = _ffn_gate_up(h2, ffn_w_gate, ffn_w_up, j)
            x = _mm_resid(act, ffn_w_down, x, mod, i, j, 5, TM, 512)
        else:
            x, h2, idx, wts = _out_proj(o, w_out, x, mod, norm2_g, i, router=(w_router[j], b_router[j]))
            x = _moe(x, h2, idx, wts, mod, i, moe_w_gate[j], moe_w_up[j], moe_w_down[j])
        new_k.append(kf.reshape(BATCH, SEQ, N_KV_HEADS, HEAD_DIM))
        new_v.append(vf.reshape(BATCH, SEQ, N_KV_HEADS, HEAD_DIM))

    y_prompt = x[0].reshape(BATCH, SEQ, D)
    y_sample = x[1].reshape(DEC_BATCH, DEC_SEQ, D)
    return (y_prompt, y_sample, jnp.stack(new_k, axis=1), jnp.stack(new_v, axis=1))
```

```python
import functools

import jax
import jax.numpy as jnp
from jax import lax
from jax.experimental import pallas as pl
from jax.experimental.pallas import tpu as pltpu

F32 = jnp.float32
BF16 = jnp.bfloat16
I32 = jnp.int32

D = 2048
BATCH, SEQ = 16, 256
DEC_BATCH, DEC_SEQ = 4, 2048
PAST_LEN = 256
DEPTH = 2
GRID_W = 64
CHUNK = 128
HEAD_DIM = 128
N_Q_HEADS, N_KV_HEADS = 8, 2
Q_PER_KV = N_Q_HEADS // N_KV_HEADS
ATTN_WIDTH = N_Q_HEADS * HEAD_DIM
KV_WIDTH = N_KV_HEADS * HEAD_DIM
N_SGU_HEADS = 8
SGU_WIDTH = N_SGU_HEADS * HEAD_DIM
IN_WIDTH = ATTN_WIDTH + 2 * KV_WIDTH + 2 * SGU_WIDTH
ROPE_THETA = 10000.0
ROPE_AXIS_DIM = HEAD_DIM // 2
D_FF = 5632
N_EXPERTS = 8
TOP_K = 2
D_FF_EXPERT = 2816
N_MOD = 6
EPS = 1e-6
ATTN_SCALE = HEAD_DIM ** -0.5
LOG2_E = 1.4426950408889634

MP = BATCH * SEQ
MS = DEC_BATCH * DEC_SEQ
M = MP + MS
N_COND = 8
LANES = 128
SUBLANES = 8
SUBLANE_BITS = 3

VMEM_LIMIT = 56 * 1024 * 1024

TM = 1024
TN_IN = 2 * KV_WIDTH
T_NORM = 512
T_Q = 512
ATTN_ROWS = 1024
T_MOE = 640
NT_MOE = -(-(M * TOP_K + N_EXPERTS * (T_MOE - 1)) // T_MOE)
P_MOE = NT_MOE * T_MOE
TN_GU = D_FF_EXPERT // 2
TN_DN = D
T_DISPATCH = 1024
T_COMBINE = 256
T_SGU = 1024
PAD_BITS = (T_MOE - 1).bit_length()
ZERO_ROWS = 1 << (PAD_BITS - 1)
TAIL_ROWS = 128
assert T_MOE % TAIL_ROWS == 0 and T_MOE % 32 == 0


def _params(n_axes):
    return pltpu.CompilerParams(dimension_semantics=("arbitrary",) * n_axes,
                                vmem_limit_bytes=VMEM_LIMIT)


def _cond_row(i, t):
    return jnp.where(i < MP // t, 0, 1 + (i - MP // t) // (DEC_SEQ // t))


def _is_ctx_tile(i, t):
    return i < MP // t


def _stream_in(x, t, width, row_of, col_of):
    n_ctx = MP // t
    pair = isinstance(x, tuple)
    base = 0 if pair else n_ctx
    ctx = pl.BlockSpec((t, width), lambda *g: (jnp.minimum(row_of(*g), n_ctx - 1), col_of(*g)))
    lat = pl.BlockSpec((t, width), lambda *g: (base + jnp.maximum(row_of(*g) - n_ctx, 0), col_of(*g)))
    return (list(x) if pair else [x, x]), [ctx, lat]


def _mod_spec(layer, chunk, t, row_of, col_of=None, tn=D):
    per = D // tn

    def index_map(*g):
        col = chunk * per + (col_of(*g) if col_of is not None else 0)
        return (layer, _cond_row(row_of(*g), t), 0, col)

    return pl.BlockSpec((None, None, 1, tn), index_map)


def _ada_kernel(c_ref, w_ref, b_ref, o_ref):
    c = c_ref[...]
    s = (c * jax.nn.sigmoid(c)).astype(BF16)
    o_ref[...] = jnp.dot(s, w_ref[...].astype(BF16), preferred_element_type=F32) + b_ref[...]


def _modulation(cond, w_ada, b_ada):
    tn = 1024
    width = N_MOD * D
    return pl.pallas_call(
        _ada_kernel,
        grid=(DEPTH, width // tn),
        in_specs=[pl.BlockSpec((N_COND, D), lambda l, n: (0, 0)),
                  pl.BlockSpec((None, D, tn), lambda l, n: (l, 0, n)),
                  pl.BlockSpec((None, 1, tn), lambda l, n: (l, 0, n))],
        out_specs=pl.BlockSpec((None, N_COND, tn), lambda l, n: (l, 0, n)),
        out_shape=jax.ShapeDtypeStruct((DEPTH, N_COND, width), F32),
        compiler_params=_params(2),
        name="modulation",
    )(cond, w_ada, b_ada.reshape(DEPTH, 1, width))


def _modulated_norm(x, g, sc, sh):
    y = x * lax.rsqrt(jnp.mean(x * x, axis=-1, keepdims=True) + EPS)
    return y * (g * (1.0 + sc)) + sh


def _pick(xa_ref, xb_ref, axis):
    t = xa_ref.shape[0]
    return jnp.where(_is_ctx_tile(pl.program_id(axis), t), xa_ref[...], xb_ref[...])


def _stream_tile_copy(xa_ref, xb_ref, lat_row0, buf_ref, sem, tile, slot, start):
    t = buf_ref.shape[1]

    def copy(src_ref, row):
        return pltpu.make_async_copy(src_ref.at[pl.ds(pl.multiple_of(row, t), t), :],
                                     buf_ref.at[slot], sem.at[slot])

    if not start:
        copy(xa_ref, 0).wait()
        return
    is_ctx = _is_ctx_tile(tile, t)

    @pl.when(is_ctx)
    def _():
        copy(xa_ref, tile * t).start()

    @pl.when(jnp.logical_not(is_ctx))
    def _():
        copy(xb_ref, lat_row0 + (tile - MP // t) * t).start()


def _next_stream_tile(xa_ref, xb_ref, lat_row0, buf_ref, sem, tile, n_tiles):
    @pl.when(tile == 0)
    def _():
        _stream_tile_copy(xa_ref, xb_ref, lat_row0, buf_ref, sem, 0, 0, True)

    @pl.when(tile + 1 < n_tiles)
    def _():
        _stream_tile_copy(xa_ref, xb_ref, lat_row0, buf_ref, sem, tile + 1, (tile + 1) % 2, True)

    slot = tile % 2
    _stream_tile_copy(xa_ref, xb_ref, lat_row0, buf_ref, sem, tile, slot, False)
    return slot


def _stream_hbm(x):
    return (x[0], x[1], 0) if isinstance(x, tuple) else (x, x, MP)


def _split_bf16(a):
    hi = a.astype(BF16)
    return hi, (a - hi.astype(F32)).astype(BF16)


def _route_top2(h, wr, br, idx_ref, wt_ref):
    h_hi, h_lo = _split_bf16(h)
    w_hi, w_lo = _split_bf16(wr)
    logits = (jnp.dot(h_hi, w_hi, preferred_element_type=F32)
              + jnp.dot(h_lo, w_hi, preferred_element_type=F32)
              + jnp.dot(h_hi, w_lo, preferred_element_type=F32)) + br
    lane = lax.broadcasted_iota(I32, logits.shape, 1)
    neg = jnp.float32(-jnp.inf)
    lg = jnp.where(lane < N_EXPERTS, logits, neg)
    m1 = jnp.max(lg, axis=-1, keepdims=True)
    i1 = jnp.min(jnp.where(lg == m1, lane, LANES), axis=-1, keepdims=True)
    lg2 = jnp.where(lane == i1, neg, lg)
    m2 = jnp.max(lg2, axis=-1, keepdims=True)
    i2 = jnp.min(jnp.where(lg2 == m2, lane, LANES), axis=-1, keepdims=True)
    e = jnp.exp(m2 - m1)
    w1 = 1.0 / (1.0 + e)
    w2 = e / (1.0 + e)
    idx_ref[...] = jnp.where(lane == 0, i1, jnp.where(lane == 1, i2, 0))
    wt_ref[...] = jnp.where(lane == 0, w1, jnp.where(lane == 1, w2, 0.0))


W_PIECE = 512


def _out_proj_kernel(*refs, layer, lat_row0, route):
    if route:
        (o_ref, w_ref, xa_ref, xb_ref, gate_ref, g_ref, sc_ref, sh_ref, wr_ref, br_ref,
         xn_ref, h_ref, idx_ref, wt_ref, stage_ref, wbf_ref, xbuf_ref, wsem, xsem) = refs
    else:
        (o_ref, w_ref, xa_ref, xb_ref, gate_ref, g_ref, sc_ref, sh_ref,
         xn_ref, h_ref, stage_ref, wbf_ref, xbuf_ref, wsem, xsem) = refs
    i = pl.program_id(0)

    @pl.when(i == 0)
    def _():
        for p in range(D // W_PIECE):
            cols = pl.ds(p * W_PIECE, W_PIECE)
            cp = pltpu.make_async_copy(w_ref.at[layer, :, cols], stage_ref, wsem.at[0])
            cp.start()
            cp.wait()
            _cast_rows(stage_ref, wbf_ref.at[:, cols])

    slot = _next_stream_tile(xa_ref, xb_ref, lat_row0, xbuf_ref, xsem, i, pl.num_programs(0))
    acc = jnp.dot(o_ref[...], wbf_ref[...], preferred_element_type=F32)
    x_new = xbuf_ref[slot] + gate_ref[...] * acc
    xn_ref[...] = x_new
    h = _modulated_norm(x_new, g_ref[...], sc_ref[...], sh_ref[...])
    if route:
        h_ref[...] = h
        _route_top2(h, wr_ref[...], br_ref[...], idx_ref, wt_ref)
    else:
        h_ref[...] = h.astype(BF16)


def _out_proj(o, w_out, x, mod, norm2_g, layer, router=None):
    t = T_NORM
    row = lambda i: i
    xa, xb, lat_row0 = _stream_hbm(x)
    route = router is not None
    anyspace = pl.BlockSpec(memory_space=pl.ANY)
    rows = lambda w: pl.BlockSpec((t, w), lambda i: (i, 0))
    in_specs = [rows(D), anyspace, anyspace, anyspace,
                _mod_spec(layer, 2, t, row),
                pl.BlockSpec((None, 1, D), lambda i: (layer, 0, 0)),
                _mod_spec(layer, 4, t, row), _mod_spec(layer, 3, t, row)]
    args = [o, w_out, xa, xb, mod, norm2_g.reshape(DEPTH, 1, D), mod, mod]
    out_specs = [rows(D), rows(D)]
    out_shape = [jax.ShapeDtypeStruct((M, D), F32), jax.ShapeDtypeStruct((M, D), F32 if route else BF16)]
    if route:
        w_router, b_router = router
        args += [jnp.zeros((D, LANES), F32).at[:, :N_EXPERTS].set(w_router),
                 jnp.zeros((1, LANES), F32).at[0, :N_EXPERTS].set(b_router)]
        in_specs += [pl.BlockSpec((D, LANES), lambda i: (0, 0)), pl.BlockSpec((1, LANES), lambda i: (0, 0))]
        out_specs += [rows(LANES), rows(LANES)]
        out_shape += [jax.ShapeDtypeStruct((M, LANES), I32), jax.ShapeDtypeStruct((M, LANES), F32)]
    return pl.pallas_call(
        functools.partial(_out_proj_kernel, layer=layer, lat_row0=lat_row0, route=route),
        grid=(M // t,),
        in_specs=in_specs,
        out_specs=out_specs,
        out_shape=out_shape,
        scratch_shapes=[pltpu.VMEM((D, W_PIECE), F32), pltpu.VMEM((D, D), BF16),
                        pltpu.VMEM((2, t, D), F32),
                        pltpu.SemaphoreType.DMA((1,)), pltpu.SemaphoreType.DMA((2,))],
        compiler_params=_params(1),
        name="out_proj_router" if route else "out_proj",
    )(*args)


CAST_ROWS = 256


def _cast_rows(src_ref, dst_ref):
    def body(r, carry):
        rs = pl.ds(pl.multiple_of(r * CAST_ROWS, CAST_ROWS), CAST_ROWS)
        dst_ref[rs, :] = src_ref[rs, :].astype(BF16)
        return carry

    lax.fori_loop(0, src_ref.shape[0] // CAST_ROWS, body, 0)


def _head_rms(a):
    return lax.rsqrt(jnp.mean(a * a, axis=-1, keepdims=True) + EPS)


def _rope_partner(ag, perm):
    hi, lo = _split_bf16(ag)
    return (jnp.dot(hi, perm, preferred_element_type=F32)
            + jnp.dot(lo, perm, preferred_element_type=F32))


def _cast_kernel(w_ref, o_ref):
    o_ref[...] = w_ref[...].astype(BF16)


def _cast_in_weights(w_in):
    spec = pl.BlockSpec((None, D, TN_IN), lambda l, n: (l, 0, n))
    return pl.pallas_call(
        _cast_kernel,
        grid=(DEPTH, IN_WIDTH // TN_IN),
        in_specs=[spec],
        out_specs=spec,
        out_shape=jax.ShapeDtypeStruct(w_in.shape, BF16),
        compiler_params=_params(2),
        name="cast_w_in",
    )(w_in)


N_Q_TILES = ATTN_WIDTH // TN_IN
KV_TILE = N_Q_TILES
U_TILE0 = KV_TILE + 1
G_TILE0 = U_TILE0 + SGU_WIDTH // TN_IN
N_IN_TILES = IN_WIDTH // TN_IN


def _in_proj_kernel(xa_ref, xb_ref, n1_ref, sc0_ref, sh0_ref, sc1_ref, sh1_ref, w_ref,
                    qg_ref, kg_ref, sg_ref, pq_ref, pk_ref, cos_ref, sin_ref,
                    q_ref, kf_ref, kb_ref, vf_ref, vb_ref, u_ref, gh_ref,
                    xbuf_ref, h_ref, xsem, *, lat_row0):
    m = pl.program_id(0)
    n = pl.program_id(1)
    nm = pl.num_programs(0)
    cur = m % 2
    nxt = (m + 1) % 2
    has_next = m + 1 < nm
    fetch = functools.partial(_stream_tile_copy, xa_ref, xb_ref, lat_row0, xbuf_ref, xsem)

    def norm_rows(slot, rows, sc_ref, sh_ref):
        x = xbuf_ref[slot, rows, :]
        h_ref[slot, rows, :] = _modulated_norm(x, n1_ref[...], sc_ref[...], sh_ref[...]).astype(BF16)

    @pl.when((m == 0) & (n == 0))
    def _():
        fetch(0, 0, True)
        fetch(0, 0, False)
        norm_rows(0, slice(None), sc0_ref, sh0_ref)

    @pl.when((n == 0) & has_next)
    def _():
        fetch(m + 1, nxt, True)

    def matmul():
        return jnp.dot(h_ref[cur], w_ref[...], preferred_element_type=F32)

    for rope in (False, True):
        positioned = jnp.logical_not(_is_ctx_tile(m, TM)) if rope else _is_ctx_tile(m, TM)

        @pl.when((n < N_Q_TILES) & positioned)
        def _():
            acc = matmul()
            ag = acc * qg_ref[...]
            if rope:
                partner = _rope_partner(ag, pq_ref[...])
                cos, sin = cos_ref[...], sin_ref[...]
            for h in range(TN_IN // HEAD_DIM):
                sl = slice(h * HEAD_DIM, (h + 1) * HEAD_DIM)
                r = _head_rms(acc[:, sl]) * (ATTN_SCALE * LOG2_E)
                qh = ag[:, sl] * cos + partner[:, sl] * sin if rope else ag[:, sl]
                q_ref[:, sl] = (qh * r).astype(BF16)

        @pl.when((n == KV_TILE) & positioned)
        def _():
            acc = matmul()
            k = acc[:, :KV_WIDTH]
            ag = k * kg_ref[...]
            if rope:
                partner = _rope_partner(ag, pk_ref[...])
                cos, sin = cos_ref[...], sin_ref[...]
            for h in range(N_KV_HEADS):
                sl = slice(h * HEAD_DIM, (h + 1) * HEAD_DIM)
                r = _head_rms(k[:, sl])
                kf = ag[:, sl] * r
                if not rope:
                    kf_ref[:, sl] = kf
                kb_ref[:, sl] = ((ag[:, sl] * cos + partner[:, sl] * sin) * r if rope else kf).astype(BF16)
            v = acc[:, KV_WIDTH:]
            if not rope:
                vf_ref[...] = v
            vb_ref[...] = v.astype(BF16)

    half = TM // (G_TILE0 - U_TILE0)
    for k in range(G_TILE0 - U_TILE0):
        for parity in range(2):
            @pl.when((n == U_TILE0 + k) & has_next & (cur == parity))
            def _():
                if k == 0:
                    fetch(m + 1, 1 - parity, False)
                u_ref[...] = jnp.dot(h_ref[parity], w_ref[...], preferred_element_type=F32).astype(BF16)
                norm_rows(1 - parity, slice(k * half, (k + 1) * half), sc1_ref, sh1_ref)

        @pl.when((n == U_TILE0 + k) & jnp.logical_not(has_next))
        def _():
            u_ref[...] = matmul().astype(BF16)

    @pl.when(n >= G_TILE0)
    def _():
        acc = matmul()
        for h in range(TN_IN // HEAD_DIM):
            sl = slice(h * HEAD_DIM, (h + 1) * HEAD_DIM)
            a = acc[:, sl]
            gh_ref[:, sl] = (a * _head_rms(a) * sg_ref[:, sl]).astype(BF16)


def _rope_tables():
    n_rows = DEC_SEQ // GRID_W
    rows = jnp.broadcast_to(jnp.arange(n_rows)[:, None], (n_rows, GRID_W)).reshape(-1)
    cols = jnp.broadcast_to(jnp.arange(GRID_W)[None, :], (n_rows, GRID_W)).reshape(-1)
    inv = ROPE_THETA ** (-jnp.arange(0, ROPE_AXIS_DIM, 2, dtype=F32) / ROPE_AXIS_DIM)
    ang_r = rows.astype(F32)[:, None] * inv
    ang_c = cols.astype(F32)[:, None] * inv
    cos = jnp.concatenate([jnp.cos(ang_r), jnp.cos(ang_r), jnp.cos(ang_c), jnp.cos(ang_c)], axis=1)
    sin = jnp.concatenate([-jnp.sin(ang_r), jnp.sin(ang_r), -jnp.sin(ang_c), jnp.sin(ang_c)], axis=1)
    return cos, sin


def _partner_matrix(n_heads):
    w = n_heads * HEAD_DIM
    quarter = ROPE_AXIS_DIM // 2
    j = jnp.arange(w)
    partner = jnp.where((j % ROPE_AXIS_DIM) < quarter, j + quarter, j - quarter)
    return (jnp.arange(w)[:, None] == partner[None, :]).astype(BF16)


def _rope_block(m):
    return jnp.maximum(m - MP // TM, 0) % (DEC_SEQ // TM)


def _in_projections(x, w_in_bf, mod, norm1_g, q_norm_g, k_norm_g, sgu_norm_g, cos, sin, layer):
    tn = TN_IN
    xa, xb, lat_row0 = _stream_hbm(x)
    row = lambda m, n: m
    next_row = lambda m, n: jnp.minimum(m + 1, M // TM - 1)
    anyspace = pl.BlockSpec(memory_space=pl.ANY)
    const = lambda shape: pl.BlockSpec(shape, lambda m, n: (0,) * len(shape))
    rope_spec = pl.BlockSpec((TM, HEAD_DIM), lambda m, n: (_rope_block(m), 0))
    q_heads = tn // HEAD_DIM
    q_gain = jnp.tile(q_norm_g[layer], q_heads)[None, :]
    k_gain = jnp.tile(k_norm_g[layer], N_KV_HEADS)[None, :]
    g_tile = lambda n: jnp.clip(n - G_TILE0, 0, SGU_WIDTH // tn - 1)
    kv_out = pl.BlockSpec((TM, KV_WIDTH), lambda m, n: (m, 0))
    kv_ctx = pl.BlockSpec((TM, KV_WIDTH), lambda m, n: (jnp.minimum(m, MP // TM - 1), 0))
    kv_shape = lambda rows, dt: jax.ShapeDtypeStruct((rows, KV_WIDTH), dt)
    return pl.pallas_call(
        functools.partial(_in_proj_kernel, lat_row0=lat_row0),
        grid=(M // TM, N_IN_TILES),
        in_specs=[anyspace, anyspace,
                  pl.BlockSpec((None, 1, D), lambda m, n: (layer, 0, 0)),
                  _mod_spec(layer, 1, TM, row), _mod_spec(layer, 0, TM, row),
                  _mod_spec(layer, 1, TM, next_row), _mod_spec(layer, 0, TM, next_row),
                  pl.BlockSpec((None, D, tn), lambda m, n: (layer, 0, n)),
                  const((1, tn)), const((1, KV_WIDTH)),
                  pl.BlockSpec((None, 1, tn), lambda m, n: (layer, 0, g_tile(n))),
                  const((tn, tn)), const((KV_WIDTH, KV_WIDTH)), rope_spec, rope_spec],
        out_specs=[pl.BlockSpec((TM, tn), lambda m, n: (m, jnp.minimum(n, N_Q_TILES - 1))),
                   kv_ctx, kv_out, kv_ctx, kv_out,
                   pl.BlockSpec((TM, tn), lambda m, n: (m, jnp.clip(n - U_TILE0, 0, SGU_WIDTH // tn - 1))),
                   pl.BlockSpec((TM, tn), lambda m, n: (m, g_tile(n)))],
        out_shape=[jax.ShapeDtypeStruct((M, ATTN_WIDTH), BF16),
                   kv_shape(MP, F32), kv_shape(M, BF16), kv_shape(MP, F32), kv_shape(M, BF16),
                   jax.ShapeDtypeStruct((M, SGU_WIDTH), BF16),
                   jax.ShapeDtypeStruct((M, SGU_WIDTH), BF16)],
        scratch_shapes=[pltpu.VMEM((2, TM, D), F32), pltpu.VMEM((2, TM, D), BF16),
                        pltpu.SemaphoreType.DMA((2,))],
        compiler_params=_params(2),
        name="in_proj",
    )(xa, xb, norm1_g.reshape(DEPTH, 1, D), mod, mod, mod, mod, w_in_bf, q_gain, k_gain,
      sgu_norm_g.reshape(DEPTH, 1, SGU_WIDTH), _partner_matrix(q_heads), _partner_matrix(N_KV_HEADS),
      cos, sin)


def _qk(q, k):
    return lax.dot_general(q, k, (((1,), (1,)), ((), ())), preferred_element_type=F32)


def _attn_kernel(*refs, has_cache, n_batch, seq, tq):
    def with_ones(v):
        return jnp.concatenate([v, jnp.ones_like(v)], axis=1)

    if has_cache:
        q_ref, k_ref, v_ref, kc_ref, vc_ref, o_ref = refs
    else:
        q_ref, k_ref, v_ref, o_ref = refs
    for b in range(n_batch):
        rows_q = slice(b * tq, (b + 1) * tq)
        rows_k = slice(b * seq, (b + 1) * seq)
        for kv in range(N_KV_HEADS):
            kv_cols = slice(kv * HEAD_DIM, (kv + 1) * HEAD_DIM)
            k = k_ref[rows_k, kv_cols]
            v = with_ones(v_ref[rows_k, kv_cols])
            if has_cache:
                kc = kc_ref[:, kv_cols].astype(BF16)
                vc = with_ones(vc_ref[:, kv_cols].astype(BF16))
            for g in range(Q_PER_KV):
                head = kv * Q_PER_KV + g
                sl = slice(head * HEAD_DIM, (head + 1) * HEAD_DIM)
                q = q_ref[rows_q, sl]
                s = _qk(q, k)
                m = jnp.max(s, axis=-1, keepdims=True)
                if has_cache:
                    sc = _qk(q, kc)
                    m = jnp.maximum(m, jnp.max(sc, axis=-1, keepdims=True))
                o = jnp.dot(jnp.exp2(s - m).astype(BF16), v, preferred_element_type=F32)
                if has_cache:
                    o = o + jnp.dot(jnp.exp2(sc - m).astype(BF16), vc, preferred_element_type=F32)
                o_ref[rows_q, sl] = (o[:, :HEAD_DIM] / o[:, HEAD_DIM:]).astype(BF16)


def _attention(q, kb, vb, cache_k, cache_v, layer, *, batch, seq, row0):
    has_cache = cache_k is not None
    tq = min(T_Q, seq)
    nq = seq // tq
    n_batch = max(1, ATTN_ROWS // seq) if nq == 1 else 1
    q_spec = pl.BlockSpec((n_batch * tq, ATTN_WIDTH), lambda b, i: (row0 // (n_batch * tq) + b * nq + i, 0))
    kv_spec = pl.BlockSpec((n_batch * seq, KV_WIDTH), lambda b, i: (row0 // (n_batch * seq) + b, 0))
    in_specs = [q_spec, kv_spec, kv_spec]
    args = [q, kb, vb]
    if has_cache:
        c_spec = pl.BlockSpec((None, None, PAST_LEN, KV_WIDTH), lambda b, i: (b, layer, 0, 0))
        in_specs += [c_spec, c_spec]
        args += [cache_k.reshape(DEC_BATCH, DEPTH, PAST_LEN, KV_WIDTH),
                 cache_v.reshape(DEC_BATCH, DEPTH, PAST_LEN, KV_WIDTH)]
    return pl.pallas_call(
        functools.partial(_attn_kernel, has_cache=has_cache, n_batch=n_batch, seq=seq, tq=tq),
        grid=(batch // n_batch, nq),
        in_specs=in_specs,
        out_specs=pl.BlockSpec((n_batch * tq, ATTN_WIDTH), lambda b, i: (b * nq + i, 0)),
        out_shape=jax.ShapeDtypeStruct((batch * seq, ATTN_WIDTH), BF16),
        compiler_params=_params(2),
        name="attention_cached" if has_cache else "attention",
    )(*args)


def _sgu_merge_kernel(u_ref, gh_ref, ap_ref, as_ref, ws_ref, bs_ref, gn_ref, o_ref, sgu_ref):
    t = u_ref.shape[0]
    a = _pick(ap_ref, as_ref, 0).astype(F32)
    a = a * lax.rsqrt(jnp.mean(a * a, axis=-1, keepdims=True) + EPS) * gn_ref[:, :ATTN_WIDTH]
    o_ref[:, :ATTN_WIDTH] = a.astype(BF16)
    for h in range(N_SGU_HEADS):
        cs = slice(h * HEAD_DIM, (h + 1) * HEAD_DIM)
        w = ws_ref[h].astype(BF16)
        b = bs_ref[h]
        for c in range(t // CHUNK):
            rs = slice(c * CHUNK, (c + 1) * CHUNK)
            mixed = jnp.dot(w, gh_ref[rs, cs], preferred_element_type=F32) + b
            sgu_ref[rs, cs] = u_ref[rs, cs].astype(F32) * mixed
    s = sgu_ref[...]
    s = s * lax.rsqrt(jnp.mean(s * s, axis=-1, keepdims=True) + EPS) * gn_ref[:, ATTN_WIDTH:]
    o_ref[:, ATTN_WIDTH:] = s.astype(BF16)


def _sgu_merge(u, gh, attn_ctx, attn_lat, w_spatial, b_spatial, out_norm_g, layer):
    t = T_SGU
    bias = jnp.broadcast_to(b_spatial[:, :, :, None], (DEPTH, N_SGU_HEADS, CHUNK, HEAD_DIM))
    row = lambda w: pl.BlockSpec((t, w), lambda i: (i, 0))
    a_args, a_specs = _stream_in((attn_ctx, attn_lat), t, ATTN_WIDTH, lambda i: i, lambda i: 0)
    return pl.pallas_call(
        _sgu_merge_kernel,
        grid=(M // t,),
        in_specs=[row(SGU_WIDTH), row(SGU_WIDTH)] + a_specs + [
            pl.BlockSpec((None, N_SGU_HEADS, CHUNK, CHUNK), lambda i: (layer, 0, 0, 0)),
            pl.BlockSpec((None, N_SGU_HEADS, CHUNK, HEAD_DIM), lambda i: (layer, 0, 0, 0)),
            pl.BlockSpec((None, 1, D), lambda i: (layer, 0, 0))],
        out_specs=row(D),
        out_shape=jax.ShapeDtypeStruct((M, D), BF16),
        scratch_shapes=[pltpu.VMEM((t, SGU_WIDTH), F32)],
        compiler_params=_params(1),
        name="sgu_merge",
    )(u, gh, *a_args, w_spatial, bias, out_norm_g.reshape(DEPTH, 1, D))


def _mm_resid_kernel(a_ref, w_ref, x_ref, g_ref, o_ref, stage_ref, wbf_ref, sem, *, w_index):
    n = pl.program_id(0)
    tn = stage_ref.shape[1]

    def weight_copy(col_block):
        cols = pl.ds(pl.multiple_of(col_block * tn, LANES), tn)
        return pltpu.make_async_copy(w_ref.at[w_index, :, cols], stage_ref, sem.at[0])

    @pl.when(pl.program_id(1) == 0)
    def _():
        @pl.when(n == 0)
        def _():
            weight_copy(0).start()

        weight_copy(0).wait()
        _cast_rows(stage_ref, wbf_ref)

        @pl.when(n + 1 < pl.num_programs(0))
        def _():
            weight_copy(n + 1).start()

    acc = jnp.dot(a_ref[...], wbf_ref[...], preferred_element_type=F32)
    o_ref[...] = x_ref[...] + g_ref[...] * acc


def _mm_resid(a, w, x, mod, layer, w_index, gate_chunk, tm, tn):
    k = a.shape[1]
    row = lambda n, m: m
    col = lambda n, m: n
    return pl.pallas_call(
        functools.partial(_mm_resid_kernel, w_index=w_index),
        grid=(D // tn, M // tm),
        in_specs=[pl.BlockSpec((tm, k), lambda n, m: (m, 0)),
                  pl.BlockSpec(memory_space=pl.ANY),
                  pl.BlockSpec((tm, tn), lambda n, m: (m, n)),
                  _mod_spec(layer, gate_chunk, tm, row, col, tn=tn)],
        out_specs=pl.BlockSpec((tm, tn), lambda n, m: (m, n)),
        out_shape=jax.ShapeDtypeStruct((M, D), F32),
        scratch_shapes=[pltpu.VMEM((k, tn), F32), pltpu.VMEM((k, tn), BF16),
                        pltpu.SemaphoreType.DMA((1,))],
        compiler_params=_params(2),
        name="mm_resid",
    )(a, w, x, mod)


def _swiglu(a, b):
    return a * jax.nn.sigmoid(a) * b


def _ffn_gu_kernel(x_ref, wg_ref, wu_ref, o_ref, wgb_ref, wub_ref):
    @pl.when(pl.program_id(1) == 0)
    def _():
        _cast_rows(wg_ref, wgb_ref)
        _cast_rows(wu_ref, wub_ref)
    for r in range(x_ref.shape[0] // TM):
        rs = slice(r * TM, (r + 1) * TM)
        x = x_ref[rs, :]
        a = jnp.dot(x, wgb_ref[...], preferred_element_type=F32)
        b = jnp.dot(x, wub_ref[...], preferred_element_type=F32)
        o_ref[rs, :] = _swiglu(a, b).astype(BF16)


def _ffn_gate_up(h, w_gate, w_up, j):
    tm, tn = 2 * TM, 512
    w_spec = pl.BlockSpec((None, D, tn), lambda n, m: (j, 0, n))
    return pl.pallas_call(
        _ffn_gu_kernel,
        grid=(D_FF // tn, M // tm),
        in_specs=[pl.BlockSpec((tm, D), lambda n, m: (m, 0)), w_spec, w_spec],
        out_specs=pl.BlockSpec((tm, tn), lambda n, m: (m, n)),
        out_shape=jax.ShapeDtypeStruct((M, D_FF), BF16),
        scratch_shapes=[pltpu.VMEM((D, tn), BF16), pltpu.VMEM((D, tn), BF16)],
        compiler_params=_params(2),
        name="ffn_gate_up",
    )(h, w_gate, w_up)


def _route_meta(idx):
    t = T_MOE
    experts = jnp.arange(N_EXPERTS, dtype=I32)
    sel = idx.T[:, None, :] == experts[None, :, None]
    onehot = (sel[0] | sel[1]).astype(I32)
    csum = jnp.cumsum(onehot, axis=1)
    rank = csum - onehot
    count = csum[:, -1]
    ntile = (count + t - 1) // t
    tile_end = jnp.cumsum(ntile)
    tile_start = tile_end - ntile
    nused = tile_end[-1]
    row = (tile_start * t)[:, None] + rank
    pos = jnp.sum(jnp.where(sel, row[None], 0), axis=1)
    j = jnp.arange(NT_MOE, dtype=I32)
    te_raw = jnp.minimum(jnp.sum(j[:, None] >= tile_end[None, :], axis=1), N_EXPERTS - 1).astype(I32)
    te = jnp.where(j < nused, te_raw, te_raw[nused - 1])
    first = ((j == tile_start[te]) & (j < nused)).astype(I32)
    later = (ntile[None, :] > 0) & (experts[None, :] > te[:, None])
    nxt = jnp.min(jnp.where(later, experts[None, :], N_EXPERTS), axis=1)
    nxt = jnp.where(nxt == N_EXPERTS, -1, nxt).astype(I32)
    pad_start = tile_start * t + count
    pad_len = ntile * t - count
    tail = jnp.stack([nused * t, (NT_MOE - nused) * (t // TAIL_ROWS)])
    zinfo = jnp.concatenate([pad_start, pad_len, tail]).astype(I32)
    return pos.astype(I32), zinfo, (te, first, nxt, nused.reshape(1).astype(I32))


def _dispatch_kernel(p0_ref, p1_ref, z_ref, h_ref, xs_ref, zero_ref, sem, zsem):
    t = h_ref.shape[0]
    i = pl.program_id(0)
    base = i * t

    def clear_padding(start):
        def go(n, off):
            cp = pltpu.make_async_copy(zero_ref.at[pl.ds(0, n), :], xs_ref.at[pl.ds(off, n), :], zsem.at[0])
            cp.start() if start else cp.wait()

        for e in range(N_EXPERTS):
            off, ln = z_ref[e], z_ref[N_EXPERTS + e]
            end = off + ln
            for b in range(SUBLANE_BITS, PAD_BITS):
                @pl.when(((ln >> b) & 1) == 1)
                def _():
                    go(1 << b, pl.multiple_of(end - ((ln >> b) << b), SUBLANES))
            for k in range(SUBLANES - 1):
                @pl.when(k < (ln & (SUBLANES - 1)))
                def _():
                    go(1, off + k)
        tail0, n_tail = z_ref[2 * N_EXPERTS], z_ref[2 * N_EXPERTS + 1]

        def tail_body(k, carry):
            go(TAIL_ROWS, pl.multiple_of(tail0 + k * TAIL_ROWS, SUBLANES))
            return carry

        lax.fori_loop(0, n_tail, tail_body, 0)

    @pl.when(i == 0)
    def _():
        zero_ref[...] = jnp.zeros(zero_ref.shape, zero_ref.dtype)
        clear_padding(True)

    def issue(r, carry):
        src = h_ref.at[pl.ds(r, 1), :]
        pltpu.make_async_copy(src, xs_ref.at[pl.ds(p0_ref[base + r], 1), :], sem.at[0]).start()
        pltpu.make_async_copy(src, xs_ref.at[pl.ds(p1_ref[base + r], 1), :], sem.at[1]).start()
        return carry

    lax.fori_loop(0, t, issue, 0, unroll=8)
    pltpu.make_async_copy(h_ref, xs_ref.at[pl.ds(0, t), :], sem.at[0]).wait()
    pltpu.make_async_copy(h_ref, xs_ref.at[pl.ds(0, t), :], sem.at[1]).wait()

    @pl.when(i == 0)
    def _():
        clear_padding(False)


def _dispatch(h, pos0, pos1, zinfo):
    t = T_DISPATCH
    return pl.pallas_call(
        _dispatch_kernel,
        grid_spec=pltpu.PrefetchScalarGridSpec(
            num_scalar_prefetch=3,
            grid=(M // t,),
            in_specs=[pl.BlockSpec((t, D), lambda i, p0, p1, z: (i, 0))],
            out_specs=pl.BlockSpec(memory_space=pl.ANY),
            scratch_shapes=[pltpu.VMEM((ZERO_ROWS, D), F32),
                            pltpu.SemaphoreType.DMA((2,)), pltpu.SemaphoreType.DMA((1,))]),
        out_shape=jax.ShapeDtypeStruct((P_MOE, D), F32),
        compiler_params=_params(1),
        name="moe_dispatch",
    )(pos0, pos1, zinfo, h)


def _expert_weight_stream(w_refs, stage_refs, bf_refs, sem, te_ref, first_ref, nxt_ref, tn):
    c = pl.program_id(0)
    j = pl.program_id(1)
    nc = pl.num_programs(0)

    def copies(e, cc):
        col = pl.multiple_of(cc * tn, LANES)
        return [pltpu.make_async_copy(w.at[e, :, pl.ds(col, tn)], st, sem.at[k])
                for k, (w, st) in enumerate(zip(w_refs, stage_refs))]

    def start(e, cc):
        for cp in copies(e, cc):
            cp.start()

    @pl.when((c == 0) & (j == 0))
    def _():
        start(te_ref[0], 0)

    @pl.when(first_ref[j] == 1)
    def _():
        for cp in copies(0, 0):
            cp.wait()
        for st, bf in zip(stage_refs, bf_refs):
            _cast_rows(st, bf)
        ne = nxt_ref[j]

        @pl.when(ne >= 0)
        def _():
            start(ne, c)

        @pl.when((ne < 0) & (c + 1 < nc))
        def _():
            start(te_ref[0], c + 1)


def _gmm_gate_up_kernel(te_ref, first_ref, nxt_ref, nused_ref, xs_ref, wg_ref, wu_ref, o_ref,
                        sg_ref, su_ref, wgb_ref, wub_ref, sem):
    _expert_weight_stream((wg_ref, wu_ref), (sg_ref, su_ref), (wgb_ref, wub_ref), sem,
                          te_ref, first_ref, nxt_ref, TN_GU)

    @pl.when(pl.program_id(1) < nused_ref[0])
    def _():
        half = T_MOE // 2
        for r in range(2):
            rs = slice(r * half, (r + 1) * half)
            x = xs_ref[rs, :].astype(BF16)
            a = jnp.dot(x, wgb_ref[...], preferred_element_type=F32)
            b = jnp.dot(x, wub_ref[...], preferred_element_type=F32)
            o_ref[rs, :] = _swiglu(a, b).astype(BF16)

    @pl.when(pl.program_id(1) >= nused_ref[0])
    def _():
        o_ref[...] = jnp.zeros(o_ref.shape, o_ref.dtype)


def _gmm_down_kernel(te_ref, first_ref, nxt_ref, nused_ref, a_ref, wd_ref, o_ref,
                     sd_ref, wdb_ref, sem):
    _expert_weight_stream((wd_ref,), (sd_ref,), (wdb_ref,), sem, te_ref, first_ref, nxt_ref, TN_DN)

    @pl.when(pl.program_id(1) < nused_ref[0])
    def _():
        o_ref[...] = jnp.dot(a_ref[...], wdb_ref[...], preferred_element_type=F32)

    @pl.when(pl.program_id(1) >= nused_ref[0])
    def _():
        o_ref[...] = jnp.zeros(o_ref.shape, o_ref.dtype)


def _used_tile(j, nused):
    return jnp.minimum(j, nused[0] - 1)


def _gmm_gate_up(xs, w_gate, w_up, meta):
    te, first, nxt, nused = meta
    tn = TN_GU
    return pl.pallas_call(
        _gmm_gate_up_kernel,
        grid_spec=pltpu.PrefetchScalarGridSpec(
            num_scalar_prefetch=4,
            grid=(D_FF_EXPERT // tn, NT_MOE),
            in_specs=[pl.BlockSpec((T_MOE, D), lambda c, j, te, fi, nx, nu: (_used_tile(j, nu), 0)),
                      pl.BlockSpec(memory_space=pl.ANY),
                      pl.BlockSpec(memory_space=pl.ANY)],
            out_specs=pl.BlockSpec((T_MOE, tn), lambda c, j, te, fi, nx, nu: (j, c)),
            scratch_shapes=[pltpu.VMEM((D, tn), F32), pltpu.VMEM((D, tn), F32),
                            pltpu.VMEM((D, tn), BF16), pltpu.VMEM((D, tn), BF16),
                            pltpu.SemaphoreType.DMA((2,))]),
        out_shape=jax.ShapeDtypeStruct((P_MOE, D_FF_EXPERT), BF16),
        compiler_params=_params(2),
        name="moe_gate_up",
    )(te, first, nxt, nused, xs, w_gate, w_up)


def _gmm_down(act, w_down, meta):
    te, first, nxt, nused = meta
    tn = TN_DN
    return pl.pallas_call(
        _gmm_down_kernel,
        grid_spec=pltpu.PrefetchScalarGridSpec(
            num_scalar_prefetch=4,
            grid=(D // tn, NT_MOE),
            in_specs=[pl.BlockSpec((T_MOE, D_FF_EXPERT), lambda c, j, te, fi, nx, nu: (_used_tile(j, nu), 0)),
                      pl.BlockSpec(memory_space=pl.ANY)],
            out_specs=pl.BlockSpec((T_MOE, tn), lambda c, j, te, fi, nx, nu: (j, c)),
            scratch_shapes=[pltpu.VMEM((D_FF_EXPERT, tn), F32), pltpu.VMEM((D_FF_EXPERT, tn), BF16),
                            pltpu.SemaphoreType.DMA((1,))]),
        out_shape=jax.ShapeDtypeStruct((P_MOE, D), F32),
        compiler_params=_params(2),
        name="moe_down",
    )(te, first, nxt, nused, act, w_down)


def _combine_kernel(p0_ref, p1_ref, ys_ref, xa_ref, xb_ref, g_ref, w_ref, oc_ref, ol_ref,
                    a_ref, b_ref, sem):
    t = xa_ref.shape[0]
    i = pl.program_id(0)
    n = pl.num_programs(0)

    def issue(step, slot):
        base = step * t

        def body(r, carry):
            pltpu.make_async_copy(ys_ref.at[pl.ds(p0_ref[base + r], 1), :],
                                  a_ref.at[slot, pl.ds(r, 1), :], sem.at[0, slot]).start()
            pltpu.make_async_copy(ys_ref.at[pl.ds(p1_ref[base + r], 1), :],
                                  b_ref.at[slot, pl.ds(r, 1), :], sem.at[1, slot]).start()
            return carry

        lax.fori_loop(0, t, body, 0, unroll=8)

    @pl.when(i == 0)
    def _():
        issue(0, 0)

    @pl.when(i + 1 < n)
    def _():
        issue(i + 1, (i + 1) % 2)

    slot = i % 2
    pltpu.make_async_copy(ys_ref.at[pl.ds(0, t), :], a_ref.at[slot], sem.at[0, slot]).wait()
    pltpu.make_async_copy(ys_ref.at[pl.ds(0, t), :], b_ref.at[slot], sem.at[1, slot]).wait()
    w = w_ref[...]
    moe = w[:, 0:1] * a_ref[slot] + w[:, 1:2] * b_ref[slot]
    y = _pick(xa_ref, xb_ref, 0) + g_ref[...] * moe
    is_ctx = _is_ctx_tile(i, t)

    @pl.when(is_ctx)
    def _():
        oc_ref[...] = y

    @pl.when(jnp.logical_not(is_ctx))
    def _():
        ol_ref[...] = y


def _combine(ys, x, mod, layer, gate_chunk, wts, pos0, pos1):
    t = T_COMBINE
    n_ctx = MP // t
    row = lambda i, *_: i
    x_args, x_specs = _stream_in(x, t, D, row, lambda i, *_: 0)
    return pl.pallas_call(
        _combine_kernel,
        grid_spec=pltpu.PrefetchScalarGridSpec(
            num_scalar_prefetch=2,
            grid=(M // t,),
            in_specs=[pl.BlockSpec(memory_space=pl.ANY)] + x_specs + [
                _mod_spec(layer, gate_chunk, t, row),
                pl.BlockSpec((t, LANES), lambda i, p0, p1: (i, 0))],
            out_specs=[pl.BlockSpec((t, D), lambda i, p0, p1: (jnp.minimum(i, n_ctx - 1), 0)),
                       pl.BlockSpec((t, D), lambda i, p0, p1: (jnp.maximum(i - n_ctx, 0), 0))],
            scratch_shapes=[pltpu.VMEM((2, t, D), F32), pltpu.VMEM((2, t, D), F32),
                            pltpu.SemaphoreType.DMA((2, 2))]),
        out_shape=[jax.ShapeDtypeStruct((MP, D), F32), jax.ShapeDtypeStruct((MS, D), F32)],
        compiler_params=_params(1),
        name="moe_combine",
    )(pos0, pos1, ys, *x_args, mod, wts)


def _moe(x, h, idx, wts, mod, layer, w_gate, w_up, w_down):
    pos, zinfo, meta = _route_meta(idx[:, :TOP_K])
    pos0, pos1 = pos[0], pos[1]
    xs = _dispatch(h, pos0, pos1, zinfo)
    act = _gmm_gate_up(xs, w_gate, w_up, meta)
    ys = _gmm_down(act, w_down, meta)
    return _combine(ys, x, mod, layer, 5, wts, pos0, pos1)


def kernel(x_prompt, x_sample, cache_k, cache_v, c, c_ctx, w_ada, b_ada, norm1_g, norm2_g, w_in, q_norm_g, k_norm_g, sgu_norm_g, w_spatial, b_spatial, out_norm_g, w_out, ffn_w_gate, ffn_w_up, ffn_w_down, w_router, b_router, moe_w_gate, moe_w_up, moe_w_down):
    assert DEPTH == 2
    x = (x_prompt.reshape(MP, D), x_sample.reshape(MS, D))
    cond = jnp.concatenate([c_ctx[None, :], c, jnp.zeros((N_COND - 1 - DEC_BATCH, D), F32)], axis=0)
    mod = _modulation(cond, w_ada, b_ada).reshape(DEPTH, N_COND, 1, N_MOD * D)
    cos, sin = _rope_tables()
    w_in_bf = _cast_in_weights(w_in)

    new_k, new_v = [], []
    for i in range(DEPTH):
        q, kf, kb, vf, vb, u, gh = _in_projections(x, w_in_bf, mod, norm1_g, q_norm_g, k_norm_g,
                                                   sgu_norm_g, cos, sin, i)
        attn_ctx = _attention(q, kb, vb, None, None, i, batch=BATCH, seq=SEQ, row0=0)
        attn_lat = _attention(q, kb, vb, cache_k, cache_v, i, batch=DEC_BATCH, seq=DEC_SEQ, row0=MP)
        o = _sgu_merge(u, gh, attn_ctx, attn_lat, w_spatial, b_spatial, out_norm_g, i)
        j = i // 2
        if i % 2 == 0:
            x, h2 = _out_proj(o, w_out, x, mod, norm2_g, i)
            act = _ffn_gate_up(h2, ffn_w_gate, ffn_w_up, j)
            x = _mm_resid(act, ffn_w_down, x, mod, i, j, 5, TM, 512)
        else:
            x, h2, idx, wts = _out_proj(o, w_out, x, mod, norm2_g, i, router=(w_router[j], b_router[j]))
            x = _moe(x, h2, idx, wts, mod, i, moe_w_gate[j], moe_w_up[j], moe_w_down[j])
        new_k.append(kf.reshape(BATCH, SEQ, N_KV_HEADS, HEAD_DIM))
        new_v.append(vf.reshape(BATCH, SEQ, N_KV_HEADS, HEAD_DIM))

    y_prompt = x[0].reshape(BATCH, SEQ, D)
    y_sample = x[1].reshape(DEC_BATCH, DEC_SEQ, D)
    return (y_prompt, y_sample, jnp.stack(new_k, axis=1), jnp.stack(new_v, axis=1))
```

```python
import functools

import jax
import jax.numpy as jnp
from jax import lax
from jax.experimental import pallas as pl
from jax.experimental.pallas import tpu as pltpu

F32 = jnp.float32
BF16 = jnp.bfloat16
I32 = jnp.int32

D = 2048
BATCH, SEQ = 16, 256
DEC_BATCH, DEC_SEQ = 4, 2048
PAST_LEN = 256
DEPTH = 2
GRID_W = 64
CHUNK = 128
HEAD_DIM = 128
N_Q_HEADS, N_KV_HEADS = 8, 2
Q_PER_KV = N_Q_HEADS // N_KV_HEADS
ATTN_WIDTH = N_Q_HEADS * HEAD_DIM
KV_WIDTH = N_KV_HEADS * HEAD_DIM
N_SGU_HEADS = 8
SGU_WIDTH = N_SGU_HEADS * HEAD_DIM
IN_WIDTH = ATTN_WIDTH + 2 * KV_WIDTH + 2 * SGU_WIDTH
ROPE_THETA = 10000.0
ROPE_AXIS_DIM = HEAD_DIM // 2
D_FF = 5632
N_EXPERTS = 8
TOP_K = 2
D_FF_EXPERT = 2816
N_MOD = 6
EPS = 1e-6
ATTN_SCALE = HEAD_DIM ** -0.5
LOG2_E = 1.4426950408889634

MP = BATCH * SEQ
MS = DEC_BATCH * DEC_SEQ
M = MP + MS
N_COND = 8
LANES = 128
SUBLANES = 8
SUBLANE_BITS = 3

VMEM_LIMIT = 56 * 1024 * 1024

TM = 1024
TN_IN = 2 * KV_WIDTH
T_NORM = 512
T_Q = 512
ATTN_ROWS = 1024
T_MOE = 512
P_MOE = M * TOP_K + N_EXPERTS * T_MOE
NT_MOE = P_MOE // T_MOE
TN_GU = D_FF_EXPERT // 2
TN_DN = D
T_DISPATCH = 1024
T_COMBINE = 256
T_SGU = 1024
ZERO_ROWS = T_MOE // 2
PAD_BITS = ZERO_ROWS.bit_length()


def _params(n_axes):
    return pltpu.CompilerParams(dimension_semantics=("arbitrary",) * n_axes,
                                vmem_limit_bytes=VMEM_LIMIT)


def _cond_row(i, t):
    return jnp.where(i < MP // t, 0, 1 + (i - MP // t) // (DEC_SEQ // t))


def _is_ctx_tile(i, t):
    return i < MP // t


def _stream_in(x, t, width, row_of, col_of):
    n_ctx = MP // t
    pair = isinstance(x, tuple)
    base = 0 if pair else n_ctx
    ctx = pl.BlockSpec((t, width), lambda *g: (jnp.minimum(row_of(*g), n_ctx - 1), col_of(*g)))
    lat = pl.BlockSpec((t, width), lambda *g: (base + jnp.maximum(row_of(*g) - n_ctx, 0), col_of(*g)))
    return (list(x) if pair else [x, x]), [ctx, lat]


def _mod_spec(layer, chunk, t, row_of, col_of=None, tn=D):
    per = D // tn

    def index_map(*g):
        col = chunk * per + (col_of(*g) if col_of is not None else 0)
        return (layer, _cond_row(row_of(*g), t), 0, col)

    return pl.BlockSpec((None, None, 1, tn), index_map)


def _ada_kernel(c_ref, w_ref, b_ref, o_ref):
    c = c_ref[...]
    s = (c * jax.nn.sigmoid(c)).astype(BF16)
    o_ref[...] = jnp.dot(s, w_ref[...].astype(BF16), preferred_element_type=F32) + b_ref[...]


def _modulation(cond, w_ada, b_ada):
    tn = 1024
    width = N_MOD * D
    return pl.pallas_call(
        _ada_kernel,
        grid=(DEPTH, width // tn),
        in_specs=[pl.BlockSpec((N_COND, D), lambda l, n: (0, 0)),
                  pl.BlockSpec((None, D, tn), lambda l, n: (l, 0, n)),
                  pl.BlockSpec((None, 1, tn), lambda l, n: (l, 0, n))],
        out_specs=pl.BlockSpec((None, N_COND, tn), lambda l, n: (l, 0, n)),
        out_shape=jax.ShapeDtypeStruct((DEPTH, N_COND, width), F32),
        compiler_params=_params(2),
        name="modulation",
    )(cond, w_ada, b_ada.reshape(DEPTH, 1, width))


def _modulated_norm(x, g, sc, sh):
    y = x * lax.rsqrt(jnp.mean(x * x, axis=-1, keepdims=True) + EPS)
    return y * (g * (1.0 + sc)) + sh


def _pick(xa_ref, xb_ref, axis):
    t = xa_ref.shape[0]
    return jnp.where(_is_ctx_tile(pl.program_id(axis), t), xa_ref[...], xb_ref[...])


def _stream_tile_copy(xa_ref, xb_ref, lat_row0, buf_ref, sem, tile, slot, start):
    t = buf_ref.shape[1]

    def copy(src_ref, row):
        return pltpu.make_async_copy(src_ref.at[pl.ds(pl.multiple_of(row, t), t), :],
                                     buf_ref.at[slot], sem.at[slot])

    if not start:
        copy(xa_ref, 0).wait()
        return
    is_ctx = _is_ctx_tile(tile, t)

    @pl.when(is_ctx)
    def _():
        copy(xa_ref, tile * t).start()

    @pl.when(jnp.logical_not(is_ctx))
    def _():
        copy(xb_ref, lat_row0 + (tile - MP // t) * t).start()


def _next_stream_tile(xa_ref, xb_ref, lat_row0, buf_ref, sem, tile, n_tiles):
    @pl.when(tile == 0)
    def _():
        _stream_tile_copy(xa_ref, xb_ref, lat_row0, buf_ref, sem, 0, 0, True)

    @pl.when(tile + 1 < n_tiles)
    def _():
        _stream_tile_copy(xa_ref, xb_ref, lat_row0, buf_ref, sem, tile + 1, (tile + 1) % 2, True)

    slot = tile % 2
    _stream_tile_copy(xa_ref, xb_ref, lat_row0, buf_ref, sem, tile, slot, False)
    return slot


def _stream_hbm(x):
    return (x[0], x[1], 0) if isinstance(x, tuple) else (x, x, MP)


def _split_bf16(a):
    hi = a.astype(BF16)
    return hi, (a - hi.astype(F32)).astype(BF16)


def _route_top2(h, wr, br, idx_ref, wt_ref):
    h_hi, h_lo = _split_bf16(h)
    w_hi, w_lo = _split_bf16(wr)
    logits = (jnp.dot(h_hi, w_hi, preferred_element_type=F32)
              + jnp.dot(h_lo, w_hi, preferred_element_type=F32)
              + jnp.dot(h_hi, w_lo, preferred_element_type=F32)) + br
    lane = lax.broadcasted_iota(I32, logits.shape, 1)
    neg = jnp.float32(-jnp.inf)
    lg = jnp.where(lane < N_EXPERTS, logits, neg)
    m1 = jnp.max(lg, axis=-1, keepdims=True)
    i1 = jnp.min(jnp.where(lg == m1, lane, LANES), axis=-1, keepdims=True)
    lg2 = jnp.where(lane == i1, neg, lg)
    m2 = jnp.max(lg2, axis=-1, keepdims=True)
    i2 = jnp.min(jnp.where(lg2 == m2, lane, LANES), axis=-1, keepdims=True)
    e = jnp.exp(m2 - m1)
    w1 = 1.0 / (1.0 + e)
    w2 = e / (1.0 + e)
    idx_ref[...] = jnp.where(lane == 0, i1, jnp.where(lane == 1, i2, 0))
    wt_ref[...] = jnp.where(lane == 0, w1, jnp.where(lane == 1, w2, 0.0))


W_PIECE = 512


def _out_proj_kernel(*refs, layer, lat_row0, route):
    if route:
        (o_ref, w_ref, xa_ref, xb_ref, gate_ref, g_ref, sc_ref, sh_ref, wr_ref, br_ref,
         xn_ref, h_ref, idx_ref, wt_ref, stage_ref, wbf_ref, xbuf_ref, wsem, xsem) = refs
    else:
        (o_ref, w_ref, xa_ref, xb_ref, gate_ref, g_ref, sc_ref, sh_ref,
         xn_ref, h_ref, stage_ref, wbf_ref, xbuf_ref, wsem, xsem) = refs
    i = pl.program_id(0)

    @pl.when(i == 0)
    def _():
        for p in range(D // W_PIECE):
            cols = pl.ds(p * W_PIECE, W_PIECE)
            cp = pltpu.make_async_copy(w_ref.at[layer, :, cols], stage_ref, wsem.at[0])
            cp.start()
            cp.wait()
            _cast_rows(stage_ref, wbf_ref.at[:, cols])

    slot = _next_stream_tile(xa_ref, xb_ref, lat_row0, xbuf_ref, xsem, i, pl.num_programs(0))
    acc = jnp.dot(o_ref[...], wbf_ref[...], preferred_element_type=F32)
    x_new = xbuf_ref[slot] + gate_ref[...] * acc
    xn_ref[...] = x_new
    h = _modulated_norm(x_new, g_ref[...], sc_ref[...], sh_ref[...])
    if route:
        h_ref[...] = h
        _route_top2(h, wr_ref[...], br_ref[...], idx_ref, wt_ref)
    else:
        h_ref[...] = h.astype(BF16)


def _out_proj(o, w_out, x, mod, norm2_g, layer, router=None):
    t = T_NORM
    row = lambda i: i
    xa, xb, lat_row0 = _stream_hbm(x)
    route = router is not None
    anyspace = pl.BlockSpec(memory_space=pl.ANY)
    rows = lambda w: pl.BlockSpec((t, w), lambda i: (i, 0))
    in_specs = [rows(D), anyspace, anyspace, anyspace,
                _mod_spec(layer, 2, t, row),
                pl.BlockSpec((None, 1, D), lambda i: (layer, 0, 0)),
                _mod_spec(layer, 4, t, row), _mod_spec(layer, 3, t, row)]
    args = [o, w_out, xa, xb, mod, norm2_g.reshape(DEPTH, 1, D), mod, mod]
    out_specs = [rows(D), rows(D)]
    out_shape = [jax.ShapeDtypeStruct((M, D), F32), jax.ShapeDtypeStruct((M, D), F32 if route else BF16)]
    if route:
        w_router, b_router = router
        args += [jnp.zeros((D, LANES), F32).at[:, :N_EXPERTS].set(w_router),
                 jnp.zeros((1, LANES), F32).at[0, :N_EXPERTS].set(b_router)]
        in_specs += [pl.BlockSpec((D, LANES), lambda i: (0, 0)), pl.BlockSpec((1, LANES), lambda i: (0, 0))]
        out_specs += [rows(LANES), rows(LANES)]
        out_shape += [jax.ShapeDtypeStruct((M, LANES), I32), jax.ShapeDtypeStruct((M, LANES), F32)]
    return pl.pallas_call(
        functools.partial(_out_proj_kernel, layer=layer, lat_row0=lat_row0, route=route),
        grid=(M // t,),
        in_specs=in_specs,
        out_specs=out_specs,
        out_shape=out_shape,
        scratch_shapes=[pltpu.VMEM((D, W_PIECE), F32), pltpu.VMEM((D, D), BF16),
                        pltpu.VMEM((2, t, D), F32),
                        pltpu.SemaphoreType.DMA((1,)), pltpu.SemaphoreType.DMA((2,))],
        compiler_params=_params(1),
        name="out_proj_router" if route else "out_proj",
    )(*args)


CAST_ROWS = 256


def _cast_rows(src_ref, dst_ref):
    def body(r, carry):
        rs = pl.ds(pl.multiple_of(r * CAST_ROWS, CAST_ROWS), CAST_ROWS)
        dst_ref[rs, :] = src_ref[rs, :].astype(BF16)
        return carry

    lax.fori_loop(0, src_ref.shape[0] // CAST_ROWS, body, 0)


def _head_rms(a):
    return lax.rsqrt(jnp.mean(a * a, axis=-1, keepdims=True) + EPS)


def _rope_partner(ag, perm):
    hi, lo = _split_bf16(ag)
    return (jnp.dot(hi, perm, preferred_element_type=F32)
            + jnp.dot(lo, perm, preferred_element_type=F32))


def _cast_kernel(w_ref, o_ref):
    o_ref[...] = w_ref[...].astype(BF16)


def _cast_in_weights(w_in):
    spec = pl.BlockSpec((None, D, TN_IN), lambda l, n: (l, 0, n))
    return pl.pallas_call(
        _cast_kernel,
        grid=(DEPTH, IN_WIDTH // TN_IN),
        in_specs=[spec],
        out_specs=spec,
        out_shape=jax.ShapeDtypeStruct(w_in.shape, BF16),
        compiler_params=_params(2),
        name="cast_w_in",
    )(w_in)


N_Q_TILES = ATTN_WIDTH // TN_IN
KV_TILE = N_Q_TILES
U_TILE0 = KV_TILE + 1
G_TILE0 = U_TILE0 + SGU_WIDTH // TN_IN
N_IN_TILES = IN_WIDTH // TN_IN


def _in_proj_kernel(xa_ref, xb_ref, n1_ref, sc0_ref, sh0_ref, sc1_ref, sh1_ref, w_ref,
                    qg_ref, kg_ref, sg_ref, pq_ref, pk_ref, cos_ref, sin_ref,
                    q_ref, kf_ref, kb_ref, vf_ref, vb_ref, u_ref, gh_ref,
                    xbuf_ref, h_ref, xsem, *, lat_row0):
    m = pl.program_id(0)
    n = pl.program_id(1)
    nm = pl.num_programs(0)
    cur = m % 2
    nxt = (m + 1) % 2
    has_next = m + 1 < nm
    fetch = functools.partial(_stream_tile_copy, xa_ref, xb_ref, lat_row0, xbuf_ref, xsem)

    def norm_rows(slot, rows, sc_ref, sh_ref):
        x = xbuf_ref[slot, rows, :]
        h_ref[slot, rows, :] = _modulated_norm(x, n1_ref[...], sc_ref[...], sh_ref[...]).astype(BF16)

    @pl.when((m == 0) & (n == 0))
    def _():
        fetch(0, 0, True)
        fetch(0, 0, False)
        norm_rows(0, slice(None), sc0_ref, sh0_ref)

    @pl.when((n == 0) & has_next)
    def _():
        fetch(m + 1, nxt, True)

    def matmul():
        return jnp.dot(h_ref[cur], w_ref[...], preferred_element_type=F32)

    for rope in (False, True):
        positioned = jnp.logical_not(_is_ctx_tile(m, TM)) if rope else _is_ctx_tile(m, TM)

        @pl.when((n < N_Q_TILES) & positioned)
        def _():
            acc = matmul()
            ag = acc * qg_ref[...]
            if rope:
                partner = _rope_partner(ag, pq_ref[...])
                cos, sin = cos_ref[...], sin_ref[...]
            for h in range(TN_IN // HEAD_DIM):
                sl = slice(h * HEAD_DIM, (h + 1) * HEAD_DIM)
                r = _head_rms(acc[:, sl]) * (ATTN_SCALE * LOG2_E)
                qh = ag[:, sl] * cos + partner[:, sl] * sin if rope else ag[:, sl]
                q_ref[:, sl] = (qh * r).astype(BF16)

        @pl.when((n == KV_TILE) & positioned)
        def _():
            acc = matmul()
            k = acc[:, :KV_WIDTH]
            ag = k * kg_ref[...]
            if rope:
                partner = _rope_partner(ag, pk_ref[...])
                cos, sin = cos_ref[...], sin_ref[...]
            for h in range(N_KV_HEADS):
                sl = slice(h * HEAD_DIM, (h + 1) * HEAD_DIM)
                r = _head_rms(k[:, sl])
                kf = ag[:, sl] * r
                if not rope:
                    kf_ref[:, :, h, :] = kf.reshape(TM // SEQ, SEQ, HEAD_DIM)
                kb_ref[:, sl] = ((ag[:, sl] * cos + partner[:, sl] * sin) * r if rope else kf).astype(BF16)
            v = acc[:, KV_WIDTH:]
            if not rope:
                for h in range(N_KV_HEADS):
                    vh = v[:, h * HEAD_DIM:(h + 1) * HEAD_DIM]
                    vf_ref[:, :, h, :] = vh.reshape(TM // SEQ, SEQ, HEAD_DIM)
            vb_ref[...] = v.astype(BF16)

    half = TM // (G_TILE0 - U_TILE0)
    for k in range(G_TILE0 - U_TILE0):
        for parity in range(2):
            @pl.when((n == U_TILE0 + k) & has_next & (cur == parity))
            def _():
                if k == 0:
                    fetch(m + 1, 1 - parity, False)
                u_ref[...] = jnp.dot(h_ref[parity], w_ref[...], preferred_element_type=F32).astype(BF16)
                norm_rows(1 - parity, slice(k * half, (k + 1) * half), sc1_ref, sh1_ref)

        @pl.when((n == U_TILE0 + k) & jnp.logical_not(has_next))
        def _():
            u_ref[...] = matmul().astype(BF16)

    @pl.when(n >= G_TILE0)
    def _():
        acc = matmul()
        for h in range(TN_IN // HEAD_DIM):
            sl = slice(h * HEAD_DIM, (h + 1) * HEAD_DIM)
            a = acc[:, sl]
            gh_ref[:, sl] = (a * _head_rms(a) * sg_ref[:, sl]).astype(BF16)


def _rope_tables():
    n_rows = DEC_SEQ // GRID_W
    rows = jnp.broadcast_to(jnp.arange(n_rows)[:, None], (n_rows, GRID_W)).reshape(-1)
    cols = jnp.broadcast_to(jnp.arange(GRID_W)[None, :], (n_rows, GRID_W)).reshape(-1)
    inv = ROPE_THETA ** (-jnp.arange(0, ROPE_AXIS_DIM, 2, dtype=F32) / ROPE_AXIS_DIM)
    ang_r = rows.astype(F32)[:, None] * inv
    ang_c = cols.astype(F32)[:, None] * inv
    cos = jnp.concatenate([jnp.cos(ang_r), jnp.cos(ang_r), jnp.cos(ang_c), jnp.cos(ang_c)], axis=1)
    sin = jnp.concatenate([-jnp.sin(ang_r), jnp.sin(ang_r), -jnp.sin(ang_c), jnp.sin(ang_c)], axis=1)
    return cos, sin


def _partner_matrix(n_heads):
    w = n_heads * HEAD_DIM
    quarter = ROPE_AXIS_DIM // 2
    j = jnp.arange(w)
    partner = jnp.where((j % ROPE_AXIS_DIM) < quarter, j + quarter, j - quarter)
    return (jnp.arange(w)[:, None] == partner[None, :]).astype(BF16)


def _rope_block(m):
    return jnp.maximum(m - MP // TM, 0) % (DEC_SEQ // TM)


def _in_projections(x, w_in_bf, mod, norm1_g, q_norm_g, k_norm_g, sgu_norm_g, cos, sin, layer):
    tn = TN_IN
    xa, xb, lat_row0 = _stream_hbm(x)
    row = lambda m, n: m
    next_row = lambda m, n: jnp.minimum(m + 1, M // TM - 1)
    anyspace = pl.BlockSpec(memory_space=pl.ANY)
    const = lambda shape: pl.BlockSpec(shape, lambda m, n: (0,) * len(shape))
    rope_spec = pl.BlockSpec((TM, HEAD_DIM), lambda m, n: (_rope_block(m), 0))
    q_heads = tn // HEAD_DIM
    q_gain = jnp.tile(q_norm_g[layer], q_heads)[None, :]
    k_gain = jnp.tile(k_norm_g[layer], N_KV_HEADS)[None, :]
    g_tile = lambda n: jnp.clip(n - G_TILE0, 0, SGU_WIDTH // tn - 1)
    kv_out = pl.BlockSpec((TM, KV_WIDTH), lambda m, n: (m, 0))
    kv_ctx = pl.BlockSpec((TM // SEQ, SEQ, N_KV_HEADS, HEAD_DIM),
                          lambda m, n: (jnp.minimum(m, MP // TM - 1), 0, 0, 0))
    cache_shape = jax.ShapeDtypeStruct((BATCH, SEQ, N_KV_HEADS, HEAD_DIM), F32)
    kv_shape = lambda rows, dt: jax.ShapeDtypeStruct((rows, KV_WIDTH), dt)
    return pl.pallas_call(
        functools.partial(_in_proj_kernel, lat_row0=lat_row0),
        grid=(M // TM, N_IN_TILES),
        in_specs=[anyspace, anyspace,
                  pl.BlockSpec((None, 1, D), lambda m, n: (layer, 0, 0)),
                  _mod_spec(layer, 1, TM, row), _mod_spec(layer, 0, TM, row),
                  _mod_spec(layer, 1, TM, next_row), _mod_spec(layer, 0, TM, next_row),
                  pl.BlockSpec((None, D, tn), lambda m, n: (layer, 0, n)),
                  const((1, tn)), const((1, KV_WIDTH)),
                  pl.BlockSpec((None, 1, tn), lambda m, n: (layer, 0, g_tile(n))),
                  const((tn, tn)), const((KV_WIDTH, KV_WIDTH)), rope_spec, rope_spec],
        out_specs=[pl.BlockSpec((TM, tn), lambda m, n: (m, jnp.minimum(n, N_Q_TILES - 1))),
                   kv_ctx, kv_out, kv_ctx, kv_out,
                   pl.BlockSpec((TM, tn), lambda m, n: (m, jnp.clip(n - U_TILE0, 0, SGU_WIDTH // tn - 1))),
                   pl.BlockSpec((TM, tn), lambda m, n: (m, g_tile(n)))],
        out_shape=[jax.ShapeDtypeStruct((M, ATTN_WIDTH), BF16),
                   cache_shape, kv_shape(M, BF16), cache_shape, kv_shape(M, BF16),
                   jax.ShapeDtypeStruct((M, SGU_WIDTH), BF16),
                   jax.ShapeDtypeStruct((M, SGU_WIDTH), BF16)],
        scratch_shapes=[pltpu.VMEM((2, TM, D), F32), pltpu.VMEM((2, TM, D), BF16),
                        pltpu.SemaphoreType.DMA((2,))],
        compiler_params=_params(2),
        name="in_proj",
    )(xa, xb, norm1_g.reshape(DEPTH, 1, D), mod, mod, mod, mod, w_in_bf, q_gain, k_gain,
      sgu_norm_g.reshape(DEPTH, 1, SGU_WIDTH), _partner_matrix(q_heads), _partner_matrix(N_KV_HEADS),
      cos, sin)


def _qk(q, k):
    return lax.dot_general(q, k, (((1,), (1,)), ((), ())), preferred_element_type=F32)


def _attn_kernel(*refs, has_cache, n_batch, seq, tq):
    def with_ones(v):
        return jnp.concatenate([v, jnp.ones_like(v)], axis=1)

    if has_cache:
        q_ref, k_ref, v_ref, kc_ref, vc_ref, o_ref = refs
    else:
        q_ref, k_ref, v_ref, o_ref = refs
    for b in range(n_batch):
        rows_q = slice(b * tq, (b + 1) * tq)
        rows_k = slice(b * seq, (b + 1) * seq)
        for kv in range(N_KV_HEADS):
            kv_cols = slice(kv * HEAD_DIM, (kv + 1) * HEAD_DIM)
            k = k_ref[rows_k, kv_cols]
            v = with_ones(v_ref[rows_k, kv_cols])
            if has_cache:
                kc = kc_ref[:, kv_cols].astype(BF16)
                vc = with_ones(vc_ref[:, kv_cols].astype(BF16))
            for g in range(Q_PER_KV):
                head = kv * Q_PER_KV + g
                sl = slice(head * HEAD_DIM, (head + 1) * HEAD_DIM)
                q = q_ref[rows_q, sl]
                s = _qk(q, k)
                m = jnp.max(s, axis=-1, keepdims=True)
                if has_cache:
                    sc = _qk(q, kc)
                    m = jnp.maximum(m, jnp.max(sc, axis=-1, keepdims=True))
                o = jnp.dot(jnp.exp2(s - m).astype(BF16), v, preferred_element_type=F32)
                if has_cache:
                    o = o + jnp.dot(jnp.exp2(sc - m).astype(BF16), vc, preferred_element_type=F32)
                o_ref[rows_q, sl] = (o[:, :HEAD_DIM] / o[:, HEAD_DIM:]).astype(BF16)


def _attention(q, kb, vb, cache_k, cache_v, layer, *, batch, seq, row0):
    has_cache = cache_k is not None
    tq = min(T_Q, seq)
    nq = seq // tq
    n_batch = max(1, ATTN_ROWS // seq) if nq == 1 else 1
    q_spec = pl.BlockSpec((n_batch * tq, ATTN_WIDTH), lambda b, i: (row0 // (n_batch * tq) + b * nq + i, 0))
    kv_spec = pl.BlockSpec((n_batch * seq, KV_WIDTH), lambda b, i: (row0 // (n_batch * seq) + b, 0))
    in_specs = [q_spec, kv_spec, kv_spec]
    args = [q, kb, vb]
    if has_cache:
        c_spec = pl.BlockSpec((None, None, PAST_LEN, KV_WIDTH), lambda b, i: (b, layer, 0, 0))
        in_specs += [c_spec, c_spec]
        args += [cache_k.reshape(DEC_BATCH, DEPTH, PAST_LEN, KV_WIDTH),
                 cache_v.reshape(DEC_BATCH, DEPTH, PAST_LEN, KV_WIDTH)]
    return pl.pallas_call(
        functools.partial(_attn_kernel, has_cache=has_cache, n_batch=n_batch, seq=seq, tq=tq),
        grid=(batch // n_batch, nq),
        in_specs=in_specs,
        out_specs=pl.BlockSpec((n_batch * tq, ATTN_WIDTH), lambda b, i: (b * nq + i, 0)),
        out_shape=jax.ShapeDtypeStruct((batch * seq, ATTN_WIDTH), BF16),
        compiler_params=_params(2),
        name="attention_cached" if has_cache else "attention",
    )(*args)


def _sgu_merge_kernel(u_ref, gh_ref, ap_ref, as_ref, ws_ref, bs_ref, gn_ref, o_ref, sgu_ref):
    t = u_ref.shape[0]
    a = _pick(ap_ref, as_ref, 0).astype(F32)
    a = a * lax.rsqrt(jnp.mean(a * a, axis=-1, keepdims=True) + EPS) * gn_ref[:, :ATTN_WIDTH]
    o_ref[:, :ATTN_WIDTH] = a.astype(BF16)
    for h in range(N_SGU_HEADS):
        cs = slice(h * HEAD_DIM, (h + 1) * HEAD_DIM)
        w = ws_ref[h].astype(BF16)
        b = bs_ref[h]
        for c in range(t // CHUNK):
            rs = slice(c * CHUNK, (c + 1) * CHUNK)
            mixed = jnp.dot(w, gh_ref[rs, cs], preferred_element_type=F32) + b
            sgu_ref[rs, cs] = u_ref[rs, cs].astype(F32) * mixed
    s = sgu_ref[...]
    s = s * lax.rsqrt(jnp.mean(s * s, axis=-1, keepdims=True) + EPS) * gn_ref[:, ATTN_WIDTH:]
    o_ref[:, ATTN_WIDTH:] = s.astype(BF16)


def _sgu_merge(u, gh, attn_ctx, attn_lat, w_spatial, b_spatial, out_norm_g, layer):
    t = T_SGU
    bias = jnp.broadcast_to(b_spatial[:, :, :, None], (DEPTH, N_SGU_HEADS, CHUNK, HEAD_DIM))
    row = lambda w: pl.BlockSpec((t, w), lambda i: (i, 0))
    a_args, a_specs = _stream_in((attn_ctx, attn_lat), t, ATTN_WIDTH, lambda i: i, lambda i: 0)
    return pl.pallas_call(
        _sgu_merge_kernel,
        grid=(M // t,),
        in_specs=[row(SGU_WIDTH), row(SGU_WIDTH)] + a_specs + [
            pl.BlockSpec((None, N_SGU_HEADS, CHUNK, CHUNK), lambda i: (layer, 0, 0, 0)),
            pl.BlockSpec((None, N_SGU_HEADS, CHUNK, HEAD_DIM), lambda i: (layer, 0, 0, 0)),
            pl.BlockSpec((None, 1, D), lambda i: (layer, 0, 0))],
        out_specs=row(D),
        out_shape=jax.ShapeDtypeStruct((M, D), BF16),
        scratch_shapes=[pltpu.VMEM((t, SGU_WIDTH), F32)],
        compiler_params=_params(1),
        name="sgu_merge",
    )(u, gh, *a_args, w_spatial, bias, out_norm_g.reshape(DEPTH, 1, D))


def _mm_resid_kernel(a_ref, w_ref, x_ref, g_ref, o_ref, stage_ref, wbf_ref, sem, *, w_index):
    n = pl.program_id(0)
    tn = stage_ref.shape[1]

    def weight_copy(col_block):
        cols = pl.ds(pl.multiple_of(col_block * tn, LANES), tn)
        return pltpu.make_async_copy(w_ref.at[w_index, :, cols], stage_ref, sem.at[0])

    @pl.when(pl.program_id(1) == 0)
    def _():
        @pl.when(n == 0)
        def _():
            weight_copy(0).start()

        weight_copy(0).wait()
        _cast_rows(stage_ref, wbf_ref)

        @pl.when(n + 1 < pl.num_programs(0))
        def _():
            weight_copy(n + 1).start()

    acc = jnp.dot(a_ref[...], wbf_ref[...], preferred_element_type=F32)
    o_ref[...] = x_ref[...] + g_ref[...] * acc


def _mm_resid(a, w, x, mod, layer, w_index, gate_chunk, tm, tn):
    k = a.shape[1]
    row = lambda n, m: m
    col = lambda n, m: n
    return pl.pallas_call(
        functools.partial(_mm_resid_kernel, w_index=w_index),
        grid=(D // tn, M // tm),
        in_specs=[pl.BlockSpec((tm, k), lambda n, m: (m, 0)),
                  pl.BlockSpec(memory_space=pl.ANY),
                  pl.BlockSpec((tm, tn), lambda n, m: (m, n)),
                  _mod_spec(layer, gate_chunk, tm, row, col, tn=tn)],
        out_specs=pl.BlockSpec((tm, tn), lambda n, m: (m, n)),
        out_shape=jax.ShapeDtypeStruct((M, D), F32),
        scratch_shapes=[pltpu.VMEM((k, tn), F32), pltpu.VMEM((k, tn), BF16),
                        pltpu.SemaphoreType.DMA((1,))],
        compiler_params=_params(2),
        name="mm_resid",
    )(a, w, x, mod)


def _swiglu(a, b):
    return a * jax.nn.sigmoid(a) * b


def _ffn_gu_kernel(x_ref, wg_ref, wu_ref, o_ref, wgb_ref, wub_ref):
    @pl.when(pl.program_id(1) == 0)
    def _():
        _cast_rows(wg_ref, wgb_ref)
        _cast_rows(wu_ref, wub_ref)
    for r in range(x_ref.shape[0] // TM):
        rs = slice(r * TM, (r + 1) * TM)
        x = x_ref[rs, :]
        a = jnp.dot(x, wgb_ref[...], preferred_element_type=F32)
        b = jnp.dot(x, wub_ref[...], preferred_element_type=F32)
        o_ref[rs, :] = _swiglu(a, b).astype(BF16)


def _ffn_gate_up(h, w_gate, w_up, j):
    tm, tn = 2 * TM, 512
    w_spec = pl.BlockSpec((None, D, tn), lambda n, m: (j, 0, n))
    return pl.pallas_call(
        _ffn_gu_kernel,
        grid=(D_FF // tn, M // tm),
        in_specs=[pl.BlockSpec((tm, D), lambda n, m: (m, 0)), w_spec, w_spec],
        out_specs=pl.BlockSpec((tm, tn), lambda n, m: (m, n)),
        out_shape=jax.ShapeDtypeStruct((M, D_FF), BF16),
        scratch_shapes=[pltpu.VMEM((D, tn), BF16), pltpu.VMEM((D, tn), BF16)],
        compiler_params=_params(2),
        name="ffn_gate_up",
    )(h, w_gate, w_up)


def _route_meta(idx):
    t = T_MOE
    experts = jnp.arange(N_EXPERTS, dtype=I32)
    sel = idx.T[:, None, :] == experts[None, :, None]
    onehot = (sel[0] | sel[1]).astype(I32)
    csum = jnp.cumsum(onehot, axis=1)
    rank = csum - onehot
    count = csum[:, -1]
    ntile = (count + t - 1) // t
    tile_end = jnp.cumsum(ntile)
    tile_start = tile_end - ntile
    nused = tile_end[-1]
    row = (tile_start * t)[:, None] + rank
    pos = jnp.sum(jnp.where(sel, row[None], 0), axis=1)
    j = jnp.arange(NT_MOE, dtype=I32)
    te_raw = jnp.minimum(jnp.sum(j[:, None] >= tile_end[None, :], axis=1), N_EXPERTS - 1).astype(I32)
    te = jnp.where(j < nused, te_raw, te_raw[nused - 1])
    first = ((j == tile_start[te]) & (j < nused)).astype(I32)
    later = (ntile[None, :] > 0) & (experts[None, :] > te[:, None])
    nxt = jnp.min(jnp.where(later, experts[None, :], N_EXPERTS), axis=1)
    nxt = jnp.where(nxt == N_EXPERTS, -1, nxt).astype(I32)
    pad_start = tile_start * t + count
    pad_len = ntile * t - count
    tail = jnp.stack([nused * t, (NT_MOE - nused) * (t // ZERO_ROWS)])
    zinfo = jnp.concatenate([pad_start, pad_len, tail]).astype(I32)
    return pos.astype(I32), zinfo, (te, first, nxt, nused.reshape(1).astype(I32))


def _dispatch_kernel(p0_ref, p1_ref, z_ref, h_ref, xs_ref, zero_ref, sem, zsem):
    t = h_ref.shape[0]
    i = pl.program_id(0)
    base = i * t

    def clear_padding(start):
        def go(n, off):
            cp = pltpu.make_async_copy(zero_ref.at[pl.ds(0, n), :], xs_ref.at[pl.ds(off, n), :], zsem.at[0])
            cp.start() if start else cp.wait()

        for e in range(N_EXPERTS):
            off, ln = z_ref[e], z_ref[N_EXPERTS + e]
            end = off + ln
            for b in range(SUBLANE_BITS, PAD_BITS):
                @pl.when(((ln >> b) & 1) == 1)
                def _():
                    go(1 << b, pl.multiple_of(end - ((ln >> b) << b), SUBLANES))
            for k in range(SUBLANES - 1):
                @pl.when(k < (ln & (SUBLANES - 1)))
                def _():
                    go(1, off + k)
        tail0, n_tail = z_ref[2 * N_EXPERTS], z_ref[2 * N_EXPERTS + 1]

        def tail_body(k, carry):
            go(ZERO_ROWS, pl.multiple_of(tail0 + k * ZERO_ROWS, SUBLANES))
            return carry

        lax.fori_loop(0, n_tail, tail_body, 0)

    @pl.when(i == 0)
    def _():
        zero_ref[...] = jnp.zeros(zero_ref.shape, zero_ref.dtype)
        clear_padding(True)

    def issue(r, carry):
        src = h_ref.at[pl.ds(r, 1), :]
        pltpu.make_async_copy(src, xs_ref.at[pl.ds(p0_ref[base + r], 1), :], sem.at[0]).start()
        pltpu.make_async_copy(src, xs_ref.at[pl.ds(p1_ref[base + r], 1), :], sem.at[1]).start()
        return carry

    lax.fori_loop(0, t, issue, 0, unroll=8)
    pltpu.make_async_copy(h_ref, xs_ref.at[pl.ds(0, t), :], sem.at[0]).wait()
    pltpu.make_async_copy(h_ref, xs_ref.at[pl.ds(0, t), :], sem.at[1]).wait()

    @pl.when(i == 0)
    def _():
        clear_padding(False)


def _dispatch(h, pos0, pos1, zinfo):
    t = T_DISPATCH
    return pl.pallas_call(
        _dispatch_kernel,
        grid_spec=pltpu.PrefetchScalarGridSpec(
            num_scalar_prefetch=3,
            grid=(M // t,),
            in_specs=[pl.BlockSpec((t, D), lambda i, p0, p1, z: (i, 0))],
            out_specs=pl.BlockSpec(memory_space=pl.ANY),
            scratch_shapes=[pltpu.VMEM((ZERO_ROWS, D), F32),
                            pltpu.SemaphoreType.DMA((2,)), pltpu.SemaphoreType.DMA((1,))]),
        out_shape=jax.ShapeDtypeStruct((P_MOE, D), F32),
        compiler_params=_params(1),
        name="moe_dispatch",
    )(pos0, pos1, zinfo, h)


def _expert_weight_stream(w_refs, stage_refs, bf_refs, sem, te_ref, first_ref, nxt_ref, tn):
    c = pl.program_id(0)
    j = pl.program_id(1)
    nc = pl.num_programs(0)

    def copies(e, cc):
        col = pl.multiple_of(cc * tn, LANES)
        return [pltpu.make_async_copy(w.at[e, :, pl.ds(col, tn)], st, sem.at[k])
                for k, (w, st) in enumerate(zip(w_refs, stage_refs))]

    def start(e, cc):
        for cp in copies(e, cc):
            cp.start()

    @pl.when((c == 0) & (j == 0))
    def _():
        start(te_ref[0], 0)

    @pl.when(first_ref[j] == 1)
    def _():
        for cp in copies(0, 0):
            cp.wait()
        for st, bf in zip(stage_refs, bf_refs):
            _cast_rows(st, bf)
        ne = nxt_ref[j]

        @pl.when(ne >= 0)
        def _():
            start(ne, c)

        @pl.when((ne < 0) & (c + 1 < nc))
        def _():
            start(te_ref[0], c + 1)


def _gmm_gate_up_kernel(te_ref, first_ref, nxt_ref, nused_ref, xs_ref, wg_ref, wu_ref, o_ref,
                        sg_ref, su_ref, wgb_ref, wub_ref, sem):
    _expert_weight_stream((wg_ref, wu_ref), (sg_ref, su_ref), (wgb_ref, wub_ref), sem,
                          te_ref, first_ref, nxt_ref, TN_GU)

    @pl.when(pl.program_id(1) < nused_ref[0])
    def _():
        half = T_MOE // 2
        for r in range(2):
            rs = slice(r * half, (r + 1) * half)
            x = xs_ref[rs, :].astype(BF16)
            a = jnp.dot(x, wgb_ref[...], preferred_element_type=F32)
            b = jnp.dot(x, wub_ref[...], preferred_element_type=F32)
            o_ref[rs, :] = _swiglu(a, b).astype(BF16)

    @pl.when(pl.program_id(1) >= nused_ref[0])
    def _():
        o_ref[...] = jnp.zeros(o_ref.shape, o_ref.dtype)


def _gmm_down_kernel(te_ref, first_ref, nxt_ref, nused_ref, a_ref, wd_ref, o_ref,
                     sd_ref, wdb_ref, sem):
    _expert_weight_stream((wd_ref,), (sd_ref,), (wdb_ref,), sem, te_ref, first_ref, nxt_ref, TN_DN)

    @pl.when(pl.program_id(1) < nused_ref[0])
    def _():
        o_ref[...] = jnp.dot(a_ref[...], wdb_ref[...], preferred_element_type=F32)

    @pl.when(pl.program_id(1) >= nused_ref[0])
    def _():
        o_ref[...] = jnp.zeros(o_ref.shape, o_ref.dtype)


def _used_tile(j, nused):
    return jnp.minimum(j, nused[0] - 1)


def _gmm_gate_up(xs, w_gate, w_up, meta):
    te, first, nxt, nused = meta
    tn = TN_GU
    return pl.pallas_call(
        _gmm_gate_up_kernel,
        grid_spec=pltpu.PrefetchScalarGridSpec(
            num_scalar_prefetch=4,
            grid=(D_FF_EXPERT // tn, NT_MOE),
            in_specs=[pl.BlockSpec((T_MOE, D), lambda c, j, te, fi, nx, nu: (_used_tile(j, nu), 0)),
                      pl.BlockSpec(memory_space=pl.ANY),
                      pl.BlockSpec(memory_space=pl.ANY)],
            out_specs=pl.BlockSpec((T_MOE, tn), lambda c, j, te, fi, nx, nu: (j, c)),
            scratch_shapes=[pltpu.VMEM((D, tn), F32), pltpu.VMEM((D, tn), F32),
                            pltpu.VMEM((D, tn), BF16), pltpu.VMEM((D, tn), BF16),
                            pltpu.SemaphoreType.DMA((2,))]),
        out_shape=jax.ShapeDtypeStruct((P_MOE, D_FF_EXPERT), BF16),
        compiler_params=_params(2),
        name="moe_gate_up",
    )(te, first, nxt, nused, xs, w_gate, w_up)


def _gmm_down(act, w_down, meta):
    te, first, nxt, nused = meta
    tn = TN_DN
    return pl.pallas_call(
        _gmm_down_kernel,
        grid_spec=pltpu.PrefetchScalarGridSpec(
            num_scalar_prefetch=4,
            grid=(D // tn, NT_MOE),
            in_specs=[pl.BlockSpec((T_MOE, D_FF_EXPERT), lambda c, j, te, fi, nx, nu: (_used_tile(j, nu), 0)),
                      pl.BlockSpec(memory_space=pl.ANY)],
            out_specs=pl.BlockSpec((T_MOE, tn), lambda c, j, te, fi, nx, nu: (j, c)),
            scratch_shapes=[pltpu.VMEM((D_FF_EXPERT, tn), F32), pltpu.VMEM((D_FF_EXPERT, tn), BF16),
                            pltpu.SemaphoreType.DMA((1,))]),
        out_shape=jax.ShapeDtypeStruct((P_MOE, D), F32),
        compiler_params=_params(2),
        name="moe_down",
    )(te, first, nxt, nused, act, w_down)


def _combine_kernel(p0_ref, p1_ref, ys_ref, xa_ref, xb_ref, g_ref, w_ref, oc_ref, ol_ref,
                    a_ref, b_ref, sem):
    t = xa_ref.shape[0]
    i = pl.program_id(0)
    n = pl.num_programs(0)

    def issue(step, slot):
        base = step * t

        def body(r, carry):
            pltpu.make_async_copy(ys_ref.at[pl.ds(p0_ref[base + r], 1), :],
                                  a_ref.at[slot, pl.ds(r, 1), :], sem.at[0, slot]).start()
            pltpu.make_async_copy(ys_ref.at[pl.ds(p1_ref[base + r], 1), :],
                                  b_ref.at[slot, pl.ds(r, 1), :], sem.at[1, slot]).start()
            return carry

        lax.fori_loop(0, t, body, 0, unroll=8)

    @pl.when(i == 0)
    def _():
        issue(0, 0)

    @pl.when(i + 1 < n)
    def _():
        issue(i + 1, (i + 1) % 2)

    slot = i % 2
    pltpu.make_async_copy(ys_ref.at[pl.ds(0, t), :], a_ref.at[slot], sem.at[0, slot]).wait()
    pltpu.make_async_copy(ys_ref.at[pl.ds(0, t), :], b_ref.at[slot], sem.at[1, slot]).wait()
    w = w_ref[...]
    moe = w[:, 0:1] * a_ref[slot] + w[:, 1:2] * b_ref[slot]
    y = _pick(xa_ref, xb_ref, 0) + g_ref[...] * moe
    is_ctx = _is_ctx_tile(i, t)

    @pl.when(is_ctx)
    def _():
        oc_ref[...] = y

    @pl.when(jnp.logical_not(is_ctx))
    def _():
        ol_ref[...] = y


def _combine(ys, x, mod, layer, gate_chunk, wts, pos0, pos1):
    t = T_COMBINE
    n_ctx = MP // t
    row = lambda i, *_: i
    x_args, x_specs = _stream_in(x, t, D, row, lambda i, *_: 0)
    return pl.pallas_call(
        _combine_kernel,
        grid_spec=pltpu.PrefetchScalarGridSpec(
            num_scalar_prefetch=2,
            grid=(M // t,),
            in_specs=[pl.BlockSpec(memory_space=pl.ANY)] + x_specs + [
                _mod_spec(layer, gate_chunk, t, row),
                pl.BlockSpec((t, LANES), lambda i, p0, p1: (i, 0))],
            out_specs=[pl.BlockSpec((t, D), lambda i, p0, p1: (jnp.minimum(i, n_ctx - 1), 0)),
                       pl.BlockSpec((t, D), lambda i, p0, p1: (jnp.maximum(i - n_ctx, 0), 0))],
            scratch_shapes=[pltpu.VMEM((2, t, D), F32), pltpu.VMEM((2, t, D), F32),
                            pltpu.SemaphoreType.DMA((2, 2))]),
        out_shape=[jax.ShapeDtypeStruct((MP, D), F32), jax.ShapeDtypeStruct((MS, D), F32)],
        compiler_params=_params(1),
        name="moe_combine",
    )(pos0, pos1, ys, *x_args, mod, wts)


def _moe(x, h, idx, wts, mod, layer, w_gate, w_up, w_down):
    pos, zinfo, meta = _route_meta(idx[:, :TOP_K])
    pos0, pos1 = pos[0], pos[1]
    xs = _dispatch(h, pos0, pos1, zinfo)
    act = _gmm_gate_up(xs, w_gate, w_up, meta)
    ys = _gmm_down(act, w_down, meta)
    return _combine(ys, x, mod, layer, 5, wts, pos0, pos1)


def kernel(x_prompt, x_sample, cache_k, cache_v, c, c_ctx, w_ada, b_ada, norm1_g, norm2_g, w_in, q_norm_g, k_norm_g, sgu_norm_g, w_spatial, b_spatial, out_norm_g, w_out, ffn_w_gate, ffn_w_up, ffn_w_down, w_router, b_router, moe_w_gate, moe_w_up, moe_w_down):
    assert DEPTH == 2
    x = (x_prompt.reshape(MP, D), x_sample.reshape(MS, D))
    cond = jnp.concatenate([c_ctx[None, :], c, jnp.zeros((N_COND - 1 - DEC_BATCH, D), F32)], axis=0)
    mod = _modulation(cond, w_ada, b_ada).reshape(DEPTH, N_COND, 1, N_MOD * D)
    cos, sin = _rope_tables()
    w_in_bf = _cast_in_weights(w_in)

    new_k, new_v = [], []
    for i in range(DEPTH):
        q, kf, kb, vf, vb, u, gh = _in_projections(x, w_in_bf, mod, norm1_g, q_norm_g, k_norm_g,
                                                   sgu_norm_g, cos, sin, i)
        attn_ctx = _attention(q, kb, vb, None, None, i, batch=BATCH, seq=SEQ, row0=0)
        attn_lat = _attention(q, kb, vb, cache_k, cache_v, i, batch=DEC_BATCH, seq=DEC_SEQ, row0=MP)
        o = _sgu_merge(u, gh, attn_ctx, attn_lat, w_spatial, b_spatial, out_norm_g, i)
        j = i // 2
        if i % 2 == 0:
            x, h2 = _out_proj(o, w_out, x, mod, norm2_g, i)
            act = _ffn_gate_up(h2, ffn_w_gate, ffn_w_up, j)
            x = _mm_resid(act, ffn_w_down, x, mod, i, j, 5, TM, 512)
        else:
            x, h2, idx, wts = _out_proj(o, w_out, x, mod, norm2_g, i, router=(w_router[j], b_router[j]))
            x = _moe(x, h2, idx, wts, mod, i, moe_w_gate[j], moe_w_up[j], moe_w_down[j])
        new_k.append(kf)
        new_v.append(vf)

    y_prompt = x[0].reshape(BATCH, SEQ, D)
    y_sample = x[1].reshape(DEC_BATCH, DEC_SEQ, D)
    return (y_prompt, y_sample, jnp.stack(new_k, axis=1), jnp.stack(new_v, axis=1))
```

```python
import functools

import jax
import jax.numpy as jnp
from jax import lax
from jax.experimental import pallas as pl
from jax.experimental.pallas import tpu as pltpu

F32 = jnp.float32
BF16 = jnp.bfloat16
I32 = jnp.int32

D = 2048
BATCH, SEQ = 16, 256
DEC_BATCH, DEC_SEQ = 4, 2048
PAST_LEN = 256
DEPTH = 2
GRID_W = 64
CHUNK = 128
HEAD_DIM = 128
N_Q_HEADS, N_KV_HEADS = 8, 2
Q_PER_KV = N_Q_HEADS // N_KV_HEADS
ATTN_WIDTH = N_Q_HEADS * HEAD_DIM
KV_WIDTH = N_KV_HEADS * HEAD_DIM
N_SGU_HEADS = 8
SGU_WIDTH = N_SGU_HEADS * HEAD_DIM
IN_WIDTH = ATTN_WIDTH + 2 * KV_WIDTH + 2 * SGU_WIDTH
ROPE_THETA = 10000.0
ROPE_AXIS_DIM = HEAD_DIM // 2
D_FF = 5632
N_EXPERTS = 8
TOP_K = 2
D_FF_EXPERT = 2816
N_MOD = 6
EPS = 1e-6
ATTN_SCALE = HEAD_DIM ** -0.5
LOG2_E = 1.4426950408889634

MP = BATCH * SEQ
MS = DEC_BATCH * DEC_SEQ
M = MP + MS
N_COND = 8
LANES = 128
SUBLANES = 8
SUBLANE_BITS = 3

VMEM_LIMIT = 56 * 1024 * 1024

TM = 1024
TN_IN = 2 * KV_WIDTH
T_NORM = 512
T_Q = 512
ATTN_ROWS = 1024
T_MOE = 512
P_MOE = M * TOP_K + N_EXPERTS * T_MOE
NT_MOE = P_MOE // T_MOE
TN_GU = D_FF_EXPERT // 2
TN_DN = D
T_DISPATCH = 1024
T_COMBINE = 256
T_SGU = 1024
ZERO_ROWS = T_MOE // 2
PAD_BITS = ZERO_ROWS.bit_length()


def _params(n_axes):
    return pltpu.CompilerParams(dimension_semantics=("arbitrary",) * n_axes,
                                vmem_limit_bytes=VMEM_LIMIT)


def _cond_row(i, t):
    return jnp.where(i < MP // t, 0, 1 + (i - MP // t) // (DEC_SEQ // t))


def _is_ctx_tile(i, t):
    return i < MP // t


def _stream_in(x, t, width, row_of, col_of):
    n_ctx = MP // t
    pair = isinstance(x, tuple)
    base = 0 if pair else n_ctx
    ctx = pl.BlockSpec((t, width), lambda *g: (jnp.minimum(row_of(*g), n_ctx - 1), col_of(*g)))
    lat = pl.BlockSpec((t, width), lambda *g: (base + jnp.maximum(row_of(*g) - n_ctx, 0), col_of(*g)))
    return (list(x) if pair else [x, x]), [ctx, lat]


def _mod_spec(layer, chunk, t, row_of, col_of=None, tn=D):
    per = D // tn

    def index_map(*g):
        col = chunk * per + (col_of(*g) if col_of is not None else 0)
        return (layer, _cond_row(row_of(*g), t), 0, col)

    return pl.BlockSpec((None, None, 1, tn), index_map)


def _ada_kernel(c_ref, w_ref, b_ref, o_ref):
    c = c_ref[...]
    s = (c * jax.nn.sigmoid(c)).astype(BF16)
    o_ref[...] = jnp.dot(s, w_ref[...].astype(BF16), preferred_element_type=F32) + b_ref[...]


def _modulation(cond, w_ada, b_ada):
    tn = 1024
    width = N_MOD * D
    return pl.pallas_call(
        _ada_kernel,
        grid=(DEPTH, width // tn),
        in_specs=[pl.BlockSpec((N_COND, D), lambda l, n: (0, 0)),
                  pl.BlockSpec((None, D, tn), lambda l, n: (l, 0, n)),
                  pl.BlockSpec((None, 1, tn), lambda l, n: (l, 0, n))],
        out_specs=pl.BlockSpec((None, N_COND, tn), lambda l, n: (l, 0, n)),
        out_shape=jax.ShapeDtypeStruct((DEPTH, N_COND, width), F32),
        compiler_params=_params(2),
        name="modulation",
    )(cond, w_ada, b_ada.reshape(DEPTH, 1, width))


def _modulated_norm(x, g, sc, sh):
    y = x * lax.rsqrt(jnp.mean(x * x, axis=-1, keepdims=True) + EPS)
    return y * (g * (1.0 + sc)) + sh


def _pick(xa_ref, xb_ref, axis):
    t = xa_ref.shape[0]
    return jnp.where(_is_ctx_tile(pl.program_id(axis), t), xa_ref[...], xb_ref[...])


def _stream_tile_copy(xa_ref, xb_ref, lat_row0, buf_ref, sem, tile, slot, start):
    t = buf_ref.shape[1]

    def copy(src_ref, row):
        return pltpu.make_async_copy(src_ref.at[pl.ds(pl.multiple_of(row, t), t), :],
                                     buf_ref.at[slot], sem.at[slot])

    if not start:
        copy(xa_ref, 0).wait()
        return
    is_ctx = _is_ctx_tile(tile, t)

    @pl.when(is_ctx)
    def _():
        copy(xa_ref, tile * t).start()

    @pl.when(jnp.logical_not(is_ctx))
    def _():
        copy(xb_ref, lat_row0 + (tile - MP // t) * t).start()


def _next_stream_tile(xa_ref, xb_ref, lat_row0, buf_ref, sem, tile, n_tiles):
    @pl.when(tile == 0)
    def _():
        _stream_tile_copy(xa_ref, xb_ref, lat_row0, buf_ref, sem, 0, 0, True)

    @pl.when(tile + 1 < n_tiles)
    def _():
        _stream_tile_copy(xa_ref, xb_ref, lat_row0, buf_ref, sem, tile + 1, (tile + 1) % 2, True)

    slot = tile % 2
    _stream_tile_copy(xa_ref, xb_ref, lat_row0, buf_ref, sem, tile, slot, False)
    return slot


def _stream_hbm(x):
    return (x[0], x[1], 0) if isinstance(x, tuple) else (x, x, MP)


def _split_bf16(a):
    hi = a.astype(BF16)
    return hi, (a - hi.astype(F32)).astype(BF16)


def _route_top2(h, wr, br, idx_ref, wt_ref):
    h_hi, h_lo = _split_bf16(h)
    w_hi, w_lo = _split_bf16(wr)
    logits = (jnp.dot(h_hi, w_hi, preferred_element_type=F32)
              + jnp.dot(h_lo, w_hi, preferred_element_type=F32)
              + jnp.dot(h_hi, w_lo, preferred_element_type=F32)) + br
    lane = lax.broadcasted_iota(I32, logits.shape, 1)
    neg = jnp.float32(-jnp.inf)
    lg = jnp.where(lane < N_EXPERTS, logits, neg)
    m1 = jnp.max(lg, axis=-1, keepdims=True)
    i1 = jnp.min(jnp.where(lg == m1, lane, LANES), axis=-1, keepdims=True)
    lg2 = jnp.where(lane == i1, neg, lg)
    m2 = jnp.max(lg2, axis=-1, keepdims=True)
    i2 = jnp.min(jnp.where(lg2 == m2, lane, LANES), axis=-1, keepdims=True)
    e = jnp.exp(m2 - m1)
    w1 = 1.0 / (1.0 + e)
    w2 = e / (1.0 + e)
    idx_ref[...] = jnp.where(lane == 0, i1, jnp.where(lane == 1, i2, 0))
    wt_ref[...] = jnp.where(lane == 0, w1, jnp.where(lane == 1, w2, 0.0))


W_PIECE = 512


def _out_proj_kernel(*refs, layer, lat_row0, route):
    if route:
        (o_ref, w_ref, xa_ref, xb_ref, gate_ref, g_ref, sc_ref, sh_ref, wr_ref, br_ref,
         xn_ref, h_ref, idx_ref, wt_ref, stage_ref, wbf_ref, xbuf_ref, wsem, xsem) = refs
    else:
        (o_ref, w_ref, xa_ref, xb_ref, gate_ref, g_ref, sc_ref, sh_ref,
         xn_ref, h_ref, stage_ref, wbf_ref, xbuf_ref, wsem, xsem) = refs
    i = pl.program_id(0)

    @pl.when(i == 0)
    def _():
        for p in range(D // W_PIECE):
            cols = pl.ds(p * W_PIECE, W_PIECE)
            cp = pltpu.make_async_copy(w_ref.at[layer, :, cols], stage_ref, wsem.at[0])
            cp.start()
            cp.wait()
            _cast_rows(stage_ref, wbf_ref.at[:, cols])

    slot = _next_stream_tile(xa_ref, xb_ref, lat_row0, xbuf_ref, xsem, i, pl.num_programs(0))
    acc = jnp.dot(o_ref[...], wbf_ref[...], preferred_element_type=F32)
    x_new = xbuf_ref[slot] + gate_ref[...] * acc
    xn_ref[...] = x_new
    h = _modulated_norm(x_new, g_ref[...], sc_ref[...], sh_ref[...])
    if route:
        h_ref[...] = h
        _route_top2(h, wr_ref[...], br_ref[...], idx_ref, wt_ref)
    else:
        h_ref[...] = h.astype(BF16)


def _out_proj(o, w_out, x, mod, norm2_g, layer, router=None):
    t = T_NORM
    row = lambda i: i
    xa, xb, lat_row0 = _stream_hbm(x)
    route = router is not None
    anyspace = pl.BlockSpec(memory_space=pl.ANY)
    rows = lambda w: pl.BlockSpec((t, w), lambda i: (i, 0))
    in_specs = [rows(D), anyspace, anyspace, anyspace,
                _mod_spec(layer, 2, t, row),
                pl.BlockSpec((None, 1, D), lambda i: (layer, 0, 0)),
                _mod_spec(layer, 4, t, row), _mod_spec(layer, 3, t, row)]
    args = [o, w_out, xa, xb, mod, norm2_g.reshape(DEPTH, 1, D), mod, mod]
    out_specs = [rows(D), rows(D)]
    out_shape = [jax.ShapeDtypeStruct((M, D), F32), jax.ShapeDtypeStruct((M, D), F32 if route else BF16)]
    if route:
        w_router, b_router = router
        args += [jnp.zeros((D, LANES), F32).at[:, :N_EXPERTS].set(w_router),
                 jnp.zeros((1, LANES), F32).at[0, :N_EXPERTS].set(b_router)]
        in_specs += [pl.BlockSpec((D, LANES), lambda i: (0, 0)), pl.BlockSpec((1, LANES), lambda i: (0, 0))]
        out_specs += [rows(LANES), rows(LANES)]
        out_shape += [jax.ShapeDtypeStruct((M, LANES), I32), jax.ShapeDtypeStruct((M, LANES), F32)]
    return pl.pallas_call(
        functools.partial(_out_proj_kernel, layer=layer, lat_row0=lat_row0, route=route),
        grid=(M // t,),
        in_specs=in_specs,
        out_specs=out_specs,
        out_shape=out_shape,
        scratch_shapes=[pltpu.VMEM((D, W_PIECE), F32), pltpu.VMEM((D, D), BF16),
                        pltpu.VMEM((2, t, D), F32),
                        pltpu.SemaphoreType.DMA((1,)), pltpu.SemaphoreType.DMA((2,))],
        compiler_params=_params(1),
        name="out_proj_router" if route else "out_proj",
    )(*args)


CAST_ROWS = 256


def _cast_rows(src_ref, dst_ref):
    def body(r, carry):
        rs = pl.ds(pl.multiple_of(r * CAST_ROWS, CAST_ROWS), CAST_ROWS)
        dst_ref[rs, :] = src_ref[rs, :].astype(BF16)
        return carry

    lax.fori_loop(0, src_ref.shape[0] // CAST_ROWS, body, 0)


def _head_rms(a):
    return lax.rsqrt(jnp.mean(a * a, axis=-1, keepdims=True) + EPS)


def _rope_partner(ag, perm):
    hi, lo = _split_bf16(ag)
    return (jnp.dot(hi, perm, preferred_element_type=F32)
            + jnp.dot(lo, perm, preferred_element_type=F32))


def _cast_kernel(w_ref, o_ref):
    o_ref[...] = w_ref[...].astype(BF16)


def _cast_in_weights(w_in):
    spec = pl.BlockSpec((None, D, TN_IN), lambda l, n: (l, 0, n))
    return pl.pallas_call(
        _cast_kernel,
        grid=(DEPTH, IN_WIDTH // TN_IN),
        in_specs=[spec],
        out_specs=spec,
        out_shape=jax.ShapeDtypeStruct(w_in.shape, BF16),
        compiler_params=_params(2),
        name="cast_w_in",
    )(w_in)


N_Q_TILES = ATTN_WIDTH // TN_IN
KV_TILE = N_Q_TILES
U_TILE0 = KV_TILE + 1
G_TILE0 = U_TILE0 + SGU_WIDTH // TN_IN
N_IN_TILES = IN_WIDTH // TN_IN


def _in_proj_kernel(xa_ref, xb_ref, n1_ref, sc0_ref, sh0_ref, sc1_ref, sh1_ref, w_ref,
                    qg_ref, kg_ref, sg_ref, pq_ref, pk_ref, cos_ref, sin_ref,
                    q_ref, kf_ref, kb_ref, vf_ref, vb_ref, u_ref, gh_ref,
                    xbuf_ref, h_ref, xsem, *, lat_row0):
    m = pl.program_id(0)
    n = pl.program_id(1)
    nm = pl.num_programs(0)
    cur = m % 2
    nxt = (m + 1) % 2
    has_next = m + 1 < nm
    fetch = functools.partial(_stream_tile_copy, xa_ref, xb_ref, lat_row0, xbuf_ref, xsem)

    def norm_rows(slot, rows, sc_ref, sh_ref):
        x = xbuf_ref[slot, rows, :]
        h_ref[slot, rows, :] = _modulated_norm(x, n1_ref[...], sc_ref[...], sh_ref[...]).astype(BF16)

    @pl.when((m == 0) & (n == 0))
    def _():
        fetch(0, 0, True)
        fetch(0, 0, False)
        norm_rows(0, slice(None), sc0_ref, sh0_ref)

    @pl.when((n == 0) & has_next)
    def _():
        fetch(m + 1, nxt, True)

    def matmul():
        return jnp.dot(h_ref[cur], w_ref[...], preferred_element_type=F32)

    for rope in (False, True):
        positioned = jnp.logical_not(_is_ctx_tile(m, TM)) if rope else _is_ctx_tile(m, TM)

        @pl.when((n < N_Q_TILES) & positioned)
        def _():
            acc = matmul()
            ag = acc * qg_ref[...]
            if rope:
                partner = _rope_partner(ag, pq_ref[...])
                cos, sin = cos_ref[...], sin_ref[...]
            for h in range(TN_IN // HEAD_DIM):
                sl = slice(h * HEAD_DIM, (h + 1) * HEAD_DIM)
                r = _head_rms(acc[:, sl]) * (ATTN_SCALE * LOG2_E)
                qh = ag[:, sl] * cos + partner[:, sl] * sin if rope else ag[:, sl]
                q_ref[:, sl] = (qh * r).astype(BF16)

        @pl.when((n == KV_TILE) & positioned)
        def _():
            acc = matmul()
            k = acc[:, :KV_WIDTH]
            ag = k * kg_ref[...]
            if rope:
                partner = _rope_partner(ag, pk_ref[...])
                cos, sin = cos_ref[...], sin_ref[...]
            for h in range(N_KV_HEADS):
                sl = slice(h * HEAD_DIM, (h + 1) * HEAD_DIM)
                r = _head_rms(k[:, sl])
                kf = ag[:, sl] * r
                if not rope:
                    kf_ref[:, sl] = kf
                kb_ref[:, sl] = ((ag[:, sl] * cos + partner[:, sl] * sin) * r if rope else kf).astype(BF16)
            v = acc[:, KV_WIDTH:]
            if not rope:
                vf_ref[...] = v
            vb_ref[...] = v.astype(BF16)

    half = TM // (G_TILE0 - U_TILE0)
    for k in range(G_TILE0 - U_TILE0):
        @pl.when((n == U_TILE0 + k) & has_next)
        def _():
            if k == 0:
                fetch(m + 1, nxt, False)
            u_ref[...] = matmul().astype(BF16)
            norm_rows(nxt, slice(k * half, (k + 1) * half), sc1_ref, sh1_ref)

        @pl.when((n == U_TILE0 + k) & jnp.logical_not(has_next))
        def _():
            u_ref[...] = matmul().astype(BF16)

    @pl.when(n >= G_TILE0)
    def _():
        acc = matmul()
        for h in range(TN_IN // HEAD_DIM):
            sl = slice(h * HEAD_DIM, (h + 1) * HEAD_DIM)
            a = acc[:, sl]
            gh_ref[:, sl] = (a * _head_rms(a) * sg_ref[:, sl]).astype(BF16)


def _rope_tables():
    n_rows = DEC_SEQ // GRID_W
    rows = jnp.broadcast_to(jnp.arange(n_rows)[:, None], (n_rows, GRID_W)).reshape(-1)
    cols = jnp.broadcast_to(jnp.arange(GRID_W)[None, :], (n_rows, GRID_W)).reshape(-1)
    inv = ROPE_THETA ** (-jnp.arange(0, ROPE_AXIS_DIM, 2, dtype=F32) / ROPE_AXIS_DIM)
    ang_r = rows.astype(F32)[:, None] * inv
    ang_c = cols.astype(F32)[:, None] * inv
    cos = jnp.concatenate([jnp.cos(ang_r), jnp.cos(ang_r), jnp.cos(ang_c), jnp.cos(ang_c)], axis=1)
    sin = jnp.concatenate([-jnp.sin(ang_r), jnp.sin(ang_r), -jnp.sin(ang_c), jnp.sin(ang_c)], axis=1)
    return cos, sin


def _partner_matrix(n_heads):
    w = n_heads * HEAD_DIM
    quarter = ROPE_AXIS_DIM // 2
    j = jnp.arange(w)
    partner = jnp.where((j % ROPE_AXIS_DIM) < quarter, j + quarter, j - quarter)
    return (jnp.arange(w)[:, None] == partner[None, :]).astype(BF16)


def _rope_block(m):
    return jnp.maximum(m - MP // TM, 0) % (DEC_SEQ // TM)


def _in_projections(x, w_in_bf, mod, norm1_g, q_norm_g, k_norm_g, sgu_norm_g, cos, sin, layer):
    tn = TN_IN
    xa, xb, lat_row0 = _stream_hbm(x)
    row = lambda m, n: m
    next_row = lambda m, n: jnp.minimum(m + 1, M // TM - 1)
    anyspace = pl.BlockSpec(memory_space=pl.ANY)
    const = lambda shape: pl.BlockSpec(shape, lambda m, n: (0,) * len(shape))
    rope_spec = pl.BlockSpec((TM, HEAD_DIM), lambda m, n: (_rope_block(m), 0))
    q_heads = tn // HEAD_DIM
    q_gain = jnp.tile(q_norm_g[layer], q_heads)[None, :]
    k_gain = jnp.tile(k_norm_g[layer], N_KV_HEADS)[None, :]
    g_tile = lambda n: jnp.clip(n - G_TILE0, 0, SGU_WIDTH // tn - 1)
    kv_out = pl.BlockSpec((TM, KV_WIDTH), lambda m, n: (m, 0))
    kv_ctx = pl.BlockSpec((TM, KV_WIDTH), lambda m, n: (jnp.minimum(m, MP // TM - 1), 0))
    kv_shape = lambda rows, dt: jax.ShapeDtypeStruct((rows, KV_WIDTH), dt)
    return pl.pallas_call(
        functools.partial(_in_proj_kernel, lat_row0=lat_row0),
        grid=(M // TM, N_IN_TILES),
        in_specs=[anyspace, anyspace,
                  pl.BlockSpec((None, 1, D), lambda m, n: (layer, 0, 0)),
                  _mod_spec(layer, 1, TM, row), _mod_spec(layer, 0, TM, row),
                  _mod_spec(layer, 1, TM, next_row), _mod_spec(layer, 0, TM, next_row),
                  pl.BlockSpec((None, D, tn), lambda m, n: (layer, 0, n)),
                  const((1, tn)), const((1, KV_WIDTH)),
                  pl.BlockSpec((None, 1, tn), lambda m, n: (layer, 0, g_tile(n))),
                  const((tn, tn)), const((KV_WIDTH, KV_WIDTH)), rope_spec, rope_spec],
        out_specs=[pl.BlockSpec((TM, tn), lambda m, n: (m, jnp.minimum(n, N_Q_TILES - 1))),
                   kv_ctx, kv_out, kv_ctx, kv_out,
                   pl.BlockSpec((TM, tn), lambda m, n: (m, jnp.clip(n - U_TILE0, 0, SGU_WIDTH // tn - 1))),
                   pl.BlockSpec((TM, tn), lambda m, n: (m, g_tile(n)))],
        out_shape=[jax.ShapeDtypeStruct((M, ATTN_WIDTH), BF16),
                   kv_shape(MP, F32), kv_shape(M, BF16), kv_shape(MP, F32), kv_shape(M, BF16),
                   jax.ShapeDtypeStruct((M, SGU_WIDTH), BF16),
                   jax.ShapeDtypeStruct((M, SGU_WIDTH), BF16)],
        scratch_shapes=[pltpu.VMEM((2, TM, D), F32), pltpu.VMEM((2, TM, D), BF16),
                        pltpu.SemaphoreType.DMA((2,))],
        compiler_params=_params(2),
        name="in_proj",
    )(xa, xb, norm1_g.reshape(DEPTH, 1, D), mod, mod, mod, mod, w_in_bf, q_gain, k_gain,
      sgu_norm_g.reshape(DEPTH, 1, SGU_WIDTH), _partner_matrix(q_heads), _partner_matrix(N_KV_HEADS),
      cos, sin)


def _qk(q, k):
    return lax.dot_general(q, k, (((1,), (1,)), ((), ())), preferred_element_type=F32)


def _attn_kernel(*refs, has_cache, n_batch, seq, tq):
    def with_ones(v):
        return jnp.concatenate([v, jnp.ones_like(v)], axis=1)

    if has_cache:
        q_ref, k_ref, v_ref, kc_ref, vc_ref, o_ref = refs
    else:
        q_ref, k_ref, v_ref, o_ref = refs
    for b in range(n_batch):
        rows_q = slice(b * tq, (b + 1) * tq)
        rows_k = slice(b * seq, (b + 1) * seq)
        for kv in range(N_KV_HEADS):
            kv_cols = slice(kv * HEAD_DIM, (kv + 1) * HEAD_DIM)
            k = k_ref[rows_k, kv_cols]
            v = with_ones(v_ref[rows_k, kv_cols])
            if has_cache:
                kc = kc_ref[:, kv_cols].astype(BF16)
                vc = with_ones(vc_ref[:, kv_cols].astype(BF16))
            for g in range(Q_PER_KV):
                head = kv * Q_PER_KV + g
                sl = slice(head * HEAD_DIM, (head + 1) * HEAD_DIM)
                q = q_ref[rows_q, sl]
                s = _qk(q, k)
                m = jnp.max(s, axis=-1, keepdims=True)
                if has_cache:
                    sc = _qk(q, kc)
                    m = jnp.maximum(m, jnp.max(sc, axis=-1, keepdims=True))
                o = jnp.dot(jnp.exp2(s - m).astype(BF16), v, preferred_element_type=F32)
                if has_cache:
                    o = o + jnp.dot(jnp.exp2(sc - m).astype(BF16), vc, preferred_element_type=F32)
                o_ref[rows_q, sl] = (o[:, :HEAD_DIM] / o[:, HEAD_DIM:]).astype(BF16)


def _attention(q, kb, vb, cache_k, cache_v, layer, *, batch, seq, row0):
    has_cache = cache_k is not None
    tq = min(T_Q, seq)
    nq = seq // tq
    n_batch = max(1, ATTN_ROWS // seq) if nq == 1 else 1
    q_spec = pl.BlockSpec((n_batch * tq, ATTN_WIDTH), lambda b, i: (row0 // (n_batch * tq) + b * nq + i, 0))
    kv_spec = pl.BlockSpec((n_batch * seq, KV_WIDTH), lambda b, i: (row0 // (n_batch * seq) + b, 0))
    in_specs = [q_spec, kv_spec, kv_spec]
    args = [q, kb, vb]
    if has_cache:
        c_spec = pl.BlockSpec((None, None, PAST_LEN, KV_WIDTH), lambda b, i: (b, layer, 0, 0))
        in_specs += [c_spec, c_spec]
        args += [cache_k.reshape(DEC_BATCH, DEPTH, PAST_LEN, KV_WIDTH),
                 cache_v.reshape(DEC_BATCH, DEPTH, PAST_LEN, KV_WIDTH)]
    return pl.pallas_call(
        functools.partial(_attn_kernel, has_cache=has_cache, n_batch=n_batch, seq=seq, tq=tq),
        grid=(batch // n_batch, nq),
        in_specs=in_specs,
        out_specs=pl.BlockSpec((n_batch * tq, ATTN_WIDTH), lambda b, i: (b * nq + i, 0)),
        out_shape=jax.ShapeDtypeStruct((batch * seq, ATTN_WIDTH), BF16),
        compiler_params=_params(2),
        name="attention_cached" if has_cache else "attention",
    )(*args)


def _sgu_merge_kernel(u_ref, gh_ref, ap_ref, as_ref, ws_ref, bs_ref, gn_ref, o_ref, sgu_ref):
    t = u_ref.shape[0]
    a = _pick(ap_ref, as_ref, 0).astype(F32)
    a = a * lax.rsqrt(jnp.mean(a * a, axis=-1, keepdims=True) + EPS) * gn_ref[:, :ATTN_WIDTH]
    o_ref[:, :ATTN_WIDTH] = a.astype(BF16)
    for h in range(N_SGU_HEADS):
        cs = slice(h * HEAD_DIM, (h + 1) * HEAD_DIM)
        w = ws_ref[h].astype(BF16)
        b = bs_ref[h]
        for c in range(t // CHUNK):
            rs = slice(c * CHUNK, (c + 1) * CHUNK)
            mixed = jnp.dot(w, gh_ref[rs, cs], preferred_element_type=F32) + b
            sgu_ref[rs, cs] = u_ref[rs, cs].astype(F32) * mixed
    s = sgu_ref[...]
    s = s * lax.rsqrt(jnp.mean(s * s, axis=-1, keepdims=True) + EPS) * gn_ref[:, ATTN_WIDTH:]
    o_ref[:, ATTN_WIDTH:] = s.astype(BF16)


def _sgu_merge(u, gh, attn_ctx, attn_lat, w_spatial, b_spatial, out_norm_g, layer):
    t = T_SGU
    bias = jnp.broadcast_to(b_spatial[:, :, :, None], (DEPTH, N_SGU_HEADS, CHUNK, HEAD_DIM))
    row = lambda w: pl.BlockSpec((t, w), lambda i: (i, 0))
    a_args, a_specs = _stream_in((attn_ctx, attn_lat), t, ATTN_WIDTH, lambda i: i, lambda i: 0)
    return pl.pallas_call(
        _sgu_merge_kernel,
        grid=(M // t,),
        in_specs=[row(SGU_WIDTH), row(SGU_WIDTH)] + a_specs + [
            pl.BlockSpec((None, N_SGU_HEADS, CHUNK, CHUNK), lambda i: (layer, 0, 0, 0)),
            pl.BlockSpec((None, N_SGU_HEADS, CHUNK, HEAD_DIM), lambda i: (layer, 0, 0, 0)),
            pl.BlockSpec((None, 1, D), lambda i: (layer, 0, 0))],
        out_specs=row(D),
        out_shape=jax.ShapeDtypeStruct((M, D), BF16),
        scratch_shapes=[pltpu.VMEM((t, SGU_WIDTH), F32)],
        compiler_params=_params(1),
        name="sgu_merge",
    )(u, gh, *a_args, w_spatial, bias, out_norm_g.reshape(DEPTH, 1, D))


def _mm_resid_kernel(a_ref, w_ref, x_ref, g_ref, o_ref, stage_ref, wbf_ref, sem, *, w_index):
    n = pl.program_id(0)
    tn = stage_ref.shape[1]

    def weight_copy(col_block):
        cols = pl.ds(pl.multiple_of(col_block * tn, LANES), tn)
        return pltpu.make_async_copy(w_ref.at[w_index, :, cols], stage_ref, sem.at[0])

    @pl.when(pl.program_id(1) == 0)
    def _():
        @pl.when(n == 0)
        def _():
            weight_copy(0).start()

        weight_copy(0).wait()
        _cast_rows(stage_ref, wbf_ref)

        @pl.when(n + 1 < pl.num_programs(0))
        def _():
            weight_copy(n + 1).start()

    acc = jnp.dot(a_ref[...], wbf_ref[...], preferred_element_type=F32)
    o_ref[...] = x_ref[...] + g_ref[...] * acc


def _mm_resid(a, w, x, mod, layer, w_index, gate_chunk, tm, tn):
    k = a.shape[1]
    row = lambda n, m: m
    col = lambda n, m: n
    return pl.pallas_call(
        functools.partial(_mm_resid_kernel, w_index=w_index),
        grid=(D // tn, M // tm),
        in_specs=[pl.BlockSpec((tm, k), lambda n, m: (m, 0)),
                  pl.BlockSpec(memory_space=pl.ANY),
                  pl.BlockSpec((tm, tn), lambda n, m: (m, n)),
                  _mod_spec(layer, gate_chunk, tm, row, col, tn=tn)],
        out_specs=pl.BlockSpec((tm, tn), lambda n, m: (m, n)),
        out_shape=jax.ShapeDtypeStruct((M, D), F32),
        scratch_shapes=[pltpu.VMEM((k, tn), F32), pltpu.VMEM((k, tn), BF16),
                        pltpu.SemaphoreType.DMA((1,))],
        compiler_params=_params(2),
        name="mm_resid",
    )(a, w, x, mod)


def _swiglu(a, b):
    return a * jax.nn.sigmoid(a) * b


def _ffn_gu_kernel(x_ref, wg_ref, wu_ref, o_ref, wgb_ref, wub_ref):
    @pl.when(pl.program_id(1) == 0)
    def _():
        _cast_rows(wg_ref, wgb_ref)
        _cast_rows(wu_ref, wub_ref)
    for r in range(x_ref.shape[0] // TM):
        rs = slice(r * TM, (r + 1) * TM)
        x = x_ref[rs, :]
        a = jnp.dot(x, wgb_ref[...], preferred_element_type=F32)
        b = jnp.dot(x, wub_ref[...], preferred_element_type=F32)
        o_ref[rs, :] = _swiglu(a, b).astype(BF16)


def _ffn_gate_up(h, w_gate, w_up, j):
    tm, tn = 2 * TM, 512
    w_spec = pl.BlockSpec((None, D, tn), lambda n, m: (j, 0, n))
    return pl.pallas_call(
        _ffn_gu_kernel,
        grid=(D_FF // tn, M // tm),
        in_specs=[pl.BlockSpec((tm, D), lambda n, m: (m, 0)), w_spec, w_spec],
        out_specs=pl.BlockSpec((tm, tn), lambda n, m: (m, n)),
        out_shape=jax.ShapeDtypeStruct((M, D_FF), BF16),
        scratch_shapes=[pltpu.VMEM((D, tn), BF16), pltpu.VMEM((D, tn), BF16)],
        compiler_params=_params(2),
        name="ffn_gate_up",
    )(h, w_gate, w_up)


def _route_meta(idx):
    t = T_MOE
    experts = jnp.arange(N_EXPERTS, dtype=I32)
    sel = idx.T[:, None, :] == experts[None, :, None]
    onehot = (sel[0] | sel[1]).astype(I32)
    csum = jnp.cumsum(onehot, axis=1)
    rank = csum - onehot
    count = csum[:, -1]
    ntile = (count + t - 1) // t
    tile_end = jnp.cumsum(ntile)
    tile_start = tile_end - ntile
    nused = tile_end[-1]
    row = (tile_start * t)[:, None] + rank
    pos = jnp.sum(jnp.where(sel, row[None], 0), axis=1)
    j = jnp.arange(NT_MOE, dtype=I32)
    te_raw = jnp.minimum(jnp.sum(j[:, None] >= tile_end[None, :], axis=1), N_EXPERTS - 1).astype(I32)
    te = jnp.where(j < nused, te_raw, te_raw[nused - 1])
    first = ((j == tile_start[te]) & (j < nused)).astype(I32)
    later = (ntile[None, :] > 0) & (experts[None, :] > te[:, None])
    nxt = jnp.min(jnp.where(later, experts[None, :], N_EXPERTS), axis=1)
    nxt = jnp.where(nxt == N_EXPERTS, -1, nxt).astype(I32)
    pad_start = tile_start * t + count
    pad_len = ntile * t - count
    tail = jnp.stack([nused * t, (NT_MOE - nused) * (t // ZERO_ROWS)])
    zinfo = jnp.concatenate([pad_start, pad_len, tail]).astype(I32)
    return pos.astype(I32), zinfo, (te, first, nxt, nused.reshape(1).astype(I32))


def _dispatch_kernel(p0_ref, p1_ref, z_ref, h_ref, xs_ref, zero_ref, sem, zsem):
    t = h_ref.shape[0]
    i = pl.program_id(0)
    base = i * t

    def clear_padding(start):
        def go(n, off):
            cp = pltpu.make_async_copy(zero_ref.at[pl.ds(0, n), :], xs_ref.at[pl.ds(off, n), :], zsem.at[0])
            cp.start() if start else cp.wait()

        for e in range(N_EXPERTS):
            off, ln = z_ref[e], z_ref[N_EXPERTS + e]
            end = off + ln
            for b in range(SUBLANE_BITS, PAD_BITS):
                @pl.when(((ln >> b) & 1) == 1)
                def _():
                    go(1 << b, pl.multiple_of(end - ((ln >> b) << b), SUBLANES))
            for k in range(SUBLANES - 1):
                @pl.when(k < (ln & (SUBLANES - 1)))
                def _():
                    go(1, off + k)
        tail0, n_tail = z_ref[2 * N_EXPERTS], z_ref[2 * N_EXPERTS + 1]

        def tail_body(k, carry):
            go(ZERO_ROWS, pl.multiple_of(tail0 + k * ZERO_ROWS, SUBLANES))
            return carry

        lax.fori_loop(0, n_tail, tail_body, 0)

    @pl.when(i == 0)
    def _():
        zero_ref[...] = jnp.zeros(zero_ref.shape, zero_ref.dtype)
        clear_padding(True)

    def issue(r, carry):
        src = h_ref.at[pl.ds(r, 1), :]
        pltpu.make_async_copy(src, xs_ref.at[pl.ds(p0_ref[base + r], 1), :], sem.at[0]).start()
        pltpu.make_async_copy(src, xs_ref.at[pl.ds(p1_ref[base + r], 1), :], sem.at[1]).start()
        return carry

    lax.fori_loop(0, t, issue, 0, unroll=8)
    pltpu.make_async_copy(h_ref, xs_ref.at[pl.ds(0, t), :], sem.at[0]).wait()
    pltpu.make_async_copy(h_ref, xs_ref.at[pl.ds(0, t), :], sem.at[1]).wait()

    @pl.when(i == 0)
    def _():
        clear_padding(False)


def _dispatch(h, pos0, pos1, zinfo):
    t = T_DISPATCH
    return pl.pallas_call(
        _dispatch_kernel,
        grid_spec=pltpu.PrefetchScalarGridSpec(
            num_scalar_prefetch=3,
            grid=(M // t,),
            in_specs=[pl.BlockSpec((t, D), lambda i, p0, p1, z: (i, 0))],
            out_specs=pl.BlockSpec(memory_space=pl.ANY),
            scratch_shapes=[pltpu.VMEM((ZERO_ROWS, D), F32),
                            pltpu.SemaphoreType.DMA((2,)), pltpu.SemaphoreType.DMA((1,))]),
        out_shape=jax.ShapeDtypeStruct((P_MOE, D), F32),
        compiler_params=_params(1),
        name="moe_dispatch",
    )(pos0, pos1, zinfo, h)


def _expert_weight_stream(w_refs, stage_refs, bf_refs, sem, te_ref, first_ref, nxt_ref, tn):
    c = pl.program_id(0)
    j = pl.program_id(1)
    nc = pl.num_programs(0)

    def copies(e, cc):
        col = pl.multiple_of(cc * tn, LANES)
        return [pltpu.make_async_copy(w.at[e, :, pl.ds(col, tn)], st, sem.at[k])
                for k, (w, st) in enumerate(zip(w_refs, stage_refs))]

    def start(e, cc):
        for cp in copies(e, cc):
            cp.start()

    @pl.when((c == 0) & (j == 0))
    def _():
        start(te_ref[0], 0)

    @pl.when(first_ref[j] == 1)
    def _():
        for cp in copies(0, 0):
            cp.wait()
        for st, bf in zip(stage_refs, bf_refs):
            _cast_rows(st, bf)
        ne = nxt_ref[j]

        @pl.when(ne >= 0)
        def _():
            start(ne, c)

        @pl.when((ne < 0) & (c + 1 < nc))
        def _():
            start(te_ref[0], c + 1)


def _gmm_gate_up_kernel(te_ref, first_ref, nxt_ref, nused_ref, xs_ref, wg_ref, wu_ref, o_ref,
                        sg_ref, su_ref, wgb_ref, wub_ref, sem):
    _expert_weight_stream((wg_ref, wu_ref), (sg_ref, su_ref), (wgb_ref, wub_ref), sem,
                          te_ref, first_ref, nxt_ref, TN_GU)

    @pl.when(pl.program_id(1) < nused_ref[0])
    def _():
        half = T_MOE // 2
        for r in range(2):
            rs = slice(r * half, (r + 1) * half)
            x = xs_ref[rs, :].astype(BF16)
            a = jnp.dot(x, wgb_ref[...], preferred_element_type=F32)
            b = jnp.dot(x, wub_ref[...], preferred_element_type=F32)
            o_ref[rs, :] = _swiglu(a, b).astype(BF16)

    @pl.when(pl.program_id(1) >= nused_ref[0])
    def _():
        o_ref[...] = jnp.zeros(o_ref.shape, o_ref.dtype)


def _gmm_down_kernel(te_ref, first_ref, nxt_ref, nused_ref, a_ref, wd_ref, o_ref,
                     sd_ref, wdb_ref, sem):
    _expert_weight_stream((wd_ref,), (sd_ref,), (wdb_ref,), sem, te_ref, first_ref, nxt_ref, TN_DN)

    @pl.when(pl.program_id(1) < nused_ref[0])
    def _():
        o_ref[...] = jnp.dot(a_ref[...], wdb_ref[...], preferred_element_type=F32)

    @pl.when(pl.program_id(1) >= nused_ref[0])
    def _():
        o_ref[...] = jnp.zeros(o_ref.shape, o_ref.dtype)


def _used_tile(j, nused):
    return jnp.minimum(j, nused[0] - 1)


def _gmm_gate_up(xs, w_gate, w_up, meta):
    te, first, nxt, nused = meta
    tn = TN_GU
    return pl.pallas_call(
        _gmm_gate_up_kernel,
        grid_spec=pltpu.PrefetchScalarGridSpec(
            num_scalar_prefetch=4,
            grid=(D_FF_EXPERT // tn, NT_MOE),
            in_specs=[pl.BlockSpec((T_MOE, D), lambda c, j, te, fi, nx, nu: (_used_tile(j, nu), 0)),
                      pl.BlockSpec(memory_space=pl.ANY),
                      pl.BlockSpec(memory_space=pl.ANY)],
            out_specs=pl.BlockSpec((T_MOE, tn), lambda c, j, te, fi, nx, nu: (j, c)),
            scratch_shapes=[pltpu.VMEM((D, tn), F32), pltpu.VMEM((D, tn), F32),
                            pltpu.VMEM((D, tn), BF16), pltpu.VMEM((D, tn), BF16),
                            pltpu.SemaphoreType.DMA((2,))]),
        out_shape=jax.ShapeDtypeStruct((P_MOE, D_FF_EXPERT), BF16),
        compiler_params=_params(2),
        name="moe_gate_up",
    )(te, first, nxt, nused, xs, w_gate, w_up)


def _gmm_down(act, w_down, meta):
    te, first, nxt, nused = meta
    tn = TN_DN
    return pl.pallas_call(
        _gmm_down_kernel,
        grid_spec=pltpu.PrefetchScalarGridSpec(
            num_scalar_prefetch=4,
            grid=(D // tn, NT_MOE),
            in_specs=[pl.BlockSpec((T_MOE, D_FF_EXPERT), lambda c, j, te, fi, nx, nu: (_used_tile(j, nu), 0)),
                      pl.BlockSpec(memory_space=pl.ANY)],
            out_specs=pl.BlockSpec((T_MOE, tn), lambda c, j, te, fi, nx, nu: (j, c)),
            scratch_shapes=[pltpu.VMEM((D_FF_EXPERT, tn), F32), pltpu.VMEM((D_FF_EXPERT, tn), BF16),
                            pltpu.SemaphoreType.DMA((1,))]),
        out_shape=jax.ShapeDtypeStruct((P_MOE, D), F32),
        compiler_params=_params(2),
        name="moe_down",
    )(te, first, nxt, nused, act, w_down)


def _combine_kernel(p0_ref, p1_ref, ys_ref, xa_ref, xb_ref, g_ref, w_ref, oc_ref, ol_ref,
                    a_ref, b_ref, sem):
    t = xa_ref.shape[0]
    i = pl.program_id(0)
    n = pl.num_programs(0)

    def issue(step, slot):
        base = step * t

        def body(r, carry):
            pltpu.make_async_copy(ys_ref.at[pl.ds(p0_ref[base + r], 1), :],
                                  a_ref.at[slot, pl.ds(r, 1), :], sem.at[0, slot]).start()
            pltpu.make_async_copy(ys_ref.at[pl.ds(p1_ref[base + r], 1), :],
                                  b_ref.at[slot, pl.ds(r, 1), :], sem.at[1, slot]).start()
            return carry

        lax.fori_loop(0, t, body, 0, unroll=8)

    @pl.when(i == 0)
    def _():
        issue(0, 0)

    @pl.when(i + 1 < n)
    def _():
        issue(i + 1, (i + 1) % 2)

    slot = i % 2
    pltpu.make_async_copy(ys_ref.at[pl.ds(0, t), :], a_ref.at[slot], sem.at[0, slot]).wait()
    pltpu.make_async_copy(ys_ref.at[pl.ds(0, t), :], b_ref.at[slot], sem.at[1, slot]).wait()
    w = w_ref[...]
    moe = w[:, 0:1] * a_ref[slot] + w[:, 1:2] * b_ref[slot]
    y = _pick(xa_ref, xb_ref, 0) + g_ref[...] * moe
    is_ctx = _is_ctx_tile(i, t)

    @pl.when(is_ctx)
    def _():
        oc_ref[...] = y

    @pl.when(jnp.logical_not(is_ctx))
    def _():
        ol_ref[...] = y


def _combine(ys, x, mod, layer, gate_chunk, wts, pos0, pos1):
    t = T_COMBINE
    n_ctx = MP // t
    row = lambda i, *_: i
    x_args, x_specs = _stream_in(x, t, D, row, lambda i, *_: 0)
    return pl.pallas_call(
        _combine_kernel,
        grid_spec=pltpu.PrefetchScalarGridSpec(
            num_scalar_prefetch=2,
            grid=(M // t,),
            in_specs=[pl.BlockSpec(memory_space=pl.ANY)] + x_specs + [
                _mod_spec(layer, gate_chunk, t, row),
                pl.BlockSpec((t, LANES), lambda i, p0, p1: (i, 0))],
            out_specs=[pl.BlockSpec((t, D), lambda i, p0, p1: (jnp.minimum(i, n_ctx - 1), 0)),
                       pl.BlockSpec((t, D), lambda i, p0, p1: (jnp.maximum(i - n_ctx, 0), 0))],
            scratch_shapes=[pltpu.VMEM((2, t, D), F32), pltpu.VMEM((2, t, D), F32),
                            pltpu.SemaphoreType.DMA((2, 2))]),
        out_shape=[jax.ShapeDtypeStruct((MP, D), F32), jax.ShapeDtypeStruct((MS, D), F32)],
        compiler_params=_params(1),
        name="moe_combine",
    )(pos0, pos1, ys, *x_args, mod, wts)


def _moe(x, h, idx, wts, mod, layer, w_gate, w_up, w_down):
    pos, zinfo, meta = _route_meta(idx[:, :TOP_K])
    pos0, pos1 = pos[0], pos[1]
    xs = _dispatch(h, pos0, pos1, zinfo)
    act = _gmm_gate_up(xs, w_gate, w_up, meta)
    ys = _gmm_down(act, w_down, meta)
    return _combine(ys, x, mod, layer, 5, wts, pos0, pos1)


def kernel(x_prompt, x_sample, cache_k, cache_v, c, c_ctx, w_ada, b_ada, norm1_g, norm2_g, w_in, q_norm_g, k_norm_g, sgu_norm_g, w_spatial, b_spatial, out_norm_g, w_out, ffn_w_gate, ffn_w_up, ffn_w_down, w_router, b_router, moe_w_gate, moe_w_up, moe_w_down):
    assert DEPTH == 2
    x = (x_prompt.reshape(MP, D), x_sample.reshape(MS, D))
    cond = jnp.concatenate([c_ctx[None, :], c, jnp.zeros((N_COND - 1 - DEC_BATCH, D), F32)], axis=0)
    mod = _modulation(cond, w_ada, b_ada).reshape(DEPTH, N_COND, 1, N_MOD * D)
    cos, sin = _rope_tables()
    w_in_bf = _cast_in_weights(w_in)

    new_k, new_v = [], []
    for i in range(DEPTH):
        q, kf, kb, vf, vb, u, gh = _in_projections(x, w_in_bf, mod, norm1_g, q_norm_g, k_norm_g,
                                                   sgu_norm_g, cos, sin, i)
        attn_ctx = _attention(q, kb, vb, None, None, i, batch=BATCH, seq=SEQ, row0=0)
        attn_lat = _attention(q, kb, vb, cache_k, cache_v, i, batch=DEC_BATCH, seq=DEC_SEQ, row0=MP)
        o = _sgu_merge(u, gh, attn_ctx, attn_lat, w_spatial, b_spatial, out_norm_g, i)
        j = i // 2
        if i % 2 == 0:
            x, h2 = _out_proj(o, w_out, x, mod, norm2_g, i)
            act = _ffn_gate_up(h2, ffn_w_gate, ffn_w_up, j)
            x = _mm_resid(act, ffn_w_down, x, mod, i, j, 5, TM, 512)
        else:
            x, h2, idx, wts = _out_proj(o, w_out, x, mod, norm2_g, i, router=(w_router[j], b_router[j]))
            x = _moe(x, h2, idx, wts, mod, i, moe_w_gate[j], moe_w_up[j], moe_w_down[j])
        new_k.append(kf.reshape(BATCH, SEQ, N_KV_HEADS, HEAD_DIM))
        new_v.append(vf.reshape(BATCH, SEQ, N_KV_HEADS, HEAD_DIM))

    y_prompt = x[0].reshape(BATCH, SEQ, D)
    y_sample = x[1].reshape(DEC_BATCH, DEC_SEQ, D)
    return (y_prompt, y_sample, jnp.stack(new_k, axis=1), jnp.stack(new_v, axis=1))
```

```python
import functools

import jax
import jax.numpy as jnp
from jax import lax
from jax.experimental import pallas as pl
from jax.experimental.pallas import tpu as pltpu

F32 = jnp.float32
BF16 = jnp.bfloat16
I32 = jnp.int32

D = 2048
BATCH, SEQ = 16, 256
DEC_BATCH, DEC_SEQ = 4, 2048
PAST_LEN = 256
DEPTH = 2
GRID_W = 64
CHUNK = 128
HEAD_DIM = 128
N_Q_HEADS, N_KV_HEADS = 8, 2
Q_PER_KV = N_Q_HEADS // N_KV_HEADS
ATTN_WIDTH = N_Q_HEADS * HEAD_DIM
KV_WIDTH = N_KV_HEADS * HEAD_DIM
N_SGU_HEADS = 8
SGU_WIDTH = N_SGU_HEADS * HEAD_DIM
IN_WIDTH = ATTN_WIDTH + 2 * KV_WIDTH + 2 * SGU_WIDTH
ROPE_THETA = 10000.0
ROPE_AXIS_DIM = HEAD_DIM // 2
D_FF = 5632
N_EXPERTS = 8
TOP_K = 2
D_FF_EXPERT = 2816
N_MOD = 6
EPS = 1e-6
ATTN_SCALE = HEAD_DIM ** -0.5
LOG2_E = 1.4426950408889634

MP = BATCH * SEQ
MS = DEC_BATCH * DEC_SEQ
M = MP + MS
N_COND = 8
LANES = 128
SUBLANES = 8
SUBLANE_BITS = 3

VMEM_LIMIT = 56 * 1024 * 1024

TM = 1024
TN_IN = 2 * KV_WIDTH
T_NORM = 512
T_Q = 512
ATTN_ROWS = 1024
T_MOE = 512
P_MOE = M * TOP_K + N_EXPERTS * T_MOE
NT_MOE = P_MOE // T_MOE
TN_GU = D_FF_EXPERT // 2
TN_DN = D
T_DISPATCH = 1024
T_COMBINE = 256
T_SGU = 1024
ZERO_ROWS = T_MOE // 2
PAD_BITS = ZERO_ROWS.bit_length()


def _params(n_axes):
    return pltpu.CompilerParams(dimension_semantics=("arbitrary",) * n_axes,
                                vmem_limit_bytes=VMEM_LIMIT)


def _cond_row(i, t):
    return jnp.where(i < MP // t, 0, 1 + (i - MP // t) // (DEC_SEQ // t))


def _is_ctx_tile(i, t):
    return i < MP // t


def _stream_in(x, t, width, row_of, col_of):
    n_ctx = MP // t
    pair = isinstance(x, tuple)
    base = 0 if pair else n_ctx
    ctx = pl.BlockSpec((t, width), lambda *g: (jnp.minimum(row_of(*g), n_ctx - 1), col_of(*g)))
    lat = pl.BlockSpec((t, width), lambda *g: (base + jnp.maximum(row_of(*g) - n_ctx, 0), col_of(*g)))
    return (list(x) if pair else [x, x]), [ctx, lat]


def _mod_spec(layer, chunk, t, row_of, col_of=None, tn=D):
    per = D // tn

    def index_map(*g):
        col = chunk * per + (col_of(*g) if col_of is not None else 0)
        return (layer, _cond_row(row_of(*g), t), 0, col)

    return pl.BlockSpec((None, None, 1, tn), index_map)


def _ada_kernel(c_ref, w_ref, b_ref, o_ref):
    c = c_ref[...]
    s = (c * jax.nn.sigmoid(c)).astype(BF16)
    o_ref[...] = jnp.dot(s, w_ref[...].astype(BF16), preferred_element_type=F32) + b_ref[...]


def _modulation(cond, w_ada, b_ada):
    tn = 1024
    width = N_MOD * D
    return pl.pallas_call(
        _ada_kernel,
        grid=(DEPTH, width // tn),
        in_specs=[pl.BlockSpec((N_COND, D), lambda l, n: (0, 0)),
                  pl.BlockSpec((None, D, tn), lambda l, n: (l, 0, n)),
                  pl.BlockSpec((None, 1, tn), lambda l, n: (l, 0, n))],
        out_specs=pl.BlockSpec((None, N_COND, tn), lambda l, n: (l, 0, n)),
        out_shape=jax.ShapeDtypeStruct((DEPTH, N_COND, width), F32),
        compiler_params=_params(2),
        name="modulation",
    )(cond, w_ada, b_ada.reshape(DEPTH, 1, width))


def _modulated_norm(x, g, sc, sh):
    y = x * lax.rsqrt(jnp.mean(x * x, axis=-1, keepdims=True) + EPS)
    return y * (g * (1.0 + sc)) + sh


def _pick(xa_ref, xb_ref, axis):
    t = xa_ref.shape[0]
    return jnp.where(_is_ctx_tile(pl.program_id(axis), t), xa_ref[...], xb_ref[...])


def _stream_tile_copy(xa_ref, xb_ref, lat_row0, buf_ref, sem, tile, slot, start):
    t = buf_ref.shape[1]

    def copy(src_ref, row):
        return pltpu.make_async_copy(src_ref.at[pl.ds(pl.multiple_of(row, t), t), :],
                                     buf_ref.at[slot], sem.at[slot])

    if not start:
        copy(xa_ref, 0).wait()
        return
    is_ctx = _is_ctx_tile(tile, t)

    @pl.when(is_ctx)
    def _():
        copy(xa_ref, tile * t).start()

    @pl.when(jnp.logical_not(is_ctx))
    def _():
        copy(xb_ref, lat_row0 + (tile - MP // t) * t).start()


def _next_stream_tile(xa_ref, xb_ref, lat_row0, buf_ref, sem, tile, n_tiles):
    @pl.when(tile == 0)
    def _():
        _stream_tile_copy(xa_ref, xb_ref, lat_row0, buf_ref, sem, 0, 0, True)

    @pl.when(tile + 1 < n_tiles)
    def _():
        _stream_tile_copy(xa_ref, xb_ref, lat_row0, buf_ref, sem, tile + 1, (tile + 1) % 2, True)

    slot = tile % 2
    _stream_tile_copy(xa_ref, xb_ref, lat_row0, buf_ref, sem, tile, slot, False)
    return slot


def _stream_hbm(x):
    return (x[0], x[1], 0) if isinstance(x, tuple) else (x, x, MP)


def _split_bf16(a):
    hi = a.astype(BF16)
    return hi, (a - hi.astype(F32)).astype(BF16)


def _route_top2(h, wr, br, idx_ref, wt_ref):
    h_hi, h_lo = _split_bf16(h)
    w_hi, w_lo = _split_bf16(wr)
    both = jnp.dot(h_hi, jnp.concatenate([w_hi, w_lo], axis=1), preferred_element_type=F32)
    logits = (both[:, :LANES] + jnp.dot(h_lo, w_hi, preferred_element_type=F32) + both[:, LANES:]) + br
    lane = lax.broadcasted_iota(I32, logits.shape, 1)
    neg = jnp.float32(-jnp.inf)
    lg = jnp.where(lane < N_EXPERTS, logits, neg)
    m1 = jnp.max(lg, axis=-1, keepdims=True)
    i1 = jnp.min(jnp.where(lg == m1, lane, LANES), axis=-1, keepdims=True)
    lg2 = jnp.where(lane == i1, neg, lg)
    m2 = jnp.max(lg2, axis=-1, keepdims=True)
    i2 = jnp.min(jnp.where(lg2 == m2, lane, LANES), axis=-1, keepdims=True)
    e = jnp.exp(m2 - m1)
    w1 = 1.0 / (1.0 + e)
    w2 = e / (1.0 + e)
    idx_ref[...] = jnp.where(lane == 0, i1, jnp.where(lane == 1, i2, 0))
    wt_ref[...] = jnp.where(lane == 0, w1, jnp.where(lane == 1, w2, 0.0))


W_PIECE = 512


def _out_proj_kernel(*refs, layer, lat_row0, route):
    if route:
        (o_ref, w_ref, xa_ref, xb_ref, gate_ref, g_ref, sc_ref, sh_ref, wr_ref, br_ref,
         xn_ref, h_ref, idx_ref, wt_ref, stage_ref, wbf_ref, xbuf_ref, wsem, xsem) = refs
    else:
        (o_ref, w_ref, xa_ref, xb_ref, gate_ref, g_ref, sc_ref, sh_ref,
         xn_ref, h_ref, stage_ref, wbf_ref, xbuf_ref, wsem, xsem) = refs
    i = pl.program_id(0)

    @pl.when(i == 0)
    def _():
        for p in range(D // W_PIECE):
            cols = pl.ds(p * W_PIECE, W_PIECE)
            cp = pltpu.make_async_copy(w_ref.at[layer, :, cols], stage_ref, wsem.at[0])
            cp.start()
            cp.wait()
            _cast_rows(stage_ref, wbf_ref.at[:, cols])

    slot = _next_stream_tile(xa_ref, xb_ref, lat_row0, xbuf_ref, xsem, i, pl.num_programs(0))
    acc = jnp.dot(o_ref[...], wbf_ref[...], preferred_element_type=F32)
    x_new = xbuf_ref[slot] + gate_ref[...] * acc
    xn_ref[...] = x_new
    h = _modulated_norm(x_new, g_ref[...], sc_ref[...], sh_ref[...])
    if route:
        h_ref[...] = h
        _route_top2(h, wr_ref[...], br_ref[...], idx_ref, wt_ref)
    else:
        h_ref[...] = h.astype(BF16)


def _out_proj(o, w_out, x, mod, norm2_g, layer, router=None):
    t = T_NORM
    row = lambda i: i
    xa, xb, lat_row0 = _stream_hbm(x)
    route = router is not None
    anyspace = pl.BlockSpec(memory_space=pl.ANY)
    rows = lambda w: pl.BlockSpec((t, w), lambda i: (i, 0))
    in_specs = [rows(D), anyspace, anyspace, anyspace,
                _mod_spec(layer, 2, t, row),
                pl.BlockSpec((None, 1, D), lambda i: (layer, 0, 0)),
                _mod_spec(layer, 4, t, row), _mod_spec(layer, 3, t, row)]
    args = [o, w_out, xa, xb, mod, norm2_g.reshape(DEPTH, 1, D), mod, mod]
    out_specs = [rows(D), rows(D)]
    out_shape = [jax.ShapeDtypeStruct((M, D), F32), jax.ShapeDtypeStruct((M, D), F32 if route else BF16)]
    if route:
        w_router, b_router = router
        args += [jnp.zeros((D, LANES), F32).at[:, :N_EXPERTS].set(w_router),
                 jnp.zeros((1, LANES), F32).at[0, :N_EXPERTS].set(b_router)]
        in_specs += [pl.BlockSpec((D, LANES), lambda i: (0, 0)), pl.BlockSpec((1, LANES), lambda i: (0, 0))]
        out_specs += [rows(LANES), rows(LANES)]
        out_shape += [jax.ShapeDtypeStruct((M, LANES), I32), jax.ShapeDtypeStruct((M, LANES), F32)]
    return pl.pallas_call(
        functools.partial(_out_proj_kernel, layer=layer, lat_row0=lat_row0, route=route),
        grid=(M // t,),
        in_specs=in_specs,
        out_specs=out_specs,
        out_shape=out_shape,
        scratch_shapes=[pltpu.VMEM((D, W_PIECE), F32), pltpu.VMEM((D, D), BF16),
                        pltpu.VMEM((2, t, D), F32),
                        pltpu.SemaphoreType.DMA((1,)), pltpu.SemaphoreType.DMA((2,))],
        compiler_params=_params(1),
        name="out_proj_router" if route else "out_proj",
    )(*args)


CAST_ROWS = 256


def _cast_rows(src_ref, dst_ref):
    def body(r, carry):
        rs = pl.ds(pl.multiple_of(r * CAST_ROWS, CAST_ROWS), CAST_ROWS)
        dst_ref[rs, :] = src_ref[rs, :].astype(BF16)
        return carry

    lax.fori_loop(0, src_ref.shape[0] // CAST_ROWS, body, 0)


def _head_rms(a):
    return lax.rsqrt(jnp.mean(a * a, axis=-1, keepdims=True) + EPS)


def _rope_partner(ag, perm):
    hi, lo = _split_bf16(ag)
    return (jnp.dot(hi, perm, preferred_element_type=F32)
            + jnp.dot(lo, perm, preferred_element_type=F32))


def _cast_kernel(w_ref, o_ref):
    o_ref[...] = w_ref[...].astype(BF16)


def _cast_in_weights(w_in):
    spec = pl.BlockSpec((None, D, TN_IN), lambda l, n: (l, 0, n))
    return pl.pallas_call(
        _cast_kernel,
        grid=(DEPTH, IN_WIDTH // TN_IN),
        in_specs=[spec],
        out_specs=spec,
        out_shape=jax.ShapeDtypeStruct(w_in.shape, BF16),
        compiler_params=_params(2),
        name="cast_w_in",
    )(w_in)


N_Q_TILES = ATTN_WIDTH // TN_IN
KV_TILE = N_Q_TILES
U_TILE0 = KV_TILE + 1
G_TILE0 = U_TILE0 + SGU_WIDTH // TN_IN
N_IN_TILES = IN_WIDTH // TN_IN


def _in_proj_kernel(xa_ref, xb_ref, n1_ref, sc0_ref, sh0_ref, sc1_ref, sh1_ref, w_ref,
                    qg_ref, kg_ref, sg_ref, pq_ref, pk_ref, cos_ref, sin_ref,
                    q_ref, kf_ref, kb_ref, vf_ref, vb_ref, u_ref, gh_ref,
                    xbuf_ref, h_ref, xsem, *, lat_row0):
    m = pl.program_id(0)
    n = pl.program_id(1)
    nm = pl.num_programs(0)
    cur = m % 2
    nxt = (m + 1) % 2
    has_next = m + 1 < nm
    fetch = functools.partial(_stream_tile_copy, xa_ref, xb_ref, lat_row0, xbuf_ref, xsem)

    def norm_rows(slot, rows, sc_ref, sh_ref):
        x = xbuf_ref[slot, rows, :]
        h_ref[slot, rows, :] = _modulated_norm(x, n1_ref[...], sc_ref[...], sh_ref[...]).astype(BF16)

    @pl.when((m == 0) & (n == 0))
    def _():
        fetch(0, 0, True)
        fetch(0, 0, False)
        norm_rows(0, slice(None), sc0_ref, sh0_ref)

    @pl.when((n == 0) & has_next)
    def _():
        fetch(m + 1, nxt, True)

    def matmul():
        return jnp.dot(h_ref[cur], w_ref[...], preferred_element_type=F32)

    for rope in (False, True):
        positioned = jnp.logical_not(_is_ctx_tile(m, TM)) if rope else _is_ctx_tile(m, TM)

        @pl.when((n < N_Q_TILES) & positioned)
        def _():
            acc = matmul()
            ag = acc * qg_ref[...]
            if rope:
                partner = _rope_partner(ag, pq_ref[...])
                cos, sin = cos_ref[...], sin_ref[...]
            for h in range(TN_IN // HEAD_DIM):
                sl = slice(h * HEAD_DIM, (h + 1) * HEAD_DIM)
                r = _head_rms(acc[:, sl]) * (ATTN_SCALE * LOG2_E)
                qh = ag[:, sl] * cos + partner[:, sl] * sin if rope else ag[:, sl]
                q_ref[:, sl] = (qh * r).astype(BF16)

        @pl.when((n == KV_TILE) & positioned)
        def _():
            acc = matmul()
            k = acc[:, :KV_WIDTH]
            ag = k * kg_ref[...]
            if rope:
                partner = _rope_partner(ag, pk_ref[...])
                cos, sin = cos_ref[...], sin_ref[...]
            for h in range(N_KV_HEADS):
                sl = slice(h * HEAD_DIM, (h + 1) * HEAD_DIM)
                r = _head_rms(k[:, sl])
                kf = ag[:, sl] * r
                if not rope:
                    kf_ref[:, sl] = kf
                kb_ref[:, sl] = ((ag[:, sl] * cos + partner[:, sl] * sin) * r if rope else kf).astype(BF16)
            v = acc[:, KV_WIDTH:]
            if not rope:
                vf_ref[...] = v
            vb_ref[...] = v.astype(BF16)

    half = TM // (G_TILE0 - U_TILE0)
    for k in range(G_TILE0 - U_TILE0):
        @pl.when((n == U_TILE0 + k) & has_next)
        def _():
            if k == 0:
                fetch(m + 1, nxt, False)
            u_ref[...] = matmul().astype(BF16)
            norm_rows(nxt, slice(k * half, (k + 1) * half), sc1_ref, sh1_ref)

        @pl.when((n == U_TILE0 + k) & jnp.logical_not(has_next))
        def _():
            u_ref[...] = matmul().astype(BF16)

    @pl.when(n >= G_TILE0)
    def _():
        acc = matmul()
        for h in range(TN_IN // HEAD_DIM):
            sl = slice(h * HEAD_DIM, (h + 1) * HEAD_DIM)
            a = acc[:, sl]
            gh_ref[:, sl] = (a * _head_rms(a) * sg_ref[:, sl]).astype(BF16)


def _rope_tables():
    n_rows = DEC_SEQ // GRID_W
    rows = jnp.broadcast_to(jnp.arange(n_rows)[:, None], (n_rows, GRID_W)).reshape(-1)
    cols = jnp.broadcast_to(jnp.arange(GRID_W)[None, :], (n_rows, GRID_W)).reshape(-1)
    inv = ROPE_THETA ** (-jnp.arange(0, ROPE_AXIS_DIM, 2, dtype=F32) / ROPE_AXIS_DIM)
    ang_r = rows.astype(F32)[:, None] * inv
    ang_c = cols.astype(F32)[:, None] * inv
    cos = jnp.concatenate([jnp.cos(ang_r), jnp.cos(ang_r), jnp.cos(ang_c), jnp.cos(ang_c)], axis=1)
    sin = jnp.concatenate([-jnp.sin(ang_r), jnp.sin(ang_r), -jnp.sin(ang_c), jnp.sin(ang_c)], axis=1)
    return cos, sin


def _partner_matrix(n_heads):
    w = n_heads * HEAD_DIM
    quarter = ROPE_AXIS_DIM // 2
    j = jnp.arange(w)
    partner = jnp.where((j % ROPE_AXIS_DIM) < quarter, j + quarter, j - quarter)
    return (jnp.arange(w)[:, None] == partner[None, :]).astype(BF16)


def _rope_block(m):
    return jnp.maximum(m - MP // TM, 0) % (DEC_SEQ // TM)


def _in_projections(x, w_in_bf, mod, norm1_g, q_norm_g, k_norm_g, sgu_norm_g, cos, sin, layer):
    tn = TN_IN
    xa, xb, lat_row0 = _stream_hbm(x)
    row = lambda m, n: m
    next_row = lambda m, n: jnp.minimum(m + 1, M // TM - 1)
    anyspace = pl.BlockSpec(memory_space=pl.ANY)
    const = lambda shape: pl.BlockSpec(shape, lambda m, n: (0,) * len(shape))
    rope_spec = pl.BlockSpec((TM, HEAD_DIM), lambda m, n: (_rope_block(m), 0))
    q_heads = tn // HEAD_DIM
    q_gain = jnp.tile(q_norm_g[layer], q_heads)[None, :]
    k_gain = jnp.tile(k_norm_g[layer], N_KV_HEADS)[None, :]
    g_tile = lambda n: jnp.clip(n - G_TILE0, 0, SGU_WIDTH // tn - 1)
    kv_out = pl.BlockSpec((TM, KV_WIDTH), lambda m, n: (m, 0))
    kv_ctx = pl.BlockSpec((TM, KV_WIDTH), lambda m, n: (jnp.minimum(m, MP // TM - 1), 0))
    kv_shape = lambda rows, dt: jax.ShapeDtypeStruct((rows, KV_WIDTH), dt)
    return pl.pallas_call(
        functools.partial(_in_proj_kernel, lat_row0=lat_row0),
        grid=(M // TM, N_IN_TILES),
        in_specs=[anyspace, anyspace,
                  pl.BlockSpec((None, 1, D), lambda m, n: (layer, 0, 0)),
                  _mod_spec(layer, 1, TM, row), _mod_spec(layer, 0, TM, row),
                  _mod_spec(layer, 1, TM, next_row), _mod_spec(layer, 0, TM, next_row),
                  pl.BlockSpec((None, D, tn), lambda m, n: (layer, 0, n)),
                  const((1, tn)), const((1, KV_WIDTH)),
                  pl.BlockSpec((None, 1, tn), lambda m, n: (layer, 0, g_tile(n))),
                  const((tn, tn)), const((KV_WIDTH, KV_WIDTH)), rope_spec, rope_spec],
        out_specs=[pl.BlockSpec((TM, tn), lambda m, n: (m, jnp.minimum(n, N_Q_TILES - 1))),
                   kv_ctx, kv_out, kv_ctx, kv_out,
                   pl.BlockSpec((TM, tn), lambda m, n: (m, jnp.clip(n - U_TILE0, 0, SGU_WIDTH // tn - 1))),
                   pl.BlockSpec((TM, tn), lambda m, n: (m, g_tile(n)))],
        out_shape=[jax.ShapeDtypeStruct((M, ATTN_WIDTH), BF16),
                   kv_shape(MP, F32), kv_shape(M, BF16), kv_shape(MP, F32), kv_shape(M, BF16),
                   jax.ShapeDtypeStruct((M, SGU_WIDTH), BF16),
                   jax.ShapeDtypeStruct((M, SGU_WIDTH), BF16)],
        scratch_shapes=[pltpu.VMEM((2, TM, D), F32), pltpu.VMEM((2, TM, D), BF16),
                        pltpu.SemaphoreType.DMA((2,))],
        compiler_params=_params(2),
        name="in_proj",
    )(xa, xb, norm1_g.reshape(DEPTH, 1, D), mod, mod, mod, mod, w_in_bf, q_gain, k_gain,
      sgu_norm_g.reshape(DEPTH, 1, SGU_WIDTH), _partner_matrix(q_heads), _partner_matrix(N_KV_HEADS),
      cos, sin)


def _qk(q, k):
    return lax.dot_general(q, k, (((1,), (1,)), ((), ())), preferred_element_type=F32)


def _attn_kernel(*refs, has_cache, n_batch, seq, tq):
    def with_ones(v):
        return jnp.concatenate([v, jnp.ones_like(v)], axis=1)

    if has_cache:
        q_ref, k_ref, v_ref, kc_ref, vc_ref, o_ref = refs
    else:
        q_ref, k_ref, v_ref, o_ref = refs
    for b in range(n_batch):
        rows_q = slice(b * tq, (b + 1) * tq)
        rows_k = slice(b * seq, (b + 1) * seq)
        for kv in range(N_KV_HEADS):
            kv_cols = slice(kv * HEAD_DIM, (kv + 1) * HEAD_DIM)
            k = k_ref[rows_k, kv_cols]
            v = with_ones(v_ref[rows_k, kv_cols])
            if has_cache:
                kc = kc_ref[:, kv_cols].astype(BF16)
                vc = with_ones(vc_ref[:, kv_cols].astype(BF16))
            for g in range(Q_PER_KV):
                head = kv * Q_PER_KV + g
                sl = slice(head * HEAD_DIM, (head + 1) * HEAD_DIM)
                q = q_ref[rows_q, sl]
                s = _qk(q, k)
                m = jnp.max(s, axis=-1, keepdims=True)
                if has_cache:
                    sc = _qk(q, kc)
                    m = jnp.maximum(m, jnp.max(sc, axis=-1, keepdims=True))
                o = jnp.dot(jnp.exp2(s - m).astype(BF16), v, preferred_element_type=F32)
                if has_cache:
                    o = o + jnp.dot(jnp.exp2(sc - m).astype(BF16), vc, preferred_element_type=F32)
                o_ref[rows_q, sl] = (o[:, :HEAD_DIM] / o[:, HEAD_DIM:]).astype(BF16)


def _attention(q, kb, vb, cache_k, cache_v, layer, *, batch, seq, row0):
    has_cache = cache_k is not None
    tq = min(T_Q, seq)
    nq = seq // tq
    n_batch = max(1, ATTN_ROWS // seq) if nq == 1 else 1
    q_spec = pl.BlockSpec((n_batch * tq, ATTN_WIDTH), lambda b, i: (row0 // (n_batch * tq) + b * nq + i, 0))
    kv_spec = pl.BlockSpec((n_batch * seq, KV_WIDTH), lambda b, i: (row0 // (n_batch * seq) + b, 0))
    in_specs = [q_spec, kv_spec, kv_spec]
    args = [q, kb, vb]
    if has_cache:
        c_spec = pl.BlockSpec((None, None, PAST_LEN, KV_WIDTH), lambda b, i: (b, layer, 0, 0))
        in_specs += [c_spec, c_spec]
        args += [cache_k.reshape(DEC_BATCH, DEPTH, PAST_LEN, KV_WIDTH),
                 cache_v.reshape(DEC_BATCH, DEPTH, PAST_LEN, KV_WIDTH)]
    return pl.pallas_call(
        functools.partial(_attn_kernel, has_cache=has_cache, n_batch=n_batch, seq=seq, tq=tq),
        grid=(batch // n_batch, nq),
        in_specs=in_specs,
        out_specs=pl.BlockSpec((n_batch * tq, ATTN_WIDTH), lambda b, i: (b * nq + i, 0)),
        out_shape=jax.ShapeDtypeStruct((batch * seq, ATTN_WIDTH), BF16),
        compiler_params=_params(2),
        name="attention_cached" if has_cache else "attention",
    )(*args)


def _sgu_merge_kernel(u_ref, gh_ref, ap_ref, as_ref, ws_ref, bs_ref, gn_ref, o_ref, sgu_ref):
    t = u_ref.shape[0]
    a = _pick(ap_ref, as_ref, 0).astype(F32)
    a = a * lax.rsqrt(jnp.mean(a * a, axis=-1, keepdims=True) + EPS) * gn_ref[:, :ATTN_WIDTH]
    o_ref[:, :ATTN_WIDTH] = a.astype(BF16)
    for h in range(N_SGU_HEADS):
        cs = slice(h * HEAD_DIM, (h + 1) * HEAD_DIM)
        w = ws_ref[h].astype(BF16)
        b = bs_ref[h]
        for c in range(t // CHUNK):
            rs = slice(c * CHUNK, (c + 1) * CHUNK)
            mixed = jnp.dot(w, gh_ref[rs, cs], preferred_element_type=F32) + b
            sgu_ref[rs, cs] = u_ref[rs, cs].astype(F32) * mixed
    s = sgu_ref[...]
    s = s * lax.rsqrt(jnp.mean(s * s, axis=-1, keepdims=True) + EPS) * gn_ref[:, ATTN_WIDTH:]
    o_ref[:, ATTN_WIDTH:] = s.astype(BF16)


def _sgu_merge(u, gh, attn_ctx, attn_lat, w_spatial, b_spatial, out_norm_g, layer):
    t = T_SGU
    bias = jnp.broadcast_to(b_spatial[:, :, :, None], (DEPTH, N_SGU_HEADS, CHUNK, HEAD_DIM))
    row = lambda w: pl.BlockSpec((t, w), lambda i: (i, 0))
    a_args, a_specs = _stream_in((attn_ctx, attn_lat), t, ATTN_WIDTH, lambda i: i, lambda i: 0)
    return pl.pallas_call(
        _sgu_merge_kernel,
        grid=(M // t,),
        in_specs=[row(SGU_WIDTH), row(SGU_WIDTH)] + a_specs + [
            pl.BlockSpec((None, N_SGU_HEADS, CHUNK, CHUNK), lambda i: (layer, 0, 0, 0)),
            pl.BlockSpec((None, N_SGU_HEADS, CHUNK, HEAD_DIM), lambda i: (layer, 0, 0, 0)),
            pl.BlockSpec((None, 1, D), lambda i: (layer, 0, 0))],
        out_specs=row(D),
        out_shape=jax.ShapeDtypeStruct((M, D), BF16),
        scratch_shapes=[pltpu.VMEM((t, SGU_WIDTH), F32)],
        compiler_params=_params(1),
        name="sgu_merge",
    )(u, gh, *a_args, w_spatial, bias, out_norm_g.reshape(DEPTH, 1, D))


def _mm_resid_kernel(a_ref, w_ref, x_ref, g_ref, o_ref, stage_ref, wbf_ref, sem, *, w_index):
    n = pl.program_id(0)
    tn = stage_ref.shape[1]

    def weight_copy(col_block):
        cols = pl.ds(pl.multiple_of(col_block * tn, LANES), tn)
        return pltpu.make_async_copy(w_ref.at[w_index, :, cols], stage_ref, sem.at[0])

    @pl.when(pl.program_id(1) == 0)
    def _():
        @pl.when(n == 0)
        def _():
            weight_copy(0).start()

        weight_copy(0).wait()
        _cast_rows(stage_ref, wbf_ref)

        @pl.when(n + 1 < pl.num_programs(0))
        def _():
            weight_copy(n + 1).start()

    acc = jnp.dot(a_ref[...], wbf_ref[...], preferred_element_type=F32)
    o_ref[...] = x_ref[...] + g_ref[...] * acc


def _mm_resid(a, w, x, mod, layer, w_index, gate_chunk, tm, tn):
    k = a.shape[1]
    row = lambda n, m: m
    col = lambda n, m: n
    return pl.pallas_call(
        functools.partial(_mm_resid_kernel, w_index=w_index),
        grid=(D // tn, M // tm),
        in_specs=[pl.BlockSpec((tm, k), lambda n, m: (m, 0)),
                  pl.BlockSpec(memory_space=pl.ANY),
                  pl.BlockSpec((tm, tn), lambda n, m: (m, n)),
                  _mod_spec(layer, gate_chunk, tm, row, col, tn=tn)],
        out_specs=pl.BlockSpec((tm, tn), lambda n, m: (m, n)),
        out_shape=jax.ShapeDtypeStruct((M, D), F32),
        scratch_shapes=[pltpu.VMEM((k, tn), F32), pltpu.VMEM((k, tn), BF16),
                        pltpu.SemaphoreType.DMA((1,))],
        compiler_params=_params(2),
        name="mm_resid",
    )(a, w, x, mod)


def _swiglu(a, b):
    return a * jax.nn.sigmoid(a) * b


def _ffn_gu_kernel(x_ref, wg_ref, wu_ref, o_ref, wgb_ref, wub_ref):
    @pl.when(pl.program_id(1) == 0)
    def _():
        _cast_rows(wg_ref, wgb_ref)
        _cast_rows(wu_ref, wub_ref)
    for r in range(x_ref.shape[0] // TM):
        rs = slice(r * TM, (r + 1) * TM)
        x = x_ref[rs, :]
        a = jnp.dot(x, wgb_ref[...], preferred_element_type=F32)
        b = jnp.dot(x, wub_ref[...], preferred_element_type=F32)
        o_ref[rs, :] = _swiglu(a, b).astype(BF16)


def _ffn_gate_up(h, w_gate, w_up, j):
    tm, tn = 2 * TM, 512
    w_spec = pl.BlockSpec((None, D, tn), lambda n, m: (j, 0, n))
    return pl.pallas_call(
        _ffn_gu_kernel,
        grid=(D_FF // tn, M // tm),
        in_specs=[pl.BlockSpec((tm, D), lambda n, m: (m, 0)), w_spec, w_spec],
        out_specs=pl.BlockSpec((tm, tn), lambda n, m: (m, n)),
        out_shape=jax.ShapeDtypeStruct((M, D_FF), BF16),
        scratch_shapes=[pltpu.VMEM((D, tn), BF16), pltpu.VMEM((D, tn), BF16)],
        compiler_params=_params(2),
        name="ffn_gate_up",
    )(h, w_gate, w_up)


def _route_meta(idx):
    t = T_MOE
    experts = jnp.arange(N_EXPERTS, dtype=I32)
    sel = idx.T[:, None, :] == experts[None, :, None]
    onehot = (sel[0] | sel[1]).astype(I32)
    csum = jnp.cumsum(onehot, axis=1)
    rank = csum - onehot
    count = csum[:, -1]
    ntile = (count + t - 1) // t
    tile_end = jnp.cumsum(ntile)
    tile_start = tile_end - ntile
    nused = tile_end[-1]
    row = (tile_start * t)[:, None] + rank
    pos = jnp.sum(jnp.where(sel, row[None], 0), axis=1)
    j = jnp.arange(NT_MOE, dtype=I32)
    te_raw = jnp.minimum(jnp.sum(j[:, None] >= tile_end[None, :], axis=1), N_EXPERTS - 1).astype(I32)
    te = jnp.where(j < nused, te_raw, te_raw[nused - 1])
    first = ((j == tile_start[te]) & (j < nused)).astype(I32)
    later = (ntile[None, :] > 0) & (experts[None, :] > te[:, None])
    nxt = jnp.min(jnp.where(later, experts[None, :], N_EXPERTS), axis=1)
    nxt = jnp.where(nxt == N_EXPERTS, -1, nxt).astype(I32)
    pad_start = tile_start * t + count
    pad_len = ntile * t - count
    tail = jnp.stack([nused * t, (NT_MOE - nused) * (t // ZERO_ROWS)])
    zinfo = jnp.concatenate([pad_start, pad_len, tail]).astype(I32)
    return pos.astype(I32), zinfo, (te, first, nxt, nused.reshape(1).astype(I32))


def _dispatch_kernel(p0_ref, p1_ref, z_ref, h_ref, xs_ref, zero_ref, sem, zsem):
    t = h_ref.shape[0]
    i = pl.program_id(0)
    base = i * t

    def clear_padding(start):
        def go(n, off):
            cp = pltpu.make_async_copy(zero_ref.at[pl.ds(0, n), :], xs_ref.at[pl.ds(off, n), :], zsem.at[0])
            cp.start() if start else cp.wait()

        for e in range(N_EXPERTS):
            off, ln = z_ref[e], z_ref[N_EXPERTS + e]
            end = off + ln
            for b in range(SUBLANE_BITS, PAD_BITS):
                @pl.when(((ln >> b) & 1) == 1)
                def _():
                    go(1 << b, pl.multiple_of(end - ((ln >> b) << b), SUBLANES))
            for k in range(SUBLANES - 1):
                @pl.when(k < (ln & (SUBLANES - 1)))
                def _():
                    go(1, off + k)
        tail0, n_tail = z_ref[2 * N_EXPERTS], z_ref[2 * N_EXPERTS + 1]

        def tail_body(k, carry):
            go(ZERO_ROWS, pl.multiple_of(tail0 + k * ZERO_ROWS, SUBLANES))
            return carry

        lax.fori_loop(0, n_tail, tail_body, 0)

    @pl.when(i == 0)
    def _():
        zero_ref[...] = jnp.zeros(zero_ref.shape, zero_ref.dtype)
        clear_padding(True)

    def issue(r, carry):
        src = h_ref.at[pl.ds(r, 1), :]
        pltpu.make_async_copy(src, xs_ref.at[pl.ds(p0_ref[base + r], 1), :], sem.at[0]).start()
        pltpu.make_async_copy(src, xs_ref.at[pl.ds(p1_ref[base + r], 1), :], sem.at[1]).start()
        return carry

    lax.fori_loop(0, t, issue, 0, unroll=8)
    pltpu.make_async_copy(h_ref, xs_ref.at[pl.ds(0, t), :], sem.at[0]).wait()
    pltpu.make_async_copy(h_ref, xs_ref.at[pl.ds(0, t), :], sem.at[1]).wait()

    @pl.when(i == 0)
    def _():
        clear_padding(False)


def _dispatch(h, pos0, pos1, zinfo):
    t = T_DISPATCH
    return pl.pallas_call(
        _dispatch_kernel,
        grid_spec=pltpu.PrefetchScalarGridSpec(
            num_scalar_prefetch=3,
            grid=(M // t,),
            in_specs=[pl.BlockSpec((t, D), lambda i, p0, p1, z: (i, 0))],
            out_specs=pl.BlockSpec(memory_space=pl.ANY),
            scratch_shapes=[pltpu.VMEM((ZERO_ROWS, D), F32),
                            pltpu.SemaphoreType.DMA((2,)), pltpu.SemaphoreType.DMA((1,))]),
        out_shape=jax.ShapeDtypeStruct((P_MOE, D), F32),
        compiler_params=_params(1),
        name="moe_dispatch",
    )(pos0, pos1, zinfo, h)


def _expert_weight_stream(w_refs, stage_refs, bf_refs, sem, te_ref, first_ref, nxt_ref, tn):
    c = pl.program_id(0)
    j = pl.program_id(1)
    nc = pl.num_programs(0)

    def copies(e, cc):
        col = pl.multiple_of(cc * tn, LANES)
        return [pltpu.make_async_copy(w.at[e, :, pl.ds(col, tn)], st, sem.at[k])
                for k, (w, st) in enumerate(zip(w_refs, stage_refs))]

    def start(e, cc):
        for cp in copies(e, cc):
            cp.start()

    @pl.when((c == 0) & (j == 0))
    def _():
        start(te_ref[0], 0)

    @pl.when(first_ref[j] == 1)
    def _():
        for cp in copies(0, 0):
            cp.wait()
        for st, bf in zip(stage_refs, bf_refs):
            _cast_rows(st, bf)
        ne = nxt_ref[j]

        @pl.when(ne >= 0)
        def _():
            start(ne, c)

        @pl.when((ne < 0) & (c + 1 < nc))
        def _():
            start(te_ref[0], c + 1)


def _gmm_gate_up_kernel(te_ref, first_ref, nxt_ref, nused_ref, xs_ref, wg_ref, wu_ref, o_ref,
                        sg_ref, su_ref, wgb_ref, wub_ref, sem):
    _expert_weight_stream((wg_ref, wu_ref), (sg_ref, su_ref), (wgb_ref, wub_ref), sem,
                          te_ref, first_ref, nxt_ref, TN_GU)

    @pl.when(pl.program_id(1) < nused_ref[0])
    def _():
        half = T_MOE // 2
        for r in range(2):
            rs = slice(r * half, (r + 1) * half)
            x = xs_ref[rs, :].astype(BF16)
            a = jnp.dot(x, wgb_ref[...], preferred_element_type=F32)
            b = jnp.dot(x, wub_ref[...], preferred_element_type=F32)
            o_ref[rs, :] = _swiglu(a, b).astype(BF16)

    @pl.when(pl.program_id(1) >= nused_ref[0])
    def _():
        o_ref[...] = jnp.zeros(o_ref.shape, o_ref.dtype)


def _gmm_down_kernel(te_ref, first_ref, nxt_ref, nused_ref, a_ref, wd_ref, o_ref,
                     sd_ref, wdb_ref, sem):
    _expert_weight_stream((wd_ref,), (sd_ref,), (wdb_ref,), sem, te_ref, first_ref, nxt_ref, TN_DN)

    @pl.when(pl.program_id(1) < nused_ref[0])
    def _():
        o_ref[...] = jnp.dot(a_ref[...], wdb_ref[...], preferred_element_type=F32)

    @pl.when(pl.program_id(1) >= nused_ref[0])
    def _():
        o_ref[...] = jnp.zeros(o_ref.shape, o_ref.dtype)


def _used_tile(j, nused):
    return jnp.minimum(j, nused[0] - 1)


def _gmm_gate_up(xs, w_gate, w_up, meta):
    te, first, nxt, nused = meta
    tn = TN_GU
    return pl.pallas_call(
        _gmm_gate_up_kernel,
        grid_spec=pltpu.PrefetchScalarGridSpec(
            num_scalar_prefetch=4,
            grid=(D_FF_EXPERT // tn, NT_MOE),
            in_specs=[pl.BlockSpec((T_MOE, D), lambda c, j, te, fi, nx, nu: (_used_tile(j, nu), 0)),
                      pl.BlockSpec(memory_space=pl.ANY),
                      pl.BlockSpec(memory_space=pl.ANY)],
            out_specs=pl.BlockSpec((T_MOE, tn), lambda c, j, te, fi, nx, nu: (j, c)),
            scratch_shapes=[pltpu.VMEM((D, tn), F32), pltpu.VMEM((D, tn), F32),
                            pltpu.VMEM((D, tn), BF16), pltpu.VMEM((D, tn), BF16),
                            pltpu.SemaphoreType.DMA((2,))]),
        out_shape=jax.ShapeDtypeStruct((P_MOE, D_FF_EXPERT), BF16),
        compiler_params=_params(2),
        name="moe_gate_up",
    )(te, first, nxt, nused, xs, w_gate, w_up)


def _gmm_down(act, w_down, meta):
    te, first, nxt, nused = meta
    tn = TN_DN
    return pl.pallas_call(
        _gmm_down_kernel,
        grid_spec=pltpu.PrefetchScalarGridSpec(
            num_scalar_prefetch=4,
            grid=(D // tn, NT_MOE),
            in_specs=[pl.BlockSpec((T_MOE, D_FF_EXPERT), lambda c, j, te, fi, nx, nu: (_used_tile(j, nu), 0)),
                      pl.BlockSpec(memory_space=pl.ANY)],
            out_specs=pl.BlockSpec((T_MOE, tn), lambda c, j, te, fi, nx, nu: (j, c)),
            scratch_shapes=[pltpu.VMEM((D_FF_EXPERT, tn), F32), pltpu.VMEM((D_FF_EXPERT, tn), BF16),
                            pltpu.SemaphoreType.DMA((1,))]),
        out_shape=jax.ShapeDtypeStruct((P_MOE, D), F32),
        compiler_params=_params(2),
        name="moe_down",
    )(te, first, nxt, nused, act, w_down)


def _combine_kernel(p0_ref, p1_ref, ys_ref, xa_ref, xb_ref, g_ref, w_ref, oc_ref, ol_ref,
                    a_ref, b_ref, sem):
    t = xa_ref.shape[0]
    i = pl.program_id(0)
    n = pl.num_programs(0)

    def issue(step, slot):
        base = step * t

        def body(r, carry):
            pltpu.make_async_copy(ys_ref.at[pl.ds(p0_ref[base + r], 1), :],
                                  a_ref.at[slot, pl.ds(r, 1), :], sem.at[0, slot]).start()
            pltpu.make_async_copy(ys_ref.at[pl.ds(p1_ref[base + r], 1), :],
                                  b_ref.at[slot, pl.ds(r, 1), :], sem.at[1, slot]).start()
            return carry

        lax.fori_loop(0, t, body, 0, unroll=8)

    @pl.when(i == 0)
    def _():
        issue(0, 0)

    @pl.when(i + 1 < n)
    def _():
        issue(i + 1, (i + 1) % 2)

    slot = i % 2
    pltpu.make_async_copy(ys_ref.at[pl.ds(0, t), :], a_ref.at[slot], sem.at[0, slot]).wait()
    pltpu.make_async_copy(ys_ref.at[pl.ds(0, t), :], b_ref.at[slot], sem.at[1, slot]).wait()
    w = w_ref[...]
    moe = w[:, 0:1] * a_ref[slot] + w[:, 1:2] * b_ref[slot]
    y = _pick(xa_ref, xb_ref, 0) + g_ref[...] * moe
    is_ctx = _is_ctx_tile(i, t)

    @pl.when(is_ctx)
    def _():
        oc_ref[...] = y

    @pl.when(jnp.logical_not(is_ctx))
    def _():
        ol_ref[...] = y


def _combine(ys, x, mod, layer, gate_chunk, wts, pos0, pos1):
    t = T_COMBINE
    n_ctx = MP // t
    row = lambda i, *_: i
    x_args, x_specs = _stream_in(x, t, D, row, lambda i, *_: 0)
    return pl.pallas_call(
        _combine_kernel,
        grid_spec=pltpu.PrefetchScalarGridSpec(
            num_scalar_prefetch=2,
            grid=(M // t,),
            in_specs=[pl.BlockSpec(memory_space=pl.ANY)] + x_specs + [
                _mod_spec(layer, gate_chunk, t, row),
                pl.BlockSpec((t, LANES), lambda i, p0, p1: (i, 0))],
            out_specs=[pl.BlockSpec((t, D), lambda i, p0, p1: (jnp.minimum(i, n_ctx - 1), 0)),
                       pl.BlockSpec((t, D), lambda i, p0, p1: (jnp.maximum(i - n_ctx, 0), 0))],
            scratch_shapes=[pltpu.VMEM((2, t, D), F32), pltpu.VMEM((2, t, D), F32),
                            pltpu.SemaphoreType.DMA((2, 2))]),
        out_shape=[jax.ShapeDtypeStruct((MP, D), F32), jax.ShapeDtypeStruct((MS, D), F32)],
        compiler_params=_params(1),
        name="moe_combine",
    )(pos0, pos1, ys, *x_args, mod, wts)


def _moe(x, h, idx, wts, mod, layer, w_gate, w_up, w_down):
    pos, zinfo, meta = _route_meta(idx[:, :TOP_K])
    pos0, pos1 = pos[0], pos[1]
    xs = _dispatch(h, pos0, pos1, zinfo)
    act = _gmm_gate_up(xs, w_gate, w_up, meta)
    ys = _gmm_down(act, w_down, meta)
    return _combine(ys, x, mod, layer, 5, wts, pos0, pos1)


def kernel(x_prompt, x_sample, cache_k, cache_v, c, c_ctx, w_ada, b_ada, norm1_g, norm2_g, w_in, q_norm_g, k_norm_g, sgu_norm_g, w_spatial, b_spatial, out_norm_g, w_out, ffn_w_gate, ffn_w_up, ffn_w_down, w_router, b_router, moe_w_gate, moe_w_up, moe_w_down):
    assert DEPTH == 2
    x = (x_prompt.reshape(MP, D), x_sample.reshape(MS, D))
    cond = jnp.concatenate([c_ctx[None, :], c, jnp.zeros((N_COND - 1 - DEC_BATCH, D), F32)], axis=0)
    mod = _modulation(cond, w_ada, b_ada).reshape(DEPTH, N_COND, 1, N_MOD * D)
    cos, sin = _rope_tables()
    w_in_bf = _cast_in_weights(w_in)

    new_k, new_v = [], []
    for i in range(DEPTH):
        q, kf, kb, vf, vb, u, gh = _in_projections(x, w_in_bf, mod, norm1_g, q_norm_g, k_norm_g,
                                                   sgu_norm_g, cos, sin, i)
        attn_ctx = _attention(q, kb, vb, None, None, i, batch=BATCH, seq=SEQ, row0=0)
        attn_lat = _attention(q, kb, vb, cache_k, cache_v, i, batch=DEC_BATCH, seq=DEC_SEQ, row0=MP)
        o = _sgu_merge(u, gh, attn_ctx, attn_lat, w_spatial, b_spatial, out_norm_g, i)
        j = i // 2
        if i % 2 == 0:
            x, h2 = _out_proj(o, w_out, x, mod, norm2_g, i)
            act = _ffn_gate_up(h2, ffn_w_gate, ffn_w_up, j)
            x = _mm_resid(act, ffn_w_down, x, mod, i, j, 5, TM, 512)
        else:
            x, h2, idx, wts = _out_proj(o, w_out, x, mod, norm2_g, i, router=(w_router[j], b_router[j]))
            x = _moe(x, h2, idx, wts, mod, i, moe_w_gate[j], moe_w_up[j], moe_w_down[j])
        new_k.append(kf.reshape(BATCH, SEQ, N_KV_HEADS, HEAD_DIM))
        new_v.append(vf.reshape(BATCH, SEQ, N_KV_HEADS, HEAD_DIM))

    y_prompt = x[0].reshape(BATCH, SEQ, D)
    y_sample = x[1].reshape(DEC_BATCH, DEC_SEQ, D)
    return (y_prompt, y_sample, jnp.stack(new_k, axis=1), jnp.stack(new_v, axis=1))
```
